```python
import jax, jax.numpy as jnp
from jax import lax
import numpy as np

D_MODEL = 2048
BATCH = 2
SEQ = 8192
DEPTH = 1

RET_WIDTH = D_MODEL // 2
CONV_WIDTH = D_MODEL - RET_WIDTH
MIX_WIDTH = RET_WIDTH + CONV_WIDTH
RET_HEADS = 8
RET_HEAD_DIM = RET_WIDTH // RET_HEADS
CONV_GROUPS = 8
CONV_TAPS = 31
CHUNK = 128
ROPE_BASE = 10000.0
IN_COLS = 4 * RET_WIDTH + 2 * CONV_WIDTH
N_GROUPS = 4
EXPERTS_PER_GROUP = 4
N_EXPERTS = N_GROUPS * EXPERTS_PER_GROUP
TOP_K_INNER = 2
D_FF_EXPERT = D_MODEL // 2
LN_EPS = 1e-5
ALPHA = (2.0 * DEPTH) ** 0.25
BETA = (8.0 * DEPTH) ** -0.25

kernel_name = "hybrid_retention_conformer_hmoe_deepnorm"


def layer_norm(x, g, b):
    xf = x.astype(jnp.float32)
    mu = jnp.mean(xf, axis=-1, keepdims=True)
    var = jnp.mean(jnp.square(xf - mu), axis=-1, keepdims=True)
    y = (xf - mu) * lax.rsqrt(var + LN_EPS)
    return (y * g.astype(jnp.float32) + b.astype(jnp.float32)).astype(x.dtype)


def group_norm(x, n_groups, g, b):
    B, S, C = x.shape
    xf = x.astype(jnp.float32).reshape(B, S, n_groups, C // n_groups)
    mu = jnp.mean(xf, axis=-1, keepdims=True)
    var = jnp.mean(jnp.square(xf - mu), axis=-1, keepdims=True)
    y = ((xf - mu) * lax.rsqrt(var + LN_EPS)).reshape(B, S, C)
    return (y * g.astype(jnp.float32) + b.astype(jnp.float32)).astype(x.dtype)


def rotary(t):
    S, dh = t.shape[1], t.shape[-1]
    inv = ROPE_BASE ** (-jnp.arange(0, dh, 2, dtype=jnp.float32) / dh)
    ang = jnp.arange(S, dtype=jnp.float32)[:, None] * inv[None, :]
    cos = jnp.cos(ang)[None, :, None, :].astype(t.dtype)
    sin = jnp.sin(ang)[None, :, None, :].astype(t.dtype)
    t1, t2 = t[..., 0::2], t[..., 1::2]
    return jnp.stack([t1 * cos - t2 * sin, t1 * sin + t2 * cos], axis=-1).reshape(t.shape)


def retention_chunkwise(q, k, v):
    B, S, H, Dh = q.shape
    N = S // CHUNK
    f32 = jnp.float32
    log_gamma = jnp.log1p(-jnp.exp2(-5.0 - jnp.arange(H, dtype=f32)))
    qc = q.astype(f32).reshape(B, N, CHUNK, H, Dh)
    kc = k.astype(f32).reshape(B, N, CHUNK, H, Dh)
    vc = v.astype(f32).reshape(B, N, CHUNK, H, Dh)
    pos = jnp.arange(CHUNK, dtype=f32)
    diff = pos[:, None] - pos[None, :]
    causal = diff >= 0
    decay_intra = jnp.where(causal[None],
                            jnp.exp(log_gamma[:, None, None] * jnp.where(causal, diff, 0.0)[None]),
                            0.0)
    scores = jnp.einsum('bnqhd,bnkhd->bnhqk', qc, kc) * decay_intra[None, None]
    inner = jnp.einsum('bnhqk,bnkhd->bnqhd', scores, vc)
    zeta = jnp.exp(log_gamma[:, None] * (CHUNK - 1 - pos)[None])
    kv = jnp.einsum('bnkhd,hk,bnkhe->bnhde', kc, zeta, vc)
    chunk_decay = jnp.exp(log_gamma * CHUNK)[None, :, None, None]

    def step(state, kv_i):
        return chunk_decay * state + kv_i, state

    init = jnp.zeros((B, H, Dh, Dh), f32)
    _, prev = lax.scan(step, init, jnp.moveaxis(kv, 1, 0))
    prev = jnp.moveaxis(prev, 0, 1)
    xi = jnp.exp(log_gamma[:, None] * (pos + 1.0)[None])
    cross = jnp.einsum('bnqhd,bnhde,hq->bnqhe', qc, prev, xi)
    return (inner + cross).reshape(B, S, H, Dh)


def hybrid_mixer(x, w_in, ret_norm_g, ret_norm_b, conv_w, conv_b, conv_norm_g, conv_norm_b, w_out):
    B, S, _ = x.shape
    proj = x @ w_in
    q, k, v, gate, glu_a, glu_b = jnp.split(
        proj, np.cumsum([RET_WIDTH] * 4 + [CONV_WIDTH]).tolist(), axis=-1)
    q = rotary(q.reshape(B, S, RET_HEADS, RET_HEAD_DIM))
    k = rotary(k.reshape(B, S, RET_HEADS, RET_HEAD_DIM)) * (RET_HEAD_DIM ** -0.5)
    v = v.reshape(B, S, RET_HEADS, RET_HEAD_DIM)
    ret = retention_chunkwise(q, k, v).reshape(B, S, RET_WIDTH).astype(x.dtype)
    ret = jax.nn.silu(gate) * group_norm(ret, RET_HEADS, ret_norm_g, ret_norm_b)
    u = glu_a * jax.nn.sigmoid(glu_b)
    u = jnp.pad(u, ((0, 0), (CONV_TAPS - 1, 0), (0, 0)))
    c = lax.conv_general_dilated(u, conv_w, window_strides=(1,), padding='VALID',
                                 dimension_numbers=('NWC', 'WIO', 'NWC'),
                                 feature_group_count=CONV_WIDTH) + conv_b
    c = jax.nn.silu(group_norm(c, CONV_GROUPS, conv_norm_g, conv_norm_b))
    return jnp.concatenate([ret, c], axis=-1) @ w_out


def hierarchical_moe(h, w_router_group, b_router_group, w_router_expert, b_router_expert, w1, w3, w2):
    B, S, D = h.shape
    T = B * S
    hf = h.reshape(T, D)
    g_logits = (hf @ w_router_group).astype(jnp.float32) + b_router_group.astype(jnp.float32)
    g_prob = jax.nn.softmax(g_logits, axis=-1)
    g_onehot = jax.nn.one_hot(jnp.argmax(g_logits, axis=-1), N_GROUPS, dtype=jnp.float32)
    g_w = jnp.sum(g_prob * g_onehot, axis=-1)
    e_all = jnp.einsum('td,gde->tge', hf, w_router_expert).astype(jnp.float32) \
        + b_router_expert.astype(jnp.float32)[None]
    e_logits = jnp.einsum('tge,tg->te', e_all, g_onehot)
    top_vals, top_idx = lax.top_k(e_logits, TOP_K_INNER)
    inner_w = jax.nn.softmax(top_vals, axis=-1) * g_w[:, None]
    g_idx = jnp.argmax(g_onehot, axis=-1).astype(jnp.int32)
    expert_id = g_idx[:, None] * EXPERTS_PER_GROUP + top_idx
    combine = jnp.sum(jax.nn.one_hot(expert_id, N_EXPERTS, dtype=jnp.float32)
                      * inner_w[..., None], axis=1).astype(h.dtype)
    y = jnp.zeros_like(hf)
    for e in range(N_EXPERTS):
        a = jax.nn.silu(hf @ w1[e]) * (hf @ w3[e])
        y = y + combine[:, e:e + 1] * (a @ w2[e])
    return y.reshape(B, S, D)


def setup_inputs(seed: int = 0) -> dict:
    key = jax.random.key(seed)
    ks = jax.random.split(key, 20)
    f32 = jnp.float32
    nrm = lambda k, shape, s: jax.random.normal(k, shape, f32) * s
    return {
        "x": nrm(ks[0], (BATCH, SEQ, D_MODEL), 1.0),
        "w_in": nrm(ks[1], (D_MODEL, IN_COLS), D_MODEL ** -0.5),
        "ret_norm_g": 1.0 + nrm(ks[2], (RET_WIDTH,), 0.02),
        "ret_norm_b": nrm(ks[3], (RET_WIDTH,), 0.02),
        "conv_w": nrm(ks[4], (CONV_TAPS, 1, CONV_WIDTH), CONV_TAPS ** -0.5),
        "conv_b": nrm(ks[5], (CONV_WIDTH,), 0.02),
        "conv_norm_g": 1.0 + nrm(ks[6], (CONV_WIDTH,), 0.02),
        "conv_norm_b": nrm(ks[7], (CONV_WIDTH,), 0.02),
        "w_out": nrm(ks[8], (MIX_WIDTH, D_MODEL), BETA * MIX_WIDTH ** -0.5),
        "ln1_g": 1.0 + nrm(ks[9], (D_MODEL,), 0.02),
        "ln1_b": nrm(ks[10], (D_MODEL,), 0.02),
        "w_router_group": nrm(ks[11], (D_MODEL, N_GROUPS), D_MODEL ** -0.5),
        "b_router_group": nrm(ks[12], (N_GROUPS,), 0.01),
        "w_router_expert": nrm(ks[13], (N_GROUPS, D_MODEL, EXPERTS_PER_GROUP), D_MODEL ** -0.5),
        "b_router_expert": nrm(ks[14], (N_GROUPS, EXPERTS_PER_GROUP), 0.01),
        "w1": nrm(ks[15], (N_EXPERTS, D_MODEL, D_FF_EXPERT), D_MODEL ** -0.5),
        "w3": nrm(ks[16], (N_EXPERTS, D_MODEL, D_FF_EXPERT), D_MODEL ** -0.5),
        "w2": nrm(ks[17], (N_EXPERTS, D_FF_EXPERT, D_MODEL), BETA * D_FF_EXPERT ** -0.5),
        "ln2_g": 1.0 + nrm(ks[18], (D_MODEL,), 0.02),
        "ln2_b": nrm(ks[19], (D_MODEL,), 0.02),
    }


def reference(x, w_in, ret_norm_g, ret_norm_b, conv_w, conv_b, conv_norm_g, conv_norm_b, w_out,
              ln1_g, ln1_b, w_router_group, b_router_group, w_router_expert, b_router_expert,
              w1, w3, w2, ln2_g, ln2_b):
    h = x
    for _ in range(DEPTH):
        mix = hybrid_mixer(h, w_in, ret_norm_g, ret_norm_b, conv_w, conv_b,
                           conv_norm_g, conv_norm_b, w_out)
        h = layer_norm(ALPHA * h + mix, ln1_g, ln1_b)
        ffn = hierarchical_moe(h, w_router_group, b_router_group, w_router_expert,
                               b_router_expert, w1, w3, w2)
        h = layer_norm(ALPHA * h + ffn, ln2_g, ln2_b)
    return h
```

```python
import functools
import math

import numpy as np
import jax
import jax.numpy as jnp
from jax import lax
from jax.experimental import pallas as pl
from jax.experimental.pallas import tpu as pltpu

D_MODEL = 2048
RET_WIDTH = D_MODEL // 2
CONV_WIDTH = D_MODEL - RET_WIDTH
HEADS = 8
HEAD_DIM = RET_WIDTH // HEADS
CONV_GROUPS = 8
CONV_TAPS = 31
CHUNK = 128
ROPE_BASE = 10000.0
IN_COLS = 4 * RET_WIDTH + 2 * CONV_WIDTH
N_GROUPS = 4
EXPERTS_PER_GROUP = 4
N_EXPERTS = N_GROUPS * EXPERTS_PER_GROUP
D_FF = D_MODEL // 2
LN_EPS = 1e-5
ALPHA = 2.0 ** 0.25

LANES = 128
SUBLANES = 8
VMEM_LIMIT = 56 * 1024 * 1024
CONV_HALO = 32

F32 = jnp.float32
BF16 = jnp.bfloat16


def _sigmoid(x):
    return 1.0 / (1.0 + jnp.exp(-x))


def _lane_norm(x, g, b):
    mu = jnp.mean(x, axis=-1, keepdims=True)
    xc = x - mu
    var = jnp.mean(xc * xc, axis=-1, keepdims=True)
    return xc * lax.rsqrt(var + LN_EPS) * g + b


def _inproj_kernel(x_ref, w_ref, o_ref):
    res = jnp.dot(x_ref[...].astype(BF16), w_ref[...], preferred_element_type=F32)
    for s in range(o_ref.shape[0]):
        o_ref[s] = res[:, s * LANES:(s + 1) * LANES].astype(BF16)


def _in_proj(x2, w_in_b, tm=512, tn=2048):
    T = x2.shape[0]
    nblk = tn // LANES
    return pl.pallas_call(
        _inproj_kernel,
        grid=(IN_COLS // tn, T // tm),
        in_specs=[pl.BlockSpec((tm, D_MODEL), lambda j, i: (i, 0)),
                  pl.BlockSpec((D_MODEL, tn), lambda j, i: (0, j))],
        out_specs=pl.BlockSpec((nblk, tm, LANES), lambda j, i: (j, i, 0)),
        out_shape=jax.ShapeDtypeStruct((IN_COLS // LANES, T, LANES), BF16),
        compiler_params=pltpu.CompilerParams(
            dimension_semantics=("arbitrary", "arbitrary"), vmem_limit_bytes=VMEM_LIMIT),
        name="in_proj",
    )(x2, w_in_b)


def _rotary(t, cos, sin_signed, even_lane):
    partner = jnp.where(even_lane, pltpu.roll(t, LANES - 1, 1), pltpu.roll(t, 1, 1))
    return t * cos + partner * sin_signed


def _mixer_kernel(cdec_ref, q_ref, k_ref, v_ref, gate_ref, ga_ref, gb_ref, cos_ref, sin_ref,
                  decay_ref, zeta_ref, xi_ref, rg_ref, rb_ref, cw_ref, cb_ref, cg_ref, cbeta_ref,
                  ret_ref, conv_ref, state_ref, uext_ref, shift_ref, *, rows):
    h = pl.program_id(1)
    n = pl.program_id(2)

    @pl.when(n == 0)
    def _():
        state_ref[...] = jnp.zeros_like(state_ref)
        uext_ref[0:CONV_HALO, :] = jnp.zeros((CONV_HALO, LANES), F32)

    even_lane = (lax.broadcasted_iota(jnp.int32, (CHUNK, LANES), 1) % 2) == 0
    cdec = cdec_ref[h]
    decay = decay_ref[0]
    zeta = zeta_ref[0]
    xi = xi_ref[0]
    rg = rg_ref[0]
    rb = rb_ref[0]
    scale = HEAD_DIM ** -0.5

    def chunk_body(c, carry):
        r0 = pl.multiple_of(c * CHUNK, CHUNK)
        cos = cos_ref[pl.ds(r0, CHUNK), :]
        sin = sin_ref[pl.ds(r0, CHUNK), :]
        q = _rotary(q_ref[0, pl.ds(r0, CHUNK), :].astype(F32), cos, sin, even_lane)
        k = _rotary(k_ref[0, pl.ds(r0, CHUNK), :].astype(F32), cos, sin, even_lane) * scale
        v = v_ref[0, pl.ds(r0, CHUNK), :]
        scores = lax.dot_general(q.astype(BF16), k.astype(BF16), (((1,), (1,)), ((), ())),
                                 preferred_element_type=F32) * decay
        inner = jnp.dot(scores.astype(BF16), v, preferred_element_type=F32)
        prev = state_ref[...]
        cross = jnp.dot((q * xi).astype(BF16), prev.astype(BF16), preferred_element_type=F32)
        kz_t = jnp.transpose(k * zeta).astype(BF16)
        kv = jnp.dot(kz_t, v, preferred_element_type=F32)
        state_ref[...] = cdec * prev + kv
        y = _lane_norm(inner + cross, rg, rb)
        g = gate_ref[0, pl.ds(r0, CHUNK), :].astype(F32)
        ret_ref[0, pl.ds(r0, CHUNK), :] = (g * _sigmoid(g) * y).astype(BF16)
        return carry

    lax.fori_loop(0, rows // CHUNK, chunk_body, 0)

    a = ga_ref[0].astype(F32)
    b = gb_ref[0].astype(F32)
    uext_ref[CONV_HALO:CONV_HALO + rows, :] = a * _sigmoid(b)
    span = rows + CONV_HALO - SUBLANES
    for s in range(1, SUBLANES):
        shift_ref[s, 0:span, :] = uext_ref[s:s + span, :]

    cb = cb_ref[0]
    cg = cg_ref[0]
    cbeta = cbeta_ref[0]
    blk = 64
    first = CONV_HALO - (CONV_TAPS - 1)

    def row_body(rbk, carry):
        r0 = pl.multiple_of(rbk * blk, blk)
        acc = jnp.zeros((blk, LANES), F32)
        for j in range(CONV_TAPS):
            off = first + j
            start = r0 + (off // SUBLANES) * SUBLANES
            if off % SUBLANES == 0:
                sl = uext_ref[pl.ds(start, blk), :]
            else:
                sl = shift_ref[off % SUBLANES, pl.ds(start, blk), :]
            acc = acc + cw_ref[0, j:j + 1, :] * sl
        y = _lane_norm(acc + cb, cg, cbeta)
        conv_ref[0, pl.ds(r0, blk), :] = (y * _sigmoid(y)).astype(BF16)
        return carry

    lax.fori_loop(0, rows // blk, row_body, 0)
    uext_ref[0:CONV_HALO, :] = uext_ref[rows:rows + CONV_HALO, :]


def _mixer(proj3, cos_t, sin_t, consts, ret_g, ret_b, cw, cb, cg, cbeta, batch, seq, rows=512):
    T = batch * seq
    ns = seq // rows
    cdec, decay, zeta, xi = consts

    def pspec(off):
        return pl.BlockSpec((1, rows, LANES), lambda b, h, n: (off + h, b * ns + n, 0))

    def hspec(d1):
        return pl.BlockSpec((1, d1, LANES), lambda b, h, n: (h, 0, 0))

    return pl.pallas_call(
        functools.partial(_mixer_kernel, rows=rows),
        grid=(batch, HEADS, ns),
        in_specs=[pl.BlockSpec(memory_space=pltpu.SMEM),
                  pspec(0), pspec(8), pspec(16), pspec(24), pspec(32), pspec(40),
                  pl.BlockSpec((rows, LANES), lambda b, h, n: (n, 0)),
                  pl.BlockSpec((rows, LANES), lambda b, h, n: (n, 0)),
                  hspec(CHUNK), hspec(CHUNK), hspec(CHUNK),
                  hspec(1), hspec(1), hspec(CONV_TAPS), hspec(1), hspec(1), hspec(1)],
        out_specs=[pl.BlockSpec((1, rows, LANES), lambda b, h, n: (h, b * ns + n, 0)),
                   pl.BlockSpec((1, rows, LANES), lambda b, h, n: (h, b * ns + n, 0))],
        out_shape=[jax.ShapeDtypeStruct((HEADS, T, LANES), BF16),
                   jax.ShapeDtypeStruct((CONV_GROUPS, T, LANES), BF16)],
        scratch_shapes=[pltpu.VMEM((HEAD_DIM, HEAD_DIM), F32),
                        pltpu.VMEM((rows + CONV_HALO, LANES), F32),
                        pltpu.VMEM((SUBLANES, rows + CONV_HALO, LANES), F32)],
        compiler_params=pltpu.CompilerParams(
            dimension_semantics=("arbitrary", "arbitrary", "arbitrary"), vmem_limit_bytes=VMEM_LIMIT),
        name="mixer",
    )(cdec, proj3, proj3, proj3, proj3, proj3, proj3, cos_t, sin_t, decay, zeta, xi,
      ret_g, ret_b, cw, cb, cg, cbeta)


def _layer_norm(z, g, b):
    mu = jnp.mean(z, axis=-1, keepdims=True)
    zc = z - mu
    var = jnp.mean(zc * zc, axis=-1, keepdims=True)
    return zc * lax.rsqrt(var + LN_EPS) * g + b


def _outproj_kernel(ret_ref, conv_ref, x_ref, w_ref, g_ref, b_ref, wrh_ref, wrl_ref, br_ref,
                    h_ref, comb_ref):
    mix = jnp.concatenate([ret_ref[s] for s in range(HEADS)] +
                          [conv_ref[s] for s in range(CONV_GROUPS)], axis=-1)
    y = jnp.dot(mix, w_ref[...], preferred_element_type=F32)
    h = _layer_norm(ALPHA * x_ref[...] + y, g_ref[...], b_ref[...])
    h_ref[...] = h

    hb = h.astype(BF16)
    hl = (h - hb.astype(F32)).astype(BF16)
    wrh = wrh_ref[...]
    logits = (jnp.dot(hb, wrh, preferred_element_type=F32)
              + jnp.dot(hb, wrl_ref[...], preferred_element_type=F32)
              + jnp.dot(hl, wrh, preferred_element_type=F32)) + br_ref[...]

    tm = logits.shape[0]
    lane = lax.broadcasted_iota(jnp.int32, (tm, LANES), 1)
    lane_f = lane.astype(F32)
    neg = jnp.float32(-jnp.inf)
    big = jnp.float32(1e9)
    gmask = (lane >= N_EXPERTS) & (lane < N_EXPERTS + N_GROUPS)
    gl = jnp.where(gmask, logits, neg)
    gmax = jnp.max(gl, axis=-1, keepdims=True)
    gidx = jnp.min(jnp.where(gl == gmax, lane_f, big), axis=-1, keepdims=True) - N_EXPERTS
    g_w = 1.0 / jnp.sum(jnp.where(gmask, jnp.exp(gl - gmax), 0.0), axis=-1, keepdims=True)
    lo = gidx * EXPERTS_PER_GROUP
    emask = (lane_f >= lo) & (lane_f < lo + EXPERTS_PER_GROUP)
    el = jnp.where(emask, logits, neg)
    top1 = jnp.max(el, axis=-1, keepdims=True)
    idx1 = jnp.min(jnp.where(el == top1, lane_f, big), axis=-1, keepdims=True)
    el2 = jnp.where(lane_f == idx1, neg, el)
    top2 = jnp.max(el2, axis=-1, keepdims=True)
    idx2 = jnp.min(jnp.where(el2 == top2, lane_f, big), axis=-1, keepdims=True)
    e2 = jnp.exp(top2 - top1)
    w1 = g_w / (1.0 + e2)
    w2 = w1 * e2
    comb_ref[...] = jnp.where(lane_f == idx1, w1, 0.0) + jnp.where(lane_f == idx2, w2, 0.0)


def _out_proj(ret3, conv3, x2, w_out_b, ln_g, ln_b, wr_hi, wr_lo, br, tm=512):
    T = x2.shape[0]
    row = lambda i: (i, 0)
    const = lambda i: (0, 0)
    return pl.pallas_call(
        _outproj_kernel,
        grid=(T // tm,),
        in_specs=[pl.BlockSpec((HEADS, tm, LANES), lambda i: (0, i, 0)),
                  pl.BlockSpec((CONV_GROUPS, tm, LANES), lambda i: (0, i, 0)),
                  pl.BlockSpec((tm, D_MODEL), row),
                  pl.BlockSpec((D_MODEL, D_MODEL), const),
                  pl.BlockSpec((1, D_MODEL), const),
                  pl.BlockSpec((1, D_MODEL), const),
                  pl.BlockSpec((D_MODEL, LANES), const),
                  pl.BlockSpec((D_MODEL, LANES), const),
                  pl.BlockSpec((1, LANES), const)],
        out_specs=[pl.BlockSpec((tm, D_MODEL), row),
                   pl.BlockSpec((tm, LANES), row)],
        out_shape=[jax.ShapeDtypeStruct((T, D_MODEL), F32),
                   jax.ShapeDtypeStruct((T, LANES), F32)],
        compiler_params=pltpu.CompilerParams(
            dimension_semantics=("arbitrary",), vmem_limit_bytes=VMEM_LIMIT),
        name="out_proj_ln_route",
    )(ret3, conv3, x2, w_out_b, ln_g, ln_b, wr_hi, wr_lo, br)


def _moe_kernel(h_ref, comb_ref, w1_ref, w3_ref, w2_ref, g_ref, b_ref, o_ref):
    e = pl.program_id(1)

    @pl.when(e == 0)
    def _():
        o_ref[...] = ALPHA * h_ref[...]

    hb = h_ref[...].astype(BF16)
    a1 = jnp.dot(hb, w1_ref[0], preferred_element_type=F32)
    a3 = jnp.dot(hb, w3_ref[0], preferred_element_type=F32)
    comb = comb_ref[...]
    lane = lax.broadcasted_iota(jnp.int32, comb.shape, 1)
    ce = jnp.sum(jnp.where(lane == e, comb, 0.0), axis=-1, keepdims=True)
    act = (a1 * _sigmoid(a1) * a3).astype(BF16)
    o_ref[...] += ce * jnp.dot(act, w2_ref[0], preferred_element_type=F32)

    @pl.when(e == N_EXPERTS - 1)
    def _():
        o_ref[...] = _layer_norm(o_ref[...], g_ref[...], b_ref[...])


def _moe(h, comb, w1b, w3b, w2b, ln_g, ln_b, tm=512):
    T = h.shape[0]
    row = lambda i, e: (i, 0)
    const = lambda i, e: (0, 0)
    return pl.pallas_call(
        _moe_kernel,
        grid=(T // tm, N_EXPERTS),
        in_specs=[pl.BlockSpec((tm, D_MODEL), row),
                  pl.BlockSpec((tm, LANES), row),
                  pl.BlockSpec((1, D_MODEL, D_FF), lambda i, e: (e, 0, 0)),
                  pl.BlockSpec((1, D_MODEL, D_FF), lambda i, e: (e, 0, 0)),
                  pl.BlockSpec((1, D_FF, D_MODEL), lambda i, e: (e, 0, 0)),
                  pl.BlockSpec((1, D_MODEL), const),
                  pl.BlockSpec((1, D_MODEL), const)],
        out_specs=pl.BlockSpec((tm, D_MODEL), row),
        out_shape=jax.ShapeDtypeStruct((T, D_MODEL), F32),
        compiler_params=pltpu.CompilerParams(
            dimension_semantics=("arbitrary", "arbitrary"), vmem_limit_bytes=VMEM_LIMIT),
        name="moe_ln",
    )(h, comb, w1b, w3b, w2b, ln_g, ln_b)


def _retention_constants():
    hh = np.arange(HEADS, dtype=np.float64)
    log_gamma = np.log1p(-np.exp2(-5.0 - hh))
    pos = np.arange(CHUNK, dtype=np.float64)
    diff = pos[:, None] - pos[None, :]
    causal = diff >= 0
    decay = np.where(causal[None], np.exp(log_gamma[:, None, None] * np.where(causal, diff, 0.0)[None]), 0.0)
    zeta = np.exp(log_gamma[:, None] * (CHUNK - 1 - pos)[None])
    xi = np.exp(log_gamma[:, None] * (pos + 1.0)[None])
    cdec = np.exp(log_gamma * CHUNK)
    bc = lambda t: np.broadcast_to(t[:, :, None], (HEADS, CHUNK, LANES))
    return (jnp.asarray(cdec, F32), jnp.asarray(decay, F32),
            jnp.asarray(bc(zeta), F32), jnp.asarray(bc(xi), F32))


def _rope_tables(seq):
    inv = ROPE_BASE ** (-jnp.arange(0, HEAD_DIM, 2, dtype=F32) / HEAD_DIM)
    ang = jnp.arange(seq, dtype=F32)[:, None] * inv[None, :]
    cos = jnp.repeat(jnp.cos(ang), 2, axis=-1)
    sin = jnp.sin(ang)
    sin_signed = jnp.stack([-sin, sin], axis=-1).reshape(seq, HEAD_DIM)
    return cos, sin_signed


def kernel(x, w_in, ret_norm_g, ret_norm_b, conv_w, conv_b, conv_norm_g, conv_norm_b, w_out, ln1_g, ln1_b, w_router_group, b_router_group, w_router_expert, b_router_expert, w1, w3, w2, ln2_g, ln2_b):
    batch, seq, _ = x.shape
    T = batch * seq
    x2 = x.reshape(T, D_MODEL)

    proj3 = _in_proj(x2, w_in.astype(BF16))

    cos_t, sin_t = _rope_tables(seq)
    per_head = lambda t: t.reshape(HEADS, 1, LANES)
    cw = jnp.transpose(conv_w.reshape(CONV_TAPS, CONV_GROUPS, LANES), (1, 0, 2))
    ret3, conv3 = _mixer(proj3, cos_t, sin_t, _retention_constants(),
                         per_head(ret_norm_g), per_head(ret_norm_b), cw,
                         per_head(conv_b), per_head(conv_norm_g), per_head(conv_norm_b), batch, seq)

    wr = jnp.concatenate([jnp.transpose(w_router_expert, (1, 0, 2)).reshape(D_MODEL, N_EXPERTS),
                          w_router_group], axis=1)
    wr = jnp.pad(wr, ((0, 0), (0, LANES - wr.shape[1])))
    br = jnp.pad(jnp.concatenate([b_router_expert.reshape(-1), b_router_group]),
                 (0, LANES - N_EXPERTS - N_GROUPS)).reshape(1, LANES)
    wr_hi = wr.astype(BF16)
    wr_lo = (wr - wr_hi.astype(F32)).astype(BF16)

    h, comb = _out_proj(ret3, conv3, x2, w_out.astype(BF16), ln1_g.reshape(1, -1), ln1_b.reshape(1, -1),
                        wr_hi, wr_lo, br)

    y = _moe(h, comb, w1.astype(BF16), w3.astype(BF16), w2.astype(BF16),
             ln2_g.reshape(1, -1), ln2_b.reshape(1, -1))
    return y.reshape(batch, seq, D_MODEL)
```

```python
import functools
import math

import numpy as np
import jax
import jax.numpy as jnp
from jax import lax
from jax.experimental import pallas as pl
from jax.experimental.pallas import tpu as pltpu

D_MODEL = 2048
RET_WIDTH = D_MODEL // 2
CONV_WIDTH = D_MODEL - RET_WIDTH
HEADS = 8
HEAD_DIM = RET_WIDTH // HEADS
CONV_GROUPS = 8
CONV_TAPS = 31
CHUNK = 128
ROPE_BASE = 10000.0
IN_COLS = 4 * RET_WIDTH + 2 * CONV_WIDTH
N_GROUPS = 4
EXPERTS_PER_GROUP = 4
N_EXPERTS = N_GROUPS * EXPERTS_PER_GROUP
TOP_K = 2
D_FF = D_MODEL // 2
LN_EPS = 1e-5
ALPHA = 2.0 ** 0.25

LANES = 128
SUBLANES = 8
VMEM_LIMIT = 56 * 1024 * 1024
CONV_HALO = 32

F32 = jnp.float32
BF16 = jnp.bfloat16


def _sigmoid(x):
    return 1.0 / (1.0 + jnp.exp(-x))


def _lane_norm(x, g, b):
    mu = jnp.mean(x, axis=-1, keepdims=True)
    xc = x - mu
    var = jnp.mean(xc * xc, axis=-1, keepdims=True)
    return xc * lax.rsqrt(var + LN_EPS) * g + b


def _inproj_kernel(x_ref, w_ref, o_ref):
    res = jnp.dot(x_ref[...].astype(BF16), w_ref[...], preferred_element_type=F32)
    for s in range(o_ref.shape[0]):
        o_ref[s] = res[:, s * LANES:(s + 1) * LANES].astype(BF16)


def _in_proj(x2, w_in_b, tm=512, tn=2048):
    T = x2.shape[0]
    nblk = tn // LANES
    return pl.pallas_call(
        _inproj_kernel,
        grid=(IN_COLS // tn, T // tm),
        in_specs=[pl.BlockSpec((tm, D_MODEL), lambda j, i: (i, 0)),
                  pl.BlockSpec((D_MODEL, tn), lambda j, i: (0, j))],
        out_specs=pl.BlockSpec((nblk, tm, LANES), lambda j, i: (j, i, 0)),
        out_shape=jax.ShapeDtypeStruct((IN_COLS // LANES, T, LANES), BF16),
        compiler_params=pltpu.CompilerParams(
            dimension_semantics=("arbitrary", "arbitrary"), vmem_limit_bytes=VMEM_LIMIT),
        name="in_proj",
    )(x2, w_in_b)


def _rotary(t, cos, sin_signed, even_lane):
    partner = jnp.where(even_lane, pltpu.roll(t, LANES - 1, 1), pltpu.roll(t, 1, 1))
    return t * cos + partner * sin_signed


def _mixer_kernel(cdec_ref, q_ref, k_ref, v_ref, gate_ref, ga_ref, gb_ref, cos_ref, sin_ref,
                  decay_ref, zeta_ref, xi_ref, rg_ref, rb_ref, cw_ref, cb_ref, cg_ref, cbeta_ref,
                  ret_ref, conv_ref, state_ref, uext_ref, shift_ref, *, rows):
    h = pl.program_id(1)
    n = pl.program_id(2)

    @pl.when(n == 0)
    def _():
        state_ref[...] = jnp.zeros_like(state_ref)
        uext_ref[0:CONV_HALO, :] = jnp.zeros((CONV_HALO, LANES), F32)

    even_lane = (lax.broadcasted_iota(jnp.int32, (CHUNK, LANES), 1) % 2) == 0
    cdec = cdec_ref[h]
    decay = decay_ref[0]
    zeta = zeta_ref[0]
    xi = xi_ref[0]
    rg = rg_ref[0]
    rb = rb_ref[0]
    scale = HEAD_DIM ** -0.5

    def chunk_body(c, carry):
        r0 = pl.multiple_of(c * CHUNK, CHUNK)
        cos = cos_ref[pl.ds(r0, CHUNK), :]
        sin = sin_ref[pl.ds(r0, CHUNK), :]
        q = _rotary(q_ref[0, pl.ds(r0, CHUNK), :].astype(F32), cos, sin, even_lane)
        k = _rotary(k_ref[0, pl.ds(r0, CHUNK), :].astype(F32), cos, sin, even_lane) * scale
        v = v_ref[0, pl.ds(r0, CHUNK), :]
        scores = lax.dot_general(q.astype(BF16), k.astype(BF16), (((1,), (1,)), ((), ())),
                                 preferred_element_type=F32) * decay
        inner = jnp.dot(scores.astype(BF16), v, preferred_element_type=F32)
        prev = state_ref[...]
        cross = jnp.dot((q * xi).astype(BF16), prev.astype(BF16), preferred_element_type=F32)
        kz_t = jnp.transpose(k * zeta).astype(BF16)
        kv = jnp.dot(kz_t, v, preferred_element_type=F32)
        state_ref[...] = cdec * prev + kv
        y = _lane_norm(inner + cross, rg, rb)
        g = gate_ref[0, pl.ds(r0, CHUNK), :].astype(F32)
        ret_ref[0, pl.ds(r0, CHUNK), :] = (g * _sigmoid(g) * y).astype(BF16)
        return carry

    lax.fori_loop(0, rows // CHUNK, chunk_body, 0)

    a = ga_ref[0].astype(F32)
    b = gb_ref[0].astype(F32)
    uext_ref[CONV_HALO:CONV_HALO + rows, :] = a * _sigmoid(b)
    span = rows + CONV_HALO - SUBLANES
    for s in range(1, SUBLANES):
        shift_ref[s, 0:span, :] = uext_ref[s:s + span, :]

    cb = cb_ref[0]
    cg = cg_ref[0]
    cbeta = cbeta_ref[0]
    blk = 64
    first = CONV_HALO - (CONV_TAPS - 1)

    def row_body(rbk, carry):
        r0 = pl.multiple_of(rbk * blk, blk)
        acc = jnp.zeros((blk, LANES), F32)
        for j in range(CONV_TAPS):
            off = first + j
            start = r0 + (off // SUBLANES) * SUBLANES
            if off % SUBLANES == 0:
                sl = uext_ref[pl.ds(start, blk), :]
            else:
                sl = shift_ref[off % SUBLANES, pl.ds(start, blk), :]
            acc = acc + cw_ref[0, j:j + 1, :] * sl
        y = _lane_norm(acc + cb, cg, cbeta)
        conv_ref[0, pl.ds(r0, blk), :] = (y * _sigmoid(y)).astype(BF16)
        return carry

    lax.fori_loop(0, rows // blk, row_body, 0)
    uext_ref[0:CONV_HALO, :] = uext_ref[rows:rows + CONV_HALO, :]


def _mixer(proj3, cos_t, sin_t, consts, ret_g, ret_b, cw, cb, cg, cbeta, batch, seq, rows=512):
    T = batch * seq
    ns = seq // rows
    cdec, decay, zeta, xi = consts

    def pspec(off):
        return pl.BlockSpec((1, rows, LANES), lambda b, h, n: (off + h, b * ns + n, 0))

    def hspec(d1):
        return pl.BlockSpec((1, d1, LANES), lambda b, h, n: (h, 0, 0))

    return pl.pallas_call(
        functools.partial(_mixer_kernel, rows=rows),
        grid=(batch, HEADS, ns),
        in_specs=[pl.BlockSpec(memory_space=pltpu.SMEM),
                  pspec(0), pspec(8), pspec(16), pspec(24), pspec(32), pspec(40),
                  pl.BlockSpec((rows, LANES), lambda b, h, n: (n, 0)),
                  pl.BlockSpec((rows, LANES), lambda b, h, n: (n, 0)),
                  hspec(CHUNK), hspec(CHUNK), hspec(CHUNK),
                  hspec(1), hspec(1), hspec(CONV_TAPS), hspec(1), hspec(1), hspec(1)],
        out_specs=[pl.BlockSpec((1, rows, LANES), lambda b, h, n: (h, b * ns + n, 0)),
                   pl.BlockSpec((1, rows, LANES), lambda b, h, n: (h, b * ns + n, 0))],
        out_shape=[jax.ShapeDtypeStruct((HEADS, T, LANES), BF16),
                   jax.ShapeDtypeStruct((CONV_GROUPS, T, LANES), BF16)],
        scratch_shapes=[pltpu.VMEM((HEAD_DIM, HEAD_DIM), F32),
                        pltpu.VMEM((rows + CONV_HALO, LANES), F32),
                        pltpu.VMEM((SUBLANES, rows + CONV_HALO, LANES), F32)],
        compiler_params=pltpu.CompilerParams(
            dimension_semantics=("arbitrary", "arbitrary", "arbitrary"), vmem_limit_bytes=VMEM_LIMIT),
        name="mixer",
    )(cdec, proj3, proj3, proj3, proj3, proj3, proj3, cos_t, sin_t, decay, zeta, xi,
      ret_g, ret_b, cw, cb, cg, cbeta)


def _layer_norm(z, g, b):
    mu = jnp.mean(z, axis=-1, keepdims=True)
    zc = z - mu
    var = jnp.mean(zc * zc, axis=-1, keepdims=True)
    return zc * lax.rsqrt(var + LN_EPS) * g + b


def _outproj_kernel(ret_ref, conv_ref, x_ref, w_ref, g_ref, b_ref, wrh_ref, wrl_ref, br_ref,
                    h_ref, route_ref, counts_ref, carry_ref):
    @pl.when(pl.program_id(0) == 0)
    def _():
        carry_ref[...] = jnp.zeros_like(carry_ref)

    mix = jnp.concatenate([ret_ref[s] for s in range(HEADS)] +
                          [conv_ref[s] for s in range(CONV_GROUPS)], axis=-1)
    y = jnp.dot(mix, w_ref[...], preferred_element_type=F32)
    h = _layer_norm(ALPHA * x_ref[...] + y, g_ref[...], b_ref[...])
    h_ref[...] = h

    hb = h.astype(BF16)
    hl = (h - hb.astype(F32)).astype(BF16)
    wrh = wrh_ref[...]
    logits = (jnp.dot(hb, wrh, preferred_element_type=F32)
              + jnp.dot(hb, wrl_ref[...], preferred_element_type=F32)
              + jnp.dot(hl, wrh, preferred_element_type=F32)) + br_ref[...]

    tm = logits.shape[0]
    lane = lax.broadcasted_iota(jnp.int32, (tm, LANES), 1)
    lane_f = lane.astype(F32)
    neg = jnp.float32(-jnp.inf)
    big = jnp.float32(1e9)
    gmask = (lane >= N_EXPERTS) & (lane < N_EXPERTS + N_GROUPS)
    gl = jnp.where(gmask, logits, neg)
    gmax = jnp.max(gl, axis=-1, keepdims=True)
    gidx = jnp.min(jnp.where(gl == gmax, lane_f, big), axis=-1, keepdims=True) - N_EXPERTS
    g_w = 1.0 / jnp.sum(jnp.where(gmask, jnp.exp(gl - gmax), 0.0), axis=-1, keepdims=True)
    lo = gidx * EXPERTS_PER_GROUP
    emask = (lane_f >= lo) & (lane_f < lo + EXPERTS_PER_GROUP)
    el = jnp.where(emask, logits, neg)
    top1 = jnp.max(el, axis=-1, keepdims=True)
    idx1 = jnp.min(jnp.where(el == top1, lane_f, big), axis=-1, keepdims=True)
    el2 = jnp.where(lane_f == idx1, neg, el)
    top2 = jnp.max(el2, axis=-1, keepdims=True)
    idx2 = jnp.min(jnp.where(el2 == top2, lane_f, big), axis=-1, keepdims=True)
    e2 = jnp.exp(top2 - top1)
    w1 = g_w / (1.0 + e2)
    w2 = w1 * e2

    sel1 = lane_f == idx1
    sel2 = lane_f == idx2
    onehot = jnp.where(sel1, 1.0, 0.0) + jnp.where(sel2, 1.0, 0.0)
    r_i = lax.broadcasted_iota(jnp.int32, (tm, tm), 0)
    c_i = lax.broadcasted_iota(jnp.int32, (tm, tm), 1)
    tri = jnp.where(c_i < r_i, 1.0, 0.0).astype(BF16)
    prefix = jnp.dot(tri, onehot.astype(BF16), preferred_element_type=F32) + carry_ref[0:1, :]
    rank1 = jnp.sum(jnp.where(sel1, prefix, 0.0), axis=-1, keepdims=True)
    rank2 = jnp.sum(jnp.where(sel2, prefix, 0.0), axis=-1, keepdims=True)
    total = carry_ref[0:1, :] + jnp.sum(onehot, axis=0, keepdims=True)
    carry_ref[...] = jnp.broadcast_to(total, carry_ref.shape)
    counts_ref[...] = jnp.broadcast_to(total, counts_ref.shape)

    route = jnp.where(lane == 0, w1, 0.0)
    for k, val in enumerate((w2, idx1, idx2, rank1, rank2), start=1):
        route = jnp.where(lane == k, val, route)
    route_ref[...] = route


def _out_proj(ret3, conv3, x2, w_out_b, ln_g, ln_b, wr_hi, wr_lo, br, tm=512):
    T = x2.shape[0]
    row = lambda i: (i, 0)
    const = lambda i: (0, 0)
    return pl.pallas_call(
        _outproj_kernel,
        grid=(T // tm,),
        in_specs=[pl.BlockSpec((HEADS, tm, LANES), lambda i: (0, i, 0)),
                  pl.BlockSpec((CONV_GROUPS, tm, LANES), lambda i: (0, i, 0)),
                  pl.BlockSpec((tm, D_MODEL), row),
                  pl.BlockSpec((D_MODEL, D_MODEL), const),
                  pl.BlockSpec((1, D_MODEL), const),
                  pl.BlockSpec((1, D_MODEL), const),
                  pl.BlockSpec((D_MODEL, LANES), const),
                  pl.BlockSpec((D_MODEL, LANES), const),
                  pl.BlockSpec((1, LANES), const)],
        out_specs=[pl.BlockSpec((tm, D_MODEL), row),
                   pl.BlockSpec((tm, LANES), row),
                   pl.BlockSpec((SUBLANES, LANES), const)],
        out_shape=[jax.ShapeDtypeStruct((T, D_MODEL), F32),
                   jax.ShapeDtypeStruct((T, LANES), F32),
                   jax.ShapeDtypeStruct((SUBLANES, LANES), F32)],
        scratch_shapes=[pltpu.VMEM((SUBLANES, LANES), F32)],
        compiler_params=pltpu.CompilerParams(
            dimension_semantics=("arbitrary",), vmem_limit_bytes=VMEM_LIMIT),
        name="out_proj_ln_route",
    )(ret3, conv3, x2, w_out_b, ln_g, ln_b, wr_hi, wr_lo, br)


MOE_TM = 512
MOE_TB = 512


def _row_copy(src_ref, src_row, dst_ref, dst_row, sem):
    return pltpu.make_async_copy(src_ref.at[pl.ds(src_row, 1), :], dst_ref.at[pl.ds(dst_row, 1), :], sem)


def _dispatch_kernel(meta_ref, pos_ref, h_ref, xs_ref, zero_ref, sems):
    tb = h_ref.shape[0]

    def zero_copy(j):
        return pltpu.make_async_copy(
            zero_ref, xs_ref.at[pl.ds(pl.multiple_of(meta_ref[j], MOE_TM), MOE_TM), :], sems.at[1])

    @pl.when(pl.program_id(0) == 0)
    def _():
        zero_ref[...] = jnp.zeros_like(zero_ref)
        for j in range(2 * N_EXPERTS):
            @pl.when(meta_ref[2 * N_EXPERTS + j] > 0)
            def _():
                zero_copy(j).start()
        for j in range(2 * N_EXPERTS):
            @pl.when(meta_ref[2 * N_EXPERTS + j] > 0)
            def _():
                zero_copy(j).wait()

    def issue(t, carry):
        _row_copy(h_ref, t, xs_ref, pos_ref[0, 0, t], sems.at[0]).start()
        _row_copy(h_ref, t, xs_ref, pos_ref[0, 0, tb + t], sems.at[0]).start()
        return carry

    lax.fori_loop(0, tb, issue, 0)

    def drain(t, carry):
        _row_copy(h_ref, t, xs_ref, pos_ref[0, 0, t], sems.at[0]).wait()
        _row_copy(h_ref, t, xs_ref, pos_ref[0, 0, tb + t], sems.at[0]).wait()
        return carry

    lax.fori_loop(0, tb, drain, 0)


def _dispatch(meta, pos3, h, n_rows):
    T = h.shape[0]
    tb = MOE_TB
    return pl.pallas_call(
        _dispatch_kernel,
        grid_spec=pltpu.PrefetchScalarGridSpec(
            num_scalar_prefetch=1,
            grid=(T // tb,),
            in_specs=[pl.BlockSpec((1, 1, 2 * tb), lambda i, m: (i, 0, 0), memory_space=pltpu.SMEM),
                      pl.BlockSpec((tb, D_MODEL), lambda i, m: (i, 0))],
            out_specs=pl.BlockSpec(memory_space=pl.ANY),
            scratch_shapes=[pltpu.VMEM((MOE_TM, D_MODEL), F32),
                            pltpu.SemaphoreType.DMA((2,))]),
        out_shape=jax.ShapeDtypeStruct((n_rows, D_MODEL), F32),
        compiler_params=pltpu.CompilerParams(
            dimension_semantics=("arbitrary",), vmem_limit_bytes=VMEM_LIMIT),
        name="moe_dispatch",
    )(meta, pos3, h)


def _experts_kernel(te_ref, nv_ref, x_ref, w1_ref, w3_ref, w2_ref, y_ref):
    used = pl.program_id(0) < nv_ref[0]

    @pl.when(used)
    def _():
        xb = x_ref[...].astype(BF16)
        a1 = jnp.dot(xb, w1_ref[0], preferred_element_type=F32)
        a3 = jnp.dot(xb, w3_ref[0], preferred_element_type=F32)
        act = (a1 * _sigmoid(a1) * a3).astype(BF16)
        y_ref[...] = jnp.dot(act, w2_ref[0], preferred_element_type=F32)

    @pl.when(jnp.logical_not(used))
    def _():
        y_ref[...] = jnp.zeros_like(y_ref)


def _experts(tile_expert, n_valid, xs, w1b, w3b, w2b):
    n_tiles = xs.shape[0] // MOE_TM
    row_in = lambda i, te, nv: (jnp.minimum(i, nv[0] - 1), 0)
    row = lambda i, te, nv: (i, 0)
    wsel = lambda i, te, nv: (te[i], 0, 0)
    return pl.pallas_call(
        _experts_kernel,
        grid_spec=pltpu.PrefetchScalarGridSpec(
            num_scalar_prefetch=2,
            grid=(n_tiles,),
            in_specs=[pl.BlockSpec((MOE_TM, D_MODEL), row_in),
                      pl.BlockSpec((1, D_MODEL, D_FF), wsel),
                      pl.BlockSpec((1, D_MODEL, D_FF), wsel),
                      pl.BlockSpec((1, D_FF, D_MODEL), wsel)],
            out_specs=pl.BlockSpec((MOE_TM, D_MODEL), row)),
        out_shape=jax.ShapeDtypeStruct(xs.shape, F32),
        compiler_params=pltpu.CompilerParams(
            dimension_semantics=("arbitrary",), vmem_limit_bytes=VMEM_LIMIT),
        name="moe_experts",
    )(tile_expert, n_valid, xs, w1b, w3b, w2b)


def _combine_kernel(pos_ref, route_ref, h_ref, g_ref, b_ref, ys_ref, o_ref, y1_ref, y2_ref, sems):
    tb = h_ref.shape[0]

    def issue(t, carry):
        _row_copy(ys_ref, pos_ref[0, 0, t], y1_ref, t, sems.at[0]).start()
        _row_copy(ys_ref, pos_ref[0, 0, tb + t], y2_ref, t, sems.at[0]).start()
        return carry

    lax.fori_loop(0, tb, issue, 0)

    def drain(t, carry):
        _row_copy(ys_ref, pos_ref[0, 0, t], y1_ref, t, sems.at[0]).wait()
        _row_copy(ys_ref, pos_ref[0, 0, tb + t], y2_ref, t, sems.at[0]).wait()
        return carry

    lax.fori_loop(0, tb, drain, 0)

    route = route_ref[...]
    ffn = route[:, 0:1] * y1_ref[...] + route[:, 1:2] * y2_ref[...]
    o_ref[...] = _layer_norm(ALPHA * h_ref[...] + ffn, g_ref[...], b_ref[...])


def _combine(pos3, route, h, ln_g, ln_b, ys):
    T = h.shape[0]
    tb = MOE_TB
    row = lambda i: (i, 0)
    const = lambda i: (0, 0)
    return pl.pallas_call(
        _combine_kernel,
        grid=(T // tb,),
        in_specs=[pl.BlockSpec((1, 1, 2 * tb), lambda i: (i, 0, 0), memory_space=pltpu.SMEM),
                  pl.BlockSpec((tb, LANES), row),
                  pl.BlockSpec((tb, D_MODEL), row),
                  pl.BlockSpec((1, D_MODEL), const),
                  pl.BlockSpec((1, D_MODEL), const),
                  pl.BlockSpec(memory_space=pl.ANY)],
        out_specs=pl.BlockSpec((tb, D_MODEL), row),
        out_shape=jax.ShapeDtypeStruct((T, D_MODEL), F32),
        scratch_shapes=[pltpu.VMEM((tb, D_MODEL), F32),
                        pltpu.VMEM((tb, D_MODEL), F32),
                        pltpu.SemaphoreType.DMA((1,))],
        compiler_params=pltpu.CompilerParams(
            dimension_semantics=("arbitrary",), vmem_limit_bytes=VMEM_LIMIT),
        name="moe_combine_ln",
    )(pos3, route, h, ln_g, ln_b, ys)


def _routed_moe(h, route, counts, w1b, w3b, w2b, ln_g, ln_b):
    T = h.shape[0]
    n_tiles = (TOP_K * T) // MOE_TM + N_EXPERTS
    i32 = jnp.int32
    cnt = counts[0, :N_EXPERTS].astype(i32)
    ntile = (cnt + MOE_TM - 1) // MOE_TM
    tile_end = jnp.cumsum(ntile)
    n_valid = tile_end[-1:]
    seg_start = (tile_end - ntile) * MOE_TM
    pos1 = jnp.take(seg_start, route[:, 2].astype(i32)) + route[:, 4].astype(i32)
    pos2 = jnp.take(seg_start, route[:, 3].astype(i32)) + route[:, 5].astype(i32)
    pos3 = jnp.concatenate([pos1.reshape(T // MOE_TB, 1, MOE_TB), pos2.reshape(T // MOE_TB, 1, MOE_TB)], axis=2)
    tile_ids = jnp.minimum(jnp.arange(n_tiles, dtype=i32), n_valid[0] - 1)
    tile_expert = jnp.searchsorted(tile_end, tile_ids, side="right").astype(i32)
    tail = n_valid[0] + jnp.arange(N_EXPERTS, dtype=i32)
    meta = jnp.concatenate([(tile_end - 1) * MOE_TM, tail * MOE_TM,
                            ntile, (tail < n_tiles).astype(i32)]).astype(i32)

    xs = _dispatch(meta, pos3, h, n_tiles * MOE_TM)
    ys = _experts(tile_expert, n_valid.astype(i32), xs, w1b, w3b, w2b)
    return _combine(pos3, route, h, ln_g, ln_b, ys)


def _retention_constants():
    hh = np.arange(HEADS, dtype=np.float64)
    log_gamma = np.log1p(-np.exp2(-5.0 - hh))
    pos = np.arange(CHUNK, dtype=np.float64)
    diff = pos[:, None] - pos[None, :]
    causal = diff >= 0
    decay = np.where(causal[None], np.exp(log_gamma[:, None, None] * np.where(causal, diff, 0.0)[None]), 0.0)
    zeta = np.exp(log_gamma[:, None] * (CHUNK - 1 - pos)[None])
    xi = np.exp(log_gamma[:, None] * (pos + 1.0)[None])
    cdec = np.exp(log_gamma * CHUNK)
    bc = lambda t: np.broadcast_to(t[:, :, None], (HEADS, CHUNK, LANES))
    return (jnp.asarray(cdec, F32), jnp.asarray(decay, F32),
            jnp.asarray(bc(zeta), F32), jnp.asarray(bc(xi), F32))


def _rope_tables(seq):
    inv = ROPE_BASE ** (-jnp.arange(0, HEAD_DIM, 2, dtype=F32) / HEAD_DIM)
    ang = jnp.arange(seq, dtype=F32)[:, None] * inv[None, :]
    cos = jnp.repeat(jnp.cos(ang), 2, axis=-1)
    sin = jnp.sin(ang)
    sin_signed = jnp.stack([-sin, sin], axis=-1).reshape(seq, HEAD_DIM)
    return cos, sin_signed


def kernel(x, w_in, ret_norm_g, ret_norm_b, conv_w, conv_b, conv_norm_g, conv_norm_b, w_out, ln1_g, ln1_b, w_router_group, b_router_group, w_router_expert, b_router_expert, w1, w3, w2, ln2_g, ln2_b):
    batch, seq, _ = x.shape
    T = batch * seq
    x2 = x.reshape(T, D_MODEL)

    proj3 = _in_proj(x2, w_in.astype(BF16))

    cos_t, sin_t = _rope_tables(seq)
    per_head = lambda t: t.reshape(HEADS, 1, LANES)
    cw = jnp.transpose(conv_w.reshape(CONV_TAPS, CONV_GROUPS, LANES), (1, 0, 2))
    ret3, conv3 = _mixer(proj3, cos_t, sin_t, _retention_constants(),
                         per_head(ret_norm_g), per_head(ret_norm_b), cw,
                         per_head(conv_b), per_head(conv_norm_g), per_head(conv_norm_b), batch, seq)

    wr = jnp.concatenate([jnp.transpose(w_router_expert, (1, 0, 2)).reshape(D_MODEL, N_EXPERTS),
                          w_router_group], axis=1)
    wr = jnp.pad(wr, ((0, 0), (0, LANES - wr.shape[1])))
    br = jnp.pad(jnp.concatenate([b_router_expert.reshape(-1), b_router_group]),
                 (0, LANES - N_EXPERTS - N_GROUPS)).reshape(1, LANES)
    wr_hi = wr.astype(BF16)
    wr_lo = (wr - wr_hi.astype(F32)).astype(BF16)

    h, route, counts = _out_proj(ret3, conv3, x2, w_out.astype(BF16), ln1_g.reshape(1, -1),
                                 ln1_b.reshape(1, -1), wr_hi, wr_lo, br)

    y = _routed_moe(h, route, counts, w1.astype(BF16), w3.astype(BF16), w2.astype(BF16),
                    ln2_g.reshape(1, -1), ln2_b.reshape(1, -1))
    return y.reshape(batch, seq, D_MODEL)
```

```python
import functools
import math

import numpy as np
import jax
import jax.numpy as jnp
from jax import lax
from jax.experimental import pallas as pl
from jax.experimental.pallas import tpu as pltpu

D_MODEL = 2048
RET_WIDTH = D_MODEL // 2
CONV_WIDTH = D_MODEL - RET_WIDTH
HEADS = 8
HEAD_DIM = RET_WIDTH // HEADS
CONV_GROUPS = 8
CONV_TAPS = 31
CHUNK = 128
ROPE_BASE = 10000.0
IN_COLS = 4 * RET_WIDTH + 2 * CONV_WIDTH
N_GROUPS = 4
EXPERTS_PER_GROUP = 4
N_EXPERTS = N_GROUPS * EXPERTS_PER_GROUP
TOP_K = 2
D_FF = D_MODEL // 2
LN_EPS = 1e-5
ALPHA = 2.0 ** 0.25

LANES = 128
SUBLANES = 8
VMEM_LIMIT = 56 * 1024 * 1024
CONV_HALO = 32

F32 = jnp.float32
BF16 = jnp.bfloat16


def _sigmoid(x):
    return 1.0 / (1.0 + jnp.exp(-x))


def _lane_norm(x, g, b):
    mu = jnp.mean(x, axis=-1, keepdims=True)
    xc = x - mu
    var = jnp.mean(xc * xc, axis=-1, keepdims=True)
    return xc * lax.rsqrt(var + LN_EPS) * g + b


def _inproj_kernel(x_ref, w_ref, o_ref):
    res = jnp.dot(x_ref[...].astype(BF16), w_ref[...], preferred_element_type=F32)
    for s in range(o_ref.shape[0]):
        o_ref[s] = res[:, s * LANES:(s + 1) * LANES].astype(BF16)


def _in_proj(x2, w_in_b, tm=512, tn=2048):
    T = x2.shape[0]
    nblk = tn // LANES
    return pl.pallas_call(
        _inproj_kernel,
        grid=(IN_COLS // tn, T // tm),
        in_specs=[pl.BlockSpec((tm, D_MODEL), lambda j, i: (i, 0)),
                  pl.BlockSpec((D_MODEL, tn), lambda j, i: (0, j))],
        out_specs=pl.BlockSpec((nblk, tm, LANES), lambda j, i: (j, i, 0)),
        out_shape=jax.ShapeDtypeStruct((IN_COLS // LANES, T, LANES), BF16),
        compiler_params=pltpu.CompilerParams(
            dimension_semantics=("arbitrary", "arbitrary"), vmem_limit_bytes=VMEM_LIMIT),
        name="in_proj",
    )(x2, w_in_b)


def _rotary(t, cos, sin_signed, even_lane):
    partner = jnp.where(even_lane, pltpu.roll(t, LANES - 1, 1), pltpu.roll(t, 1, 1))
    return t * cos + partner * sin_signed


def _mixer_kernel(cdec_ref, q_ref, k_ref, v_ref, gate_ref, ga_ref, gb_ref, cos_ref, sin_ref,
                  decay_ref, zeta_ref, xi_ref, rg_ref, rb_ref, cw_ref, cb_ref, cg_ref, cbeta_ref,
                  ret_ref, conv_ref, state_ref, uext_ref, *, rows):
    h = pl.program_id(1)
    n = pl.program_id(2)

    @pl.when(n == 0)
    def _():
        state_ref[...] = jnp.zeros_like(state_ref)
        uext_ref[0:CONV_HALO, :] = jnp.zeros((CONV_HALO, LANES), F32)

    even_lane = (lax.broadcasted_iota(jnp.int32, (CHUNK, LANES), 1) % 2) == 0
    cdec = cdec_ref[h]
    decay = decay_ref[0]
    zeta = zeta_ref[0]
    xi = xi_ref[0]
    rg = rg_ref[0]
    rb = rb_ref[0]
    scale = HEAD_DIM ** -0.5

    state = state_ref[...]
    for c in range(rows // CHUNK):
        rs = slice(c * CHUNK, (c + 1) * CHUNK)
        cos = cos_ref[rs, :]
        sin = sin_ref[rs, :]
        q = _rotary(q_ref[0, rs, :].astype(F32), cos, sin, even_lane)
        k = _rotary(k_ref[0, rs, :].astype(F32), cos, sin, even_lane) * scale
        v = v_ref[0, rs, :]
        scores = lax.dot_general(q.astype(BF16), k.astype(BF16), (((1,), (1,)), ((), ())),
                                 preferred_element_type=F32) * decay
        inner = jnp.dot(scores.astype(BF16), v, preferred_element_type=F32)
        cross = jnp.dot((q * xi).astype(BF16), state.astype(BF16), preferred_element_type=F32)
        kz_t = jnp.transpose(k * zeta).astype(BF16)
        state = cdec * state + jnp.dot(kz_t, v, preferred_element_type=F32)
        y = _lane_norm(inner + cross, rg, rb)
        g = gate_ref[0, rs, :].astype(F32)
        ret_ref[0, rs, :] = (g * _sigmoid(g) * y).astype(BF16)
    state_ref[...] = state

    a = ga_ref[0].astype(F32)
    b = gb_ref[0].astype(F32)
    uext_ref[CONV_HALO:CONV_HALO + rows, :] = a * _sigmoid(b)

    cb = cb_ref[0]
    cg = cg_ref[0]
    cbeta = cbeta_ref[0]
    blk = 64
    first = CONV_HALO - (CONV_TAPS - 1)
    for r0 in range(0, rows, blk):
        acc = jnp.zeros((blk, LANES), F32)
        for j in range(CONV_TAPS):
            acc = acc + cw_ref[0, j:j + 1, :] * uext_ref[r0 + first + j:r0 + first + j + blk, :]
        y = _lane_norm(acc + cb, cg, cbeta)
        conv_ref[0, r0:r0 + blk, :] = (y * _sigmoid(y)).astype(BF16)
    uext_ref[0:CONV_HALO, :] = uext_ref[rows:rows + CONV_HALO, :]


def _mixer(proj3, cos_t, sin_t, consts, ret_g, ret_b, cw, cb, cg, cbeta, batch, seq, rows=512):
    T = batch * seq
    ns = seq // rows
    cdec, decay, zeta, xi = consts

    def pspec(off):
        return pl.BlockSpec((1, rows, LANES), lambda b, h, n: (off + h, b * ns + n, 0))

    def hspec(d1):
        return pl.BlockSpec((1, d1, LANES), lambda b, h, n: (h, 0, 0))

    return pl.pallas_call(
        functools.partial(_mixer_kernel, rows=rows),
        grid=(batch, HEADS, ns),
        in_specs=[pl.BlockSpec(memory_space=pltpu.SMEM),
                  pspec(0), pspec(8), pspec(16), pspec(24), pspec(32), pspec(40),
                  pl.BlockSpec((rows, LANES), lambda b, h, n: (n, 0)),
                  pl.BlockSpec((rows, LANES), lambda b, h, n: (n, 0)),
                  hspec(CHUNK), hspec(CHUNK), hspec(CHUNK),
                  hspec(1), hspec(1), hspec(CONV_TAPS), hspec(1), hspec(1), hspec(1)],
        out_specs=[pl.BlockSpec((1, rows, LANES), lambda b, h, n: (h, b * ns + n, 0)),
                   pl.BlockSpec((1, rows, LANES), lambda b, h, n: (h, b * ns + n, 0))],
        out_shape=[jax.ShapeDtypeStruct((HEADS, T, LANES), BF16),
                   jax.ShapeDtypeStruct((CONV_GROUPS, T, LANES), BF16)],
        scratch_shapes=[pltpu.VMEM((HEAD_DIM, HEAD_DIM), F32),
                        pltpu.VMEM((rows + CONV_HALO, LANES), F32)],
        compiler_params=pltpu.CompilerParams(
            dimension_semantics=("arbitrary", "arbitrary", "arbitrary"), vmem_limit_bytes=VMEM_LIMIT),
        name="mixer",
    )(cdec, proj3, proj3, proj3, proj3, proj3, proj3, cos_t, sin_t, decay, zeta, xi,
      ret_g, ret_b, cw, cb, cg, cbeta)


def _layer_norm(z, g, b):
    mu = jnp.mean(z, axis=-1, keepdims=True)
    zc = z - mu
    var = jnp.mean(zc * zc, axis=-1, keepdims=True)
    return zc * lax.rsqrt(var + LN_EPS) * g + b


def _outproj_kernel(ret_ref, conv_ref, x_ref, w_ref, g_ref, b_ref, wrh_ref, wrl_ref, br_ref,
                    h_ref, route_ref, counts_ref, carry_ref):
    @pl.when(pl.program_id(0) == 0)
    def _():
        carry_ref[...] = jnp.zeros_like(carry_ref)

    mix = jnp.concatenate([ret_ref[s] for s in range(HEADS)] +
                          [conv_ref[s] for s in range(CONV_GROUPS)], axis=-1)
    y = jnp.dot(mix, w_ref[...], preferred_element_type=F32)
    h = _layer_norm(ALPHA * x_ref[...] + y, g_ref[...], b_ref[...])
    h_ref[...] = h

    hb = h.astype(BF16)
    hl = (h - hb.astype(F32)).astype(BF16)
    wrh = wrh_ref[...]
    logits = (jnp.dot(hb, wrh, preferred_element_type=F32)
              + jnp.dot(hb, wrl_ref[...], preferred_element_type=F32)
              + jnp.dot(hl, wrh, preferred_element_type=F32)) + br_ref[...]

    tm = logits.shape[0]
    lane = lax.broadcasted_iota(jnp.int32, (tm, LANES), 1)
    lane_f = lane.astype(F32)
    neg = jnp.float32(-jnp.inf)
    big = jnp.float32(1e9)
    gmask = (lane >= N_EXPERTS) & (lane < N_EXPERTS + N_GROUPS)
    gl = jnp.where(gmask, logits, neg)
    gmax = jnp.max(gl, axis=-1, keepdims=True)
    gidx = jnp.min(jnp.where(gl == gmax, lane_f, big), axis=-1, keepdims=True) - N_EXPERTS
    g_w = 1.0 / jnp.sum(jnp.where(gmask, jnp.exp(gl - gmax), 0.0), axis=-1, keepdims=True)
    lo = gidx * EXPERTS_PER_GROUP
    emask = (lane_f >= lo) & (lane_f < lo + EXPERTS_PER_GROUP)
    el = jnp.where(emask, logits, neg)
    top1 = jnp.max(el, axis=-1, keepdims=True)
    idx1 = jnp.min(jnp.where(el == top1, lane_f, big), axis=-1, keepdims=True)
    el2 = jnp.where(lane_f == idx1, neg, el)
    top2 = jnp.max(el2, axis=-1, keepdims=True)
    idx2 = jnp.min(jnp.where(el2 == top2, lane_f, big), axis=-1, keepdims=True)
    e2 = jnp.exp(top2 - top1)
    w1 = g_w / (1.0 + e2)
    w2 = w1 * e2

    sel1 = lane_f == idx1
    sel2 = lane_f == idx2
    onehot = jnp.where(sel1, 1.0, 0.0) + jnp.where(sel2, 1.0, 0.0)
    r_i = lax.broadcasted_iota(jnp.int32, (tm, tm), 0)
    c_i = lax.broadcasted_iota(jnp.int32, (tm, tm), 1)
    tri = jnp.where(c_i < r_i, 1.0, 0.0).astype(BF16)
    prefix = jnp.dot(tri, onehot.astype(BF16), preferred_element_type=F32) + carry_ref[0:1, :]
    rank1 = jnp.sum(jnp.where(sel1, prefix, 0.0), axis=-1, keepdims=True)
    rank2 = jnp.sum(jnp.where(sel2, prefix, 0.0), axis=-1, keepdims=True)
    total = carry_ref[0:1, :] + jnp.sum(onehot, axis=0, keepdims=True)
    carry_ref[...] = jnp.broadcast_to(total, carry_ref.shape)
    counts_ref[...] = jnp.broadcast_to(total, counts_ref.shape)

    route = jnp.where(lane == 0, w1, 0.0)
    for k, val in enumerate((w2, idx1, idx2, rank1, rank2), start=1):
        route = jnp.where(lane == k, val, route)
    route_ref[...] = route


def _out_proj(ret3, conv3, x2, w_out_b, ln_g, ln_b, wr_hi, wr_lo, br, tm=512):
    T = x2.shape[0]
    row = lambda i: (i, 0)
    const = lambda i: (0, 0)
    return pl.pallas_call(
        _outproj_kernel,
        grid=(T // tm,),
        in_specs=[pl.BlockSpec((HEADS, tm, LANES), lambda i: (0, i, 0)),
                  pl.BlockSpec((CONV_GROUPS, tm, LANES), lambda i: (0, i, 0)),
                  pl.BlockSpec((tm, D_MODEL), row),
                  pl.BlockSpec((D_MODEL, D_MODEL), const),
                  pl.BlockSpec((1, D_MODEL), const),
                  pl.BlockSpec((1, D_MODEL), const),
                  pl.BlockSpec((D_MODEL, LANES), const),
                  pl.BlockSpec((D_MODEL, LANES), const),
                  pl.BlockSpec((1, LANES), const)],
        out_specs=[pl.BlockSpec((tm, D_MODEL), row),
                   pl.BlockSpec((tm, LANES), row),
                   pl.BlockSpec((SUBLANES, LANES), const)],
        out_shape=[jax.ShapeDtypeStruct((T, D_MODEL), F32),
                   jax.ShapeDtypeStruct((T, LANES), F32),
                   jax.ShapeDtypeStruct((SUBLANES, LANES), F32)],
        scratch_shapes=[pltpu.VMEM((SUBLANES, LANES), F32)],
        compiler_params=pltpu.CompilerParams(
            dimension_semantics=("arbitrary",), vmem_limit_bytes=VMEM_LIMIT),
        name="out_proj_ln_route",
    )(ret3, conv3, x2, w_out_b, ln_g, ln_b, wr_hi, wr_lo, br)


MOE_TM = 512
MOE_TB = 512


def _row_copy(src_ref, src_row, dst_ref, dst_row, sem):
    return pltpu.make_async_copy(src_ref.at[pl.ds(src_row, 1), :], dst_ref.at[pl.ds(dst_row, 1), :], sem)


def _dispatch_kernel(meta_ref, pos_ref, h_ref, xs_ref, zero_ref, sems):
    tb = h_ref.shape[0]

    def zero_copy(j):
        return pltpu.make_async_copy(
            zero_ref, xs_ref.at[pl.ds(pl.multiple_of(meta_ref[j], MOE_TM), MOE_TM), :], sems.at[1])

    @pl.when(pl.program_id(0) == 0)
    def _():
        zero_ref[...] = jnp.zeros_like(zero_ref)
        for j in range(2 * N_EXPERTS):
            @pl.when(meta_ref[2 * N_EXPERTS + j] > 0)
            def _():
                zero_copy(j).start()
        for j in range(2 * N_EXPERTS):
            @pl.when(meta_ref[2 * N_EXPERTS + j] > 0)
            def _():
                zero_copy(j).wait()

    def issue(t, carry):
        _row_copy(h_ref, t, xs_ref, pos_ref[0, 0, t], sems.at[0]).start()
        _row_copy(h_ref, t, xs_ref, pos_ref[0, 0, tb + t], sems.at[0]).start()
        return carry

    lax.fori_loop(0, tb, issue, 0, unroll=8)

    for _ in range(TOP_K):
        pltpu.make_async_copy(h_ref, xs_ref.at[pl.ds(0, tb), :], sems.at[0]).wait()


def _dispatch(meta, pos3, h, n_rows):
    T = h.shape[0]
    tb = MOE_TB
    return pl.pallas_call(
        _dispatch_kernel,
        grid_spec=pltpu.PrefetchScalarGridSpec(
            num_scalar_prefetch=1,
            grid=(T // tb,),
            in_specs=[pl.BlockSpec((1, 1, 2 * tb), lambda i, m: (i, 0, 0), memory_space=pltpu.SMEM),
                      pl.BlockSpec((tb, D_MODEL), lambda i, m: (i, 0))],
            out_specs=pl.BlockSpec(memory_space=pl.ANY),
            scratch_shapes=[pltpu.VMEM((MOE_TM, D_MODEL), F32),
                            pltpu.SemaphoreType.DMA((2,))]),
        out_shape=jax.ShapeDtypeStruct((n_rows, D_MODEL), F32),
        compiler_params=pltpu.CompilerParams(
            dimension_semantics=("arbitrary",), vmem_limit_bytes=VMEM_LIMIT),
        name="moe_dispatch",
    )(meta, pos3, h)


def _experts_kernel(te_ref, nv_ref, x_ref, w1_ref, w3_ref, w2_ref, y_ref):
    used = pl.program_id(0) < nv_ref[0]

    @pl.when(used)
    def _():
        xb = x_ref[...].astype(BF16)
        a1 = jnp.dot(xb, w1_ref[0], preferred_element_type=F32)
        a3 = jnp.dot(xb, w3_ref[0], preferred_element_type=F32)
        act = (a1 * _sigmoid(a1) * a3).astype(BF16)
        y_ref[...] = jnp.dot(act, w2_ref[0], preferred_element_type=F32)

    @pl.when(jnp.logical_not(used))
    def _():
        y_ref[...] = jnp.zeros_like(y_ref)


def _experts(tile_expert, n_valid, xs, w1b, w3b, w2b):
    n_tiles = xs.shape[0] // MOE_TM
    row_in = lambda i, te, nv: (jnp.minimum(i, nv[0] - 1), 0)
    row = lambda i, te, nv: (i, 0)
    wsel = lambda i, te, nv: (te[i], 0, 0)
    return pl.pallas_call(
        _experts_kernel,
        grid_spec=pltpu.PrefetchScalarGridSpec(
            num_scalar_prefetch=2,
            grid=(n_tiles,),
            in_specs=[pl.BlockSpec((MOE_TM, D_MODEL), row_in),
                      pl.BlockSpec((1, D_MODEL, D_FF), wsel),
                      pl.BlockSpec((1, D_MODEL, D_FF), wsel),
                      pl.BlockSpec((1, D_FF, D_MODEL), wsel)],
            out_specs=pl.BlockSpec((MOE_TM, D_MODEL), row)),
        out_shape=jax.ShapeDtypeStruct(xs.shape, F32),
        compiler_params=pltpu.CompilerParams(
            dimension_semantics=("arbitrary",), vmem_limit_bytes=VMEM_LIMIT),
        name="moe_experts",
    )(tile_expert, n_valid, xs, w1b, w3b, w2b)


def _combine_kernel(pos_ref, route_ref, h_ref, g_ref, b_ref, ys_ref, o_ref, y1_ref, y2_ref, sems):
    tb = h_ref.shape[0]

    def issue(t, carry):
        _row_copy(ys_ref, pos_ref[0, 0, t], y1_ref, t, sems.at[0]).start()
        _row_copy(ys_ref, pos_ref[0, 0, tb + t], y2_ref, t, sems.at[0]).start()
        return carry

    lax.fori_loop(0, tb, issue, 0, unroll=8)

    pltpu.make_async_copy(ys_ref.at[pl.ds(0, tb), :], y1_ref, sems.at[0]).wait()
    pltpu.make_async_copy(ys_ref.at[pl.ds(0, tb), :], y2_ref, sems.at[0]).wait()

    route = route_ref[...]
    ffn = route[:, 0:1] * y1_ref[...] + route[:, 1:2] * y2_ref[...]
    o_ref[...] = _layer_norm(ALPHA * h_ref[...] + ffn, g_ref[...], b_ref[...])


def _combine(pos3, route, h, ln_g, ln_b, ys):
    T = h.shape[0]
    tb = MOE_TB
    row = lambda i: (i, 0)
    const = lambda i: (0, 0)
    return pl.pallas_call(
        _combine_kernel,
        grid=(T // tb,),
        in_specs=[pl.BlockSpec((1, 1, 2 * tb), lambda i: (i, 0, 0), memory_space=pltpu.SMEM),
                  pl.BlockSpec((tb, LANES), row),
                  pl.BlockSpec((tb, D_MODEL), row),
                  pl.BlockSpec((1, D_MODEL), const),
                  pl.BlockSpec((1, D_MODEL), const),
                  pl.BlockSpec(memory_space=pl.ANY)],
        out_specs=pl.BlockSpec((tb, D_MODEL), row),
        out_shape=jax.ShapeDtypeStruct((T, D_MODEL), F32),
        scratch_shapes=[pltpu.VMEM((tb, D_MODEL), F32),
                        pltpu.VMEM((tb, D_MODEL), F32),
                        pltpu.SemaphoreType.DMA((1,))],
        compiler_params=pltpu.CompilerParams(
            dimension_semantics=("arbitrary",), vmem_limit_bytes=VMEM_LIMIT),
        name="moe_combine_ln",
    )(pos3, route, h, ln_g, ln_b, ys)


def _routed_moe(h, route, counts, w1b, w3b, w2b, ln_g, ln_b):
    T = h.shape[0]
    n_tiles = (TOP_K * T) // MOE_TM + N_EXPERTS
    i32 = jnp.int32
    cnt = counts[0, :N_EXPERTS].astype(i32)
    ntile = (cnt + MOE_TM - 1) // MOE_TM
    tile_end = jnp.cumsum(ntile)
    n_valid = tile_end[-1:]
    seg_start = (tile_end - ntile) * MOE_TM
    info = route[:, 2:6].astype(i32)
    eids = jnp.arange(N_EXPERTS, dtype=i32)
    start_of = lambda e: jnp.sum(jnp.where(e[:, None] == eids[None, :], seg_start[None, :], 0), axis=1)
    pos1 = start_of(info[:, 0]) + info[:, 2]
    pos2 = start_of(info[:, 1]) + info[:, 3]
    pos3 = jnp.concatenate([pos1.reshape(T // MOE_TB, 1, MOE_TB), pos2.reshape(T // MOE_TB, 1, MOE_TB)], axis=2)
    tile_ids = jnp.minimum(jnp.arange(n_tiles, dtype=i32), n_valid[0] - 1)
    tile_expert = jnp.sum((tile_ids[:, None] >= tile_end[None, :]).astype(i32), axis=1)
    tail = n_valid[0] + jnp.arange(N_EXPERTS, dtype=i32)
    meta = jnp.concatenate([(tile_end - 1) * MOE_TM, tail * MOE_TM,
                            ntile, (tail < n_tiles).astype(i32)]).astype(i32)

    xs = _dispatch(meta, pos3, h, n_tiles * MOE_TM)
    ys = _experts(tile_expert, n_valid.astype(i32), xs, w1b, w3b, w2b)
    return _combine(pos3, route, h, ln_g, ln_b, ys)


def _retention_constants():
    hh = np.arange(HEADS, dtype=np.float64)
    log_gamma = np.log1p(-np.exp2(-5.0 - hh))
    pos = np.arange(CHUNK, dtype=np.float64)
    diff = pos[:, None] - pos[None, :]
    causal = diff >= 0
    decay = np.where(causal[None], np.exp(log_gamma[:, None, None] * np.where(causal, diff, 0.0)[None]), 0.0)
    zeta = np.exp(log_gamma[:, None] * (CHUNK - 1 - pos)[None])
    xi = np.exp(log_gamma[:, None] * (pos + 1.0)[None])
    cdec = np.exp(log_gamma * CHUNK)
    bc = lambda t: np.broadcast_to(t[:, :, None], (HEADS, CHUNK, LANES))
    return (jnp.asarray(cdec, F32), jnp.asarray(decay, F32),
            jnp.asarray(bc(zeta), F32), jnp.asarray(bc(xi), F32))


def _rope_tables(seq):
    inv = ROPE_BASE ** (-jnp.arange(0, HEAD_DIM, 2, dtype=F32) / HEAD_DIM)
    ang = jnp.arange(seq, dtype=F32)[:, None] * inv[None, :]
    cos = jnp.repeat(jnp.cos(ang), 2, axis=-1)
    sin = jnp.sin(ang)
    sin_signed = jnp.stack([-sin, sin], axis=-1).reshape(seq, HEAD_DIM)
    return cos, sin_signed


def kernel(x, w_in, ret_norm_g, ret_norm_b, conv_w, conv_b, conv_norm_g, conv_norm_b, w_out, ln1_g, ln1_b, w_router_group, b_router_group, w_router_expert, b_router_expert, w1, w3, w2, ln2_g, ln2_b):
    batch, seq, _ = x.shape
    T = batch * seq
    x2 = x.reshape(T, D_MODEL)

    proj3 = _in_proj(x2, w_in.astype(BF16))

    cos_t, sin_t = _rope_tables(seq)
    per_head = lambda t: t.reshape(HEADS, 1, LANES)
    cw = jnp.transpose(conv_w.reshape(CONV_TAPS, CONV_GROUPS, LANES), (1, 0, 2))
    ret3, conv3 = _mixer(proj3, cos_t, sin_t, _retention_constants(),
                         per_head(ret_norm_g), per_head(ret_norm_b), cw,
                         per_head(conv_b), per_head(conv_norm_g), per_head(conv_norm_b), batch, seq)

    wr = jnp.concatenate([jnp.transpose(w_router_expert, (1, 0, 2)).reshape(D_MODEL, N_EXPERTS),
                          w_router_group], axis=1)
    wr = jnp.pad(wr, ((0, 0), (0, LANES - wr.shape[1])))
    br = jnp.pad(jnp.concatenate([b_router_expert.reshape(-1), b_router_group]),
                 (0, LANES - N_EXPERTS - N_GROUPS)).reshape(1, LANES)
    wr_hi = wr.astype(BF16)
    wr_lo = (wr - wr_hi.astype(F32)).astype(BF16)

    h, route, counts = _out_proj(ret3, conv3, x2, w_out.astype(BF16), ln1_g.reshape(1, -1),
                                 ln1_b.reshape(1, -1), wr_hi, wr_lo, br)

    y = _routed_moe(h, route, counts, w1.astype(BF16), w3.astype(BF16), w2.astype(BF16),
                    ln2_g.reshape(1, -1), ln2_b.reshape(1, -1))
    return y.reshape(batch, seq, D_MODEL)
```

```python
import functools
import math

import numpy as np
import jax
import jax.numpy as jnp
from jax import lax
from jax.experimental import pallas as pl
from jax.experimental.pallas import tpu as pltpu

D_MODEL = 2048
RET_WIDTH = D_MODEL // 2
CONV_WIDTH = D_MODEL - RET_WIDTH
HEADS = 8
HEAD_DIM = RET_WIDTH // HEADS
CONV_GROUPS = 8
CONV_TAPS = 31
CHUNK = 128
ROPE_BASE = 10000.0
IN_COLS = 4 * RET_WIDTH + 2 * CONV_WIDTH
N_GROUPS = 4
EXPERTS_PER_GROUP = 4
N_EXPERTS = N_GROUPS * EXPERTS_PER_GROUP
TOP_K = 2
D_FF = D_MODEL // 2
LN_EPS = 1e-5
ALPHA = 2.0 ** 0.25

LANES = 128
SUBLANES = 8
VMEM_LIMIT = 56 * 1024 * 1024
CONV_HALO = 32

F32 = jnp.float32
BF16 = jnp.bfloat16


def _sigmoid(x):
    return 1.0 / (1.0 + jnp.exp(-x))


def _lane_norm(x, g, b):
    mu = jnp.mean(x, axis=-1, keepdims=True)
    xc = x - mu
    var = jnp.mean(xc * xc, axis=-1, keepdims=True)
    return xc * lax.rsqrt(var + LN_EPS) * g + b


def _inproj_kernel(x_ref, w_ref, o_ref):
    res = jnp.dot(x_ref[...].astype(BF16), w_ref[...], preferred_element_type=F32)
    for s in range(o_ref.shape[0]):
        o_ref[s] = res[:, s * LANES:(s + 1) * LANES].astype(BF16)


def _in_proj(x2, w_in_b, tm=512, tn=2048):
    T = x2.shape[0]
    nblk = tn // LANES
    return pl.pallas_call(
        _inproj_kernel,
        grid=(IN_COLS // tn, T // tm),
        in_specs=[pl.BlockSpec((tm, D_MODEL), lambda j, i: (i, 0)),
                  pl.BlockSpec((D_MODEL, tn), lambda j, i: (0, j))],
        out_specs=pl.BlockSpec((nblk, tm, LANES), lambda j, i: (j, i, 0)),
        out_shape=jax.ShapeDtypeStruct((IN_COLS // LANES, T, LANES), BF16),
        compiler_params=pltpu.CompilerParams(
            dimension_semantics=("arbitrary", "arbitrary"), vmem_limit_bytes=VMEM_LIMIT),
        name="in_proj",
    )(x2, w_in_b)


def _rotary(t, cos, sin_signed, even_lane):
    partner = jnp.where(even_lane, pltpu.roll(t, LANES - 1, 1), pltpu.roll(t, 1, 1))
    return t * cos + partner * sin_signed


def _mixer_kernel(cdec_ref, q_ref, k_ref, v_ref, gate_ref, ga_ref, gb_ref, cos_ref, sin_ref,
                  decay_ref, zeta_ref, xi_ref, rg_ref, rb_ref, cw_ref, cb_ref, cg_ref, cbeta_ref,
                  ret_ref, conv_ref, state_ref, uext_ref, *, rows):
    h = pl.program_id(1)
    n = pl.program_id(2)

    @pl.when(n == 0)
    def _():
        state_ref[...] = jnp.zeros_like(state_ref)
        uext_ref[0:CONV_HALO, :] = jnp.zeros((CONV_HALO, LANES), F32)

    even_lane = (lax.broadcasted_iota(jnp.int32, (CHUNK, LANES), 1) % 2) == 0
    cdec = cdec_ref[h]
    decay = decay_ref[0]
    zeta = zeta_ref[0]
    xi = xi_ref[0]
    rg = rg_ref[0]
    rb = rb_ref[0]
    scale = HEAD_DIM ** -0.5

    state = state_ref[...]
    for c in range(rows // CHUNK):
        rs = slice(c * CHUNK, (c + 1) * CHUNK)
        cos = cos_ref[rs, :]
        sin = sin_ref[rs, :]
        q = _rotary(q_ref[0, rs, :].astype(F32), cos, sin, even_lane)
        k = _rotary(k_ref[0, rs, :].astype(F32), cos, sin, even_lane) * scale
        v = v_ref[0, rs, :]
        scores = lax.dot_general(q.astype(BF16), k.astype(BF16), (((1,), (1,)), ((), ())),
                                 preferred_element_type=F32) * decay
        inner = jnp.dot(scores.astype(BF16), v, preferred_element_type=F32)
        cross = jnp.dot((q * xi).astype(BF16), state.astype(BF16), preferred_element_type=F32)
        kz_t = jnp.transpose(k * zeta).astype(BF16)
        state = cdec * state + jnp.dot(kz_t, v, preferred_element_type=F32)
        y = _lane_norm(inner + cross, rg, rb)
        g = gate_ref[0, rs, :].astype(F32)
        ret_ref[0, rs, :] = (g * _sigmoid(g) * y).astype(BF16)
    state_ref[...] = state

    a = ga_ref[0].astype(F32)
    b = gb_ref[0].astype(F32)
    uext_ref[CONV_HALO:CONV_HALO + rows, :] = a * _sigmoid(b)

    cb = cb_ref[0]
    cg = cg_ref[0]
    cbeta = cbeta_ref[0]
    blk = 64
    first = CONV_HALO - (CONV_TAPS - 1)
    for r0 in range(0, rows, blk):
        acc = jnp.zeros((blk, LANES), F32)
        for j in range(CONV_TAPS):
            acc = acc + cw_ref[0, j:j + 1, :] * uext_ref[r0 + first + j:r0 + first + j + blk, :]
        y = _lane_norm(acc + cb, cg, cbeta)
        conv_ref[0, r0:r0 + blk, :] = (y * _sigmoid(y)).astype(BF16)
    uext_ref[0:CONV_HALO, :] = uext_ref[rows:rows + CONV_HALO, :]


def _mixer(proj3, cos_t, sin_t, consts, ret_g, ret_b, cw, cb, cg, cbeta, batch, seq, rows=512):
    T = batch * seq
    ns = seq // rows
    cdec, decay, zeta, xi = consts

    def pspec(off):
        return pl.BlockSpec((1, rows, LANES), lambda b, h, n: (off + h, b * ns + n, 0))

    def hspec(d1):
        return pl.BlockSpec((1, d1, LANES), lambda b, h, n: (h, 0, 0))

    return pl.pallas_call(
        functools.partial(_mixer_kernel, rows=rows),
        grid=(batch, HEADS, ns),
        in_specs=[pl.BlockSpec(memory_space=pltpu.SMEM),
                  pspec(0), pspec(8), pspec(16), pspec(24), pspec(32), pspec(40),
                  pl.BlockSpec((rows, LANES), lambda b, h, n: (n, 0)),
                  pl.BlockSpec((rows, LANES), lambda b, h, n: (n, 0)),
                  hspec(CHUNK), hspec(CHUNK), hspec(CHUNK),
                  hspec(1), hspec(1), hspec(CONV_TAPS), hspec(1), hspec(1), hspec(1)],
        out_specs=[pl.BlockSpec((1, rows, LANES), lambda b, h, n: (h, b * ns + n, 0)),
                   pl.BlockSpec((1, rows, LANES), lambda b, h, n: (h, b * ns + n, 0))],
        out_shape=[jax.ShapeDtypeStruct((HEADS, T, LANES), BF16),
                   jax.ShapeDtypeStruct((CONV_GROUPS, T, LANES), BF16)],
        scratch_shapes=[pltpu.VMEM((HEAD_DIM, HEAD_DIM), F32),
                        pltpu.VMEM((rows + CONV_HALO, LANES), F32)],
        compiler_params=pltpu.CompilerParams(
            dimension_semantics=("arbitrary", "arbitrary", "arbitrary"), vmem_limit_bytes=VMEM_LIMIT),
        name="mixer",
    )(cdec, proj3, proj3, proj3, proj3, proj3, proj3, cos_t, sin_t, decay, zeta, xi,
      ret_g, ret_b, cw, cb, cg, cbeta)


def _layer_norm(z, g, b):
    mu = jnp.mean(z, axis=-1, keepdims=True)
    zc = z - mu
    var = jnp.mean(zc * zc, axis=-1, keepdims=True)
    return zc * lax.rsqrt(var + LN_EPS) * g + b


def _outproj_kernel(ret_ref, conv_ref, x_ref, w_ref, g_ref, b_ref, wrh_ref, wrl_ref, br_ref,
                    h_ref, route_ref, counts_ref, carry_ref):
    @pl.when(pl.program_id(0) == 0)
    def _():
        carry_ref[...] = jnp.zeros_like(carry_ref)

    mix = jnp.concatenate([ret_ref[s] for s in range(HEADS)] +
                          [conv_ref[s] for s in range(CONV_GROUPS)], axis=-1)
    y = jnp.dot(mix, w_ref[...], preferred_element_type=F32)
    h = _layer_norm(ALPHA * x_ref[...] + y, g_ref[...], b_ref[...])
    h_ref[...] = h

    hb = h.astype(BF16)
    hl = (h - hb.astype(F32)).astype(BF16)
    wrh = wrh_ref[...]
    logits = (jnp.dot(hb, wrh, preferred_element_type=F32)
              + jnp.dot(hb, wrl_ref[...], preferred_element_type=F32)
              + jnp.dot(hl, wrh, preferred_element_type=F32)) + br_ref[...]

    tm = logits.shape[0]
    lane = lax.broadcasted_iota(jnp.int32, (tm, LANES), 1)
    lane_f = lane.astype(F32)
    neg = jnp.float32(-jnp.inf)
    big = jnp.float32(1e9)
    gmask = (lane >= N_EXPERTS) & (lane < N_EXPERTS + N_GROUPS)
    gl = jnp.where(gmask, logits, neg)
    gmax = jnp.max(gl, axis=-1, keepdims=True)
    gidx = jnp.min(jnp.where(gl == gmax, lane_f, big), axis=-1, keepdims=True) - N_EXPERTS
    g_w = 1.0 / jnp.sum(jnp.where(gmask, jnp.exp(gl - gmax), 0.0), axis=-1, keepdims=True)
    lo = gidx * EXPERTS_PER_GROUP
    emask = (lane_f >= lo) & (lane_f < lo + EXPERTS_PER_GROUP)
    el = jnp.where(emask, logits, neg)
    top1 = jnp.max(el, axis=-1, keepdims=True)
    idx1 = jnp.min(jnp.where(el == top1, lane_f, big), axis=-1, keepdims=True)
    el2 = jnp.where(lane_f == idx1, neg, el)
    top2 = jnp.max(el2, axis=-1, keepdims=True)
    idx2 = jnp.min(jnp.where(el2 == top2, lane_f, big), axis=-1, keepdims=True)
    e2 = jnp.exp(top2 - top1)
    w1 = g_w / (1.0 + e2)
    w2 = w1 * e2

    sel1 = lane_f == idx1
    sel2 = lane_f == idx2
    onehot = jnp.where(sel1, 1.0, 0.0) + jnp.where(sel2, 1.0, 0.0)
    r_i = lax.broadcasted_iota(jnp.int32, (tm, tm), 0)
    c_i = lax.broadcasted_iota(jnp.int32, (tm, tm), 1)
    tri = jnp.where(c_i < r_i, 1.0, 0.0).astype(BF16)
    prefix = jnp.dot(tri, onehot.astype(BF16), preferred_element_type=F32) + carry_ref[0:1, :]
    rank1 = jnp.sum(jnp.where(sel1, prefix, 0.0), axis=-1, keepdims=True)
    rank2 = jnp.sum(jnp.where(sel2, prefix, 0.0), axis=-1, keepdims=True)
    total = carry_ref[0:1, :] + jnp.sum(onehot, axis=0, keepdims=True)
    carry_ref[...] = jnp.broadcast_to(total, carry_ref.shape)
    counts_ref[...] = jnp.broadcast_to(total, counts_ref.shape)

    route = jnp.where(lane == 0, w1, 0.0)
    for k, val in enumerate((w2, idx1, idx2, rank1, rank2), start=1):
        route = jnp.where(lane == k, val, route)
    route_ref[...] = route


def _out_proj(ret3, conv3, x2, w_out_b, ln_g, ln_b, wr_hi, wr_lo, br, tm=512):
    T = x2.shape[0]
    row = lambda i: (i, 0)
    const = lambda i: (0, 0)
    return pl.pallas_call(
        _outproj_kernel,
        grid=(T // tm,),
        in_specs=[pl.BlockSpec((HEADS, tm, LANES), lambda i: (0, i, 0)),
                  pl.BlockSpec((CONV_GROUPS, tm, LANES), lambda i: (0, i, 0)),
                  pl.BlockSpec((tm, D_MODEL), row),
                  pl.BlockSpec((D_MODEL, D_MODEL), const),
                  pl.BlockSpec((1, D_MODEL), const),
                  pl.BlockSpec((1, D_MODEL), const),
                  pl.BlockSpec((D_MODEL, LANES), const),
                  pl.BlockSpec((D_MODEL, LANES), const),
                  pl.BlockSpec((1, LANES), const)],
        out_specs=[pl.BlockSpec((tm, D_MODEL), row),
                   pl.BlockSpec((tm, LANES), row),
                   pl.BlockSpec((SUBLANES, LANES), const)],
        out_shape=[jax.ShapeDtypeStruct((T, D_MODEL), F32),
                   jax.ShapeDtypeStruct((T, LANES), F32),
                   jax.ShapeDtypeStruct((SUBLANES, LANES), F32)],
        scratch_shapes=[pltpu.VMEM((SUBLANES, LANES), F32)],
        compiler_params=pltpu.CompilerParams(
            dimension_semantics=("arbitrary",), vmem_limit_bytes=VMEM_LIMIT),
        name="out_proj_ln_route",
    )(ret3, conv3, x2, w_out_b, ln_g, ln_b, wr_hi, wr_lo, br)


MOE_TM = 512
MOE_TB = 512


def _row_copy(src_ref, src_row, dst_ref, dst_row, sem):
    return pltpu.make_async_copy(src_ref.at[pl.ds(src_row, 1), :], dst_ref.at[pl.ds(dst_row, 1), :], sem)


def _experts_kernel(te_ref, nv_ref, src_ref, src_next_ref, h_ref, w1_ref, w3_ref, w2_ref, y_ref, x_ref, sems):
    i = pl.program_id(0)
    last = pl.num_programs(0) - 1
    slot = i % 2
    used = i < nv_ref[0]

    def gather_wait(s):
        pltpu.make_async_copy(h_ref.at[pl.ds(0, MOE_TM), :], x_ref.at[s], sems.at[s]).wait()

    @pl.when(i == 0)
    def _():
        def first(t, carry):
            _row_copy(h_ref, src_ref[0, 0, t], x_ref.at[0], t, sems.at[0]).start()
            return carry
        lax.fori_loop(0, MOE_TM, first, 0, unroll=8)

    @pl.when(used)
    def _():
        gather_wait(slot)
        for t in range(MOE_TM):
            _row_copy(h_ref, src_next_ref[0, 0, t], x_ref.at[1 - slot], t, sems.at[1 - slot]).start()
        xb = x_ref[slot].astype(BF16)
        a1 = jnp.dot(xb, w1_ref[0], preferred_element_type=F32)
        a3 = jnp.dot(xb, w3_ref[0], preferred_element_type=F32)
        act = (a1 * _sigmoid(a1) * a3).astype(BF16)
        y_ref[...] = jnp.dot(act, w2_ref[0], preferred_element_type=F32)

        @pl.when(i == last)
        def _():
            gather_wait(1 - slot)

    @pl.when(jnp.logical_not(used))
    def _():
        @pl.when(i == nv_ref[0])
        def _():
            gather_wait(slot)
        y_ref[...] = jnp.zeros_like(y_ref)


def _experts(tile_expert, n_valid, src_tok, h, w1b, w3b, w2b):
    n_tiles = src_tok.shape[0]
    row = lambda i, te, nv: (i, 0)
    wsel = lambda i, te, nv: (te[i], 0, 0)
    smem_row = lambda f: pl.BlockSpec((1, 1, MOE_TM), f, memory_space=pltpu.SMEM)
    return pl.pallas_call(
        _experts_kernel,
        grid_spec=pltpu.PrefetchScalarGridSpec(
            num_scalar_prefetch=2,
            grid=(n_tiles,),
            in_specs=[smem_row(lambda i, te, nv: (i, 0, 0)),
                      smem_row(lambda i, te, nv: (jnp.minimum(i + 1, n_tiles - 1), 0, 0)),
                      pl.BlockSpec(memory_space=pl.ANY),
                      pl.BlockSpec((1, D_MODEL, D_FF), wsel),
                      pl.BlockSpec((1, D_MODEL, D_FF), wsel),
                      pl.BlockSpec((1, D_FF, D_MODEL), wsel)],
            out_specs=pl.BlockSpec((MOE_TM, D_MODEL), row),
            scratch_shapes=[pltpu.VMEM((2, MOE_TM, D_MODEL), F32),
                            pltpu.SemaphoreType.DMA((2,))]),
        out_shape=jax.ShapeDtypeStruct((n_tiles * MOE_TM, D_MODEL), F32),
        compiler_params=pltpu.CompilerParams(
            dimension_semantics=("arbitrary",), vmem_limit_bytes=VMEM_LIMIT),
        name="moe_experts",
    )(tile_expert, n_valid, src_tok, src_tok, h, w1b, w3b, w2b)


def _combine_kernel(pos_ref, pos_next_ref, route_ref, h_ref, g_ref, b_ref, ys_ref, o_ref,
                    ya0_ref, yb0_ref, ya1_ref, yb1_ref, sems):
    tb = h_ref.shape[0]
    i = pl.program_id(0)
    bufs = ((ya0_ref, yb0_ref), (ya1_ref, yb1_ref))

    def gather_wait(s):
        for k in range(TOP_K):
            pltpu.make_async_copy(ys_ref.at[pl.ds(0, tb), :], bufs[s][k], sems.at[s]).wait()

    @pl.when(i == 0)
    def _():
        def first(t, carry):
            for k in range(TOP_K):
                _row_copy(ys_ref, pos_ref[0, 0, k * tb + t], bufs[0][k], t, sems.at[0]).start()
            return carry
        lax.fori_loop(0, tb, first, 0, unroll=8)

    for par in range(2):
        @pl.when(i % 2 == par)
        def _():
            gather_wait(par)
            for t in range(tb):
                for k in range(TOP_K):
                    _row_copy(ys_ref, pos_next_ref[0, 0, k * tb + t], bufs[1 - par][k], t, sems.at[1 - par]).start()
            route = route_ref[...]
            ffn = route[:, 0:1] * bufs[par][0][...] + route[:, 1:2] * bufs[par][1][...]
            o_ref[...] = _layer_norm(ALPHA * h_ref[...] + ffn, g_ref[...], b_ref[...])

            @pl.when(i == pl.num_programs(0) - 1)
            def _():
                gather_wait(1 - par)


def _combine(pos3, route, h, ln_g, ln_b, ys):
    T = h.shape[0]
    tb = MOE_TB
    row = lambda i: (i, 0)
    const = lambda i: (0, 0)
    return pl.pallas_call(
        _combine_kernel,
        grid=(T // tb,),
        in_specs=[pl.BlockSpec((1, 1, 2 * tb), lambda i: (i, 0, 0), memory_space=pltpu.SMEM),
                  pl.BlockSpec((1, 1, 2 * tb), lambda i: (jnp.minimum(i + 1, T // tb - 1), 0, 0),
                               memory_space=pltpu.SMEM),
                  pl.BlockSpec((tb, LANES), row),
                  pl.BlockSpec((tb, D_MODEL), row),
                  pl.BlockSpec((1, D_MODEL), const),
                  pl.BlockSpec((1, D_MODEL), const),
                  pl.BlockSpec(memory_space=pl.ANY)],
        out_specs=pl.BlockSpec((tb, D_MODEL), row),
        out_shape=jax.ShapeDtypeStruct((T, D_MODEL), F32),
        scratch_shapes=[pltpu.VMEM((tb, D_MODEL), F32)] * (2 * TOP_K) + [pltpu.SemaphoreType.DMA((2,))],
        compiler_params=pltpu.CompilerParams(
            dimension_semantics=("arbitrary",), vmem_limit_bytes=VMEM_LIMIT),
        name="moe_combine_ln",
    )(pos3, pos3, route, h, ln_g, ln_b, ys)


def _routed_moe(h, route, counts, w1b, w3b, w2b, ln_g, ln_b):
    T = h.shape[0]
    n_tiles = (TOP_K * T) // MOE_TM + N_EXPERTS
    i32 = jnp.int32
    cnt = counts[0, :N_EXPERTS].astype(i32)
    ntile = (cnt + MOE_TM - 1) // MOE_TM
    tile_end = jnp.cumsum(ntile)
    n_valid = tile_end[-1:]
    seg_start = (tile_end - ntile) * MOE_TM
    info = route[:, 2:6].astype(i32)
    eids = jnp.arange(N_EXPERTS, dtype=i32)
    start_of = lambda e: jnp.sum(jnp.where(e[:, None] == eids[None, :], seg_start[None, :], 0), axis=1)
    pos1 = start_of(info[:, 0]) + info[:, 2]
    pos2 = start_of(info[:, 1]) + info[:, 3]
    pos3 = jnp.concatenate([pos1.reshape(T // MOE_TB, 1, MOE_TB), pos2.reshape(T // MOE_TB, 1, MOE_TB)], axis=2)
    tile_ids = jnp.minimum(jnp.arange(n_tiles, dtype=i32), n_valid[0] - 1)
    tile_expert = jnp.sum((tile_ids[:, None] >= tile_end[None, :]).astype(i32), axis=1)

    order = jnp.argsort(jnp.concatenate([pos1, pos2])).astype(i32)
    tok_sorted = jnp.concatenate([jnp.where(order >= T, order - T, order), jnp.zeros((MOE_TM,), i32)])
    onehot_te = tile_expert[:, None] == eids[None, :]
    pick = lambda table: jnp.sum(jnp.where(onehot_te, table[None, :], 0), axis=1)
    win_start = pick(jnp.cumsum(cnt) - cnt) + (tile_ids - pick(tile_end - ntile)) * MOE_TM
    src_tok = jax.vmap(lambda s: lax.dynamic_slice(tok_sorted, (s,), (MOE_TM,)))(win_start)

    ys = _experts(tile_expert, n_valid.astype(i32), src_tok.reshape(n_tiles, 1, MOE_TM), h, w1b, w3b, w2b)
    return _combine(pos3, route, h, ln_g, ln_b, ys)


def _retention_constants():
    hh = np.arange(HEADS, dtype=np.float64)
    log_gamma = np.log1p(-np.exp2(-5.0 - hh))
    pos = np.arange(CHUNK, dtype=np.float64)
    diff = pos[:, None] - pos[None, :]
    causal = diff >= 0
    decay = np.where(causal[None], np.exp(log_gamma[:, None, None] * np.where(causal, diff, 0.0)[None]), 0.0)
    zeta = np.exp(log_gamma[:, None] * (CHUNK - 1 - pos)[None])
    xi = np.exp(log_gamma[:, None] * (pos + 1.0)[None])
    cdec = np.exp(log_gamma * CHUNK)
    bc = lambda t: np.broadcast_to(t[:, :, None], (HEADS, CHUNK, LANES))
    return (jnp.asarray(cdec, F32), jnp.asarray(decay, F32),
            jnp.asarray(bc(zeta), F32), jnp.asarray(bc(xi), F32))


def _rope_tables(seq):
    inv = ROPE_BASE ** (-jnp.arange(0, HEAD_DIM, 2, dtype=F32) / HEAD_DIM)
    ang = jnp.arange(seq, dtype=F32)[:, None] * inv[None, :]
    cos = jnp.repeat(jnp.cos(ang), 2, axis=-1)
    sin = jnp.sin(ang)
    sin_signed = jnp.stack([-sin, sin], axis=-1).reshape(seq, HEAD_DIM)
    return cos, sin_signed


def kernel(x, w_in, ret_norm_g, ret_norm_b, conv_w, conv_b, conv_norm_g, conv_norm_b, w_out, ln1_g, ln1_b, w_router_group, b_router_group, w_router_expert, b_router_expert, w1, w3, w2, ln2_g, ln2_b):
    batch, seq, _ = x.shape
    T = batch * seq
    x2 = x.reshape(T, D_MODEL)

    proj3 = _in_proj(x2, w_in.astype(BF16))

    cos_t, sin_t = _rope_tables(seq)
    per_head = lambda t: t.reshape(HEADS, 1, LANES)
    cw = jnp.transpose(conv_w.reshape(CONV_TAPS, CONV_GROUPS, LANES), (1, 0, 2))
    ret3, conv3 = _mixer(proj3, cos_t, sin_t, _retention_constants(),
                         per_head(ret_norm_g), per_head(ret_norm_b), cw,
                         per_head(conv_b), per_head(conv_norm_g), per_head(conv_norm_b), batch, seq)

    wr = jnp.concatenate([jnp.transpose(w_router_expert, (1, 0, 2)).reshape(D_MODEL, N_EXPERTS),
                          w_router_group], axis=1)
    wr = jnp.pad(wr, ((0, 0), (0, LANES - wr.shape[1])))
    br = jnp.pad(jnp.concatenate([b_router_expert.reshape(-1), b_router_group]),
                 (0, LANES - N_EXPERTS - N_GROUPS)).reshape(1, LANES)
    wr_hi = wr.astype(BF16)
    wr_lo = (wr - wr_hi.astype(F32)).astype(BF16)

    h, route, counts = _out_proj(ret3, conv3, x2, w_out.astype(BF16), ln1_g.reshape(1, -1),
                                 ln1_b.reshape(1, -1), wr_hi, wr_lo, br)

    y = _routed_moe(h, route, counts, w1.astype(BF16), w3.astype(BF16), w2.astype(BF16),
                    ln2_g.reshape(1, -1), ln2_b.reshape(1, -1))
    return y.reshape(batch, seq, D_MODEL)
```

```python
import functools
import math

import numpy as np
import jax
import jax.numpy as jnp
from jax import lax
from jax.experimental import pallas as pl
from jax.experimental.pallas import tpu as pltpu

D_MODEL = 2048
RET_WIDTH = D_MODEL // 2
CONV_WIDTH = D_MODEL - RET_WIDTH
HEADS = 8
HEAD_DIM = RET_WIDTH // HEADS
CONV_GROUPS = 8
CONV_TAPS = 31
CHUNK = 128
ROPE_BASE = 10000.0
IN_COLS = 4 * RET_WIDTH + 2 * CONV_WIDTH
N_GROUPS = 4
EXPERTS_PER_GROUP = 4
N_EXPERTS = N_GROUPS * EXPERTS_PER_GROUP
TOP_K = 2
D_FF = D_MODEL // 2
LN_EPS = 1e-5
ALPHA = 2.0 ** 0.25

LANES = 128
SUBLANES = 8
VMEM_LIMIT = 56 * 1024 * 1024
CONV_HALO = 32

F32 = jnp.float32
BF16 = jnp.bfloat16


def _sigmoid(x):
    return 1.0 / (1.0 + jnp.exp(-x))


def _lane_norm(x, g, b):
    mu = jnp.mean(x, axis=-1, keepdims=True)
    xc = x - mu
    var = jnp.mean(xc * xc, axis=-1, keepdims=True)
    return xc * lax.rsqrt(var + LN_EPS) * g + b


def _inproj_kernel(x_ref, w_ref, o_ref):
    res = jnp.dot(x_ref[...].astype(BF16), w_ref[...], preferred_element_type=F32)
    for s in range(o_ref.shape[0]):
        o_ref[s] = res[:, s * LANES:(s + 1) * LANES].astype(BF16)


def _in_proj(x2, w_in_b, tm=512, tn=2048):
    T = x2.shape[0]
    nblk = tn // LANES
    return pl.pallas_call(
        _inproj_kernel,
        grid=(IN_COLS // tn, T // tm),
        in_specs=[pl.BlockSpec((tm, D_MODEL), lambda j, i: (i, 0)),
                  pl.BlockSpec((D_MODEL, tn), lambda j, i: (0, j))],
        out_specs=pl.BlockSpec((nblk, tm, LANES), lambda j, i: (j, i, 0)),
        out_shape=jax.ShapeDtypeStruct((IN_COLS // LANES, T, LANES), BF16),
        compiler_params=pltpu.CompilerParams(
            dimension_semantics=("arbitrary", "arbitrary"), vmem_limit_bytes=VMEM_LIMIT),
        name="in_proj",
    )(x2, w_in_b)


def _rotary(t, cos, sin_signed, even_lane):
    partner = jnp.where(even_lane, pltpu.roll(t, LANES - 1, 1), pltpu.roll(t, 1, 1))
    return t * cos + partner * sin_signed


def _mixer_kernel(cdec_ref, q_ref, k_ref, v_ref, gate_ref, ga_ref, gb_ref, cos_ref, sin_ref,
                  decay_ref, zeta_ref, xi_ref, rg_ref, rb_ref, cw_ref, cb_ref, cg_ref, cbeta_ref,
                  w1f_ref, w3f_ref, w2f_ref,
                  ret_ref, conv_ref, w1b_ref, w3b_ref, w2b_ref, state_ref, uext_ref, *, rows):
    w1b_ref[...] = w1f_ref[...].astype(BF16)
    w3b_ref[...] = w3f_ref[...].astype(BF16)
    w2b_ref[...] = w2f_ref[...].astype(BF16)

    h = pl.program_id(1)
    n = pl.program_id(2)

    @pl.when(n == 0)
    def _():
        state_ref[...] = jnp.zeros_like(state_ref)
        uext_ref[0:CONV_HALO, :] = jnp.zeros((CONV_HALO, LANES), F32)

    even_lane = (lax.broadcasted_iota(jnp.int32, (CHUNK, LANES), 1) % 2) == 0
    cdec = cdec_ref[h]
    decay = decay_ref[0]
    zeta = zeta_ref[0]
    xi = xi_ref[0]
    rg = rg_ref[0]
    rb = rb_ref[0]
    scale = HEAD_DIM ** -0.5

    state = state_ref[...]
    for c in range(rows // CHUNK):
        rs = slice(c * CHUNK, (c + 1) * CHUNK)
        cos = cos_ref[rs, :]
        sin = sin_ref[rs, :]
        q = _rotary(q_ref[0, rs, :].astype(F32), cos, sin, even_lane)
        k = _rotary(k_ref[0, rs, :].astype(F32), cos, sin, even_lane) * scale
        v = v_ref[0, rs, :]
        scores = lax.dot_general(q.astype(BF16), k.astype(BF16), (((1,), (1,)), ((), ())),
                                 preferred_element_type=F32) * decay
        inner = jnp.dot(scores.astype(BF16), v, preferred_element_type=F32)
        cross = jnp.dot((q * xi).astype(BF16), state.astype(BF16), preferred_element_type=F32)
        kz_t = jnp.transpose(k * zeta).astype(BF16)
        state = cdec * state + jnp.dot(kz_t, v, preferred_element_type=F32)
        y = _lane_norm(inner + cross, rg, rb)
        g = gate_ref[0, rs, :].astype(F32)
        ret_ref[0, rs, :] = (g * _sigmoid(g) * y).astype(BF16)
    state_ref[...] = state

    a = ga_ref[0].astype(F32)
    b = gb_ref[0].astype(F32)
    uext_ref[CONV_HALO:CONV_HALO + rows, :] = a * _sigmoid(b)

    cb = cb_ref[0]
    cg = cg_ref[0]
    cbeta = cbeta_ref[0]
    blk = 64
    first = CONV_HALO - (CONV_TAPS - 1)
    for r0 in range(0, rows, blk):
        acc = jnp.zeros((blk, LANES), F32)
        for j in range(CONV_TAPS):
            acc = acc + cw_ref[0, j:j + 1, :] * uext_ref[r0 + first + j:r0 + first + j + blk, :]
        y = _lane_norm(acc + cb, cg, cbeta)
        conv_ref[0, r0:r0 + blk, :] = (y * _sigmoid(y)).astype(BF16)
    uext_ref[0:CONV_HALO, :] = uext_ref[rows:rows + CONV_HALO, :]


def _mixer(proj3, cos_t, sin_t, consts, ret_g, ret_b, cw, cb, cg, cbeta, w1, w3, w2, batch, seq, rows=512):
    T = batch * seq
    ns = seq // rows
    cdec, decay, zeta, xi = consts
    steps = batch * HEADS * ns
    w1s = w1.reshape(-1, w1.shape[-1])
    w3s = w3.reshape(-1, w3.shape[-1])
    w2s = w2.reshape(-1, w2.shape[-1])
    step_of = lambda b, h, n: ((b * HEADS + h) * ns + n, 0)
    wspec = lambda w: pl.BlockSpec((w.shape[0] // steps, w.shape[1]), step_of)
    for w in (w1s, w3s, w2s):
        assert w.shape[0] % (steps * 16) == 0

    def pspec(off):
        return pl.BlockSpec((1, rows, LANES), lambda b, h, n: (off + h, b * ns + n, 0))

    def hspec(d1):
        return pl.BlockSpec((1, d1, LANES), lambda b, h, n: (h, 0, 0))

    return pl.pallas_call(
        functools.partial(_mixer_kernel, rows=rows),
        grid=(batch, HEADS, ns),
        in_specs=[pl.BlockSpec(memory_space=pltpu.SMEM),
                  pspec(0), pspec(8), pspec(16), pspec(24), pspec(32), pspec(40),
                  pl.BlockSpec((rows, LANES), lambda b, h, n: (n, 0)),
                  pl.BlockSpec((rows, LANES), lambda b, h, n: (n, 0)),
                  hspec(CHUNK), hspec(CHUNK), hspec(CHUNK),
                  hspec(1), hspec(1), hspec(CONV_TAPS), hspec(1), hspec(1), hspec(1),
                  wspec(w1s), wspec(w3s), wspec(w2s)],
        out_specs=[pl.BlockSpec((1, rows, LANES), lambda b, h, n: (h, b * ns + n, 0)),
                   pl.BlockSpec((1, rows, LANES), lambda b, h, n: (h, b * ns + n, 0)),
                   wspec(w1s), wspec(w3s), wspec(w2s)],
        out_shape=[jax.ShapeDtypeStruct((HEADS, T, LANES), BF16),
                   jax.ShapeDtypeStruct((CONV_GROUPS, T, LANES), BF16),
                   jax.ShapeDtypeStruct(w1s.shape, BF16),
                   jax.ShapeDtypeStruct(w3s.shape, BF16),
                   jax.ShapeDtypeStruct(w2s.shape, BF16)],
        scratch_shapes=[pltpu.VMEM((HEAD_DIM, HEAD_DIM), F32),
                        pltpu.VMEM((rows + CONV_HALO, LANES), F32)],
        compiler_params=pltpu.CompilerParams(
            dimension_semantics=("arbitrary", "arbitrary", "arbitrary"), vmem_limit_bytes=VMEM_LIMIT),
        name="mixer",
    )(cdec, proj3, proj3, proj3, proj3, proj3, proj3, cos_t, sin_t, decay, zeta, xi,
      ret_g, ret_b, cw, cb, cg, cbeta, w1s, w3s, w2s)


def _layer_norm(z, g, b):
    mu = jnp.mean(z, axis=-1, keepdims=True)
    zc = z - mu
    var = jnp.mean(zc * zc, axis=-1, keepdims=True)
    return zc * lax.rsqrt(var + LN_EPS) * g + b


def _outproj_kernel(ret_ref, conv_ref, x_ref, w_ref, g_ref, b_ref, wrh_ref, wrl_ref, br_ref,
                    h_ref, route_ref, counts_ref, carry_ref):
    @pl.when(pl.program_id(0) == 0)
    def _():
        carry_ref[...] = jnp.zeros_like(carry_ref)

    mix = jnp.concatenate([ret_ref[s] for s in range(HEADS)] +
                          [conv_ref[s] for s in range(CONV_GROUPS)], axis=-1)
    y = jnp.dot(mix, w_ref[...], preferred_element_type=F32)
    h = _layer_norm(ALPHA * x_ref[...] + y, g_ref[...], b_ref[...])
    h_ref[...] = h

    hb = h.astype(BF16)
    hl = (h - hb.astype(F32)).astype(BF16)
    wrh = wrh_ref[...]
    logits = (jnp.dot(hb, wrh, preferred_element_type=F32)
              + jnp.dot(hb, wrl_ref[...], preferred_element_type=F32)
              + jnp.dot(hl, wrh, preferred_element_type=F32)) + br_ref[...]

    tm = logits.shape[0]
    lane = lax.broadcasted_iota(jnp.int32, (tm, LANES), 1)
    lane_f = lane.astype(F32)
    neg = jnp.float32(-jnp.inf)
    big = jnp.float32(1e9)
    gmask = (lane >= N_EXPERTS) & (lane < N_EXPERTS + N_GROUPS)
    gl = jnp.where(gmask, logits, neg)
    gmax = jnp.max(gl, axis=-1, keepdims=True)
    gidx = jnp.min(jnp.where(gl == gmax, lane_f, big), axis=-1, keepdims=True) - N_EXPERTS
    g_w = 1.0 / jnp.sum(jnp.where(gmask, jnp.exp(gl - gmax), 0.0), axis=-1, keepdims=True)
    lo = gidx * EXPERTS_PER_GROUP
    emask = (lane_f >= lo) & (lane_f < lo + EXPERTS_PER_GROUP)
    el = jnp.where(emask, logits, neg)
    top1 = jnp.max(el, axis=-1, keepdims=True)
    idx1 = jnp.min(jnp.where(el == top1, lane_f, big), axis=-1, keepdims=True)
    el2 = jnp.where(lane_f == idx1, neg, el)
    top2 = jnp.max(el2, axis=-1, keepdims=True)
    idx2 = jnp.min(jnp.where(el2 == top2, lane_f, big), axis=-1, keepdims=True)
    e2 = jnp.exp(top2 - top1)
    w1 = g_w / (1.0 + e2)
    w2 = w1 * e2

    sel1 = lane_f == idx1
    sel2 = lane_f == idx2
    onehot = jnp.where(sel1, 1.0, 0.0) + jnp.where(sel2, 1.0, 0.0)
    r_i = lax.broadcasted_iota(jnp.int32, (tm, tm), 0)
    c_i = lax.broadcasted_iota(jnp.int32, (tm, tm), 1)
    tri = jnp.where(c_i < r_i, 1.0, 0.0).astype(BF16)
    prefix = jnp.dot(tri, onehot.astype(BF16), preferred_element_type=F32) + carry_ref[0:1, :]
    rank1 = jnp.sum(jnp.where(sel1, prefix, 0.0), axis=-1, keepdims=True)
    rank2 = jnp.sum(jnp.where(sel2, prefix, 0.0), axis=-1, keepdims=True)
    total = carry_ref[0:1, :] + jnp.sum(onehot, axis=0, keepdims=True)
    carry_ref[...] = jnp.broadcast_to(total, carry_ref.shape)
    counts_ref[...] = jnp.broadcast_to(total, counts_ref.shape)

    route = jnp.where(lane == 0, w1, 0.0)
    for k, val in enumerate((w2, idx1, idx2, rank1, rank2), start=1):
        route = jnp.where(lane == k, val, route)
    route_ref[...] = route


def _out_proj(ret3, conv3, x2, w_out_b, ln_g, ln_b, wr_hi, wr_lo, br, tm=512):
    T = x2.shape[0]
    cur = lambda i: (0, i, 0)
    row = lambda i: (i, 0)
    const = lambda i: (0, 0)
    return pl.pallas_call(
        _outproj_kernel,
        grid=(T // tm,),
        in_specs=[pl.BlockSpec((HEADS, tm, LANES), cur),
                  pl.BlockSpec((CONV_GROUPS, tm, LANES), cur),
                  pl.BlockSpec((tm, D_MODEL), row),
                  pl.BlockSpec((D_MODEL, D_MODEL), const),
                  pl.BlockSpec((1, D_MODEL), const),
                  pl.BlockSpec((1, D_MODEL), const),
                  pl.BlockSpec((D_MODEL, LANES), const),
                  pl.BlockSpec((D_MODEL, LANES), const),
                  pl.BlockSpec((1, LANES), const)],
        out_specs=[pl.BlockSpec((tm, D_MODEL), row),
                   pl.BlockSpec((tm, LANES), row),
                   pl.BlockSpec((SUBLANES, LANES), const)],
        out_shape=[jax.ShapeDtypeStruct((T, D_MODEL), F32),
                   jax.ShapeDtypeStruct((T, LANES), F32),
                   jax.ShapeDtypeStruct((SUBLANES, LANES), F32)],
        scratch_shapes=[pltpu.VMEM((SUBLANES, LANES), F32)],
        compiler_params=pltpu.CompilerParams(
            dimension_semantics=("arbitrary",), vmem_limit_bytes=VMEM_LIMIT),
        name="out_proj_ln_route",
    )(ret3, conv3, x2, w_out_b, ln_g, ln_b, wr_hi, wr_lo, br)


MOE_TM = 512
MOE_TB = 512


def _row_copy(src_ref, src_row, dst_ref, dst_row, sem):
    return pltpu.make_async_copy(src_ref.at[pl.ds(src_row, 1), :], dst_ref.at[pl.ds(dst_row, 1), :], sem)


def _experts_kernel(te_ref, nv_ref, src_ref, src_next_ref, h_ref, w1_ref, w3_ref, w2_ref, y_ref, x_ref, sems):
    i = pl.program_id(0)
    last = pl.num_programs(0) - 1
    slot = i % 2
    used = i < nv_ref[0]

    def gather_wait(s):
        pltpu.make_async_copy(h_ref.at[pl.ds(0, MOE_TM), :], x_ref.at[s], sems.at[s]).wait()

    @pl.when(i == 0)
    def _():
        def first(t, carry):
            _row_copy(h_ref, src_ref[0, 0, t], x_ref.at[0], t, sems.at[0]).start()
            return carry
        lax.fori_loop(0, MOE_TM, first, 0, unroll=8)

    @pl.when(used)
    def _():
        gather_wait(slot)
        for t in range(MOE_TM):
            _row_copy(h_ref, src_next_ref[0, 0, t], x_ref.at[1 - slot], t, sems.at[1 - slot]).start()
        xb = x_ref[slot].astype(BF16)
        a1 = jnp.dot(xb, w1_ref[0], preferred_element_type=F32)
        a3 = jnp.dot(xb, w3_ref[0], preferred_element_type=F32)
        act = (a1 * _sigmoid(a1) * a3).astype(BF16)
        y_ref[...] = jnp.dot(act, w2_ref[0], preferred_element_type=F32)

        @pl.when(i == last)
        def _():
            gather_wait(1 - slot)

    @pl.when(jnp.logical_not(used))
    def _():
        @pl.when(i == nv_ref[0])
        def _():
            gather_wait(slot)
        y_ref[...] = jnp.zeros_like(y_ref)


def _experts(tile_expert, n_valid, src_tok, h, w1b, w3b, w2b):
    n_tiles = src_tok.shape[0]
    row = lambda i, te, nv: (i, 0)
    wsel = lambda i, te, nv: (te[i], 0, 0)
    smem_row = lambda f: pl.BlockSpec((1, 1, MOE_TM), f, memory_space=pltpu.SMEM)
    return pl.pallas_call(
        _experts_kernel,
        grid_spec=pltpu.PrefetchScalarGridSpec(
            num_scalar_prefetch=2,
            grid=(n_tiles,),
            in_specs=[smem_row(lambda i, te, nv: (i, 0, 0)),
                      smem_row(lambda i, te, nv: (jnp.minimum(i + 1, n_tiles - 1), 0, 0)),
                      pl.BlockSpec(memory_space=pl.ANY),
                      pl.BlockSpec((1, D_MODEL, D_FF), wsel),
                      pl.BlockSpec((1, D_MODEL, D_FF), wsel),
                      pl.BlockSpec((1, D_FF, D_MODEL), wsel)],
            out_specs=pl.BlockSpec((MOE_TM, D_MODEL), row),
            scratch_shapes=[pltpu.VMEM((2, MOE_TM, D_MODEL), F32),
                            pltpu.SemaphoreType.DMA((2,))]),
        out_shape=jax.ShapeDtypeStruct((n_tiles * MOE_TM, D_MODEL), F32),
        compiler_params=pltpu.CompilerParams(
            dimension_semantics=("arbitrary",), vmem_limit_bytes=VMEM_LIMIT),
        name="moe_experts",
    )(tile_expert, n_valid, src_tok, src_tok, h, w1b, w3b, w2b)


def _combine_kernel(pos_ref, pos_next_ref, route_ref, h_ref, g_ref, b_ref, ys_ref, o_ref,
                    ya0_ref, yb0_ref, ya1_ref, yb1_ref, sems):
    tb = h_ref.shape[0]
    i = pl.program_id(0)
    bufs = ((ya0_ref, yb0_ref), (ya1_ref, yb1_ref))

    def gather_wait(s):
        for k in range(TOP_K):
            pltpu.make_async_copy(ys_ref.at[pl.ds(0, tb), :], bufs[s][k], sems.at[s]).wait()

    @pl.when(i == 0)
    def _():
        def first(t, carry):
            for k in range(TOP_K):
                _row_copy(ys_ref, pos_ref[0, 0, k * tb + t], bufs[0][k], t, sems.at[0]).start()
            return carry
        lax.fori_loop(0, tb, first, 0, unroll=8)

    for par in range(2):
        @pl.when(i % 2 == par)
        def _():
            gather_wait(par)
            for t in range(tb):
                for k in range(TOP_K):
                    _row_copy(ys_ref, pos_next_ref[0, 0, k * tb + t], bufs[1 - par][k], t, sems.at[1 - par]).start()
            route = route_ref[...]
            ffn = route[:, 0:1] * bufs[par][0][...] + route[:, 1:2] * bufs[par][1][...]
            o_ref[...] = _layer_norm(ALPHA * h_ref[...] + ffn, g_ref[...], b_ref[...])

            @pl.when(i == pl.num_programs(0) - 1)
            def _():
                gather_wait(1 - par)


def _combine(pos3, route, h, ln_g, ln_b, ys):
    T = h.shape[0]
    tb = MOE_TB
    row = lambda i: (i, 0)
    const = lambda i: (0, 0)
    return pl.pallas_call(
        _combine_kernel,
        grid=(T // tb,),
        in_specs=[pl.BlockSpec((1, 1, 2 * tb), lambda i: (i, 0, 0), memory_space=pltpu.SMEM),
                  pl.BlockSpec((1, 1, 2 * tb), lambda i: (jnp.minimum(i + 1, T // tb - 1), 0, 0),
                               memory_space=pltpu.SMEM),
                  pl.BlockSpec((tb, LANES), row),
                  pl.BlockSpec((tb, D_MODEL), row),
                  pl.BlockSpec((1, D_MODEL), const),
                  pl.BlockSpec((1, D_MODEL), const),
                  pl.BlockSpec(memory_space=pl.ANY)],
        out_specs=pl.BlockSpec((tb, D_MODEL), row),
        out_shape=jax.ShapeDtypeStruct((T, D_MODEL), F32),
        scratch_shapes=[pltpu.VMEM((tb, D_MODEL), F32)] * (2 * TOP_K) + [pltpu.SemaphoreType.DMA((2,))],
        compiler_params=pltpu.CompilerParams(
            dimension_semantics=("arbitrary",), vmem_limit_bytes=VMEM_LIMIT),
        name="moe_combine_ln",
    )(pos3, pos3, route, h, ln_g, ln_b, ys)


def _routed_moe(h, route, counts, w1b, w3b, w2b, ln_g, ln_b):
    T = h.shape[0]
    n_tiles = (TOP_K * T) // MOE_TM + N_EXPERTS
    i32 = jnp.int32
    cnt = counts[0, :N_EXPERTS].astype(i32)
    ntile = (cnt + MOE_TM - 1) // MOE_TM
    tile_end = jnp.cumsum(ntile)
    n_valid = tile_end[-1:]
    seg_start = (tile_end - ntile) * MOE_TM
    info = route[:, 2:6].astype(i32)
    eids = jnp.arange(N_EXPERTS, dtype=i32)
    start_of = lambda e: jnp.sum(jnp.where(e[:, None] == eids[None, :], seg_start[None, :], 0), axis=1)
    pos1 = start_of(info[:, 0]) + info[:, 2]
    pos2 = start_of(info[:, 1]) + info[:, 3]
    pos3 = jnp.concatenate([pos1.reshape(T // MOE_TB, 1, MOE_TB), pos2.reshape(T // MOE_TB, 1, MOE_TB)], axis=2)
    tile_ids = jnp.minimum(jnp.arange(n_tiles, dtype=i32), n_valid[0] - 1)
    tile_expert = jnp.sum((tile_ids[:, None] >= tile_end[None, :]).astype(i32), axis=1)

    order = jnp.argsort(jnp.concatenate([pos1, pos2])).astype(i32)
    n_rows = n_tiles * MOE_TM
    padded = jnp.concatenate([jnp.zeros((n_rows,), i32), jnp.where(order >= T, order - T, order),
                              jnp.zeros((2 * n_rows - TOP_K * T,), i32)])
    shift = seg_start - (jnp.cumsum(cnt) - cnt)
    row_expert = jnp.repeat(tile_expert, MOE_TM)
    src_tok = jnp.zeros((n_rows,), i32)
    for e in range(N_EXPERTS):
        view = lax.dynamic_slice(padded, (n_rows - shift[e],), (n_rows,))
        src_tok = jnp.where(row_expert == e, view, src_tok)

    ys = _experts(tile_expert, n_valid.astype(i32), src_tok.reshape(n_tiles, 1, MOE_TM), h, w1b, w3b, w2b)
    return _combine(pos3, route, h, ln_g, ln_b, ys)


def _retention_constants():
    hh = np.arange(HEADS, dtype=np.float64)
    log_gamma = np.log1p(-np.exp2(-5.0 - hh))
    pos = np.arange(CHUNK, dtype=np.float64)
    diff = pos[:, None] - pos[None, :]
    causal = diff >= 0
    decay = np.where(causal[None], np.exp(log_gamma[:, None, None] * np.where(causal, diff, 0.0)[None]), 0.0)
    zeta = np.exp(log_gamma[:, None] * (CHUNK - 1 - pos)[None])
    xi = np.exp(log_gamma[:, None] * (pos + 1.0)[None])
    cdec = np.exp(log_gamma * CHUNK)
    bc = lambda t: np.broadcast_to(t[:, :, None], (HEADS, CHUNK, LANES))
    return (jnp.asarray(cdec, F32), jnp.asarray(decay, F32),
            jnp.asarray(bc(zeta), F32), jnp.asarray(bc(xi), F32))


def _rope_tables(seq):
    inv = ROPE_BASE ** (-jnp.arange(0, HEAD_DIM, 2, dtype=F32) / HEAD_DIM)
    ang = jnp.arange(seq, dtype=F32)[:, None] * inv[None, :]
    cos = jnp.repeat(jnp.cos(ang), 2, axis=-1)
    sin = jnp.sin(ang)
    sin_signed = jnp.stack([-sin, sin], axis=-1).reshape(seq, HEAD_DIM)
    return cos, sin_signed


def kernel(x, w_in, ret_norm_g, ret_norm_b, conv_w, conv_b, conv_norm_g, conv_norm_b, w_out, ln1_g, ln1_b, w_router_group, b_router_group, w_router_expert, b_router_expert, w1, w3, w2, ln2_g, ln2_b):
    batch, seq, _ = x.shape
    T = batch * seq
    x2 = x.reshape(T, D_MODEL)

    proj3 = _in_proj(x2, w_in.astype(BF16))

    cos_t, sin_t = _rope_tables(seq)
    per_head = lambda t: t.reshape(HEADS, 1, LANES)
    cw = jnp.transpose(conv_w.reshape(CONV_TAPS, CONV_GROUPS, LANES), (1, 0, 2))
    ret3, conv3, w1b, w3b, w2b = _mixer(
        proj3, cos_t, sin_t, _retention_constants(), per_head(ret_norm_g), per_head(ret_norm_b), cw,
        per_head(conv_b), per_head(conv_norm_g), per_head(conv_norm_b), w1, w3, w2, batch, seq)

    wr = jnp.concatenate([jnp.transpose(w_router_expert, (1, 0, 2)).reshape(D_MODEL, N_EXPERTS),
                          w_router_group], axis=1)
    wr = jnp.pad(wr, ((0, 0), (0, LANES - wr.shape[1])))
    br = jnp.pad(jnp.concatenate([b_router_expert.reshape(-1), b_router_group]),
                 (0, LANES - N_EXPERTS - N_GROUPS)).reshape(1, LANES)
    wr_hi = wr.astype(BF16)
    wr_lo = (wr - wr_hi.astype(F32)).astype(BF16)

    h, route, counts = _out_proj(ret3, conv3, x2, w_out.astype(BF16), ln1_g.reshape(1, -1),
                                 ln1_b.reshape(1, -1), wr_hi, wr_lo, br)

    y = _routed_moe(h, route, counts, w1b.reshape(w1.shape), w3b.reshape(w3.shape), w2b.reshape(w2.shape),
                    ln2_g.reshape(1, -1), ln2_b.reshape(1, -1))
    return y.reshape(batch, seq, D_MODEL)
```

```python
import functools
import math

import numpy as np
import jax
import jax.numpy as jnp
from jax import lax
from jax.experimental import pallas as pl
from jax.experimental.pallas import tpu as pltpu

D_MODEL = 2048
RET_WIDTH = D_MODEL // 2
CONV_WIDTH = D_MODEL - RET_WIDTH
HEADS = 8
HEAD_DIM = RET_WIDTH // HEADS
CONV_GROUPS = 8
CONV_TAPS = 31
CHUNK = 128
ROPE_BASE = 10000.0
IN_COLS = 4 * RET_WIDTH + 2 * CONV_WIDTH
N_GROUPS = 4
EXPERTS_PER_GROUP = 4
N_EXPERTS = N_GROUPS * EXPERTS_PER_GROUP
TOP_K = 2
D_FF = D_MODEL // 2
LN_EPS = 1e-5
ALPHA = 2.0 ** 0.25

LANES = 128
SUBLANES = 8
VMEM_LIMIT = 56 * 1024 * 1024
CONV_HALO = 32

F32 = jnp.float32
BF16 = jnp.bfloat16


def _sigmoid(x):
    return 1.0 / (1.0 + jnp.exp(-x))


def _lane_norm(x, g, b):
    mu = jnp.mean(x, axis=-1, keepdims=True)
    xc = x - mu
    var = jnp.mean(xc * xc, axis=-1, keepdims=True)
    return xc * lax.rsqrt(var + LN_EPS) * g + b


def _inproj_kernel(x_ref, w_ref, o_ref):
    res = jnp.dot(x_ref[...].astype(BF16), w_ref[...], preferred_element_type=F32)
    for s in range(o_ref.shape[0]):
        o_ref[s] = res[:, s * LANES:(s + 1) * LANES].astype(BF16)


def _in_proj(x2, w_in_b, tm=512, tn=2048):
    T = x2.shape[0]
    nblk = tn // LANES
    return pl.pallas_call(
        _inproj_kernel,
        grid=(IN_COLS // tn, T // tm),
        in_specs=[pl.BlockSpec((tm, D_MODEL), lambda j, i: (i, 0)),
                  pl.BlockSpec((D_MODEL, tn), lambda j, i: (0, j))],
        out_specs=pl.BlockSpec((nblk, tm, LANES), lambda j, i: (j, i, 0)),
        out_shape=jax.ShapeDtypeStruct((IN_COLS // LANES, T, LANES), BF16),
        compiler_params=pltpu.CompilerParams(
            dimension_semantics=("arbitrary", "arbitrary"), vmem_limit_bytes=VMEM_LIMIT),
        name="in_proj",
    )(x2, w_in_b)


def _rotary(t, cos, sin_signed, even_lane):
    partner = jnp.where(even_lane, pltpu.roll(t, LANES - 1, 1), pltpu.roll(t, 1, 1))
    return t * cos + partner * sin_signed


def _mixer_kernel(cdec_ref, q_ref, k_ref, v_ref, gate_ref, ga_ref, gb_ref, cos_ref, sin_ref,
                  decay_ref, zeta_ref, xi_ref, rg_ref, rb_ref, cw_ref, cb_ref, cg_ref, cbeta_ref,
                  w1f_ref, w3f_ref, w2f_ref,
                  ret_ref, conv_ref, w1b_ref, w3b_ref, w2b_ref, state_ref, uext_ref, *, rows):
    w1b_ref[...] = w1f_ref[...].astype(BF16)
    w3b_ref[...] = w3f_ref[...].astype(BF16)
    w2b_ref[...] = w2f_ref[...].astype(BF16)

    h = pl.program_id(1)
    n = pl.program_id(2)

    @pl.when(n == 0)
    def _():
        state_ref[...] = jnp.zeros_like(state_ref)
        uext_ref[0:CONV_HALO, :] = jnp.zeros((CONV_HALO, LANES), F32)

    even_lane = (lax.broadcasted_iota(jnp.int32, (CHUNK, LANES), 1) % 2) == 0
    cdec = cdec_ref[h]
    decay = decay_ref[0]
    zeta = zeta_ref[0]
    xi = xi_ref[0]
    rg = rg_ref[0]
    rb = rb_ref[0]
    scale = HEAD_DIM ** -0.5

    state = state_ref[...]
    for c in range(rows // CHUNK):
        rs = slice(c * CHUNK, (c + 1) * CHUNK)
        cos = cos_ref[rs, :]
        sin = sin_ref[rs, :]
        q = _rotary(q_ref[0, rs, :].astype(F32), cos, sin, even_lane)
        k = _rotary(k_ref[0, rs, :].astype(F32), cos, sin, even_lane) * scale
        v = v_ref[0, rs, :]
        scores = lax.dot_general(q.astype(BF16), k.astype(BF16), (((1,), (1,)), ((), ())),
                                 preferred_element_type=F32) * decay
        inner = jnp.dot(scores.astype(BF16), v, preferred_element_type=F32)
        cross = jnp.dot((q * xi).astype(BF16), state.astype(BF16), preferred_element_type=F32)
        kz_t = jnp.transpose(k * zeta).astype(BF16)
        state = cdec * state + jnp.dot(kz_t, v, preferred_element_type=F32)
        y = _lane_norm(inner + cross, rg, rb)
        g = gate_ref[0, rs, :].astype(F32)
        ret_ref[0, rs, :] = (g * _sigmoid(g) * y).astype(BF16)
    state_ref[...] = state

    a = ga_ref[0].astype(F32)
    b = gb_ref[0].astype(F32)
    uext_ref[CONV_HALO:CONV_HALO + rows, :] = a * _sigmoid(b)

    cb = cb_ref[0]
    cg = cg_ref[0]
    cbeta = cbeta_ref[0]
    blk = 64
    first = CONV_HALO - (CONV_TAPS - 1)
    for r0 in range(0, rows, blk):
        acc = jnp.zeros((blk, LANES), F32)
        for j in range(CONV_TAPS):
            acc = acc + cw_ref[0, j:j + 1, :] * uext_ref[r0 + first + j:r0 + first + j + blk, :]
        y = _lane_norm(acc + cb, cg, cbeta)
        conv_ref[0, r0:r0 + blk, :] = (y * _sigmoid(y)).astype(BF16)
    uext_ref[0:CONV_HALO, :] = uext_ref[rows:rows + CONV_HALO, :]


def _mixer(proj3, cos_t, sin_t, consts, ret_g, ret_b, cw, cb, cg, cbeta, w1, w3, w2, batch, seq, rows=1024):
    T = batch * seq
    ns = seq // rows
    cdec, decay, zeta, xi = consts
    steps = batch * HEADS * ns
    w1s = w1.reshape(-1, w1.shape[-1])
    w3s = w3.reshape(-1, w3.shape[-1])
    w2s = w2.reshape(-1, w2.shape[-1])
    step_of = lambda b, h, n: ((b * HEADS + h) * ns + n, 0)
    wspec = lambda w: pl.BlockSpec((w.shape[0] // steps, w.shape[1]), step_of)
    for w in (w1s, w3s, w2s):
        assert w.shape[0] % (steps * 16) == 0

    def pspec(off):
        return pl.BlockSpec((1, rows, LANES), lambda b, h, n: (off + h, b * ns + n, 0))

    def hspec(d1):
        return pl.BlockSpec((1, d1, LANES), lambda b, h, n: (h, 0, 0))

    return pl.pallas_call(
        functools.partial(_mixer_kernel, rows=rows),
        grid=(batch, HEADS, ns),
        in_specs=[pl.BlockSpec(memory_space=pltpu.SMEM),
                  pspec(0), pspec(8), pspec(16), pspec(24), pspec(32), pspec(40),
                  pl.BlockSpec((rows, LANES), lambda b, h, n: (n, 0)),
                  pl.BlockSpec((rows, LANES), lambda b, h, n: (n, 0)),
                  hspec(CHUNK), hspec(CHUNK), hspec(CHUNK),
                  hspec(1), hspec(1), hspec(CONV_TAPS), hspec(1), hspec(1), hspec(1),
                  wspec(w1s), wspec(w3s), wspec(w2s)],
        out_specs=[pl.BlockSpec((1, rows, LANES), lambda b, h, n: (h, b * ns + n, 0)),
                   pl.BlockSpec((1, rows, LANES), lambda b, h, n: (h, b * ns + n, 0)),
                   wspec(w1s), wspec(w3s), wspec(w2s)],
        out_shape=[jax.ShapeDtypeStruct((HEADS, T, LANES), BF16),
                   jax.ShapeDtypeStruct((CONV_GROUPS, T, LANES), BF16),
                   jax.ShapeDtypeStruct(w1s.shape, BF16),
                   jax.ShapeDtypeStruct(w3s.shape, BF16),
                   jax.ShapeDtypeStruct(w2s.shape, BF16)],
        scratch_shapes=[pltpu.VMEM((HEAD_DIM, HEAD_DIM), F32),
                        pltpu.VMEM((rows + CONV_HALO, LANES), F32)],
        compiler_params=pltpu.CompilerParams(
            dimension_semantics=("arbitrary", "arbitrary", "arbitrary"), vmem_limit_bytes=VMEM_LIMIT),
        name="mixer",
    )(cdec, proj3, proj3, proj3, proj3, proj3, proj3, cos_t, sin_t, decay, zeta, xi,
      ret_g, ret_b, cw, cb, cg, cbeta, w1s, w3s, w2s)


def _layer_norm(z, g, b):
    mu = jnp.mean(z, axis=-1, keepdims=True)
    zc = z - mu
    var = jnp.mean(zc * zc, axis=-1, keepdims=True)
    return zc * lax.rsqrt(var + LN_EPS) * g + b


def _outproj_kernel(ret_ref, conv_ref, x_ref, w_ref, g_ref, b_ref, wrh_ref, wrl_ref, br_ref,
                    h_ref, route_ref, counts_ref, carry_ref):
    @pl.when(pl.program_id(0) == 0)
    def _():
        carry_ref[...] = jnp.zeros_like(carry_ref)

    mix = jnp.concatenate([ret_ref[s] for s in range(HEADS)] +
                          [conv_ref[s] for s in range(CONV_GROUPS)], axis=-1)
    y = jnp.dot(mix, w_ref[...], preferred_element_type=F32)
    h = _layer_norm(ALPHA * x_ref[...] + y, g_ref[...], b_ref[...])
    h_ref[...] = h

    hb = h.astype(BF16)
    hl = (h - hb.astype(F32)).astype(BF16)
    wrh = wrh_ref[...]
    logits = (jnp.dot(hb, wrh, preferred_element_type=F32)
              + jnp.dot(hb, wrl_ref[...], preferred_element_type=F32)
              + jnp.dot(hl, wrh, preferred_element_type=F32)) + br_ref[...]

    tm = logits.shape[0]
    lane = lax.broadcasted_iota(jnp.int32, (tm, LANES), 1)
    lane_f = lane.astype(F32)
    neg = jnp.float32(-jnp.inf)
    big = jnp.float32(1e9)
    gmask = (lane >= N_EXPERTS) & (lane < N_EXPERTS + N_GROUPS)
    gl = jnp.where(gmask, logits, neg)
    gmax = jnp.max(gl, axis=-1, keepdims=True)
    gidx = jnp.min(jnp.where(gl == gmax, lane_f, big), axis=-1, keepdims=True) - N_EXPERTS
    g_w = 1.0 / jnp.sum(jnp.where(gmask, jnp.exp(gl - gmax), 0.0), axis=-1, keepdims=True)
    lo = gidx * EXPERTS_PER_GROUP
    emask = (lane_f >= lo) & (lane_f < lo + EXPERTS_PER_GROUP)
    el = jnp.where(emask, logits, neg)
    top1 = jnp.max(el, axis=-1, keepdims=True)
    idx1 = jnp.min(jnp.where(el == top1, lane_f, big), axis=-1, keepdims=True)
    el2 = jnp.where(lane_f == idx1, neg, el)
    top2 = jnp.max(el2, axis=-1, keepdims=True)
    idx2 = jnp.min(jnp.where(el2 == top2, lane_f, big), axis=-1, keepdims=True)
    e2 = jnp.exp(top2 - top1)
    w1 = g_w / (1.0 + e2)
    w2 = w1 * e2

    sel1 = lane_f == idx1
    sel2 = lane_f == idx2
    onehot = jnp.where(sel1, 1.0, 0.0) + jnp.where(sel2, 1.0, 0.0)
    r_i = lax.broadcasted_iota(jnp.int32, (tm, tm), 0)
    c_i = lax.broadcasted_iota(jnp.int32, (tm, tm), 1)
    tri = jnp.where(c_i < r_i, 1.0, 0.0).astype(BF16)
    prefix = jnp.dot(tri, onehot.astype(BF16), preferred_element_type=F32) + carry_ref[0:1, :]
    rank1 = jnp.sum(jnp.where(sel1, prefix, 0.0), axis=-1, keepdims=True)
    rank2 = jnp.sum(jnp.where(sel2, prefix, 0.0), axis=-1, keepdims=True)
    total = carry_ref[0:1, :] + jnp.sum(onehot, axis=0, keepdims=True)
    carry_ref[...] = jnp.broadcast_to(total, carry_ref.shape)
    counts_ref[...] = jnp.broadcast_to(total, counts_ref.shape)

    route = jnp.where(lane == 0, w1, 0.0)
    for k, val in enumerate((w2, idx1, idx2, rank1, rank2), start=1):
        route = jnp.where(lane == k, val, route)
    route_ref[...] = route


def _out_proj(ret3, conv3, x2, w_out_b, ln_g, ln_b, wr_hi, wr_lo, br, tm=512):
    T = x2.shape[0]
    cur = lambda i: (0, i, 0)
    row = lambda i: (i, 0)
    const = lambda i: (0, 0)
    return pl.pallas_call(
        _outproj_kernel,
        grid=(T // tm,),
        in_specs=[pl.BlockSpec((HEADS, tm, LANES), cur),
                  pl.BlockSpec((CONV_GROUPS, tm, LANES), cur),
                  pl.BlockSpec((tm, D_MODEL), row),
                  pl.BlockSpec((D_MODEL, D_MODEL), const),
                  pl.BlockSpec((1, D_MODEL), const),
                  pl.BlockSpec((1, D_MODEL), const),
                  pl.BlockSpec((D_MODEL, LANES), const),
                  pl.BlockSpec((D_MODEL, LANES), const),
                  pl.BlockSpec((1, LANES), const)],
        out_specs=[pl.BlockSpec((tm, D_MODEL), row),
                   pl.BlockSpec((tm, LANES), row),
                   pl.BlockSpec((SUBLANES, LANES), const)],
        out_shape=[jax.ShapeDtypeStruct((T, D_MODEL), F32),
                   jax.ShapeDtypeStruct((T, LANES), F32),
                   jax.ShapeDtypeStruct((SUBLANES, LANES), F32)],
        scratch_shapes=[pltpu.VMEM((SUBLANES, LANES), F32)],
        compiler_params=pltpu.CompilerParams(
            dimension_semantics=("arbitrary",), vmem_limit_bytes=VMEM_LIMIT),
        name="out_proj_ln_route",
    )(ret3, conv3, x2, w_out_b, ln_g, ln_b, wr_hi, wr_lo, br)


MOE_TM = 512
MOE_TB = 512


def _row_copy(src_ref, src_row, dst_ref, dst_row, sem):
    return pltpu.make_async_copy(src_ref.at[pl.ds(src_row, 1), :], dst_ref.at[pl.ds(dst_row, 1), :], sem)


def _experts_kernel(te_ref, nv_ref, src_ref, src_next_ref, h_ref, w1_ref, w3_ref, w2_ref, y_ref,
                    x0_ref, x1_ref, sems):
    i = pl.program_id(0)
    last = pl.num_programs(0) - 1
    used = i < nv_ref[0]
    bufs = (x0_ref, x1_ref)

    def gather_wait(s):
        pltpu.make_async_copy(h_ref.at[pl.ds(0, MOE_TM), :], bufs[s], sems.at[s]).wait()

    @pl.when(i == 0)
    def _():
        def first(t, carry):
            _row_copy(h_ref, src_ref[0, 0, t], x0_ref, t, sems.at[0]).start()
            return carry
        lax.fori_loop(0, MOE_TM, first, 0, unroll=8)

    for par in range(2):
        @pl.when(jnp.logical_and(used, i % 2 == par))
        def _():
            gather_wait(par)
            xb = bufs[par][...].astype(BF16)
            for t in range(MOE_TM):
                _row_copy(h_ref, src_next_ref[0, 0, t], bufs[1 - par], t, sems.at[1 - par]).start()
            a1 = jnp.dot(xb, w1_ref[0], preferred_element_type=F32)
            a3 = jnp.dot(xb, w3_ref[0], preferred_element_type=F32)
            act = (a1 * _sigmoid(a1) * a3).astype(BF16)
            y_ref[...] = jnp.dot(act, w2_ref[0], preferred_element_type=F32)

            @pl.when(i == last)
            def _():
                gather_wait(1 - par)

    @pl.when(jnp.logical_not(used))
    def _():
        for par in range(2):
            @pl.when(jnp.logical_and(i == nv_ref[0], i % 2 == par))
            def _():
                gather_wait(par)
        y_ref[...] = jnp.zeros_like(y_ref)


def _experts(tile_expert, n_valid, src_tok, h, w1b, w3b, w2b):
    n_tiles = src_tok.shape[0]
    row = lambda i, te, nv: (i, 0)
    wsel = lambda i, te, nv: (te[i], 0, 0)
    smem_row = lambda f: pl.BlockSpec((1, 1, MOE_TM), f, memory_space=pltpu.SMEM)
    return pl.pallas_call(
        _experts_kernel,
        grid_spec=pltpu.PrefetchScalarGridSpec(
            num_scalar_prefetch=2,
            grid=(n_tiles,),
            in_specs=[smem_row(lambda i, te, nv: (i, 0, 0)),
                      smem_row(lambda i, te, nv: (jnp.minimum(i + 1, n_tiles - 1), 0, 0)),
                      pl.BlockSpec(memory_space=pl.ANY),
                      pl.BlockSpec((1, D_MODEL, D_FF), wsel),
                      pl.BlockSpec((1, D_MODEL, D_FF), wsel),
                      pl.BlockSpec((1, D_FF, D_MODEL), wsel)],
            out_specs=pl.BlockSpec((MOE_TM, D_MODEL), row),
            scratch_shapes=[pltpu.VMEM((MOE_TM, D_MODEL), F32),
                            pltpu.VMEM((MOE_TM, D_MODEL), F32),
                            pltpu.SemaphoreType.DMA((2,))]),
        out_shape=jax.ShapeDtypeStruct((n_tiles * MOE_TM, D_MODEL), F32),
        compiler_params=pltpu.CompilerParams(
            dimension_semantics=("arbitrary",), vmem_limit_bytes=VMEM_LIMIT),
        name="moe_experts",
    )(tile_expert, n_valid, src_tok, src_tok, h, w1b, w3b, w2b)


def _combine_kernel(pos_ref, pos_next_ref, route_ref, h_ref, g_ref, b_ref, ys_ref, o_ref,
                    ya0_ref, yb0_ref, ya1_ref, yb1_ref, sems):
    tb = h_ref.shape[0]
    i = pl.program_id(0)
    bufs = ((ya0_ref, yb0_ref), (ya1_ref, yb1_ref))

    def gather_wait(s):
        for k in range(TOP_K):
            pltpu.make_async_copy(ys_ref.at[pl.ds(0, tb), :], bufs[s][k], sems.at[s]).wait()

    @pl.when(i == 0)
    def _():
        def first(t, carry):
            for k in range(TOP_K):
                _row_copy(ys_ref, pos_ref[0, 0, k * tb + t], bufs[0][k], t, sems.at[0]).start()
            return carry
        lax.fori_loop(0, tb, first, 0, unroll=8)

    for par in range(2):
        @pl.when(i % 2 == par)
        def _():
            gather_wait(par)
            for t in range(tb):
                for k in range(TOP_K):
                    _row_copy(ys_ref, pos_next_ref[0, 0, k * tb + t], bufs[1 - par][k], t, sems.at[1 - par]).start()
            route = route_ref[...]
            ffn = route[:, 0:1] * bufs[par][0][...] + route[:, 1:2] * bufs[par][1][...]
            o_ref[...] = _layer_norm(ALPHA * h_ref[...] + ffn, g_ref[...], b_ref[...])

            @pl.when(i == pl.num_programs(0) - 1)
            def _():
                gather_wait(1 - par)


def _combine(pos3, route, h, ln_g, ln_b, ys):
    T = h.shape[0]
    tb = MOE_TB
    row = lambda i: (i, 0)
    const = lambda i: (0, 0)
    return pl.pallas_call(
        _combine_kernel,
        grid=(T // tb,),
        in_specs=[pl.BlockSpec((1, 1, 2 * tb), lambda i: (i, 0, 0), memory_space=pltpu.SMEM),
                  pl.BlockSpec((1, 1, 2 * tb), lambda i: (jnp.minimum(i + 1, T // tb - 1), 0, 0),
                               memory_space=pltpu.SMEM),
                  pl.BlockSpec((tb, LANES), row),
                  pl.BlockSpec((tb, D_MODEL), row),
                  pl.BlockSpec((1, D_MODEL), const),
                  pl.BlockSpec((1, D_MODEL), const),
                  pl.BlockSpec(memory_space=pl.ANY)],
        out_specs=pl.BlockSpec((tb, D_MODEL), row),
        out_shape=jax.ShapeDtypeStruct((T, D_MODEL), F32),
        scratch_shapes=[pltpu.VMEM((tb, D_MODEL), F32)] * (2 * TOP_K) + [pltpu.SemaphoreType.DMA((2,))],
        compiler_params=pltpu.CompilerParams(
            dimension_semantics=("arbitrary",), vmem_limit_bytes=VMEM_LIMIT),
        name="moe_combine_ln",
    )(pos3, pos3, route, h, ln_g, ln_b, ys)


def _routed_moe(h, route, counts, w1b, w3b, w2b, ln_g, ln_b):
    T = h.shape[0]
    n_tiles = (TOP_K * T) // MOE_TM + N_EXPERTS
    i32 = jnp.int32
    cnt = counts[0, :N_EXPERTS].astype(i32)
    ntile = (cnt + MOE_TM - 1) // MOE_TM
    tile_end = jnp.cumsum(ntile)
    n_valid = tile_end[-1:]
    seg_start = (tile_end - ntile) * MOE_TM
    info = route[:, 2:6].astype(i32)
    eids = jnp.arange(N_EXPERTS, dtype=i32)
    start_of = lambda e: jnp.sum(jnp.where(e[:, None] == eids[None, :], seg_start[None, :], 0), axis=1)
    pos1 = start_of(info[:, 0]) + info[:, 2]
    pos2 = start_of(info[:, 1]) + info[:, 3]
    pos3 = jnp.concatenate([pos1.reshape(T // MOE_TB, 1, MOE_TB), pos2.reshape(T // MOE_TB, 1, MOE_TB)], axis=2)
    tile_ids = jnp.minimum(jnp.arange(n_tiles, dtype=i32), n_valid[0] - 1)
    tile_expert = jnp.sum((tile_ids[:, None] >= tile_end[None, :]).astype(i32), axis=1)

    order = jnp.argsort(jnp.concatenate([pos1, pos2])).astype(i32)
    n_rows = n_tiles * MOE_TM
    padded = jnp.concatenate([jnp.zeros((n_rows,), i32), jnp.where(order >= T, order - T, order),
                              jnp.zeros((2 * n_rows - TOP_K * T,), i32)])
    shift = seg_start - (jnp.cumsum(cnt) - cnt)
    row_expert = jnp.repeat(tile_expert, MOE_TM)
    src_tok = jnp.zeros((n_rows,), i32)
    for e in range(N_EXPERTS):
        view = lax.dynamic_slice(padded, (n_rows - shift[e],), (n_rows,))
        src_tok = jnp.where(row_expert == e, view, src_tok)

    ys = _experts(tile_expert, n_valid.astype(i32), src_tok.reshape(n_tiles, 1, MOE_TM), h, w1b, w3b, w2b)
    return _combine(pos3, route, h, ln_g, ln_b, ys)


def _retention_constants():
    hh = np.arange(HEADS, dtype=np.float64)
    log_gamma = np.log1p(-np.exp2(-5.0 - hh))
    pos = np.arange(CHUNK, dtype=np.float64)
    diff = pos[:, None] - pos[None, :]
    causal = diff >= 0
    decay = np.where(causal[None], np.exp(log_gamma[:, None, None] * np.where(causal, diff, 0.0)[None]), 0.0)
    zeta = np.exp(log_gamma[:, None] * (CHUNK - 1 - pos)[None])
    xi = np.exp(log_gamma[:, None] * (pos + 1.0)[None])
    cdec = np.exp(log_gamma * CHUNK)
    bc = lambda t: np.broadcast_to(t[:, :, None], (HEADS, CHUNK, LANES))
    return (jnp.asarray(cdec, F32), jnp.asarray(decay, F32),
            jnp.asarray(bc(zeta), F32), jnp.asarray(bc(xi), F32))


def _rope_tables(seq):
    inv = ROPE_BASE ** (-jnp.arange(0, HEAD_DIM, 2, dtype=F32) / HEAD_DIM)
    ang = jnp.arange(seq, dtype=F32)[:, None] * inv[None, :]
    cos = jnp.repeat(jnp.cos(ang), 2, axis=-1)
    sin = jnp.sin(ang)
    sin_signed = jnp.stack([-sin, sin], axis=-1).reshape(seq, HEAD_DIM)
    return cos, sin_signed


def kernel(x, w_in, ret_norm_g, ret_norm_b, conv_w, conv_b, conv_norm_g, conv_norm_b, w_out, ln1_g, ln1_b, w_router_group, b_router_group, w_router_expert, b_router_expert, w1, w3, w2, ln2_g, ln2_b):
    batch, seq, _ = x.shape
    T = batch * seq
    x2 = x.reshape(T, D_MODEL)

    proj3 = _in_proj(x2, w_in.astype(BF16))

    cos_t, sin_t = _rope_tables(seq)
    per_head = lambda t: t.reshape(HEADS, 1, LANES)
    cw = jnp.transpose(conv_w.reshape(CONV_TAPS, CONV_GROUPS, LANES), (1, 0, 2))
    ret3, conv3, w1b, w3b, w2b = _mixer(
        proj3, cos_t, sin_t, _retention_constants(), per_head(ret_norm_g), per_head(ret_norm_b), cw,
        per_head(conv_b), per_head(conv_norm_g), per_head(conv_norm_b), w1, w3, w2, batch, seq)

    wr = jnp.concatenate([jnp.transpose(w_router_expert, (1, 0, 2)).reshape(D_MODEL, N_EXPERTS),
                          w_router_group], axis=1)
    wr = jnp.pad(wr, ((0, 0), (0, LANES - wr.shape[1])))
    br = jnp.pad(jnp.concatenate([b_router_expert.reshape(-1), b_router_group]),
                 (0, LANES - N_EXPERTS - N_GROUPS)).reshape(1, LANES)
    wr_hi = wr.astype(BF16)
    wr_lo = (wr - wr_hi.astype(F32)).astype(BF16)

    h, route, counts = _out_proj(ret3, conv3, x2, w_out.astype(BF16), ln1_g.reshape(1, -1),
                                 ln1_b.reshape(1, -1), wr_hi, wr_lo, br)

    y = _routed_moe(h, route, counts, w1b.reshape(w1.shape), w3b.reshape(w3.shape), w2b.reshape(w2.shape),
                    ln2_g.reshape(1, -1), ln2_b.reshape(1, -1))
    return y.reshape(batch, seq, D_MODEL)
```

```python
import functools
import math

import numpy as np
import jax
import jax.numpy as jnp
from jax import lax
from jax.experimental import pallas as pl
from jax.experimental.pallas import tpu as pltpu

D_MODEL = 2048
RET_WIDTH = D_MODEL // 2
CONV_WIDTH = D_MODEL - RET_WIDTH
HEADS = 8
HEAD_DIM = RET_WIDTH // HEADS
CONV_GROUPS = 8
CONV_TAPS = 31
CHUNK = 128
ROPE_BASE = 10000.0
IN_COLS = 4 * RET_WIDTH + 2 * CONV_WIDTH
N_GROUPS = 4
EXPERTS_PER_GROUP = 4
N_EXPERTS = N_GROUPS * EXPERTS_PER_GROUP
TOP_K = 2
D_FF = D_MODEL // 2
LN_EPS = 1e-5
ALPHA = 2.0 ** 0.25

LANES = 128
SUBLANES = 8
VMEM_LIMIT = 56 * 1024 * 1024
CONV_HALO = 32

F32 = jnp.float32
BF16 = jnp.bfloat16


def _sigmoid(x):
    return 1.0 / (1.0 + jnp.exp(-x))


def _lane_norm(x, g, b):
    mu = jnp.mean(x, axis=-1, keepdims=True)
    xc = x - mu
    var = jnp.mean(xc * xc, axis=-1, keepdims=True)
    return xc * lax.rsqrt(var + LN_EPS) * g + b


def _inproj_kernel(x_ref, w_ref, o_ref):
    res = jnp.dot(x_ref[...].astype(BF16), w_ref[...], preferred_element_type=F32)
    for s in range(o_ref.shape[0]):
        o_ref[s] = res[:, s * LANES:(s + 1) * LANES].astype(BF16)


def _in_proj(x2, w_in_b, tm=512, tn=2048):
    T = x2.shape[0]
    nblk = tn // LANES
    return pl.pallas_call(
        _inproj_kernel,
        grid=(IN_COLS // tn, T // tm),
        in_specs=[pl.BlockSpec((tm, D_MODEL), lambda j, i: (i, 0)),
                  pl.BlockSpec((D_MODEL, tn), lambda j, i: (0, j))],
        out_specs=pl.BlockSpec((nblk, tm, LANES), lambda j, i: (j, i, 0)),
        out_shape=jax.ShapeDtypeStruct((IN_COLS // LANES, T, LANES), BF16),
        compiler_params=pltpu.CompilerParams(
            dimension_semantics=("arbitrary", "arbitrary"), vmem_limit_bytes=VMEM_LIMIT),
        name="in_proj",
    )(x2, w_in_b)


def _rotary(t, cos, sin_signed, even_lane):
    partner = jnp.where(even_lane, pltpu.roll(t, LANES - 1, 1), pltpu.roll(t, 1, 1))
    return t * cos + partner * sin_signed


def _mixer_kernel(cdec_ref, q_ref, k_ref, v_ref, gate_ref, ga_ref, gb_ref, cos_ref, sin_ref,
                  decay_ref, zeta_ref, xi_ref, rg_ref, rb_ref, cw_ref, cb_ref, cg_ref, cbeta_ref,
                  w1f_ref, w3f_ref, w2f_ref,
                  ret_ref, conv_ref, w1b_ref, w3b_ref, w2b_ref, state_ref, uext_ref, *, rows):
    w1b_ref[...] = w1f_ref[...].astype(BF16)
    w3b_ref[...] = w3f_ref[...].astype(BF16)
    w2b_ref[...] = w2f_ref[...].astype(BF16)

    h = pl.program_id(1)
    n = pl.program_id(2)

    @pl.when(n == 0)
    def _():
        state_ref[...] = jnp.zeros_like(state_ref)
        uext_ref[0:CONV_HALO, :] = jnp.zeros((CONV_HALO, LANES), F32)

    even_lane = (lax.broadcasted_iota(jnp.int32, (CHUNK, LANES), 1) % 2) == 0
    cdec = cdec_ref[h]
    decay = decay_ref[0]
    zeta = zeta_ref[0]
    xi = xi_ref[0]
    rg = rg_ref[0]
    rb = rb_ref[0]
    scale = HEAD_DIM ** -0.5

    state = state_ref[...]
    for c in range(rows // CHUNK):
        rs = slice(c * CHUNK, (c + 1) * CHUNK)
        cos = cos_ref[rs, :]
        sin = sin_ref[rs, :]
        q = _rotary(q_ref[0, rs, :].astype(F32), cos, sin, even_lane)
        k = _rotary(k_ref[0, rs, :].astype(F32), cos, sin, even_lane) * scale
        v = v_ref[0, rs, :]
        scores = lax.dot_general(q.astype(BF16), k.astype(BF16), (((1,), (1,)), ((), ())),
                                 preferred_element_type=F32) * decay
        inner = jnp.dot(scores.astype(BF16), v, preferred_element_type=F32)
        cross = jnp.dot((q * xi).astype(BF16), state.astype(BF16), preferred_element_type=F32)
        kz_t = jnp.transpose(k * zeta).astype(BF16)
        state = cdec * state + jnp.dot(kz_t, v, preferred_element_type=F32)
        y = _lane_norm(inner + cross, rg, rb)
        g = gate_ref[0, rs, :].astype(F32)
        ret_ref[0, rs, :] = (g * _sigmoid(g) * y).astype(BF16)
    state_ref[...] = state

    a = ga_ref[0].astype(F32)
    b = gb_ref[0].astype(F32)
    uext_ref[CONV_HALO:CONV_HALO + rows, :] = a * _sigmoid(b)

    cb = cb_ref[0]
    cg = cg_ref[0]
    cbeta = cbeta_ref[0]
    blk = 64
    first = CONV_HALO - (CONV_TAPS - 1)
    for r0 in range(0, rows, blk):
        acc = jnp.zeros((blk, LANES), F32)
        for j in range(CONV_TAPS):
            acc = acc + cw_ref[0, j:j + 1, :] * uext_ref[r0 + first + j:r0 + first + j + blk, :]
        y = _lane_norm(acc + cb, cg, cbeta)
        conv_ref[0, r0:r0 + blk, :] = (y * _sigmoid(y)).astype(BF16)
    uext_ref[0:CONV_HALO, :] = uext_ref[rows:rows + CONV_HALO, :]


def _mixer(proj3, cos_t, sin_t, consts, ret_g, ret_b, cw, cb, cg, cbeta, w1, w3, w2, batch, seq, rows=1024):
    T = batch * seq
    ns = seq // rows
    cdec, decay, zeta, xi = consts
    steps = batch * HEADS * ns
    w1s = w1.reshape(-1, w1.shape[-1])
    w3s = w3.reshape(-1, w3.shape[-1])
    w2s = w2.reshape(-1, w2.shape[-1])
    step_of = lambda b, h, n: ((b * HEADS + h) * ns + n, 0)
    wspec = lambda w: pl.BlockSpec((w.shape[0] // steps, w.shape[1]), step_of)
    for w in (w1s, w3s, w2s):
        assert w.shape[0] % (steps * 16) == 0

    def pspec(off):
        return pl.BlockSpec((1, rows, LANES), lambda b, h, n: (off + h, b * ns + n, 0))

    def hspec(d1):
        return pl.BlockSpec((1, d1, LANES), lambda b, h, n: (h, 0, 0))

    return pl.pallas_call(
        functools.partial(_mixer_kernel, rows=rows),
        grid=(batch, HEADS, ns),
        in_specs=[pl.BlockSpec(memory_space=pltpu.SMEM),
                  pspec(0), pspec(8), pspec(16), pspec(24), pspec(32), pspec(40),
                  pl.BlockSpec((rows, LANES), lambda b, h, n: (n, 0)),
                  pl.BlockSpec((rows, LANES), lambda b, h, n: (n, 0)),
                  hspec(CHUNK), hspec(CHUNK), hspec(CHUNK),
                  hspec(1), hspec(1), hspec(CONV_TAPS), hspec(1), hspec(1), hspec(1),
                  wspec(w1s), wspec(w3s), wspec(w2s)],
        out_specs=[pl.BlockSpec((1, rows, LANES), lambda b, h, n: (h, b * ns + n, 0)),
                   pl.BlockSpec((1, rows, LANES), lambda b, h, n: (h, b * ns + n, 0)),
                   wspec(w1s), wspec(w3s), wspec(w2s)],
        out_shape=[jax.ShapeDtypeStruct((HEADS, T, LANES), BF16),
                   jax.ShapeDtypeStruct((CONV_GROUPS, T, LANES), BF16),
                   jax.ShapeDtypeStruct(w1s.shape, BF16),
                   jax.ShapeDtypeStruct(w3s.shape, BF16),
                   jax.ShapeDtypeStruct(w2s.shape, BF16)],
        scratch_shapes=[pltpu.VMEM((HEAD_DIM, HEAD_DIM), F32),
                        pltpu.VMEM((rows + CONV_HALO, LANES), F32)],
        compiler_params=pltpu.CompilerParams(
            dimension_semantics=("arbitrary", "arbitrary", "arbitrary"), vmem_limit_bytes=VMEM_LIMIT),
        name="mixer",
    )(cdec, proj3, proj3, proj3, proj3, proj3, proj3, cos_t, sin_t, decay, zeta, xi,
      ret_g, ret_b, cw, cb, cg, cbeta, w1s, w3s, w2s)


def _layer_norm(z, g, b):
    mu = jnp.mean(z, axis=-1, keepdims=True)
    zc = z - mu
    var = jnp.mean(zc * zc, axis=-1, keepdims=True)
    return zc * lax.rsqrt(var + LN_EPS) * g + b


def _outproj_kernel(ret_ref, conv_ref, x_ref, w_ref, g_ref, b_ref, wrh_ref, wrl_ref, br_ref,
                    h_ref, route_ref, counts_ref, carry_ref):
    @pl.when(pl.program_id(0) == 0)
    def _():
        carry_ref[...] = jnp.zeros_like(carry_ref)

    mix = jnp.concatenate([ret_ref[s] for s in range(HEADS)] +
                          [conv_ref[s] for s in range(CONV_GROUPS)], axis=-1)
    y = jnp.dot(mix, w_ref[...], preferred_element_type=F32)
    h = _layer_norm(ALPHA * x_ref[...] + y, g_ref[...], b_ref[...])
    h_ref[...] = h

    hb = h.astype(BF16)
    hl = (h - hb.astype(F32)).astype(BF16)
    wrh = wrh_ref[...]
    logits = (jnp.dot(hb, wrh, preferred_element_type=F32)
              + jnp.dot(hb, wrl_ref[...], preferred_element_type=F32)
              + jnp.dot(hl, wrh, preferred_element_type=F32)) + br_ref[...]

    tm = logits.shape[0]
    lane = lax.broadcasted_iota(jnp.int32, (tm, LANES), 1)
    lane_f = lane.astype(F32)
    neg = jnp.float32(-jnp.inf)
    big = jnp.float32(1e9)
    gmask = (lane >= N_EXPERTS) & (lane < N_EXPERTS + N_GROUPS)
    gl = jnp.where(gmask, logits, neg)
    gmax = jnp.max(gl, axis=-1, keepdims=True)
    gidx = jnp.min(jnp.where(gl == gmax, lane_f, big), axis=-1, keepdims=True) - N_EXPERTS
    g_w = 1.0 / jnp.sum(jnp.where(gmask, jnp.exp(gl - gmax), 0.0), axis=-1, keepdims=True)
    lo = gidx * EXPERTS_PER_GROUP
    emask = (lane_f >= lo) & (lane_f < lo + EXPERTS_PER_GROUP)
    el = jnp.where(emask, logits, neg)
    top1 = jnp.max(el, axis=-1, keepdims=True)
    idx1 = jnp.min(jnp.where(el == top1, lane_f, big), axis=-1, keepdims=True)
    el2 = jnp.where(lane_f == idx1, neg, el)
    top2 = jnp.max(el2, axis=-1, keepdims=True)
    idx2 = jnp.min(jnp.where(el2 == top2, lane_f, big), axis=-1, keepdims=True)
    e2 = jnp.exp(top2 - top1)
    w1 = g_w / (1.0 + e2)
    w2 = w1 * e2

    sel1 = lane_f == idx1
    sel2 = lane_f == idx2
    onehot = jnp.where(sel1, 1.0, 0.0) + jnp.where(sel2, 1.0, 0.0)
    r_i = lax.broadcasted_iota(jnp.int32, (tm, tm), 0)
    c_i = lax.broadcasted_iota(jnp.int32, (tm, tm), 1)
    tri = jnp.where(c_i < r_i, 1.0, 0.0).astype(BF16)
    prefix = jnp.dot(tri, onehot.astype(BF16), preferred_element_type=F32) + carry_ref[0:1, :]
    rank1 = jnp.sum(jnp.where(sel1, prefix, 0.0), axis=-1, keepdims=True)
    rank2 = jnp.sum(jnp.where(sel2, prefix, 0.0), axis=-1, keepdims=True)
    total = carry_ref[0:1, :] + jnp.sum(onehot, axis=0, keepdims=True)
    carry_ref[...] = jnp.broadcast_to(total, carry_ref.shape)
    counts_ref[...] = jnp.broadcast_to(total, counts_ref.shape)

    route = jnp.where(lane == 0, w1, 0.0)
    for k, val in enumerate((w2, idx1, idx2, rank1, rank2), start=1):
        route = jnp.where(lane == k, val, route)
    route_ref[...] = route


def _out_proj(ret3, conv3, x2, w_out_b, ln_g, ln_b, wr_hi, wr_lo, br, tm=512):
    T = x2.shape[0]
    cur = lambda i: (0, i, 0)
    row = lambda i: (i, 0)
    const = lambda i: (0, 0)
    return pl.pallas_call(
        _outproj_kernel,
        grid=(T // tm,),
        in_specs=[pl.BlockSpec((HEADS, tm, LANES), cur),
                  pl.BlockSpec((CONV_GROUPS, tm, LANES), cur),
                  pl.BlockSpec((tm, D_MODEL), row),
                  pl.BlockSpec((D_MODEL, D_MODEL), const),
                  pl.BlockSpec((1, D_MODEL), const),
                  pl.BlockSpec((1, D_MODEL), const),
                  pl.BlockSpec((D_MODEL, LANES), const),
                  pl.BlockSpec((D_MODEL, LANES), const),
                  pl.BlockSpec((1, LANES), const)],
        out_specs=[pl.BlockSpec((tm, D_MODEL), row),
                   pl.BlockSpec((tm, LANES), row),
                   pl.BlockSpec((SUBLANES, LANES), const)],
        out_shape=[jax.ShapeDtypeStruct((T, D_MODEL), F32),
                   jax.ShapeDtypeStruct((T, LANES), F32),
                   jax.ShapeDtypeStruct((SUBLANES, LANES), F32)],
        scratch_shapes=[pltpu.VMEM((SUBLANES, LANES), F32)],
        compiler_params=pltpu.CompilerParams(
            dimension_semantics=("arbitrary",), vmem_limit_bytes=VMEM_LIMIT),
        name="out_proj_ln_route",
    )(ret3, conv3, x2, w_out_b, ln_g, ln_b, wr_hi, wr_lo, br)


MOE_TM = 512
MOE_TB = 512


def _row_copy(src_ref, src_row, dst_ref, dst_row, sem):
    return pltpu.make_async_copy(src_ref.at[pl.ds(src_row, 1), :], dst_ref.at[pl.ds(dst_row, 1), :], sem)


def _experts_kernel(te_ref, nv_ref, src_ref, src_next_ref, h_ref, w1_ref, w3_ref, w2_ref, y_ref, x_ref, sems):
    i = pl.program_id(0)
    last = pl.num_programs(0) - 1
    slot = i % 2
    used = i < nv_ref[0]

    def gather_wait(s):
        pltpu.make_async_copy(h_ref.at[pl.ds(0, MOE_TM), :], x_ref.at[s, pl.ds(0, MOE_TM), :], sems.at[s]).wait()

    @pl.when(i == 0)
    def _():
        x_ref[:, MOE_TM:, :] = jnp.zeros((2, SUBLANES, D_MODEL), F32)

        def first(t, carry):
            _row_copy(h_ref, src_ref[0, 0, t], x_ref.at[0], t, sems.at[0]).start()
            return carry
        lax.fori_loop(0, MOE_TM, first, 0, unroll=8)

    @pl.when(used)
    def _():
        gather_wait(slot)
        xb = x_ref[slot, 0:MOE_TM, :].astype(BF16)
        a1 = jnp.dot(xb, w1_ref[0], preferred_element_type=F32)

        def issue_half(k, width):
            for t in range(k * MOE_TM // 2, (k + 1) * MOE_TM // 2):
                _row_copy(h_ref, src_next_ref[0, 0, t], x_ref.at[1 - slot], t, sems.at[1 - slot]).start()
            spare = pltpu.bitcast(x_ref[1 - slot, MOE_TM:, k * LANES:(k + 1) * LANES], jnp.uint32)
            zero = pltpu.bitcast((spare >> 16) >> 16, F32)
            return jnp.tile(zero, (MOE_TM // SUBLANES, width // LANES))

        a3 = jnp.dot(xb, w3_ref[0], preferred_element_type=F32) + issue_half(0, D_FF)
        act = (a1 * _sigmoid(a1) * a3).astype(BF16)
        y_ref[...] = jnp.dot(act, w2_ref[0], preferred_element_type=F32) + issue_half(1, D_MODEL)

        @pl.when(i == last)
        def _():
            gather_wait(1 - slot)

    @pl.when(jnp.logical_not(used))
    def _():
        @pl.when(i == nv_ref[0])
        def _():
            gather_wait(slot)
        y_ref[...] = jnp.zeros_like(y_ref)


def _experts(tile_expert, n_valid, src_tok, h, w1b, w3b, w2b):
    n_tiles = src_tok.shape[0]
    row = lambda i, te, nv: (i, 0)
    wsel = lambda i, te, nv: (te[i], 0, 0)
    smem_row = lambda f: pl.BlockSpec((1, 1, MOE_TM), f, memory_space=pltpu.SMEM)
    return pl.pallas_call(
        _experts_kernel,
        grid_spec=pltpu.PrefetchScalarGridSpec(
            num_scalar_prefetch=2,
            grid=(n_tiles,),
            in_specs=[smem_row(lambda i, te, nv: (i, 0, 0)),
                      smem_row(lambda i, te, nv: (jnp.minimum(i + 1, n_tiles - 1), 0, 0)),
                      pl.BlockSpec(memory_space=pl.ANY),
                      pl.BlockSpec((1, D_MODEL, D_FF), wsel),
                      pl.BlockSpec((1, D_MODEL, D_FF), wsel),
                      pl.BlockSpec((1, D_FF, D_MODEL), wsel)],
            out_specs=pl.BlockSpec((MOE_TM, D_MODEL), row),
            scratch_shapes=[pltpu.VMEM((2, MOE_TM + SUBLANES, D_MODEL), F32),
                            pltpu.SemaphoreType.DMA((2,))]),
        out_shape=jax.ShapeDtypeStruct((n_tiles * MOE_TM, D_MODEL), F32),
        compiler_params=pltpu.CompilerParams(
            dimension_semantics=("arbitrary",), vmem_limit_bytes=VMEM_LIMIT),
        name="moe_experts",
    )(tile_expert, n_valid, src_tok, src_tok, h, w1b, w3b, w2b)


def _combine_kernel(pos_ref, pos_next_ref, route_ref, h_ref, g_ref, b_ref, ys_ref, o_ref,
                    ya0_ref, yb0_ref, ya1_ref, yb1_ref, sems):
    tb = h_ref.shape[0]
    i = pl.program_id(0)
    bufs = ((ya0_ref, yb0_ref), (ya1_ref, yb1_ref))

    def gather_wait(s):
        for k in range(TOP_K):
            pltpu.make_async_copy(ys_ref.at[pl.ds(0, tb), :], bufs[s][k], sems.at[s]).wait()

    @pl.when(i == 0)
    def _():
        def first(t, carry):
            for k in range(TOP_K):
                _row_copy(ys_ref, pos_ref[0, 0, k * tb + t], bufs[0][k], t, sems.at[0]).start()
            return carry
        lax.fori_loop(0, tb, first, 0, unroll=8)

    for par in range(2):
        @pl.when(i % 2 == par)
        def _():
            gather_wait(par)
            for t in range(tb):
                for k in range(TOP_K):
                    _row_copy(ys_ref, pos_next_ref[0, 0, k * tb + t], bufs[1 - par][k], t, sems.at[1 - par]).start()
            route = route_ref[...]
            ffn = route[:, 0:1] * bufs[par][0][...] + route[:, 1:2] * bufs[par][1][...]
            o_ref[...] = _layer_norm(ALPHA * h_ref[...] + ffn, g_ref[...], b_ref[...])

            @pl.when(i == pl.num_programs(0) - 1)
            def _():
                gather_wait(1 - par)


def _combine(pos3, route, h, ln_g, ln_b, ys):
    T = h.shape[0]
    tb = MOE_TB
    row = lambda i: (i, 0)
    const = lambda i: (0, 0)
    return pl.pallas_call(
        _combine_kernel,
        grid=(T // tb,),
        in_specs=[pl.BlockSpec((1, 1, 2 * tb), lambda i: (i, 0, 0), memory_space=pltpu.SMEM),
                  pl.BlockSpec((1, 1, 2 * tb), lambda i: (jnp.minimum(i + 1, T // tb - 1), 0, 0),
                               memory_space=pltpu.SMEM),
                  pl.BlockSpec((tb, LANES), row),
                  pl.BlockSpec((tb, D_MODEL), row),
                  pl.BlockSpec((1, D_MODEL), const),
                  pl.BlockSpec((1, D_MODEL), const),
                  pl.BlockSpec(memory_space=pl.ANY)],
        out_specs=pl.BlockSpec((tb, D_MODEL), row),
        out_shape=jax.ShapeDtypeStruct((T, D_MODEL), F32),
        scratch_shapes=[pltpu.VMEM((tb, D_MODEL), F32)] * (2 * TOP_K) + [pltpu.SemaphoreType.DMA((2,))],
        compiler_params=pltpu.CompilerParams(
            dimension_semantics=("arbitrary",), vmem_limit_bytes=VMEM_LIMIT),
        name="moe_combine_ln",
    )(pos3, pos3, route, h, ln_g, ln_b, ys)


def _routed_moe(h, route, counts, w1b, w3b, w2b, ln_g, ln_b):
    T = h.shape[0]
    n_tiles = (TOP_K * T) // MOE_TM + N_EXPERTS
    i32 = jnp.int32
    cnt = counts[0, :N_EXPERTS].astype(i32)
    ntile = (cnt + MOE_TM - 1) // MOE_TM
    tile_end = jnp.cumsum(ntile)
    n_valid = tile_end[-1:]
    seg_start = (tile_end - ntile) * MOE_TM
    info = route[:, 2:6].astype(i32)
    eids = jnp.arange(N_EXPERTS, dtype=i32)
    start_of = lambda e: jnp.sum(jnp.where(e[:, None] == eids[None, :], seg_start[None, :], 0), axis=1)
    pos1 = start_of(info[:, 0]) + info[:, 2]
    pos2 = start_of(info[:, 1]) + info[:, 3]
    pos3 = jnp.concatenate([pos1.reshape(T // MOE_TB, 1, MOE_TB), pos2.reshape(T // MOE_TB, 1, MOE_TB)], axis=2)
    tile_ids = jnp.minimum(jnp.arange(n_tiles, dtype=i32), n_valid[0] - 1)
    tile_expert = jnp.sum((tile_ids[:, None] >= tile_end[None, :]).astype(i32), axis=1)

    order = jnp.argsort(jnp.concatenate([pos1, pos2])).astype(i32)
    n_rows = n_tiles * MOE_TM
    padded = jnp.concatenate([jnp.zeros((n_rows,), i32), jnp.where(order >= T, order - T, order),
                              jnp.zeros((2 * n_rows - TOP_K * T,), i32)])
    shift = seg_start - (jnp.cumsum(cnt) - cnt)
    row_expert = jnp.repeat(tile_expert, MOE_TM)
    src_tok = jnp.zeros((n_rows,), i32)
    for e in range(N_EXPERTS):
        view = lax.dynamic_slice(padded, (n_rows - shift[e],), (n_rows,))
        src_tok = jnp.where(row_expert == e, view, src_tok)

    ys = _experts(tile_expert, n_valid.astype(i32), src_tok.reshape(n_tiles, 1, MOE_TM), h, w1b, w3b, w2b)
    return _combine(pos3, route, h, ln_g, ln_b, ys)


def _retention_constants():
    hh = np.arange(HEADS, dtype=np.float64)
    log_gamma = np.log1p(-np.exp2(-5.0 - hh))
    pos = np.arange(CHUNK, dtype=np.float64)
    diff = pos[:, None] - pos[None, :]
    causal = diff >= 0
    decay = np.where(causal[None], np.exp(log_gamma[:, None, None] * np.where(causal, diff, 0.0)[None]), 0.0)
    zeta = np.exp(log_gamma[:, None] * (CHUNK - 1 - pos)[None])
    xi = np.exp(log_gamma[:, None] * (pos + 1.0)[None])
    cdec = np.exp(log_gamma * CHUNK)
    bc = lambda t: np.broadcast_to(t[:, :, None], (HEADS, CHUNK, LANES))
    return (jnp.asarray(cdec, F32), jnp.asarray(decay, F32),
            jnp.asarray(bc(zeta), F32), jnp.asarray(bc(xi), F32))


def _rope_tables(seq):
    inv = ROPE_BASE ** (-jnp.arange(0, HEAD_DIM, 2, dtype=F32) / HEAD_DIM)
    ang = jnp.arange(seq, dtype=F32)[:, None] * inv[None, :]
    cos = jnp.repeat(jnp.cos(ang), 2, axis=-1)
    sin = jnp.sin(ang)
    sin_signed = jnp.stack([-sin, sin], axis=-1).reshape(seq, HEAD_DIM)
    return cos, sin_signed


def kernel(x, w_in, ret_norm_g, ret_norm_b, conv_w, conv_b, conv_norm_g, conv_norm_b, w_out, ln1_g, ln1_b, w_router_group, b_router_group, w_router_expert, b_router_expert, w1, w3, w2, ln2_g, ln2_b):
    batch, seq, _ = x.shape
    T = batch * seq
    x2 = x.reshape(T, D_MODEL)

    proj3 = _in_proj(x2, w_in.astype(BF16))

    cos_t, sin_t = _rope_tables(seq)
    per_head = lambda t: t.reshape(HEADS, 1, LANES)
    cw = jnp.transpose(conv_w.reshape(CONV_TAPS, CONV_GROUPS, LANES), (1, 0, 2))
    ret3, conv3, w1b, w3b, w2b = _mixer(
        proj3, cos_t, sin_t, _retention_constants(), per_head(ret_norm_g), per_head(ret_norm_b), cw,
        per_head(conv_b), per_head(conv_norm_g), per_head(conv_norm_b), w1, w3, w2, batch, seq)

    wr = jnp.concatenate([jnp.transpose(w_router_expert, (1, 0, 2)).reshape(D_MODEL, N_EXPERTS),
                          w_router_group], axis=1)
    wr = jnp.pad(wr, ((0, 0), (0, LANES - wr.shape[1])))
    br = jnp.pad(jnp.concatenate([b_router_expert.reshape(-1), b_router_group]),
                 (0, LANES - N_EXPERTS - N_GROUPS)).reshape(1, LANES)
    wr_hi = wr.astype(BF16)
    wr_lo = (wr - wr_hi.astype(F32)).astype(BF16)

    h, route, counts = _out_proj(ret3, conv3, x2, w_out.astype(BF16), ln1_g.reshape(1, -1),
                                 ln1_b.reshape(1, -1), wr_hi, wr_lo, br)

    y = _routed_moe(h, route, counts, w1b.reshape(w1.shape), w3b.reshape(w3.shape), w2b.reshape(w2.shape),
                    ln2_g.reshape(1, -1), ln2_b.reshape(1, -1))
    return y.reshape(batch, seq, D_MODEL)
```

```python
import functools
import math

import numpy as np
import jax
import jax.numpy as jnp
from jax import lax
from jax.experimental import pallas as pl
from jax.experimental.pallas import tpu as pltpu

D_MODEL = 2048
RET_WIDTH = D_MODEL // 2
CONV_WIDTH = D_MODEL - RET_WIDTH
HEADS = 8
HEAD_DIM = RET_WIDTH // HEADS
CONV_GROUPS = 8
CONV_TAPS = 31
CHUNK = 128
ROPE_BASE = 10000.0
IN_COLS = 4 * RET_WIDTH + 2 * CONV_WIDTH
N_GROUPS = 4
EXPERTS_PER_GROUP = 4
N_EXPERTS = N_GROUPS * EXPERTS_PER_GROUP
TOP_K = 2
D_FF = D_MODEL // 2
LN_EPS = 1e-5
ALPHA = 2.0 ** 0.25

LANES = 128
SUBLANES = 8
VMEM_LIMIT = 56 * 1024 * 1024
CONV_HALO = 32

F32 = jnp.float32
BF16 = jnp.bfloat16


def _sigmoid(x):
    return 1.0 / (1.0 + jnp.exp(-x))


def _lane_norm(x, g, b):
    mu = jnp.mean(x, axis=-1, keepdims=True)
    xc = x - mu
    var = jnp.mean(xc * xc, axis=-1, keepdims=True)
    return xc * lax.rsqrt(var + LN_EPS) * g + b


def _inproj_kernel(x_ref, w_ref, o_ref):
    res = jnp.dot(x_ref[...].astype(BF16), w_ref[...], preferred_element_type=F32)
    for s in range(o_ref.shape[0]):
        o_ref[s] = res[:, s * LANES:(s + 1) * LANES].astype(BF16)


def _in_proj(x2, w_in_b, tm=512, tn=2048):
    T = x2.shape[0]
    nblk = tn // LANES
    return pl.pallas_call(
        _inproj_kernel,
        grid=(IN_COLS // tn, T // tm),
        in_specs=[pl.BlockSpec((tm, D_MODEL), lambda j, i: (i, 0)),
                  pl.BlockSpec((D_MODEL, tn), lambda j, i: (0, j))],
        out_specs=pl.BlockSpec((nblk, tm, LANES), lambda j, i: (j, i, 0)),
        out_shape=jax.ShapeDtypeStruct((IN_COLS // LANES, T, LANES), BF16),
        compiler_params=pltpu.CompilerParams(
            dimension_semantics=("arbitrary", "arbitrary"), vmem_limit_bytes=VMEM_LIMIT),
        name="in_proj",
    )(x2, w_in_b)


def _rotary(t, cos, sin_signed, even_lane):
    partner = jnp.where(even_lane, pltpu.roll(t, LANES - 1, 1), pltpu.roll(t, 1, 1))
    return t * cos + partner * sin_signed


def _mixer_kernel(cdec_ref, q_ref, k_ref, v_ref, gate_ref, ga_ref, gb_ref, cos_ref, sin_ref,
                  decay_ref, zeta_ref, xi_ref, rg_ref, rb_ref, cw_ref, cb_ref, cg_ref, cbeta_ref,
                  w1f_ref, w3f_ref, w2f_ref, wof_ref,
                  ret_ref, conv_ref, w1b_ref, w3b_ref, w2b_ref, wob_ref, state_ref, uext_ref, *, rows):
    w1b_ref[...] = w1f_ref[...].astype(BF16)
    w3b_ref[...] = w3f_ref[...].astype(BF16)
    w2b_ref[...] = w2f_ref[...].astype(BF16)
    wob_ref[...] = wof_ref[...].astype(BF16)

    h = pl.program_id(1)
    n = pl.program_id(2)

    @pl.when(n == 0)
    def _():
        state_ref[...] = jnp.zeros_like(state_ref)
        uext_ref[0:CONV_HALO, :] = jnp.zeros((CONV_HALO, LANES), F32)

    even_lane = (lax.broadcasted_iota(jnp.int32, (CHUNK, LANES), 1) % 2) == 0
    cdec = cdec_ref[h]
    decay = decay_ref[0]
    zeta = zeta_ref[0]
    xi = xi_ref[0]
    rg = rg_ref[0]
    rb = rb_ref[0]
    scale = HEAD_DIM ** -0.5

    state = state_ref[...]
    for c in range(rows // CHUNK):
        rs = slice(c * CHUNK, (c + 1) * CHUNK)
        cos = cos_ref[rs, :]
        sin = sin_ref[rs, :]
        q = _rotary(q_ref[0, rs, :].astype(F32), cos, sin, even_lane)
        k = _rotary(k_ref[0, rs, :].astype(F32), cos, sin, even_lane) * scale
        v = v_ref[0, rs, :]
        scores = lax.dot_general(q.astype(BF16), k.astype(BF16), (((1,), (1,)), ((), ())),
                                 preferred_element_type=F32) * decay
        inner = jnp.dot(scores.astype(BF16), v, preferred_element_type=F32)
        cross = jnp.dot((q * xi).astype(BF16), state.astype(BF16), preferred_element_type=F32)
        kz_t = jnp.transpose(k * zeta).astype(BF16)
        state = cdec * state + jnp.dot(kz_t, v, preferred_element_type=F32)
        y = _lane_norm(inner + cross, rg, rb)
        g = gate_ref[0, rs, :].astype(F32)
        ret_ref[0, rs, :] = (g * _sigmoid(g) * y).astype(BF16)
    state_ref[...] = state

    a = ga_ref[0].astype(F32)
    b = gb_ref[0].astype(F32)
    uext_ref[CONV_HALO:CONV_HALO + rows, :] = a * _sigmoid(b)

    cb = cb_ref[0]
    cg = cg_ref[0]
    cbeta = cbeta_ref[0]
    blk = 64
    first = CONV_HALO - (CONV_TAPS - 1)
    for r0 in range(0, rows, blk):
        acc = jnp.zeros((blk, LANES), F32)
        for j in range(CONV_TAPS):
            acc = acc + cw_ref[0, j:j + 1, :] * uext_ref[r0 + first + j:r0 + first + j + blk, :]
        y = _lane_norm(acc + cb, cg, cbeta)
        conv_ref[0, r0:r0 + blk, :] = (y * _sigmoid(y)).astype(BF16)
    uext_ref[0:CONV_HALO, :] = uext_ref[rows:rows + CONV_HALO, :]


def _mixer(proj3, cos_t, sin_t, consts, ret_g, ret_b, cw, cb, cg, cbeta, w1, w3, w2, w_out, batch, seq, rows=1024):
    T = batch * seq
    ns = seq // rows
    cdec, decay, zeta, xi = consts
    steps = batch * HEADS * ns
    w1s = w1.reshape(-1, w1.shape[-1])
    w3s = w3.reshape(-1, w3.shape[-1])
    w2s = w2.reshape(-1, w2.shape[-1])
    step_of = lambda b, h, n: ((b * HEADS + h) * ns + n, 0)
    wspec = lambda w: pl.BlockSpec((w.shape[0] // steps, w.shape[1]), step_of)
    for w in (w1s, w3s, w2s, w_out):
        assert w.shape[0] % (steps * 16) == 0

    def pspec(off):
        return pl.BlockSpec((1, rows, LANES), lambda b, h, n: (off + h, b * ns + n, 0))

    def hspec(d1):
        return pl.BlockSpec((1, d1, LANES), lambda b, h, n: (h, 0, 0))

    return pl.pallas_call(
        functools.partial(_mixer_kernel, rows=rows),
        grid=(batch, HEADS, ns),
        in_specs=[pl.BlockSpec(memory_space=pltpu.SMEM),
                  pspec(0), pspec(8), pspec(16), pspec(24), pspec(32), pspec(40),
                  pl.BlockSpec((rows, LANES), lambda b, h, n: (n, 0)),
                  pl.BlockSpec((rows, LANES), lambda b, h, n: (n, 0)),
                  hspec(CHUNK), hspec(CHUNK), hspec(CHUNK),
                  hspec(1), hspec(1), hspec(CONV_TAPS), hspec(1), hspec(1), hspec(1),
                  wspec(w1s), wspec(w3s), wspec(w2s), wspec(w_out)],
        out_specs=[pl.BlockSpec((1, rows, LANES), lambda b, h, n: (h, b * ns + n, 0)),
                   pl.BlockSpec((1, rows, LANES), lambda b, h, n: (h, b * ns + n, 0)),
                   wspec(w1s), wspec(w3s), wspec(w2s), wspec(w_out)],
        out_shape=[jax.ShapeDtypeStruct((HEADS, T, LANES), BF16),
                   jax.ShapeDtypeStruct((CONV_GROUPS, T, LANES), BF16),
                   jax.ShapeDtypeStruct(w1s.shape, BF16),
                   jax.ShapeDtypeStruct(w3s.shape, BF16),
                   jax.ShapeDtypeStruct(w2s.shape, BF16),
                   jax.ShapeDtypeStruct(w_out.shape, BF16)],
        scratch_shapes=[pltpu.VMEM((HEAD_DIM, HEAD_DIM), F32),
                        pltpu.VMEM((rows + CONV_HALO, LANES), F32)],
        compiler_params=pltpu.CompilerParams(
            dimension_semantics=("arbitrary", "arbitrary", "arbitrary"), vmem_limit_bytes=VMEM_LIMIT),
        name="mixer",
    )(cdec, proj3, proj3, proj3, proj3, proj3, proj3, cos_t, sin_t, decay, zeta, xi,
      ret_g, ret_b, cw, cb, cg, cbeta, w1s, w3s, w2s, w_out)


def _layer_norm(z, g, b):
    mu = jnp.mean(z, axis=-1, keepdims=True)
    zc = z - mu
    var = jnp.mean(zc * zc, axis=-1, keepdims=True)
    return zc * lax.rsqrt(var + LN_EPS) * g + b


def _outproj_kernel(ret_ref, conv_ref, x_ref, w_ref, g_ref, b_ref, wr_ref, br_ref,
                    h_ref, route_ref, route_t_ref, counts_ref, carry_ref):
    @pl.when(pl.program_id(0) == 0)
    def _():
        carry_ref[...] = jnp.zeros_like(carry_ref)

    mix = jnp.concatenate([ret_ref[s] for s in range(HEADS)] +
                          [conv_ref[s] for s in range(CONV_GROUPS)], axis=-1)
    y = jnp.dot(mix, w_ref[...], preferred_element_type=F32)
    h = _layer_norm(ALPHA * x_ref[...] + y, g_ref[...], b_ref[...])
    h_ref[...] = h

    logits = jnp.dot(h.astype(BF16), wr_ref[...], preferred_element_type=F32) + br_ref[...]

    tm = logits.shape[0]
    lt = jnp.transpose(logits)
    neg = jnp.float32(-jnp.inf)
    big = jnp.float32(1e9)
    grow = lax.broadcasted_iota(jnp.int32, (SUBLANES, tm), 0).astype(F32)
    gmask = grow < N_GROUPS
    gl = jnp.where(gmask, lt[N_EXPERTS:N_EXPERTS + SUBLANES, :], neg)
    gmax = jnp.max(gl, axis=0, keepdims=True)
    gidx = jnp.min(jnp.where(gl == gmax, grow, big), axis=0, keepdims=True)
    g_w = 1.0 / jnp.sum(jnp.where(gmask, jnp.exp(gl - gmax), 0.0), axis=0, keepdims=True)
    erow = lax.broadcasted_iota(jnp.int32, (N_EXPERTS, tm), 0).astype(F32)
    lo = gidx * EXPERTS_PER_GROUP
    emask = (erow >= lo) & (erow < lo + EXPERTS_PER_GROUP)
    el = jnp.where(emask, lt[0:N_EXPERTS, :], neg)
    top1 = jnp.max(el, axis=0, keepdims=True)
    idx1 = jnp.min(jnp.where(el == top1, erow, big), axis=0, keepdims=True)
    el2 = jnp.where(erow == idx1, neg, el)
    top2 = jnp.max(el2, axis=0, keepdims=True)
    idx2 = jnp.min(jnp.where(el2 == top2, erow, big), axis=0, keepdims=True)
    e2 = jnp.exp(top2 - top1)
    w1 = g_w / (1.0 + e2)
    w2 = w1 * e2

    sel1 = erow == idx1
    sel2 = erow == idx2
    onehot = jnp.where(sel1, 1.0, 0.0) + jnp.where(sel2, 1.0, 0.0)
    r_i = lax.broadcasted_iota(jnp.int32, (tm, tm), 0)
    c_i = lax.broadcasted_iota(jnp.int32, (tm, tm), 1)
    earlier = jnp.where(r_i < c_i, 1.0, 0.0).astype(BF16)
    carry = carry_ref[:, 0:1]
    prefix = jnp.dot(onehot.astype(BF16), earlier, preferred_element_type=F32) + carry
    rank1 = jnp.sum(jnp.where(sel1, prefix, 0.0), axis=0, keepdims=True)
    rank2 = jnp.sum(jnp.where(sel2, prefix, 0.0), axis=0, keepdims=True)
    total = carry + jnp.sum(onehot, axis=1, keepdims=True)
    carry_ref[...] = jnp.broadcast_to(total, carry_ref.shape)
    counts_ref[...] = jnp.broadcast_to(total, counts_ref.shape)

    route_t = jnp.concatenate([w1, w2, idx1, idx2, rank1, rank2, jnp.zeros((2, tm), F32)], axis=0)
    route_t_ref[...] = route_t
    route_ref[...] = jnp.transpose(
        jnp.concatenate([route_t, jnp.zeros((LANES - SUBLANES, tm), F32)], axis=0))


def _out_proj(ret3, conv3, x2, w_out_b, ln_g, ln_b, wr_b, br, tm=512):
    T = x2.shape[0]
    cur = lambda i: (0, i, 0)
    row = lambda i: (i, 0)
    const = lambda i: (0, 0)
    return pl.pallas_call(
        _outproj_kernel,
        grid=(T // tm,),
        in_specs=[pl.BlockSpec((HEADS, tm, LANES), cur),
                  pl.BlockSpec((CONV_GROUPS, tm, LANES), cur),
                  pl.BlockSpec((tm, D_MODEL), row),
                  pl.BlockSpec((D_MODEL, D_MODEL), const),
                  pl.BlockSpec((1, D_MODEL), const),
                  pl.BlockSpec((1, D_MODEL), const),
                  pl.BlockSpec((D_MODEL, LANES), const),
                  pl.BlockSpec((1, LANES), const)],
        out_specs=[pl.BlockSpec((tm, D_MODEL), row),
                   pl.BlockSpec((tm, LANES), row),
                   pl.BlockSpec((SUBLANES, tm), lambda i: (0, i)),
                   pl.BlockSpec((N_EXPERTS, LANES), const)],
        out_shape=[jax.ShapeDtypeStruct((T, D_MODEL), F32),
                   jax.ShapeDtypeStruct((T, LANES), F32),
                   jax.ShapeDtypeStruct((SUBLANES, T), F32),
                   jax.ShapeDtypeStruct((N_EXPERTS, LANES), F32)],
        scratch_shapes=[pltpu.VMEM((N_EXPERTS, LANES), F32)],
        compiler_params=pltpu.CompilerParams(
            dimension_semantics=("arbitrary",), vmem_limit_bytes=VMEM_LIMIT),
        name="out_proj_ln_route",
    )(ret3, conv3, x2, w_out_b, ln_g, ln_b, wr_b, br)


MOE_TM = 512
MOE_TB = 512


def _row_copy(src_ref, src_row, dst_ref, dst_row, sem):
    return pltpu.make_async_copy(src_ref.at[pl.ds(src_row, 1), :], dst_ref.at[pl.ds(dst_row, 1), :], sem)


def _experts_kernel(te_ref, nv_ref, src_ref, src_next_ref, h_ref, w1_ref, w3_ref, w2_ref, y_ref, x_ref, sems):
    i = pl.program_id(0)
    last = pl.num_programs(0) - 1
    slot = i % 2
    used = i < nv_ref[0]

    def gather_wait(s):
        pltpu.make_async_copy(h_ref.at[pl.ds(0, MOE_TM), :], x_ref.at[s, pl.ds(0, MOE_TM), :], sems.at[s]).wait()

    @pl.when(i == 0)
    def _():
        x_ref[:, MOE_TM:, :] = jnp.zeros((2, SUBLANES, D_MODEL), F32)

        def first(t, carry):
            _row_copy(h_ref, src_ref[0, 0, t], x_ref.at[0], t, sems.at[0]).start()
            return carry
        lax.fori_loop(0, MOE_TM, first, 0, unroll=8)

    @pl.when(used)
    def _():
        gather_wait(slot)
        xb = x_ref[slot, 0:MOE_TM, :].astype(BF16)
        a1 = jnp.dot(xb, w1_ref[0], preferred_element_type=F32)

        def issue_half(k, width):
            for t in range(k * MOE_TM // 2, (k + 1) * MOE_TM // 2):
                _row_copy(h_ref, src_next_ref[0, 0, t], x_ref.at[1 - slot], t, sems.at[1 - slot]).start()
            spare = pltpu.bitcast(x_ref[1 - slot, MOE_TM:, k * LANES:(k + 1) * LANES], jnp.uint32)
            zero = pltpu.bitcast((spare >> 16) >> 16, F32)
            return jnp.tile(zero, (MOE_TM // SUBLANES, width // LANES))

        a3 = jnp.dot(xb, w3_ref[0], preferred_element_type=F32) + issue_half(0, D_FF)
        act = (a1 * _sigmoid(a1) * a3).astype(BF16)
        y_ref[...] = jnp.dot(act, w2_ref[0], preferred_element_type=F32) + issue_half(1, D_MODEL)

        @pl.when(i == last)
        def _():
            gather_wait(1 - slot)

    @pl.when(jnp.logical_not(used))
    def _():
        @pl.when(i == nv_ref[0])
        def _():
            gather_wait(slot)
        y_ref[...] = jnp.zeros_like(y_ref)


def _experts(tile_expert, n_valid, src_tok, h, w1b, w3b, w2b):
    n_tiles = src_tok.shape[0]
    row = lambda i, te, nv: (i, 0)
    wsel = lambda i, te, nv: (te[i], 0, 0)
    smem_row = lambda f: pl.BlockSpec((1, 1, MOE_TM), f, memory_space=pltpu.SMEM)
    return pl.pallas_call(
        _experts_kernel,
        grid_spec=pltpu.PrefetchScalarGridSpec(
            num_scalar_prefetch=2,
            grid=(n_tiles,),
            in_specs=[smem_row(lambda i, te, nv: (i, 0, 0)),
                      smem_row(lambda i, te, nv: (jnp.minimum(i + 1, n_tiles - 1), 0, 0)),
                      pl.BlockSpec(memory_space=pl.ANY),
                      pl.BlockSpec((1, D_MODEL, D_FF), wsel),
                      pl.BlockSpec((1, D_MODEL, D_FF), wsel),
                      pl.BlockSpec((1, D_FF, D_MODEL), wsel)],
            out_specs=pl.BlockSpec((MOE_TM, D_MODEL), row),
            scratch_shapes=[pltpu.VMEM((2, MOE_TM + SUBLANES, D_MODEL), F32),
                            pltpu.SemaphoreType.DMA((2,))]),
        out_shape=jax.ShapeDtypeStruct((n_tiles * MOE_TM, D_MODEL), F32),
        compiler_params=pltpu.CompilerParams(
            dimension_semantics=("arbitrary",), vmem_limit_bytes=VMEM_LIMIT),
        name="moe_experts",
    )(tile_expert, n_valid, src_tok, src_tok, h, w1b, w3b, w2b)


def _combine_kernel(pos_ref, pos_next_ref, route_ref, h_ref, g_ref, b_ref, ys_ref, o_ref,
                    ya0_ref, yb0_ref, ya1_ref, yb1_ref, sems):
    tb = h_ref.shape[0]
    i = pl.program_id(0)
    bufs = ((ya0_ref, yb0_ref), (ya1_ref, yb1_ref))

    def gather_wait(s):
        for k in range(TOP_K):
            pltpu.make_async_copy(ys_ref.at[pl.ds(0, tb), :], bufs[s][k], sems.at[s]).wait()

    @pl.when(i == 0)
    def _():
        def first(t, carry):
            for k in range(TOP_K):
                _row_copy(ys_ref, pos_ref[0, 0, k * tb + t], bufs[0][k], t, sems.at[0]).start()
            return carry
        lax.fori_loop(0, tb, first, 0, unroll=8)

    for par in range(2):
        @pl.when(i % 2 == par)
        def _():
            gather_wait(par)
            for t in range(tb):
                for k in range(TOP_K):
                    _row_copy(ys_ref, pos_next_ref[0, 0, k * tb + t], bufs[1 - par][k], t, sems.at[1 - par]).start()
            route = route_ref[...]
            ffn = route[:, 0:1] * bufs[par][0][...] + route[:, 1:2] * bufs[par][1][...]
            o_ref[...] = _layer_norm(ALPHA * h_ref[...] + ffn, g_ref[...], b_ref[...])

            @pl.when(i == pl.num_programs(0) - 1)
            def _():
                gather_wait(1 - par)


def _combine(pos3, route, h, ln_g, ln_b, ys):
    T = h.shape[0]
    tb = MOE_TB
    row = lambda i: (i, 0)
    const = lambda i: (0, 0)
    return pl.pallas_call(
        _combine_kernel,
        grid=(T // tb,),
        in_specs=[pl.BlockSpec((1, 1, 2 * tb), lambda i: (i, 0, 0), memory_space=pltpu.SMEM),
                  pl.BlockSpec((1, 1, 2 * tb), lambda i: (jnp.minimum(i + 1, T // tb - 1), 0, 0),
                               memory_space=pltpu.SMEM),
                  pl.BlockSpec((tb, LANES), row),
                  pl.BlockSpec((tb, D_MODEL), row),
                  pl.BlockSpec((1, D_MODEL), const),
                  pl.BlockSpec((1, D_MODEL), const),
                  pl.BlockSpec(memory_space=pl.ANY)],
        out_specs=pl.BlockSpec((tb, D_MODEL), row),
        out_shape=jax.ShapeDtypeStruct((T, D_MODEL), F32),
        scratch_shapes=[pltpu.VMEM((tb, D_MODEL), F32)] * (2 * TOP_K) + [pltpu.SemaphoreType.DMA((2,))],
        compiler_params=pltpu.CompilerParams(
            dimension_semantics=("arbitrary",), vmem_limit_bytes=VMEM_LIMIT),
        name="moe_combine_ln",
    )(pos3, pos3, route, h, ln_g, ln_b, ys)


def _routed_moe(h, route, route_t, counts, w1b, w3b, w2b, ln_g, ln_b):
    T = h.shape[0]
    n_tiles = (TOP_K * T) // MOE_TM + N_EXPERTS
    i32 = jnp.int32
    cnt = counts[:, 0].astype(i32)
    ntile = (cnt + MOE_TM - 1) // MOE_TM
    tile_end = jnp.cumsum(ntile)
    n_valid = tile_end[-1:]
    seg_start = (tile_end - ntile) * MOE_TM
    info = route_t[2:6].astype(i32)
    eids = jnp.arange(N_EXPERTS, dtype=i32)
    start_of = lambda e: jnp.sum(jnp.where(e[None, :] == eids[:, None], seg_start[:, None], 0), axis=0)
    pos1 = start_of(info[0]) + info[2]
    pos2 = start_of(info[1]) + info[3]
    pos3 = jnp.concatenate([pos1.reshape(T // MOE_TB, 1, MOE_TB), pos2.reshape(T // MOE_TB, 1, MOE_TB)], axis=2)
    tile_ids = jnp.minimum(jnp.arange(n_tiles, dtype=i32), n_valid[0] - 1)
    tile_expert = jnp.sum((tile_ids[:, None] >= tile_end[None, :]).astype(i32), axis=1)

    order = jnp.argsort(jnp.concatenate([pos1, pos2])).astype(i32)
    n_rows = n_tiles * MOE_TM
    padded = jnp.concatenate([jnp.zeros((n_rows,), i32), jnp.where(order >= T, order - T, order),
                              jnp.zeros((2 * n_rows - TOP_K * T,), i32)])
    shift = seg_start - (jnp.cumsum(cnt) - cnt)
    row_expert = jnp.repeat(tile_expert, MOE_TM)
    src_tok = jnp.zeros((n_rows,), i32)
    for e in range(N_EXPERTS):
        view = lax.dynamic_slice(padded, (n_rows - shift[e],), (n_rows,))
        src_tok = jnp.where(row_expert == e, view, src_tok)

    ys = _experts(tile_expert, n_valid.astype(i32), src_tok.reshape(n_tiles, 1, MOE_TM), h, w1b, w3b, w2b)
    return _combine(pos3, route, h, ln_g, ln_b, ys)


def _retention_constants():
    hh = np.arange(HEADS, dtype=np.float64)
    log_gamma = np.log1p(-np.exp2(-5.0 - hh))
    pos = np.arange(CHUNK, dtype=np.float64)
    diff = pos[:, None] - pos[None, :]
    causal = diff >= 0
    decay = np.where(causal[None], np.exp(log_gamma[:, None, None] * np.where(causal, diff, 0.0)[None]), 0.0)
    zeta = np.exp(log_gamma[:, None] * (CHUNK - 1 - pos)[None])
    xi = np.exp(log_gamma[:, None] * (pos + 1.0)[None])
    cdec = np.exp(log_gamma * CHUNK)
    bc = lambda t: np.broadcast_to(t[:, :, None], (HEADS, CHUNK, LANES))
    return (jnp.asarray(cdec, F32), jnp.asarray(decay, F32),
            jnp.asarray(bc(zeta), F32), jnp.asarray(bc(xi), F32))


def _rope_tables(seq):
    inv = ROPE_BASE ** (-jnp.arange(0, HEAD_DIM, 2, dtype=F32) / HEAD_DIM)
    ang = jnp.arange(seq, dtype=F32)[:, None] * inv[None, :]
    cos = jnp.repeat(jnp.cos(ang), 2, axis=-1)
    sin = jnp.sin(ang)
    sin_signed = jnp.stack([-sin, sin], axis=-1).reshape(seq, HEAD_DIM)
    return cos, sin_signed


def kernel(x, w_in, ret_norm_g, ret_norm_b, conv_w, conv_b, conv_norm_g, conv_norm_b, w_out, ln1_g, ln1_b, w_router_group, b_router_group, w_router_expert, b_router_expert, w1, w3, w2, ln2_g, ln2_b):
    batch, seq, _ = x.shape
    T = batch * seq
    x2 = x.reshape(T, D_MODEL)

    proj3 = _in_proj(x2, w_in.astype(BF16))

    cos_t, sin_t = _rope_tables(seq)
    per_head = lambda t: t.reshape(HEADS, 1, LANES)
    cw = jnp.transpose(conv_w.reshape(CONV_TAPS, CONV_GROUPS, LANES), (1, 0, 2))
    ret3, conv3, w1b, w3b, w2b, w_out_b = _mixer(
        proj3, cos_t, sin_t, _retention_constants(), per_head(ret_norm_g), per_head(ret_norm_b), cw,
        per_head(conv_b), per_head(conv_norm_g), per_head(conv_norm_b), w1, w3, w2, w_out, batch, seq)

    wr = jnp.concatenate([jnp.transpose(w_router_expert, (1, 0, 2)).reshape(D_MODEL, N_EXPERTS),
                          w_router_group], axis=1)
    wr = jnp.pad(wr, ((0, 0), (0, LANES - wr.shape[1])))
    br = jnp.pad(jnp.concatenate([b_router_expert.reshape(-1), b_router_group]),
                 (0, LANES - N_EXPERTS - N_GROUPS)).reshape(1, LANES)

    h, route, route_t, counts = _out_proj(ret3, conv3, x2, w_out_b, ln1_g.reshape(1, -1),
                                          ln1_b.reshape(1, -1), wr.astype(BF16), br)

    y = _routed_moe(h, route, route_t, counts, w1b.reshape(w1.shape), w3b.reshape(w3.shape),
                    w2b.reshape(w2.shape), ln2_g.reshape(1, -1), ln2_b.reshape(1, -1))
    return y.reshape(batch, seq, D_MODEL)
```

```python
import functools
import math

import numpy as np
import jax
import jax.numpy as jnp
from jax import lax
from jax.experimental import pallas as pl
from jax.experimental.pallas import tpu as pltpu

D_MODEL = 2048
RET_WIDTH = D_MODEL // 2
CONV_WIDTH = D_MODEL - RET_WIDTH
HEADS = 8
HEAD_DIM = RET_WIDTH // HEADS
CONV_GROUPS = 8
CONV_TAPS = 31
CHUNK = 128
ROPE_BASE = 10000.0
IN_COLS = 4 * RET_WIDTH + 2 * CONV_WIDTH
N_GROUPS = 4
EXPERTS_PER_GROUP = 4
N_EXPERTS = N_GROUPS * EXPERTS_PER_GROUP
TOP_K = 2
D_FF = D_MODEL // 2
LN_EPS = 1e-5
ALPHA = 2.0 ** 0.25

LANES = 128
SUBLANES = 8
VMEM_LIMIT = 56 * 1024 * 1024
CONV_HALO = 32

F32 = jnp.float32
BF16 = jnp.bfloat16


def _sigmoid(x):
    return 1.0 / (1.0 + jnp.exp(-x))


def _lane_norm(x, g, b):
    mu = jnp.mean(x, axis=-1, keepdims=True)
    xc = x - mu
    var = jnp.mean(xc * xc, axis=-1, keepdims=True)
    return xc * lax.rsqrt(var + LN_EPS) * g + b


def _inproj_kernel(x_ref, w_ref, o_ref):
    res = jnp.dot(x_ref[...].astype(BF16), w_ref[...], preferred_element_type=F32)
    for s in range(o_ref.shape[0]):
        o_ref[s] = res[:, s * LANES:(s + 1) * LANES].astype(BF16)


def _in_proj(x2, w_in_b, tm=512, tn=2048):
    T = x2.shape[0]
    nblk = tn // LANES
    return pl.pallas_call(
        _inproj_kernel,
        grid=(IN_COLS // tn, T // tm),
        in_specs=[pl.BlockSpec((tm, D_MODEL), lambda j, i: (i, 0)),
                  pl.BlockSpec((D_MODEL, tn), lambda j, i: (0, j))],
        out_specs=pl.BlockSpec((nblk, tm, LANES), lambda j, i: (j, i, 0)),
        out_shape=jax.ShapeDtypeStruct((IN_COLS // LANES, T, LANES), BF16),
        compiler_params=pltpu.CompilerParams(
            dimension_semantics=("arbitrary", "arbitrary"), vmem_limit_bytes=VMEM_LIMIT),
        name="in_proj",
    )(x2, w_in_b)


def _rotary(t, cos, sin_signed, even_lane):
    partner = jnp.where(even_lane, pltpu.roll(t, LANES - 1, 1), pltpu.roll(t, 1, 1))
    return t * cos + partner * sin_signed


def _mixer_kernel(cdec_ref, q_ref, k_ref, v_ref, gate_ref, ga_ref, gb_ref, cos_ref, sin_ref,
                  decay_ref, zeta_ref, xi_ref, prm_ref, cw_ref,
                  w1f_ref, w3f_ref, w2f_ref, wof_ref,
                  ret_ref, conv_ref, w1b_ref, w3b_ref, w2b_ref, wob_ref, state_ref, uext_ref, *, rows):
    w1b_ref[...] = w1f_ref[...].astype(BF16)
    w3b_ref[...] = w3f_ref[...].astype(BF16)
    w2b_ref[...] = w2f_ref[...].astype(BF16)
    wob_ref[...] = wof_ref[...].astype(BF16)

    h = pl.program_id(1)
    n = pl.program_id(2)

    @pl.when(n == 0)
    def _():
        state_ref[...] = jnp.zeros_like(state_ref)
        uext_ref[0:CONV_HALO, :] = jnp.zeros((CONV_HALO, LANES), F32)

    even_lane = (lax.broadcasted_iota(jnp.int32, (CHUNK, LANES), 1) % 2) == 0
    cdec = cdec_ref[h]
    decay = decay_ref[0]
    zeta = zeta_ref[0]
    xi = xi_ref[0]
    rg, rb, cb, cg, cbeta = (prm_ref[0, r:r + 1, :] for r in range(5))
    scale = HEAD_DIM ** -0.5

    state = state_ref[...]
    for c in range(rows // CHUNK):
        rs = slice(c * CHUNK, (c + 1) * CHUNK)
        cos = cos_ref[rs, :]
        sin = sin_ref[rs, :]
        q = _rotary(q_ref[0, rs, :].astype(F32), cos, sin, even_lane)
        k = _rotary(k_ref[0, rs, :].astype(F32), cos, sin, even_lane) * scale
        v = v_ref[0, rs, :]
        scores = lax.dot_general(q.astype(BF16), k.astype(BF16), (((1,), (1,)), ((), ())),
                                 preferred_element_type=F32) * decay
        inner = jnp.dot(scores.astype(BF16), v, preferred_element_type=F32)
        cross = jnp.dot((q * xi).astype(BF16), state.astype(BF16), preferred_element_type=F32)
        kz_t = jnp.transpose(k * zeta).astype(BF16)
        state = cdec * state + jnp.dot(kz_t, v, preferred_element_type=F32)
        y = _lane_norm(inner + cross, rg, rb)
        g = gate_ref[0, rs, :].astype(F32)
        ret_ref[0, rs, :] = (g * _sigmoid(g) * y).astype(BF16)
    state_ref[...] = state

    a = ga_ref[0].astype(F32)
    b = gb_ref[0].astype(F32)
    uext_ref[CONV_HALO:CONV_HALO + rows, :] = a * _sigmoid(b)

    blk = 64
    first = CONV_HALO - (CONV_TAPS - 1)
    for r0 in range(0, rows, blk):
        acc = jnp.zeros((blk, LANES), F32)
        for j in range(CONV_TAPS):
            acc = acc + cw_ref[0, j:j + 1, :] * uext_ref[r0 + first + j:r0 + first + j + blk, :]
        y = _lane_norm(acc + cb, cg, cbeta)
        conv_ref[0, r0:r0 + blk, :] = (y * _sigmoid(y)).astype(BF16)
    uext_ref[0:CONV_HALO, :] = uext_ref[rows:rows + CONV_HALO, :]


def _mixer(proj3, cos_t, sin_t, consts, head_params, cw, w1, w3, w2, w_out, batch, seq, rows=1024):
    T = batch * seq
    ns = seq // rows
    cdec, decay, zeta, xi = consts
    steps = batch * HEADS * ns
    w1s = w1.reshape(-1, w1.shape[-1])
    w3s = w3.reshape(-1, w3.shape[-1])
    w2s = w2.reshape(-1, w2.shape[-1])
    step_of = lambda b, h, n: ((b * HEADS + h) * ns + n, 0)
    wspec = lambda w: pl.BlockSpec((w.shape[0] // steps, w.shape[1]), step_of)
    for w in (w1s, w3s, w2s, w_out):
        assert w.shape[0] % (steps * 16) == 0

    def pspec(off):
        return pl.BlockSpec((1, rows, LANES), lambda b, h, n: (off + h, b * ns + n, 0))

    def hspec(d1):
        return pl.BlockSpec((1, d1, LANES), lambda b, h, n: (h, 0, 0))

    return pl.pallas_call(
        functools.partial(_mixer_kernel, rows=rows),
        grid=(batch, HEADS, ns),
        in_specs=[pl.BlockSpec(memory_space=pltpu.SMEM),
                  pspec(0), pspec(8), pspec(16), pspec(24), pspec(32), pspec(40),
                  pl.BlockSpec((rows, LANES), lambda b, h, n: (n, 0)),
                  pl.BlockSpec((rows, LANES), lambda b, h, n: (n, 0)),
                  hspec(CHUNK), hspec(CHUNK), hspec(CHUNK),
                  hspec(SUBLANES), hspec(CONV_TAPS),
                  wspec(w1s), wspec(w3s), wspec(w2s), wspec(w_out)],
        out_specs=[pl.BlockSpec((1, rows, LANES), lambda b, h, n: (h, b * ns + n, 0)),
                   pl.BlockSpec((1, rows, LANES), lambda b, h, n: (h, b * ns + n, 0)),
                   wspec(w1s), wspec(w3s), wspec(w2s), wspec(w_out)],
        out_shape=[jax.ShapeDtypeStruct((HEADS, T, LANES), BF16),
                   jax.ShapeDtypeStruct((CONV_GROUPS, T, LANES), BF16),
                   jax.ShapeDtypeStruct(w1s.shape, BF16),
                   jax.ShapeDtypeStruct(w3s.shape, BF16),
                   jax.ShapeDtypeStruct(w2s.shape, BF16),
                   jax.ShapeDtypeStruct(w_out.shape, BF16)],
        scratch_shapes=[pltpu.VMEM((HEAD_DIM, HEAD_DIM), F32),
                        pltpu.VMEM((rows + CONV_HALO, LANES), F32)],
        compiler_params=pltpu.CompilerParams(
            dimension_semantics=("arbitrary", "arbitrary", "arbitrary"), vmem_limit_bytes=VMEM_LIMIT),
        name="mixer",
    )(cdec, proj3, proj3, proj3, proj3, proj3, proj3, cos_t, sin_t, decay, zeta, xi,
      head_params, cw, w1s, w3s, w2s, w_out)


def _layer_norm(z, g, b):
    mu = jnp.mean(z, axis=-1, keepdims=True)
    zc = z - mu
    var = jnp.mean(zc * zc, axis=-1, keepdims=True)
    return zc * lax.rsqrt(var + LN_EPS) * g + b


def _outproj_kernel(ret_ref, conv_ref, x_ref, w_ref, g_ref, b_ref, wr_ref, br_ref,
                    h_ref, route_ref, route_t_ref, counts_ref, carry_ref):
    @pl.when(pl.program_id(0) == 0)
    def _():
        carry_ref[...] = jnp.zeros_like(carry_ref)

    mix = jnp.concatenate([ret_ref[s] for s in range(HEADS)] +
                          [conv_ref[s] for s in range(CONV_GROUPS)], axis=-1)
    y = jnp.dot(mix, w_ref[...], preferred_element_type=F32)
    h = _layer_norm(ALPHA * x_ref[...] + y, g_ref[...], b_ref[...])
    h_ref[...] = h

    logits = jnp.dot(h.astype(BF16), wr_ref[...], preferred_element_type=F32) + br_ref[...]

    tm = logits.shape[0]
    lt = jnp.transpose(logits)
    neg = jnp.float32(-jnp.inf)
    big = jnp.float32(1e9)
    grow = lax.broadcasted_iota(jnp.int32, (SUBLANES, tm), 0).astype(F32)
    gmask = grow < N_GROUPS
    gl = jnp.where(gmask, lt[N_EXPERTS:N_EXPERTS + SUBLANES, :], neg)
    gmax = jnp.max(gl, axis=0, keepdims=True)
    gidx = jnp.min(jnp.where(gl == gmax, grow, big), axis=0, keepdims=True)
    g_w = 1.0 / jnp.sum(jnp.where(gmask, jnp.exp(gl - gmax), 0.0), axis=0, keepdims=True)
    erow = lax.broadcasted_iota(jnp.int32, (N_EXPERTS, tm), 0).astype(F32)
    lo = gidx * EXPERTS_PER_GROUP
    emask = (erow >= lo) & (erow < lo + EXPERTS_PER_GROUP)
    el = jnp.where(emask, lt[0:N_EXPERTS, :], neg)
    top1 = jnp.max(el, axis=0, keepdims=True)
    idx1 = jnp.min(jnp.where(el == top1, erow, big), axis=0, keepdims=True)
    el2 = jnp.where(erow == idx1, neg, el)
    top2 = jnp.max(el2, axis=0, keepdims=True)
    idx2 = jnp.min(jnp.where(el2 == top2, erow, big), axis=0, keepdims=True)
    e2 = jnp.exp(top2 - top1)
    w1 = g_w / (1.0 + e2)
    w2 = w1 * e2

    sel1 = erow == idx1
    sel2 = erow == idx2
    onehot = jnp.where(sel1, 1.0, 0.0) + jnp.where(sel2, 1.0, 0.0)
    r_i = lax.broadcasted_iota(jnp.int32, (tm, tm), 0)
    c_i = lax.broadcasted_iota(jnp.int32, (tm, tm), 1)
    earlier = jnp.where(r_i < c_i, 1.0, 0.0).astype(BF16)
    carry = carry_ref[:, 0:1]
    prefix = jnp.dot(onehot.astype(BF16), earlier, preferred_element_type=F32) + carry
    rank1 = jnp.sum(jnp.where(sel1, prefix, 0.0), axis=0, keepdims=True)
    rank2 = jnp.sum(jnp.where(sel2, prefix, 0.0), axis=0, keepdims=True)
    total = carry + jnp.sum(onehot, axis=1, keepdims=True)
    carry_ref[...] = jnp.broadcast_to(total, carry_ref.shape)
    counts_ref[...] = jnp.broadcast_to(total, counts_ref.shape)

    route_t = jnp.concatenate([w1, w2, idx1, idx2, rank1, rank2, jnp.zeros((2, tm), F32)], axis=0)
    route_t_ref[...] = route_t
    route_ref[...] = jnp.transpose(
        jnp.concatenate([route_t, jnp.zeros((LANES - SUBLANES, tm), F32)], axis=0))


def _out_proj(ret3, conv3, x2, w_out_b, ln_g, ln_b, wr_b, br, tm=512):
    T = x2.shape[0]
    cur = lambda i: (0, i, 0)
    row = lambda i: (i, 0)
    const = lambda i: (0, 0)
    return pl.pallas_call(
        _outproj_kernel,
        grid=(T // tm,),
        in_specs=[pl.BlockSpec((HEADS, tm, LANES), cur),
                  pl.BlockSpec((CONV_GROUPS, tm, LANES), cur),
                  pl.BlockSpec((tm, D_MODEL), row),
                  pl.BlockSpec((D_MODEL, D_MODEL), const),
                  pl.BlockSpec((1, D_MODEL), const),
                  pl.BlockSpec((1, D_MODEL), const),
                  pl.BlockSpec((D_MODEL, LANES), const),
                  pl.BlockSpec((1, LANES), const)],
        out_specs=[pl.BlockSpec((tm, D_MODEL), row),
                   pl.BlockSpec((tm, LANES), row),
                   pl.BlockSpec((SUBLANES, tm), lambda i: (0, i)),
                   pl.BlockSpec((N_EXPERTS, LANES), const)],
        out_shape=[jax.ShapeDtypeStruct((T, D_MODEL), F32),
                   jax.ShapeDtypeStruct((T, LANES), F32),
                   jax.ShapeDtypeStruct((SUBLANES, T), F32),
                   jax.ShapeDtypeStruct((N_EXPERTS, LANES), F32)],
        scratch_shapes=[pltpu.VMEM((N_EXPERTS, LANES), F32)],
        compiler_params=pltpu.CompilerParams(
            dimension_semantics=("arbitrary",), vmem_limit_bytes=VMEM_LIMIT),
        name="out_proj_ln_route",
    )(ret3, conv3, x2, w_out_b, ln_g, ln_b, wr_b, br)


MOE_TM = 512
MOE_TB = 512


def _row_copy(src_ref, src_row, dst_ref, dst_row, sem):
    return pltpu.make_async_copy(src_ref.at[pl.ds(src_row, 1), :], dst_ref.at[pl.ds(dst_row, 1), :], sem)


def _experts_kernel(te_ref, nv_ref, src_ref, src_next_ref, h_ref, w1_ref, w3_ref, w2_ref, y_ref, x_ref, sems):
    i = pl.program_id(0)
    last = pl.num_programs(0) - 1
    slot = i % 2
    used = i < nv_ref[0]

    def gather_wait(s):
        pltpu.make_async_copy(h_ref.at[pl.ds(0, MOE_TM), :], x_ref.at[s, pl.ds(0, MOE_TM), :], sems.at[s]).wait()

    @pl.when(i == 0)
    def _():
        x_ref[:, MOE_TM:, :] = jnp.zeros((2, SUBLANES, D_MODEL), F32)

        def first(t, carry):
            _row_copy(h_ref, src_ref[0, 0, t], x_ref.at[0], t, sems.at[0]).start()
            return carry
        lax.fori_loop(0, MOE_TM, first, 0, unroll=8)

    @pl.when(used)
    def _():
        gather_wait(slot)
        xb = x_ref[slot, 0:MOE_TM, :].astype(BF16)
        a1 = jnp.dot(xb, w1_ref[0], preferred_element_type=F32)

        def issue_half(k, width):
            for t in range(k * MOE_TM // 2, (k + 1) * MOE_TM // 2):
                _row_copy(h_ref, src_next_ref[0, 0, t], x_ref.at[1 - slot], t, sems.at[1 - slot]).start()
            spare = pltpu.bitcast(x_ref[1 - slot, MOE_TM:, k * LANES:(k + 1) * LANES], jnp.uint32)
            zero = pltpu.bitcast((spare >> 16) >> 16, F32)
            return jnp.tile(zero, (MOE_TM // SUBLANES, width // LANES))

        a3 = jnp.dot(xb, w3_ref[0], preferred_element_type=F32) + issue_half(0, D_FF)
        act = (a1 * _sigmoid(a1) * a3).astype(BF16)
        y_ref[...] = jnp.dot(act, w2_ref[0], preferred_element_type=F32) + issue_half(1, D_MODEL)

        @pl.when(i == last)
        def _():
            gather_wait(1 - slot)

    @pl.when(jnp.logical_not(used))
    def _():
        @pl.when(i == nv_ref[0])
        def _():
            gather_wait(slot)
        y_ref[...] = jnp.zeros_like(y_ref)


def _experts(tile_expert, n_valid, src_tok, h, w1b, w3b, w2b):
    n_tiles = src_tok.shape[0]
    row = lambda i, te, nv: (i, 0)
    wsel = lambda i, te, nv: (te[i], 0, 0)
    smem_row = lambda f: pl.BlockSpec((1, 1, MOE_TM), f, memory_space=pltpu.SMEM)
    return pl.pallas_call(
        _experts_kernel,
        grid_spec=pltpu.PrefetchScalarGridSpec(
            num_scalar_prefetch=2,
            grid=(n_tiles,),
            in_specs=[smem_row(lambda i, te, nv: (i, 0, 0)),
                      smem_row(lambda i, te, nv: (jnp.minimum(i + 1, n_tiles - 1), 0, 0)),
                      pl.BlockSpec(memory_space=pl.ANY),
                      pl.BlockSpec((1, D_MODEL, D_FF), wsel),
                      pl.BlockSpec((1, D_MODEL, D_FF), wsel),
                      pl.BlockSpec((1, D_FF, D_MODEL), wsel)],
            out_specs=pl.BlockSpec((MOE_TM, D_MODEL), row),
            scratch_shapes=[pltpu.VMEM((2, MOE_TM + SUBLANES, D_MODEL), F32),
                            pltpu.SemaphoreType.DMA((2,))]),
        out_shape=jax.ShapeDtypeStruct((n_tiles * MOE_TM, D_MODEL), F32),
        compiler_params=pltpu.CompilerParams(
            dimension_semantics=("arbitrary",), vmem_limit_bytes=VMEM_LIMIT),
        name="moe_experts",
    )(tile_expert, n_valid, src_tok, src_tok, h, w1b, w3b, w2b)


def _combine_kernel(pos_ref, pos_next_ref, route_ref, h_ref, g_ref, b_ref, ys_ref, o_ref,
                    ya0_ref, yb0_ref, ya1_ref, yb1_ref, sems):
    tb = h_ref.shape[0]
    i = pl.program_id(0)
    bufs = ((ya0_ref, yb0_ref), (ya1_ref, yb1_ref))

    def gather_wait(s):
        for k in range(TOP_K):
            pltpu.make_async_copy(ys_ref.at[pl.ds(0, tb), :], bufs[s][k], sems.at[s]).wait()

    @pl.when(i == 0)
    def _():
        def first(t, carry):
            for k in range(TOP_K):
                _row_copy(ys_ref, pos_ref[0, 0, k * tb + t], bufs[0][k], t, sems.at[0]).start()
            return carry
        lax.fori_loop(0, tb, first, 0, unroll=8)

    for par in range(2):
        @pl.when(i % 2 == par)
        def _():
            gather_wait(par)
            for t in range(tb):
                for k in range(TOP_K):
                    _row_copy(ys_ref, pos_next_ref[0, 0, k * tb + t], bufs[1 - par][k], t, sems.at[1 - par]).start()
            route = route_ref[...]
            ffn = route[:, 0:1] * bufs[par][0][...] + route[:, 1:2] * bufs[par][1][...]
            o_ref[...] = _layer_norm(ALPHA * h_ref[...] + ffn, g_ref[...], b_ref[...])

            @pl.when(i == pl.num_programs(0) - 1)
            def _():
                gather_wait(1 - par)


def _combine(pos3, route, h, ln_g, ln_b, ys):
    T = h.shape[0]
    tb = MOE_TB
    row = lambda i: (i, 0)
    const = lambda i: (0, 0)
    return pl.pallas_call(
        _combine_kernel,
        grid=(T // tb,),
        in_specs=[pl.BlockSpec((1, 1, 2 * tb), lambda i: (i, 0, 0), memory_space=pltpu.SMEM),
                  pl.BlockSpec((1, 1, 2 * tb), lambda i: (jnp.minimum(i + 1, T // tb - 1), 0, 0),
                               memory_space=pltpu.SMEM),
                  pl.BlockSpec((tb, LANES), row),
                  pl.BlockSpec((tb, D_MODEL), row),
                  pl.BlockSpec((1, D_MODEL), const),
                  pl.BlockSpec((1, D_MODEL), const),
                  pl.BlockSpec(memory_space=pl.ANY)],
        out_specs=pl.BlockSpec((tb, D_MODEL), row),
        out_shape=jax.ShapeDtypeStruct((T, D_MODEL), F32),
        scratch_shapes=[pltpu.VMEM((tb, D_MODEL), F32)] * (2 * TOP_K) + [pltpu.SemaphoreType.DMA((2,))],
        compiler_params=pltpu.CompilerParams(
            dimension_semantics=("arbitrary",), vmem_limit_bytes=VMEM_LIMIT),
        name="moe_combine_ln",
    )(pos3, pos3, route, h, ln_g, ln_b, ys)


def _routed_moe(h, route, route_t, counts, w1b, w3b, w2b, ln_g, ln_b):
    T = h.shape[0]
    n_tiles = (TOP_K * T) // MOE_TM + N_EXPERTS
    i32 = jnp.int32
    cnt = counts[:, 0].astype(i32)
    ntile = (cnt + MOE_TM - 1) // MOE_TM
    tile_end = jnp.cumsum(ntile)
    n_valid = tile_end[-1:]
    seg_start = (tile_end - ntile) * MOE_TM
    info = route_t[2:6].astype(i32)
    eids = jnp.arange(N_EXPERTS, dtype=i32)
    start_of = lambda e: jnp.sum(jnp.where(e[None, :] == eids[:, None], seg_start[:, None], 0), axis=0)
    pos1 = start_of(info[0]) + info[2]
    pos2 = start_of(info[1]) + info[3]
    pos3 = jnp.concatenate([pos1.reshape(T // MOE_TB, 1, MOE_TB), pos2.reshape(T // MOE_TB, 1, MOE_TB)], axis=2)
    tile_ids = jnp.minimum(jnp.arange(n_tiles, dtype=i32), n_valid[0] - 1)
    tile_expert = jnp.sum((tile_ids[:, None] >= tile_end[None, :]).astype(i32), axis=1)

    order = jnp.argsort(jnp.concatenate([pos1, pos2])).astype(i32)
    n_rows = n_tiles * MOE_TM
    padded = jnp.concatenate([jnp.zeros((n_rows,), i32), jnp.where(order >= T, order - T, order),
                              jnp.zeros((2 * n_rows - TOP_K * T,), i32)])
    shift = seg_start - (jnp.cumsum(cnt) - cnt)
    row_expert = jnp.repeat(tile_expert, MOE_TM)
    src_tok = jnp.zeros((n_rows,), i32)
    for e in range(N_EXPERTS):
        view = lax.dynamic_slice(padded, (n_rows - shift[e],), (n_rows,))
        src_tok = jnp.where(row_expert == e, view, src_tok)

    ys = _experts(tile_expert, n_valid.astype(i32), src_tok.reshape(n_tiles, 1, MOE_TM), h, w1b, w3b, w2b)
    return _combine(pos3, route, h, ln_g, ln_b, ys)


def _retention_constants():
    hh = np.arange(HEADS, dtype=np.float64)
    log_gamma = np.log1p(-np.exp2(-5.0 - hh))
    pos = np.arange(CHUNK, dtype=np.float64)
    diff = pos[:, None] - pos[None, :]
    causal = diff >= 0
    decay = np.where(causal[None], np.exp(log_gamma[:, None, None] * np.where(causal, diff, 0.0)[None]), 0.0)
    zeta = np.exp(log_gamma[:, None] * (CHUNK - 1 - pos)[None])
    xi = np.exp(log_gamma[:, None] * (pos + 1.0)[None])
    cdec = np.exp(log_gamma * CHUNK)
    bc = lambda t: np.broadcast_to(t[:, :, None], (HEADS, CHUNK, LANES))
    return (jnp.asarray(cdec, F32), jnp.asarray(decay, F32),
            jnp.asarray(bc(zeta), F32), jnp.asarray(bc(xi), F32))


def _rope_tables(seq):
    inv = ROPE_BASE ** (-jnp.arange(0, HEAD_DIM, 2, dtype=F32) / HEAD_DIM)
    ang = jnp.arange(seq, dtype=F32)[:, None] * inv[None, :]
    cos = jnp.repeat(jnp.cos(ang), 2, axis=-1)
    sin = jnp.sin(ang)
    sin_signed = jnp.stack([-sin, sin], axis=-1).reshape(seq, HEAD_DIM)
    return cos, sin_signed


def kernel(x, w_in, ret_norm_g, ret_norm_b, conv_w, conv_b, conv_norm_g, conv_norm_b, w_out, ln1_g, ln1_b, w_router_group, b_router_group, w_router_expert, b_router_expert, w1, w3, w2, ln2_g, ln2_b):
    batch, seq, _ = x.shape
    T = batch * seq
    x2 = x.reshape(T, D_MODEL)

    proj3 = _in_proj(x2, w_in.astype(BF16))

    cos_t, sin_t = _rope_tables(seq)
    head_params = jnp.stack([ret_norm_g, ret_norm_b, conv_b, conv_norm_g, conv_norm_b]
                            + [jnp.zeros_like(conv_b)] * (SUBLANES - 5)).reshape(SUBLANES, HEADS, LANES)
    head_params = jnp.transpose(head_params, (1, 0, 2))
    cw = jnp.transpose(conv_w.reshape(CONV_TAPS, CONV_GROUPS, LANES), (1, 0, 2))
    ret3, conv3, w1b, w3b, w2b, w_out_b = _mixer(
        proj3, cos_t, sin_t, _retention_constants(), head_params, cw, w1, w3, w2, w_out, batch, seq)

    wr = jnp.concatenate([jnp.transpose(w_router_expert, (1, 0, 2)).reshape(D_MODEL, N_EXPERTS),
                          w_router_group], axis=1)
    wr = jnp.pad(wr, ((0, 0), (0, LANES - wr.shape[1])))
    br = jnp.pad(jnp.concatenate([b_router_expert.reshape(-1), b_router_group]),
                 (0, LANES - N_EXPERTS - N_GROUPS)).reshape(1, LANES)

    h, route, route_t, counts = _out_proj(ret3, conv3, x2, w_out_b, ln1_g.reshape(1, -1),
                                          ln1_b.reshape(1, -1), wr.astype(BF16), br)

    y = _routed_moe(h, route, route_t, counts, w1b.reshape(w1.shape), w3b.reshape(w3.shape),
                    w2b.reshape(w2.shape), ln2_g.reshape(1, -1), ln2_b.reshape(1, -1))
    return y.reshape(batch, seq, D_MODEL)
```

```python
import functools
import math

import numpy as np
import jax
import jax.numpy as jnp
from jax import lax
from jax.experimental import pallas as pl
from jax.experimental.pallas import tpu as pltpu

D_MODEL = 2048
RET_WIDTH = D_MODEL // 2
CONV_WIDTH = D_MODEL - RET_WIDTH
HEADS = 8
HEAD_DIM = RET_WIDTH // HEADS
CONV_GROUPS = 8
CONV_TAPS = 31
CHUNK = 128
ROPE_BASE = 10000.0
IN_COLS = 4 * RET_WIDTH + 2 * CONV_WIDTH
N_GROUPS = 4
EXPERTS_PER_GROUP = 4
N_EXPERTS = N_GROUPS * EXPERTS_PER_GROUP
TOP_K = 2
D_FF = D_MODEL // 2
LN_EPS = 1e-5
ALPHA = 2.0 ** 0.25

LANES = 128
SUBLANES = 8
VMEM_LIMIT = 56 * 1024 * 1024
CONV_HALO = 32

F32 = jnp.float32
BF16 = jnp.bfloat16


def _sigmoid(x):
    return 1.0 / (1.0 + jnp.exp(-x))


def _lane_norm(x, g, b):
    mu = jnp.mean(x, axis=-1, keepdims=True)
    xc = x - mu
    var = jnp.mean(xc * xc, axis=-1, keepdims=True)
    return xc * lax.rsqrt(var + LN_EPS) * g + b


HEADS_PER_STEP = 2
MIX_ROWS = 512
HEAD_COLS = 6 * LANES


def _rotary(t, cos, sin_signed, even_lane):
    partner = jnp.where(even_lane, pltpu.roll(t, LANES - 1, 1), pltpu.roll(t, 1, 1))
    return t * cos + partner * sin_signed


def _mixer_kernel(cdec_ref, x_ref, w_ref, cos_ref, sin_ref, decay_ref, zeta_ref, xi_ref, prm_ref, cw_ref,
                  w1f_ref, w3f_ref, w2f_ref, wof_ref,
                  ret_ref, conv_ref, w1b_ref, w3b_ref, w2b_ref, wob_ref, state_ref, uext_ref):
    w1b_ref[...] = w1f_ref[...].astype(BF16)
    w3b_ref[...] = w3f_ref[...].astype(BF16)
    w2b_ref[...] = w2f_ref[...].astype(BF16)
    wob_ref[...] = wof_ref[...].astype(BF16)

    rows = x_ref.shape[0]
    hps = w_ref.shape[0]
    n = pl.program_id(1)
    head0 = pl.program_id(2) * hps

    @pl.when(n == 0)
    def _():
        for j in range(hps):
            state_ref[head0 + j] = jnp.zeros((HEAD_DIM, HEAD_DIM), F32)
            uext_ref[head0 + j, 0:CONV_HALO, :] = jnp.zeros((CONV_HALO, LANES), F32)

    even_lane = (lax.broadcasted_iota(jnp.int32, (CHUNK, LANES), 1) % 2) == 0
    scale = HEAD_DIM ** -0.5
    blk = 64
    first = CONV_HALO - (CONV_TAPS - 1)
    xb = x_ref[...].astype(BF16)

    for j in range(hps):
        head = head0 + j
        proj = jnp.dot(xb, w_ref[j], preferred_element_type=F32)
        col = lambda c: proj[:, c * LANES:(c + 1) * LANES]
        cdec = cdec_ref[head]
        decay = decay_ref[j]
        zeta = zeta_ref[j]
        xi = xi_ref[j]
        rg, rb, cb, cg, cbeta = (prm_ref[j, r:r + 1, :] for r in range(5))

        state = state_ref[head]
        for c in range(rows // CHUNK):
            rs = slice(c * CHUNK, (c + 1) * CHUNK)
            cos = cos_ref[rs, :]
            sin = sin_ref[rs, :]
            q = _rotary(col(0)[rs], cos, sin, even_lane)
            k = _rotary(col(1)[rs], cos, sin, even_lane) * scale
            v = col(2)[rs].astype(BF16)
            scores = lax.dot_general(q.astype(BF16), k.astype(BF16), (((1,), (1,)), ((), ())),
                                     preferred_element_type=F32) * decay
            inner = jnp.dot(scores.astype(BF16), v, preferred_element_type=F32)
            cross = jnp.dot((q * xi).astype(BF16), state.astype(BF16), preferred_element_type=F32)
            kz_t = jnp.transpose(k * zeta).astype(BF16)
            state = cdec * state + jnp.dot(kz_t, v, preferred_element_type=F32)
            y = _lane_norm(inner + cross, rg, rb)
            g = col(3)[rs]
            ret_ref[j, rs, :] = (g * _sigmoid(g) * y).astype(BF16)
        state_ref[head] = state

        uext_ref[head, CONV_HALO:CONV_HALO + rows, :] = col(4) * _sigmoid(col(5))
        for r0 in range(0, rows, blk):
            acc = jnp.zeros((blk, LANES), F32)
            for t in range(CONV_TAPS):
                acc = acc + cw_ref[j, t:t + 1, :] * uext_ref[head, r0 + first + t:r0 + first + t + blk, :]
            y = _lane_norm(acc + cb, cg, cbeta)
            conv_ref[j, r0:r0 + blk, :] = (y * _sigmoid(y)).astype(BF16)
        uext_ref[head, 0:CONV_HALO, :] = uext_ref[head, rows:rows + CONV_HALO, :]


def _mixer(x2, w_heads, cos_t, sin_t, consts, head_params, cw, w1, w3, w2, w_out, batch, seq):
    T = batch * seq
    rows, hps = MIX_ROWS, HEADS_PER_STEP
    ns = seq // rows
    ng = HEADS // hps
    cdec, decay, zeta, xi = consts
    steps = batch * ns * ng
    w1s = w1.reshape(-1, w1.shape[-1])
    w3s = w3.reshape(-1, w3.shape[-1])
    w2s = w2.reshape(-1, w2.shape[-1])
    step_of = lambda b, n, g: ((b * ns + n) * ng + g, 0)
    wspec = lambda w: pl.BlockSpec((w.shape[0] // steps, w.shape[1]), step_of)
    for w in (w1s, w3s, w2s, w_out):
        assert w.shape[0] % (steps * 16) == 0

    def hspec(d1, d2=LANES):
        return pl.BlockSpec((hps, d1, d2), lambda b, n, g: (g, 0, 0))

    out_rows = pl.BlockSpec((hps, rows, LANES), lambda b, n, g: (g, b * ns + n, 0))
    return pl.pallas_call(
        _mixer_kernel,
        grid=(batch, ns, ng),
        in_specs=[pl.BlockSpec(memory_space=pltpu.SMEM),
                  pl.BlockSpec((rows, D_MODEL), lambda b, n, g: (b * ns + n, 0)),
                  hspec(D_MODEL, HEAD_COLS),
                  pl.BlockSpec((rows, LANES), lambda b, n, g: (n, 0)),
                  pl.BlockSpec((rows, LANES), lambda b, n, g: (n, 0)),
                  hspec(CHUNK), hspec(CHUNK), hspec(CHUNK),
                  hspec(SUBLANES), hspec(CONV_TAPS),
                  wspec(w1s), wspec(w3s), wspec(w2s), wspec(w_out)],
        out_specs=[out_rows, out_rows,
                   wspec(w1s), wspec(w3s), wspec(w2s), wspec(w_out)],
        out_shape=[jax.ShapeDtypeStruct((HEADS, T, LANES), BF16),
                   jax.ShapeDtypeStruct((CONV_GROUPS, T, LANES), BF16),
                   jax.ShapeDtypeStruct(w1s.shape, BF16),
                   jax.ShapeDtypeStruct(w3s.shape, BF16),
                   jax.ShapeDtypeStruct(w2s.shape, BF16),
                   jax.ShapeDtypeStruct(w_out.shape, BF16)],
        scratch_shapes=[pltpu.VMEM((HEADS, HEAD_DIM, HEAD_DIM), F32),
                        pltpu.VMEM((HEADS, rows + CONV_HALO, LANES), F32)],
        compiler_params=pltpu.CompilerParams(
            dimension_semantics=("arbitrary", "arbitrary", "arbitrary"), vmem_limit_bytes=VMEM_LIMIT),
        name="in_proj_mixer",
    )(cdec, x2, w_heads, cos_t, sin_t, decay, zeta, xi, head_params, cw, w1s, w3s, w2s, w_out)


def _layer_norm(z, g, b):
    mu = jnp.mean(z, axis=-1, keepdims=True)
    zc = z - mu
    var = jnp.mean(zc * zc, axis=-1, keepdims=True)
    return zc * lax.rsqrt(var + LN_EPS) * g + b


def _outproj_kernel(ret_ref, conv_ref, x_ref, w_ref, g_ref, b_ref, wr_ref, br_ref,
                    h_ref, route_ref, route_t_ref, counts_ref, carry_ref):
    @pl.when(pl.program_id(0) == 0)
    def _():
        carry_ref[...] = jnp.zeros_like(carry_ref)

    mix = jnp.concatenate([ret_ref[s] for s in range(HEADS)] +
                          [conv_ref[s] for s in range(CONV_GROUPS)], axis=-1)
    y = jnp.dot(mix, w_ref[...], preferred_element_type=F32)
    h = _layer_norm(ALPHA * x_ref[...] + y, g_ref[...], b_ref[...])
    h_ref[...] = h

    logits = jnp.dot(h.astype(BF16), wr_ref[...], preferred_element_type=F32) + br_ref[...]

    tm = logits.shape[0]
    lt = jnp.transpose(logits)
    neg = jnp.float32(-jnp.inf)
    big = jnp.float32(1e9)
    grow = lax.broadcasted_iota(jnp.int32, (SUBLANES, tm), 0).astype(F32)
    gmask = grow < N_GROUPS
    gl = jnp.where(gmask, lt[N_EXPERTS:N_EXPERTS + SUBLANES, :], neg)
    gmax = jnp.max(gl, axis=0, keepdims=True)
    gidx = jnp.min(jnp.where(gl == gmax, grow, big), axis=0, keepdims=True)
    g_w = 1.0 / jnp.sum(jnp.where(gmask, jnp.exp(gl - gmax), 0.0), axis=0, keepdims=True)
    erow = lax.broadcasted_iota(jnp.int32, (N_EXPERTS, tm), 0).astype(F32)
    lo = gidx * EXPERTS_PER_GROUP
    emask = (erow >= lo) & (erow < lo + EXPERTS_PER_GROUP)
    el = jnp.where(emask, lt[0:N_EXPERTS, :], neg)
    top1 = jnp.max(el, axis=0, keepdims=True)
    idx1 = jnp.min(jnp.where(el == top1, erow, big), axis=0, keepdims=True)
    el2 = jnp.where(erow == idx1, neg, el)
    top2 = jnp.max(el2, axis=0, keepdims=True)
    idx2 = jnp.min(jnp.where(el2 == top2, erow, big), axis=0, keepdims=True)
    e2 = jnp.exp(top2 - top1)
    w1 = g_w / (1.0 + e2)
    w2 = w1 * e2

    sel1 = erow == idx1
    sel2 = erow == idx2
    onehot = jnp.where(sel1, 1.0, 0.0) + jnp.where(sel2, 1.0, 0.0)
    r_i = lax.broadcasted_iota(jnp.int32, (tm, tm), 0)
    c_i = lax.broadcasted_iota(jnp.int32, (tm, tm), 1)
    earlier = jnp.where(r_i < c_i, 1.0, 0.0).astype(BF16)
    carry = carry_ref[:, 0:1]
    prefix = jnp.dot(onehot.astype(BF16), earlier, preferred_element_type=F32) + carry
    rank1 = jnp.sum(jnp.where(sel1, prefix, 0.0), axis=0, keepdims=True)
    rank2 = jnp.sum(jnp.where(sel2, prefix, 0.0), axis=0, keepdims=True)
    total = carry + jnp.sum(onehot, axis=1, keepdims=True)
    carry_ref[...] = jnp.broadcast_to(total, carry_ref.shape)
    counts_ref[...] = jnp.broadcast_to(total, counts_ref.shape)

    route_t = jnp.concatenate([w1, w2, idx1, idx2, rank1, rank2, jnp.zeros((2, tm), F32)], axis=0)
    route_t_ref[...] = route_t
    route_ref[...] = jnp.transpose(
        jnp.concatenate([route_t, jnp.zeros((LANES - SUBLANES, tm), F32)], axis=0))


def _out_proj(ret3, conv3, x2, w_out_b, ln_g, ln_b, wr_b, br, tm=512):
    T = x2.shape[0]
    cur = lambda i: (0, i, 0)
    row = lambda i: (i, 0)
    const = lambda i: (0, 0)
    return pl.pallas_call(
        _outproj_kernel,
        grid=(T // tm,),
        in_specs=[pl.BlockSpec((HEADS, tm, LANES), cur),
                  pl.BlockSpec((CONV_GROUPS, tm, LANES), cur),
                  pl.BlockSpec((tm, D_MODEL), row),
                  pl.BlockSpec((D_MODEL, D_MODEL), const),
                  pl.BlockSpec((1, D_MODEL), const),
                  pl.BlockSpec((1, D_MODEL), const),
                  pl.BlockSpec((D_MODEL, LANES), const),
                  pl.BlockSpec((1, LANES), const)],
        out_specs=[pl.BlockSpec((tm, D_MODEL), row),
                   pl.BlockSpec((tm, LANES), row),
                   pl.BlockSpec((SUBLANES, tm), lambda i: (0, i)),
                   pl.BlockSpec((N_EXPERTS, LANES), const)],
        out_shape=[jax.ShapeDtypeStruct((T, D_MODEL), F32),
                   jax.ShapeDtypeStruct((T, LANES), F32),
                   jax.ShapeDtypeStruct((SUBLANES, T), F32),
                   jax.ShapeDtypeStruct((N_EXPERTS, LANES), F32)],
        scratch_shapes=[pltpu.VMEM((N_EXPERTS, LANES), F32)],
        compiler_params=pltpu.CompilerParams(
            dimension_semantics=("arbitrary",), vmem_limit_bytes=VMEM_LIMIT),
        name="out_proj_ln_route",
    )(ret3, conv3, x2, w_out_b, ln_g, ln_b, wr_b, br)


MOE_TM = 512
MOE_TB = 512


def _row_copy(src_ref, src_row, dst_ref, dst_row, sem):
    return pltpu.make_async_copy(src_ref.at[pl.ds(src_row, 1), :], dst_ref.at[pl.ds(dst_row, 1), :], sem)


def _experts_kernel(te_ref, nv_ref, src_ref, src_next_ref, h_ref, w1_ref, w3_ref, w2_ref, y_ref, x_ref, sems):
    i = pl.program_id(0)
    last = pl.num_programs(0) - 1
    slot = i % 2
    used = i < nv_ref[0]

    def gather_wait(s):
        pltpu.make_async_copy(h_ref.at[pl.ds(0, MOE_TM), :], x_ref.at[s, pl.ds(0, MOE_TM), :], sems.at[s]).wait()

    @pl.when(i == 0)
    def _():
        x_ref[:, MOE_TM:, :] = jnp.zeros((2, SUBLANES, D_MODEL), F32)

        def first(t, carry):
            _row_copy(h_ref, src_ref[0, 0, t], x_ref.at[0], t, sems.at[0]).start()
            return carry
        lax.fori_loop(0, MOE_TM, first, 0, unroll=8)

    @pl.when(used)
    def _():
        gather_wait(slot)
        xb = x_ref[slot, 0:MOE_TM, :].astype(BF16)
        a1 = jnp.dot(xb, w1_ref[0], preferred_element_type=F32)

        def issue_half(k, width):
            for t in range(k * MOE_TM // 2, (k + 1) * MOE_TM // 2):
                _row_copy(h_ref, src_next_ref[0, 0, t], x_ref.at[1 - slot], t, sems.at[1 - slot]).start()
            spare = pltpu.bitcast(x_ref[1 - slot, MOE_TM:, k * LANES:(k + 1) * LANES], jnp.uint32)
            zero = pltpu.bitcast((spare >> 16) >> 16, F32)
            return jnp.tile(zero, (MOE_TM // SUBLANES, width // LANES))

        a3 = jnp.dot(xb, w3_ref[0], preferred_element_type=F32) + issue_half(0, D_FF)
        act = (a1 * _sigmoid(a1) * a3).astype(BF16)
        y_ref[...] = jnp.dot(act, w2_ref[0], preferred_element_type=F32) + issue_half(1, D_MODEL)

        @pl.when(i == last)
        def _():
            gather_wait(1 - slot)

    @pl.when(jnp.logical_not(used))
    def _():
        @pl.when(i == nv_ref[0])
        def _():
            gather_wait(slot)
        y_ref[...] = jnp.zeros_like(y_ref)


def _experts(tile_expert, n_valid, src_tok, h, w1b, w3b, w2b):
    n_tiles = src_tok.shape[0]
    row = lambda i, te, nv: (i, 0)
    wsel = lambda i, te, nv: (te[i], 0, 0)
    smem_row = lambda f: pl.BlockSpec((1, 1, MOE_TM), f, memory_space=pltpu.SMEM)
    return pl.pallas_call(
        _experts_kernel,
        grid_spec=pltpu.PrefetchScalarGridSpec(
            num_scalar_prefetch=2,
            grid=(n_tiles,),
            in_specs=[smem_row(lambda i, te, nv: (i, 0, 0)),
                      smem_row(lambda i, te, nv: (jnp.minimum(i + 1, n_tiles - 1), 0, 0)),
                      pl.BlockSpec(memory_space=pl.ANY),
                      pl.BlockSpec((1, D_MODEL, D_FF), wsel),
                      pl.BlockSpec((1, D_MODEL, D_FF), wsel),
                      pl.BlockSpec((1, D_FF, D_MODEL), wsel)],
            out_specs=pl.BlockSpec((MOE_TM, D_MODEL), row),
            scratch_shapes=[pltpu.VMEM((2, MOE_TM + SUBLANES, D_MODEL), F32),
                            pltpu.SemaphoreType.DMA((2,))]),
        out_shape=jax.ShapeDtypeStruct((n_tiles * MOE_TM, D_MODEL), F32),
        compiler_params=pltpu.CompilerParams(
            dimension_semantics=("arbitrary",), vmem_limit_bytes=VMEM_LIMIT),
        name="moe_experts",
    )(tile_expert, n_valid, src_tok, src_tok, h, w1b, w3b, w2b)


def _combine_kernel(pos_ref, pos_next_ref, route_ref, h_ref, g_ref, b_ref, ys_ref, o_ref,
                    ya0_ref, yb0_ref, ya1_ref, yb1_ref, sems):
    tb = h_ref.shape[0]
    i = pl.program_id(0)
    bufs = ((ya0_ref, yb0_ref), (ya1_ref, yb1_ref))

    def gather_wait(s):
        for k in range(TOP_K):
            pltpu.make_async_copy(ys_ref.at[pl.ds(0, tb), :], bufs[s][k], sems.at[s]).wait()

    @pl.when(i == 0)
    def _():
        def first(t, carry):
            for k in range(TOP_K):
                _row_copy(ys_ref, pos_ref[0, 0, k * tb + t], bufs[0][k], t, sems.at[0]).start()
            return carry
        lax.fori_loop(0, tb, first, 0, unroll=8)

    for par in range(2):
        @pl.when(i % 2 == par)
        def _():
            gather_wait(par)
            for t in range(tb):
                for k in range(TOP_K):
                    _row_copy(ys_ref, pos_next_ref[0, 0, k * tb + t], bufs[1 - par][k], t, sems.at[1 - par]).start()
            route = route_ref[...]
            ffn = route[:, 0:1] * bufs[par][0][...] + route[:, 1:2] * bufs[par][1][...]
            o_ref[...] = _layer_norm(ALPHA * h_ref[...] + ffn, g_ref[...], b_ref[...])

            @pl.when(i == pl.num_programs(0) - 1)
            def _():
                gather_wait(1 - par)


def _combine(pos3, route, h, ln_g, ln_b, ys):
    T = h.shape[0]
    tb = MOE_TB
    row = lambda i: (i, 0)
    const = lambda i: (0, 0)
    return pl.pallas_call(
        _combine_kernel,
        grid=(T // tb,),
        in_specs=[pl.BlockSpec((1, 1, 2 * tb), lambda i: (i, 0, 0), memory_space=pltpu.SMEM),
                  pl.BlockSpec((1, 1, 2 * tb), lambda i: (jnp.minimum(i + 1, T // tb - 1), 0, 0),
                               memory_space=pltpu.SMEM),
                  pl.BlockSpec((tb, LANES), row),
                  pl.BlockSpec((tb, D_MODEL), row),
                  pl.BlockSpec((1, D_MODEL), const),
                  pl.BlockSpec((1, D_MODEL), const),
                  pl.BlockSpec(memory_space=pl.ANY)],
        out_specs=pl.BlockSpec((tb, D_MODEL), row),
        out_shape=jax.ShapeDtypeStruct((T, D_MODEL), F32),
        scratch_shapes=[pltpu.VMEM((tb, D_MODEL), F32)] * (2 * TOP_K) + [pltpu.SemaphoreType.DMA((2,))],
        compiler_params=pltpu.CompilerParams(
            dimension_semantics=("arbitrary",), vmem_limit_bytes=VMEM_LIMIT),
        name="moe_combine_ln",
    )(pos3, pos3, route, h, ln_g, ln_b, ys)


def _routed_moe(h, route, route_t, counts, w1b, w3b, w2b, ln_g, ln_b):
    T = h.shape[0]
    n_tiles = (TOP_K * T) // MOE_TM + N_EXPERTS
    i32 = jnp.int32
    cnt = counts[:, 0].astype(i32)
    ntile = (cnt + MOE_TM - 1) // MOE_TM
    tile_end = jnp.cumsum(ntile)
    n_valid = tile_end[-1:]
    seg_start = (tile_end - ntile) * MOE_TM
    info = route_t[2:6].astype(i32)
    eids = jnp.arange(N_EXPERTS, dtype=i32)
    start_of = lambda e: jnp.sum(jnp.where(e[None, :] == eids[:, None], seg_start[:, None], 0), axis=0)
    pos1 = start_of(info[0]) + info[2]
    pos2 = start_of(info[1]) + info[3]
    pos3 = jnp.concatenate([pos1.reshape(T // MOE_TB, 1, MOE_TB), pos2.reshape(T // MOE_TB, 1, MOE_TB)], axis=2)
    tile_ids = jnp.minimum(jnp.arange(n_tiles, dtype=i32), n_valid[0] - 1)
    tile_expert = jnp.sum((tile_ids[:, None] >= tile_end[None, :]).astype(i32), axis=1)

    order = jnp.argsort(jnp.concatenate([pos1, pos2])).astype(i32)
    n_rows = n_tiles * MOE_TM
    padded = jnp.concatenate([jnp.zeros((n_rows,), i32), jnp.where(order >= T, order - T, order),
                              jnp.zeros((2 * n_rows - TOP_K * T,), i32)])
    shift = seg_start - (jnp.cumsum(cnt) - cnt)
    row_expert = jnp.repeat(tile_expert, MOE_TM)
    src_tok = jnp.zeros((n_rows,), i32)
    for e in range(N_EXPERTS):
        view = lax.dynamic_slice(padded, (n_rows - shift[e],), (n_rows,))
        src_tok = jnp.where(row_expert == e, view, src_tok)

    ys = _experts(tile_expert, n_valid.astype(i32), src_tok.reshape(n_tiles, 1, MOE_TM), h, w1b, w3b, w2b)
    return _combine(pos3, route, h, ln_g, ln_b, ys)


def _retention_constants():
    hh = np.arange(HEADS, dtype=np.float64)
    log_gamma = np.log1p(-np.exp2(-5.0 - hh))
    pos = np.arange(CHUNK, dtype=np.float64)
    diff = pos[:, None] - pos[None, :]
    causal = diff >= 0
    decay = np.where(causal[None], np.exp(log_gamma[:, None, None] * np.where(causal, diff, 0.0)[None]), 0.0)
    zeta = np.exp(log_gamma[:, None] * (CHUNK - 1 - pos)[None])
    xi = np.exp(log_gamma[:, None] * (pos + 1.0)[None])
    cdec = np.exp(log_gamma * CHUNK)
    bc = lambda t: np.broadcast_to(t[:, :, None], (HEADS, CHUNK, LANES))
    return (jnp.asarray(cdec, F32), jnp.asarray(decay, F32),
            jnp.asarray(bc(zeta), F32), jnp.asarray(bc(xi), F32))


def _rope_tables(seq):
    inv = ROPE_BASE ** (-jnp.arange(0, HEAD_DIM, 2, dtype=F32) / HEAD_DIM)
    ang = jnp.arange(seq, dtype=F32)[:, None] * inv[None, :]
    cos = jnp.repeat(jnp.cos(ang), 2, axis=-1)
    sin = jnp.sin(ang)
    sin_signed = jnp.stack([-sin, sin], axis=-1).reshape(seq, HEAD_DIM)
    return cos, sin_signed


def kernel(x, w_in, ret_norm_g, ret_norm_b, conv_w, conv_b, conv_norm_g, conv_norm_b, w_out, ln1_g, ln1_b, w_router_group, b_router_group, w_router_expert, b_router_expert, w1, w3, w2, ln2_g, ln2_b):
    batch, seq, _ = x.shape
    T = batch * seq
    x2 = x.reshape(T, D_MODEL)

    w_heads = jnp.transpose(w_in.reshape(D_MODEL, HEAD_COLS // LANES, HEADS, LANES), (2, 0, 1, 3))
    w_heads = w_heads.reshape(HEADS, D_MODEL, HEAD_COLS).astype(BF16)

    cos_t, sin_t = _rope_tables(seq)
    head_params = jnp.stack([ret_norm_g, ret_norm_b, conv_b, conv_norm_g, conv_norm_b]
                            + [jnp.zeros_like(conv_b)] * (SUBLANES - 5)).reshape(SUBLANES, HEADS, LANES)
    head_params = jnp.transpose(head_params, (1, 0, 2))
    cw = jnp.transpose(conv_w.reshape(CONV_TAPS, CONV_GROUPS, LANES), (1, 0, 2))
    ret3, conv3, w1b, w3b, w2b, w_out_b = _mixer(
        x2, w_heads, cos_t, sin_t, _retention_constants(), head_params, cw, w1, w3, w2, w_out, batch, seq)

    wr = jnp.concatenate([jnp.transpose(w_router_expert, (1, 0, 2)).reshape(D_MODEL, N_EXPERTS),
                          w_router_group], axis=1)
    wr = jnp.pad(wr, ((0, 0), (0, LANES - wr.shape[1])))
    br = jnp.pad(jnp.concatenate([b_router_expert.reshape(-1), b_router_group]),
                 (0, LANES - N_EXPERTS - N_GROUPS)).reshape(1, LANES)

    h, route, route_t, counts = _out_proj(ret3, conv3, x2, w_out_b, ln1_g.reshape(1, -1),
                                          ln1_b.reshape(1, -1), wr.astype(BF16), br)

    y = _routed_moe(h, route, route_t, counts, w1b.reshape(w1.shape), w3b.reshape(w3.shape),
                    w2b.reshape(w2.shape), ln2_g.reshape(1, -1), ln2_b.reshape(1, -1))
    return y.reshape(batch, seq, D_MODEL)
```

```python
import functools
import math

import numpy as np
import jax
import jax.numpy as jnp
from jax import lax
from jax.experimental import pallas as pl
from jax.experimental.pallas import tpu as pltpu

D_MODEL = 2048
RET_WIDTH = D_MODEL // 2
CONV_WIDTH = D_MODEL - RET_WIDTH
HEADS = 8
HEAD_DIM = RET_WIDTH // HEADS
CONV_GROUPS = 8
CONV_TAPS = 31
CHUNK = 128
ROPE_BASE = 10000.0
IN_COLS = 4 * RET_WIDTH + 2 * CONV_WIDTH
N_GROUPS = 4
EXPERTS_PER_GROUP = 4
N_EXPERTS = N_GROUPS * EXPERTS_PER_GROUP
TOP_K = 2
D_FF = D_MODEL // 2
LN_EPS = 1e-5
ALPHA = 2.0 ** 0.25

LANES = 128
SUBLANES = 8
VMEM_LIMIT = 56 * 1024 * 1024
CONV_HALO = 32

F32 = jnp.float32
BF16 = jnp.bfloat16


def _sigmoid(x):
    return 1.0 / (1.0 + jnp.exp(-x))


def _lane_norm(x, g, b):
    mu = jnp.mean(x, axis=-1, keepdims=True)
    xc = x - mu
    var = jnp.mean(xc * xc, axis=-1, keepdims=True)
    return xc * lax.rsqrt(var + LN_EPS) * g + b


HEADS_PER_STEP = 2
MIX_ROWS = 512
HEAD_COLS = 6 * LANES
OUT_PROJ_SUB = 256


def _rotary(t, cos, sin_signed, even_lane):
    partner = jnp.where(even_lane, pltpu.roll(t, LANES - 1, 1), pltpu.roll(t, 1, 1))
    return t * cos + partner * sin_signed


def _mixer_kernel(cdec_ref, x_ref, w_ref, cos_ref, sin_ref, decay_ref, zeta_ref, xi_ref, prm_ref, cw_ref,
                  w1f_ref, w3f_ref, w2f_ref, wof_ref,
                  ret_ref, conv_ref, w1b_ref, w3b_ref, w2b_ref, wob_ref, state_ref, uext_ref):
    w1b_ref[...] = w1f_ref[...].astype(BF16)
    w3b_ref[...] = w3f_ref[...].astype(BF16)
    w2b_ref[...] = w2f_ref[...].astype(BF16)
    wob_ref[...] = wof_ref[...].astype(BF16)

    rows = x_ref.shape[0]
    hps = w_ref.shape[0]
    n = pl.program_id(1)
    head0 = pl.program_id(2) * hps

    @pl.when(n == 0)
    def _():
        for j in range(hps):
            state_ref[head0 + j] = jnp.zeros((HEAD_DIM, HEAD_DIM), F32)
            uext_ref[head0 + j, 0:CONV_HALO, :] = jnp.zeros((CONV_HALO, LANES), F32)

    even_lane = (lax.broadcasted_iota(jnp.int32, (CHUNK, LANES), 1) % 2) == 0
    scale = HEAD_DIM ** -0.5
    blk = 64
    first = CONV_HALO - (CONV_TAPS - 1)
    xb = x_ref[...].astype(BF16)

    for j in range(hps):
        head = head0 + j
        proj = jnp.dot(xb, w_ref[j], preferred_element_type=F32)
        col = lambda c: proj[:, c * LANES:(c + 1) * LANES]
        cdec = cdec_ref[head]
        decay = decay_ref[j]
        zeta = zeta_ref[j]
        xi = xi_ref[j]
        rg, rb, cb, cg, cbeta = (prm_ref[j, r:r + 1, :] for r in range(5))

        state = state_ref[head]
        for c in range(rows // CHUNK):
            rs = slice(c * CHUNK, (c + 1) * CHUNK)
            cos = cos_ref[rs, :]
            sin = sin_ref[rs, :]
            q = _rotary(col(0)[rs], cos, sin, even_lane)
            k = _rotary(col(1)[rs], cos, sin, even_lane) * scale
            v = col(2)[rs].astype(BF16)
            scores = lax.dot_general(q.astype(BF16), k.astype(BF16), (((1,), (1,)), ((), ())),
                                     preferred_element_type=F32) * decay
            inner = jnp.dot(scores.astype(BF16), v, preferred_element_type=F32)
            cross = jnp.dot((q * xi).astype(BF16), state.astype(BF16), preferred_element_type=F32)
            kz_t = jnp.transpose(k * zeta).astype(BF16)
            state = cdec * state + jnp.dot(kz_t, v, preferred_element_type=F32)
            y = _lane_norm(inner + cross, rg, rb)
            g = col(3)[rs]
            ret_ref[j, rs, :] = (g * _sigmoid(g) * y).astype(BF16)
        state_ref[head] = state

        uext_ref[head, CONV_HALO:CONV_HALO + rows, :] = col(4) * _sigmoid(col(5))
        for r0 in range(0, rows, blk):
            acc = jnp.zeros((blk, LANES), F32)
            for t in range(CONV_TAPS):
                acc = acc + cw_ref[j, t:t + 1, :] * uext_ref[head, r0 + first + t:r0 + first + t + blk, :]
            y = _lane_norm(acc + cb, cg, cbeta)
            conv_ref[j, r0:r0 + blk, :] = (y * _sigmoid(y)).astype(BF16)
        uext_ref[head, 0:CONV_HALO, :] = uext_ref[head, rows:rows + CONV_HALO, :]


def _mixer(x2, w_heads, cos_t, sin_t, consts, head_params, cw, w1, w3, w2, w_out, batch, seq):
    T = batch * seq
    rows, hps = MIX_ROWS, HEADS_PER_STEP
    ns = seq // rows
    ng = HEADS // hps
    cdec, decay, zeta, xi = consts
    steps = batch * ns * ng
    w1s = w1.reshape(-1, w1.shape[-1])
    w3s = w3.reshape(-1, w3.shape[-1])
    w2s = w2.reshape(-1, w2.shape[-1])
    step_of = lambda b, n, g: ((b * ns + n) * ng + g, 0)
    wspec = lambda w: pl.BlockSpec((w.shape[0] // steps, w.shape[1]), step_of)
    for w in (w1s, w3s, w2s, w_out):
        assert w.shape[0] % (steps * 16) == 0

    def hspec(d1, d2=LANES):
        return pl.BlockSpec((hps, d1, d2), lambda b, n, g: (g, 0, 0))

    out_rows = pl.BlockSpec((hps, rows, LANES), lambda b, n, g: (g, b * ns + n, 0))
    return pl.pallas_call(
        _mixer_kernel,
        grid=(batch, ns, ng),
        in_specs=[pl.BlockSpec(memory_space=pltpu.SMEM),
                  pl.BlockSpec((rows, D_MODEL), lambda b, n, g: (b * ns + n, 0)),
                  hspec(D_MODEL, HEAD_COLS),
                  pl.BlockSpec((rows, LANES), lambda b, n, g: (n, 0)),
                  pl.BlockSpec((rows, LANES), lambda b, n, g: (n, 0)),
                  hspec(CHUNK), hspec(CHUNK), hspec(CHUNK),
                  hspec(SUBLANES), hspec(CONV_TAPS),
                  wspec(w1s), wspec(w3s), wspec(w2s), wspec(w_out)],
        out_specs=[out_rows, out_rows,
                   wspec(w1s), wspec(w3s), wspec(w2s), wspec(w_out)],
        out_shape=[jax.ShapeDtypeStruct((HEADS, T, LANES), BF16),
                   jax.ShapeDtypeStruct((CONV_GROUPS, T, LANES), BF16),
                   jax.ShapeDtypeStruct(w1s.shape, BF16),
                   jax.ShapeDtypeStruct(w3s.shape, BF16),
                   jax.ShapeDtypeStruct(w2s.shape, BF16),
                   jax.ShapeDtypeStruct(w_out.shape, BF16)],
        scratch_shapes=[pltpu.VMEM((HEADS, HEAD_DIM, HEAD_DIM), F32),
                        pltpu.VMEM((HEADS, rows + CONV_HALO, LANES), F32)],
        compiler_params=pltpu.CompilerParams(
            dimension_semantics=("arbitrary", "arbitrary", "arbitrary"), vmem_limit_bytes=VMEM_LIMIT),
        name="in_proj_mixer",
    )(cdec, x2, w_heads, cos_t, sin_t, decay, zeta, xi, head_params, cw, w1s, w3s, w2s, w_out)


def _layer_norm(z, g, b):
    mu = jnp.mean(z, axis=-1, keepdims=True)
    zc = z - mu
    var = jnp.mean(zc * zc, axis=-1, keepdims=True)
    return zc * lax.rsqrt(var + LN_EPS) * g + b


def _outproj_kernel(ret_ref, conv_ref, x_ref, w_ref, g_ref, b_ref, wr_ref, br_ref,
                    h_ref, route_ref, route_t_ref, counts_ref, carry_ref):
    @pl.when(pl.program_id(0) == 0)
    def _():
        carry_ref[...] = jnp.zeros_like(carry_ref)

    carry = carry_ref[:, 0:1]
    for r0 in range(0, x_ref.shape[0], OUT_PROJ_SUB):
        carry = _project_and_route(slice(r0, r0 + OUT_PROJ_SUB), carry, ret_ref, conv_ref, x_ref, w_ref,
                                   g_ref, b_ref, wr_ref, br_ref, h_ref, route_ref, route_t_ref)
    carry_ref[...] = jnp.broadcast_to(carry, carry_ref.shape)
    counts_ref[...] = jnp.broadcast_to(carry, counts_ref.shape)


def _project_and_route(rs, carry, ret_ref, conv_ref, x_ref, w_ref, g_ref, b_ref, wr_ref, br_ref,
                       h_ref, route_ref, route_t_ref):
    mix = jnp.concatenate([ret_ref[s, rs, :] for s in range(HEADS)] +
                          [conv_ref[s, rs, :] for s in range(CONV_GROUPS)], axis=-1)
    y = jnp.dot(mix, w_ref[...], preferred_element_type=F32)
    h = _layer_norm(ALPHA * x_ref[rs, :] + y, g_ref[...], b_ref[...])
    h_ref[rs, :] = h

    logits = jnp.dot(h.astype(BF16), wr_ref[...], preferred_element_type=F32) + br_ref[...]

    tm = logits.shape[0]
    lt = jnp.transpose(logits)
    neg = jnp.float32(-jnp.inf)
    big = jnp.float32(1e9)
    grow = lax.broadcasted_iota(jnp.int32, (SUBLANES, tm), 0).astype(F32)
    gmask = grow < N_GROUPS
    gl = jnp.where(gmask, lt[N_EXPERTS:N_EXPERTS + SUBLANES, :], neg)
    gmax = jnp.max(gl, axis=0, keepdims=True)
    gidx = jnp.min(jnp.where(gl == gmax, grow, big), axis=0, keepdims=True)
    g_w = 1.0 / jnp.sum(jnp.where(gmask, jnp.exp(gl - gmax), 0.0), axis=0, keepdims=True)
    erow = lax.broadcasted_iota(jnp.int32, (N_EXPERTS, tm), 0).astype(F32)
    lo = gidx * EXPERTS_PER_GROUP
    emask = (erow >= lo) & (erow < lo + EXPERTS_PER_GROUP)
    el = jnp.where(emask, lt[0:N_EXPERTS, :], neg)
    top1 = jnp.max(el, axis=0, keepdims=True)
    idx1 = jnp.min(jnp.where(el == top1, erow, big), axis=0, keepdims=True)
    el2 = jnp.where(erow == idx1, neg, el)
    top2 = jnp.max(el2, axis=0, keepdims=True)
    idx2 = jnp.min(jnp.where(el2 == top2, erow, big), axis=0, keepdims=True)
    e2 = jnp.exp(top2 - top1)
    w1 = g_w / (1.0 + e2)
    w2 = w1 * e2

    sel1 = erow == idx1
    sel2 = erow == idx2
    onehot = jnp.where(sel1, 1.0, 0.0) + jnp.where(sel2, 1.0, 0.0)
    r_i = lax.broadcasted_iota(jnp.int32, (tm, tm), 0)
    c_i = lax.broadcasted_iota(jnp.int32, (tm, tm), 1)
    earlier = jnp.where(r_i < c_i, 1.0, 0.0).astype(BF16)
    prefix = jnp.dot(onehot.astype(BF16), earlier, preferred_element_type=F32) + carry
    rank1 = jnp.sum(jnp.where(sel1, prefix, 0.0), axis=0, keepdims=True)
    rank2 = jnp.sum(jnp.where(sel2, prefix, 0.0), axis=0, keepdims=True)

    route_t = jnp.concatenate([w1, w2, idx1, idx2, rank1, rank2, jnp.zeros((2, tm), F32)], axis=0)
    route_t_ref[:, rs] = route_t
    route_ref[rs, :] = jnp.transpose(
        jnp.concatenate([route_t, jnp.zeros((LANES - SUBLANES, tm), F32)], axis=0))
    return carry + jnp.sum(onehot, axis=1, keepdims=True)


def _out_proj(ret3, conv3, x2, w_out_b, ln_g, ln_b, wr_b, br, tm=512):
    T = x2.shape[0]
    cur = lambda i: (0, i, 0)
    row = lambda i: (i, 0)
    const = lambda i: (0, 0)
    return pl.pallas_call(
        _outproj_kernel,
        grid=(T // tm,),
        in_specs=[pl.BlockSpec((HEADS, tm, LANES), cur),
                  pl.BlockSpec((CONV_GROUPS, tm, LANES), cur),
                  pl.BlockSpec((tm, D_MODEL), row),
                  pl.BlockSpec((D_MODEL, D_MODEL), const),
                  pl.BlockSpec((1, D_MODEL), const),
                  pl.BlockSpec((1, D_MODEL), const),
                  pl.BlockSpec((D_MODEL, LANES), const),
                  pl.BlockSpec((1, LANES), const)],
        out_specs=[pl.BlockSpec((tm, D_MODEL), row),
                   pl.BlockSpec((tm, LANES), row),
                   pl.BlockSpec((SUBLANES, tm), lambda i: (0, i)),
                   pl.BlockSpec((N_EXPERTS, LANES), const)],
        out_shape=[jax.ShapeDtypeStruct((T, D_MODEL), F32),
                   jax.ShapeDtypeStruct((T, LANES), F32),
                   jax.ShapeDtypeStruct((SUBLANES, T), F32),
                   jax.ShapeDtypeStruct((N_EXPERTS, LANES), F32)],
        scratch_shapes=[pltpu.VMEM((N_EXPERTS, LANES), F32)],
        compiler_params=pltpu.CompilerParams(
            dimension_semantics=("arbitrary",), vmem_limit_bytes=VMEM_LIMIT),
        name="out_proj_ln_route",
    )(ret3, conv3, x2, w_out_b, ln_g, ln_b, wr_b, br)


MOE_TM = 512
MOE_TB = 512


def _row_copy(src_ref, src_row, dst_ref, dst_row, sem):
    return pltpu.make_async_copy(src_ref.at[pl.ds(src_row, 1), :], dst_ref.at[pl.ds(dst_row, 1), :], sem)


def _experts_kernel(te_ref, nv_ref, src_ref, src_next_ref, h_ref, w1_ref, w3_ref, w2_ref, y_ref, x_ref, sems):
    i = pl.program_id(0)
    last = pl.num_programs(0) - 1
    slot = i % 2
    used = i < nv_ref[0]

    def gather_wait(s):
        pltpu.make_async_copy(h_ref.at[pl.ds(0, MOE_TM), :], x_ref.at[s, pl.ds(0, MOE_TM), :], sems.at[s]).wait()

    @pl.when(i == 0)
    def _():
        x_ref[:, MOE_TM:, :] = jnp.zeros((2, SUBLANES, D_MODEL), F32)

        def first(t, carry):
            _row_copy(h_ref, src_ref[0, 0, t], x_ref.at[0], t, sems.at[0]).start()
            return carry
        lax.fori_loop(0, MOE_TM, first, 0, unroll=8)

    @pl.when(used)
    def _():
        gather_wait(slot)
        xb = x_ref[slot, 0:MOE_TM, :].astype(BF16)
        a1 = jnp.dot(xb, w1_ref[0], preferred_element_type=F32)

        def issue_half(k, width):
            for t in range(k * MOE_TM // 2, (k + 1) * MOE_TM // 2):
                _row_copy(h_ref, src_next_ref[0, 0, t], x_ref.at[1 - slot], t, sems.at[1 - slot]).start()
            spare = pltpu.bitcast(x_ref[1 - slot, MOE_TM:, k * LANES:(k + 1) * LANES], jnp.uint32)
            zero = pltpu.bitcast((spare >> 16) >> 16, F32)
            return jnp.tile(zero, (MOE_TM // SUBLANES, width // LANES))

        a3 = jnp.dot(xb, w3_ref[0], preferred_element_type=F32) + issue_half(0, D_FF)
        act = (a1 * _sigmoid(a1) * a3).astype(BF16)
        y_ref[...] = jnp.dot(act, w2_ref[0], preferred_element_type=F32) + issue_half(1, D_MODEL)

        @pl.when(i == last)
        def _():
            gather_wait(1 - slot)

    @pl.when(jnp.logical_not(used))
    def _():
        @pl.when(i == nv_ref[0])
        def _():
            gather_wait(slot)
        y_ref[...] = jnp.zeros_like(y_ref)


def _experts(tile_expert, n_valid, src_tok, h, w1b, w3b, w2b):
    n_tiles = src_tok.shape[0]
    row = lambda i, te, nv: (i, 0)
    wsel = lambda i, te, nv: (te[i], 0, 0)
    smem_row = lambda f: pl.BlockSpec((1, 1, MOE_TM), f, memory_space=pltpu.SMEM)
    return pl.pallas_call(
        _experts_kernel,
        grid_spec=pltpu.PrefetchScalarGridSpec(
            num_scalar_prefetch=2,
            grid=(n_tiles,),
            in_specs=[smem_row(lambda i, te, nv: (i, 0, 0)),
                      smem_row(lambda i, te, nv: (jnp.minimum(i + 1, n_tiles - 1), 0, 0)),
                      pl.BlockSpec(memory_space=pl.ANY),
                      pl.BlockSpec((1, D_MODEL, D_FF), wsel),
                      pl.BlockSpec((1, D_MODEL, D_FF), wsel),
                      pl.BlockSpec((1, D_FF, D_MODEL), wsel)],
            out_specs=pl.BlockSpec((MOE_TM, D_MODEL), row),
            scratch_shapes=[pltpu.VMEM((2, MOE_TM + SUBLANES, D_MODEL), F32),
                            pltpu.SemaphoreType.DMA((2,))]),
        out_shape=jax.ShapeDtypeStruct((n_tiles * MOE_TM, D_MODEL), F32),
        compiler_params=pltpu.CompilerParams(
            dimension_semantics=("arbitrary",), vmem_limit_bytes=VMEM_LIMIT),
        name="moe_experts",
    )(tile_expert, n_valid, src_tok, src_tok, h, w1b, w3b, w2b)


def _combine_kernel(pos_ref, pos_next_ref, route_ref, h_ref, g_ref, b_ref, ys_ref, o_ref,
                    ya0_ref, yb0_ref, ya1_ref, yb1_ref, sems):
    tb = h_ref.shape[0]
    i = pl.program_id(0)
    bufs = ((ya0_ref, yb0_ref), (ya1_ref, yb1_ref))

    def gather_wait(s):
        for k in range(TOP_K):
            pltpu.make_async_copy(ys_ref.at[pl.ds(0, tb), :], bufs[s][k], sems.at[s]).wait()

    @pl.when(i == 0)
    def _():
        def first(t, carry):
            for k in range(TOP_K):
                _row_copy(ys_ref, pos_ref[0, 0, k * tb + t], bufs[0][k], t, sems.at[0]).start()
            return carry
        lax.fori_loop(0, tb, first, 0, unroll=8)

    for par in range(2):
        @pl.when(i % 2 == par)
        def _():
            gather_wait(par)
            for t in range(tb):
                for k in range(TOP_K):
                    _row_copy(ys_ref, pos_next_ref[0, 0, k * tb + t], bufs[1 - par][k], t, sems.at[1 - par]).start()
            route = route_ref[...]
            ffn = route[:, 0:1] * bufs[par][0][...] + route[:, 1:2] * bufs[par][1][...]
            o_ref[...] = _layer_norm(ALPHA * h_ref[...] + ffn, g_ref[...], b_ref[...])

            @pl.when(i == pl.num_programs(0) - 1)
            def _():
                gather_wait(1 - par)


def _combine(pos3, route, h, ln_g, ln_b, ys):
    T = h.shape[0]
    tb = MOE_TB
    row = lambda i: (i, 0)
    const = lambda i: (0, 0)
    return pl.pallas_call(
        _combine_kernel,
        grid=(T // tb,),
        in_specs=[pl.BlockSpec((1, 1, 2 * tb), lambda i: (i, 0, 0), memory_space=pltpu.SMEM),
                  pl.BlockSpec((1, 1, 2 * tb), lambda i: (jnp.minimum(i + 1, T // tb - 1), 0, 0),
                               memory_space=pltpu.SMEM),
                  pl.BlockSpec((tb, LANES), row),
                  pl.BlockSpec((tb, D_MODEL), row),
                  pl.BlockSpec((1, D_MODEL), const),
                  pl.BlockSpec((1, D_MODEL), const),
                  pl.BlockSpec(memory_space=pl.ANY)],
        out_specs=pl.BlockSpec((tb, D_MODEL), row),
        out_shape=jax.ShapeDtypeStruct((T, D_MODEL), F32),
        scratch_shapes=[pltpu.VMEM((tb, D_MODEL), F32)] * (2 * TOP_K) + [pltpu.SemaphoreType.DMA((2,))],
        compiler_params=pltpu.CompilerParams(
            dimension_semantics=("arbitrary",), vmem_limit_bytes=VMEM_LIMIT),
        name="moe_combine_ln",
    )(pos3, pos3, route, h, ln_g, ln_b, ys)


def _routed_moe(h, route, route_t, counts, w1b, w3b, w2b, ln_g, ln_b):
    T = h.shape[0]
    n_tiles = (TOP_K * T) // MOE_TM + N_EXPERTS
    i32 = jnp.int32
    cnt = counts[:, 0].astype(i32)
    ntile = (cnt + MOE_TM - 1) // MOE_TM
    tile_end = jnp.cumsum(ntile)
    n_valid = tile_end[-1:]
    seg_start = (tile_end - ntile) * MOE_TM
    info = route_t[2:6].astype(i32)
    eids = jnp.arange(N_EXPERTS, dtype=i32)
    start_of = lambda e: jnp.sum(jnp.where(e[None, :] == eids[:, None], seg_start[:, None], 0), axis=0)
    pos1 = start_of(info[0]) + info[2]
    pos2 = start_of(info[1]) + info[3]
    pos3 = jnp.concatenate([pos1.reshape(T // MOE_TB, 1, MOE_TB), pos2.reshape(T // MOE_TB, 1, MOE_TB)], axis=2)
    tile_ids = jnp.minimum(jnp.arange(n_tiles, dtype=i32), n_valid[0] - 1)
    tile_expert = jnp.sum((tile_ids[:, None] >= tile_end[None, :]).astype(i32), axis=1)

    order = jnp.argsort(jnp.concatenate([pos1, pos2])).astype(i32)
    n_rows = n_tiles * MOE_TM
    padded = jnp.concatenate([jnp.zeros((n_rows,), i32), jnp.where(order >= T, order - T, order),
                              jnp.zeros((2 * n_rows - TOP_K * T,), i32)])
    shift = seg_start - (jnp.cumsum(cnt) - cnt)
    row_expert = jnp.repeat(tile_expert, MOE_TM)
    src_tok = jnp.zeros((n_rows,), i32)
    for e in range(N_EXPERTS):
        view = lax.dynamic_slice(padded, (n_rows - shift[e],), (n_rows,))
        src_tok = jnp.where(row_expert == e, view, src_tok)

    ys = _experts(tile_expert, n_valid.astype(i32), src_tok.reshape(n_tiles, 1, MOE_TM), h, w1b, w3b, w2b)
    return _combine(pos3, route, h, ln_g, ln_b, ys)


def _retention_constants():
    hh = np.arange(HEADS, dtype=np.float64)
    log_gamma = np.log1p(-np.exp2(-5.0 - hh))
    pos = np.arange(CHUNK, dtype=np.float64)
    diff = pos[:, None] - pos[None, :]
    causal = diff >= 0
    decay = np.where(causal[None], np.exp(log_gamma[:, None, None] * np.where(causal, diff, 0.0)[None]), 0.0)
    zeta = np.exp(log_gamma[:, None] * (CHUNK - 1 - pos)[None])
    xi = np.exp(log_gamma[:, None] * (pos + 1.0)[None])
    cdec = np.exp(log_gamma * CHUNK)
    bc = lambda t: np.broadcast_to(t[:, :, None], (HEADS, CHUNK, LANES))
    return (jnp.asarray(cdec, F32), jnp.asarray(decay, F32),
            jnp.asarray(bc(zeta), F32), jnp.asarray(bc(xi), F32))


def _rope_tables(seq):
    inv = ROPE_BASE ** (-jnp.arange(0, HEAD_DIM, 2, dtype=F32) / HEAD_DIM)
    ang = jnp.arange(seq, dtype=F32)[:, None] * inv[None, :]
    cos = jnp.repeat(jnp.cos(ang), 2, axis=-1)
    sin = jnp.sin(ang)
    sin_signed = jnp.stack([-sin, sin], axis=-1).reshape(seq, HEAD_DIM)
    return cos, sin_signed


def kernel(x, w_in, ret_norm_g, ret_norm_b, conv_w, conv_b, conv_norm_g, conv_norm_b, w_out, ln1_g, ln1_b, w_router_group, b_router_group, w_router_expert, b_router_expert, w1, w3, w2, ln2_g, ln2_b):
    batch, seq, _ = x.shape
    T = batch * seq
    x2 = x.reshape(T, D_MODEL)

    w_heads = jnp.transpose(w_in.astype(BF16).reshape(D_MODEL, HEAD_COLS // LANES, HEADS, LANES), (2, 0, 1, 3))
    w_heads = w_heads.reshape(HEADS, D_MODEL, HEAD_COLS)

    cos_t, sin_t = _rope_tables(seq)
    head_params = jnp.stack([ret_norm_g, ret_norm_b, conv_b, conv_norm_g, conv_norm_b]
                            + [jnp.zeros_like(conv_b)] * (SUBLANES - 5)).reshape(SUBLANES, HEADS, LANES)
    head_params = jnp.transpose(head_params, (1, 0, 2))
    cw = jnp.transpose(conv_w.reshape(CONV_TAPS, CONV_GROUPS, LANES), (1, 0, 2))
    ret3, conv3, w1b, w3b, w2b, w_out_b = _mixer(
        x2, w_heads, cos_t, sin_t, _retention_constants(), head_params, cw, w1, w3, w2, w_out, batch, seq)

    wr = jnp.concatenate([jnp.transpose(w_router_expert, (1, 0, 2)).reshape(D_MODEL, N_EXPERTS),
                          w_router_group], axis=1)
    wr = jnp.pad(wr, ((0, 0), (0, LANES - wr.shape[1])))
    br = jnp.pad(jnp.concatenate([b_router_expert.reshape(-1), b_router_group]),
                 (0, LANES - N_EXPERTS - N_GROUPS)).reshape(1, LANES)

    h, route, route_t, counts = _out_proj(ret3, conv3, x2, w_out_b, ln1_g.reshape(1, -1),
                                          ln1_b.reshape(1, -1), wr.astype(BF16), br)

    y = _routed_moe(h, route, route_t, counts, w1b.reshape(w1.shape), w3b.reshape(w3.shape),
                    w2b.reshape(w2.shape), ln2_g.reshape(1, -1), ln2_b.reshape(1, -1))
    return y.reshape(batch, seq, D_MODEL)
```

```python
import functools
import math

import numpy as np
import jax
import jax.numpy as jnp
from jax import lax
from jax.experimental import pallas as pl
from jax.experimental.pallas import tpu as pltpu

D_MODEL = 2048
RET_WIDTH = D_MODEL // 2
CONV_WIDTH = D_MODEL - RET_WIDTH
HEADS = 8
HEAD_DIM = RET_WIDTH // HEADS
CONV_GROUPS = 8
CONV_TAPS = 31
CHUNK = 128
ROPE_BASE = 10000.0
IN_COLS = 4 * RET_WIDTH + 2 * CONV_WIDTH
N_GROUPS = 4
EXPERTS_PER_GROUP = 4
N_EXPERTS = N_GROUPS * EXPERTS_PER_GROUP
TOP_K = 2
D_FF = D_MODEL // 2
LN_EPS = 1e-5
ALPHA = 2.0 ** 0.25

LANES = 128
SUBLANES = 8
VMEM_LIMIT = 56 * 1024 * 1024
CONV_HALO = 32

F32 = jnp.float32
BF16 = jnp.bfloat16


def _sigmoid(x):
    return 1.0 / (1.0 + jnp.exp(-x))


def _lane_norm(x, g, b):
    mu = jnp.mean(x, axis=-1, keepdims=True)
    xc = x - mu
    var = jnp.mean(xc * xc, axis=-1, keepdims=True)
    return xc * lax.rsqrt(var + LN_EPS) * g + b


HEADS_PER_STEP = 2
MIX_ROWS = 512
HEAD_COLS = 6 * LANES
OUT_PROJ_SUB = 256


def _rotary(t, cos, sin_signed, even_lane):
    partner = jnp.where(even_lane, pltpu.roll(t, LANES - 1, 1), pltpu.roll(t, 1, 1))
    return t * cos + partner * sin_signed


def _mixer_kernel(cdec_ref, x_ref, w_ref, cos_ref, sin_ref, decay_ref, zeta_ref, xi_ref, prm_ref, cw_ref,
                  w1f_ref, w3f_ref, w2f_ref, wof_ref,
                  ret_ref, conv_ref, w1b_ref, w3b_ref, w2b_ref, wob_ref, state_ref, uext_ref):
    w1b_ref[...] = w1f_ref[...].astype(BF16)
    w3b_ref[...] = w3f_ref[...].astype(BF16)
    w2b_ref[...] = w2f_ref[...].astype(BF16)
    wob_ref[...] = wof_ref[...].astype(BF16)

    rows = x_ref.shape[0]
    hps = w_ref.shape[1] // HEAD_COLS
    n = pl.program_id(2)
    head0 = pl.program_id(1) * hps

    @pl.when(n == 0)
    def _():
        for j in range(hps):
            state_ref[head0 + j] = jnp.zeros((HEAD_DIM, HEAD_DIM), F32)
            uext_ref[head0 + j, 0:CONV_HALO, :] = jnp.zeros((CONV_HALO, LANES), F32)

    even_lane = (lax.broadcasted_iota(jnp.int32, (CHUNK, LANES), 1) % 2) == 0
    scale = HEAD_DIM ** -0.5
    blk = 64
    first = CONV_HALO - (CONV_TAPS - 1)
    xb = x_ref[...].astype(BF16)

    for j in range(hps):
        head = head0 + j
        proj = jnp.dot(xb, w_ref[:, j * HEAD_COLS:(j + 1) * HEAD_COLS], preferred_element_type=F32)
        col = lambda c: proj[:, c * LANES:(c + 1) * LANES]
        cdec = cdec_ref[head]
        decay = decay_ref[j]
        zeta = zeta_ref[j]
        xi = xi_ref[j]
        rg, rb, cb, cg, cbeta = (prm_ref[j, r:r + 1, :] for r in range(5))

        state = state_ref[head]
        for c in range(rows // CHUNK):
            rs = slice(c * CHUNK, (c + 1) * CHUNK)
            cos = cos_ref[rs, :]
            sin = sin_ref[rs, :]
            q = _rotary(col(0)[rs], cos, sin, even_lane)
            k = _rotary(col(1)[rs], cos, sin, even_lane) * scale
            v = col(2)[rs].astype(BF16)
            scores = lax.dot_general(q.astype(BF16), k.astype(BF16), (((1,), (1,)), ((), ())),
                                     preferred_element_type=F32) * decay
            inner = jnp.dot(scores.astype(BF16), v, preferred_element_type=F32)
            cross = jnp.dot((q * xi).astype(BF16), state.astype(BF16), preferred_element_type=F32)
            kz_t = jnp.transpose(k * zeta).astype(BF16)
            state = cdec * state + jnp.dot(kz_t, v, preferred_element_type=F32)
            y = _lane_norm(inner + cross, rg, rb)
            g = col(3)[rs]
            ret_ref[j, rs, :] = (g * _sigmoid(g) * y).astype(BF16)
        state_ref[head] = state

        uext_ref[head, CONV_HALO:CONV_HALO + rows, :] = col(4) * _sigmoid(col(5))
        for r0 in range(0, rows, blk):
            acc = jnp.zeros((blk, LANES), F32)
            for t in range(CONV_TAPS):
                acc = acc + cw_ref[j, t:t + 1, :] * uext_ref[head, r0 + first + t:r0 + first + t + blk, :]
            y = _lane_norm(acc + cb, cg, cbeta)
            conv_ref[j, r0:r0 + blk, :] = (y * _sigmoid(y)).astype(BF16)
        uext_ref[head, 0:CONV_HALO, :] = uext_ref[head, rows:rows + CONV_HALO, :]


def _mixer(x2, w_heads, cos_t, sin_t, consts, head_params, cw, w1, w3, w2, w_out, batch, seq):
    T = batch * seq
    rows, hps = MIX_ROWS, HEADS_PER_STEP
    ns = seq // rows
    ng = HEADS // hps
    cdec, decay, zeta, xi = consts
    steps = batch * ns * ng
    w1s = w1.reshape(-1, w1.shape[-1])
    w3s = w3.reshape(-1, w3.shape[-1])
    w2s = w2.reshape(-1, w2.shape[-1])
    step_of = lambda b, g, n: ((b * ng + g) * ns + n, 0)
    wspec = lambda w: pl.BlockSpec((w.shape[0] // steps, w.shape[1]), step_of)
    for w in (w1s, w3s, w2s, w_out):
        assert w.shape[0] % (steps * 16) == 0

    def hspec(d1, d2=LANES):
        return pl.BlockSpec((hps, d1, d2), lambda b, g, n: (g, 0, 0))

    out_rows = pl.BlockSpec((hps, rows, LANES), lambda b, g, n: (g, b * ns + n, 0))
    return pl.pallas_call(
        _mixer_kernel,
        grid=(batch, ng, ns),
        in_specs=[pl.BlockSpec(memory_space=pltpu.SMEM),
                  pl.BlockSpec((rows, D_MODEL), lambda b, g, n: (b * ns + n, 0)),
                  pl.BlockSpec((D_MODEL, hps * HEAD_COLS), lambda b, g, n: (0, g)),
                  pl.BlockSpec((rows, LANES), lambda b, g, n: (n, 0)),
                  pl.BlockSpec((rows, LANES), lambda b, g, n: (n, 0)),
                  hspec(CHUNK), hspec(CHUNK), hspec(CHUNK),
                  hspec(SUBLANES), hspec(CONV_TAPS),
                  wspec(w1s), wspec(w3s), wspec(w2s), wspec(w_out)],
        out_specs=[out_rows, out_rows,
                   wspec(w1s), wspec(w3s), wspec(w2s), wspec(w_out)],
        out_shape=[jax.ShapeDtypeStruct((HEADS, T, LANES), BF16),
                   jax.ShapeDtypeStruct((CONV_GROUPS, T, LANES), BF16),
                   jax.ShapeDtypeStruct(w1s.shape, BF16),
                   jax.ShapeDtypeStruct(w3s.shape, BF16),
                   jax.ShapeDtypeStruct(w2s.shape, BF16),
                   jax.ShapeDtypeStruct(w_out.shape, BF16)],
        scratch_shapes=[pltpu.VMEM((HEADS, HEAD_DIM, HEAD_DIM), F32),
                        pltpu.VMEM((HEADS, rows + CONV_HALO, LANES), F32)],
        compiler_params=pltpu.CompilerParams(
            dimension_semantics=("arbitrary", "arbitrary", "arbitrary"), vmem_limit_bytes=VMEM_LIMIT),
        name="in_proj_mixer",
    )(cdec, x2, w_heads, cos_t, sin_t, decay, zeta, xi, head_params, cw, w1s, w3s, w2s, w_out)


def _layer_norm(z, g, b):
    mu = jnp.mean(z, axis=-1, keepdims=True)
    zc = z - mu
    var = jnp.mean(zc * zc, axis=-1, keepdims=True)
    return zc * lax.rsqrt(var + LN_EPS) * g + b


def _outproj_kernel(ret_ref, conv_ref, x_ref, w_ref, g_ref, b_ref, wr_ref, br_ref,
                    h_ref, route_ref, route_t_ref, counts_ref, carry_ref):
    @pl.when(pl.program_id(0) == 0)
    def _():
        carry_ref[...] = jnp.zeros_like(carry_ref)

    carry = carry_ref[:, 0:1]
    for r0 in range(0, x_ref.shape[0], OUT_PROJ_SUB):
        carry = _project_and_route(slice(r0, r0 + OUT_PROJ_SUB), carry, ret_ref, conv_ref, x_ref, w_ref,
                                   g_ref, b_ref, wr_ref, br_ref, h_ref, route_ref, route_t_ref)
    carry_ref[...] = jnp.broadcast_to(carry, carry_ref.shape)
    counts_ref[...] = jnp.broadcast_to(carry, counts_ref.shape)


def _project_and_route(rs, carry, ret_ref, conv_ref, x_ref, w_ref, g_ref, b_ref, wr_ref, br_ref,
                       h_ref, route_ref, route_t_ref):
    mix = jnp.concatenate([ret_ref[s, rs, :] for s in range(HEADS)] +
                          [conv_ref[s, rs, :] for s in range(CONV_GROUPS)], axis=-1)
    y = jnp.dot(mix, w_ref[...], preferred_element_type=F32)
    h = _layer_norm(ALPHA * x_ref[rs, :] + y, g_ref[...], b_ref[...])
    h_ref[rs, :] = h

    logits = jnp.dot(h.astype(BF16), wr_ref[...], preferred_element_type=F32) + br_ref[...]

    tm = logits.shape[0]
    lt = jnp.transpose(logits)
    neg = jnp.float32(-jnp.inf)
    big = jnp.float32(1e9)
    grow = lax.broadcasted_iota(jnp.int32, (SUBLANES, tm), 0).astype(F32)
    gmask = grow < N_GROUPS
    gl = jnp.where(gmask, lt[N_EXPERTS:N_EXPERTS + SUBLANES, :], neg)
    gmax = jnp.max(gl, axis=0, keepdims=True)
    gidx = jnp.min(jnp.where(gl == gmax, grow, big), axis=0, keepdims=True)
    g_w = 1.0 / jnp.sum(jnp.where(gmask, jnp.exp(gl - gmax), 0.0), axis=0, keepdims=True)
    erow = lax.broadcasted_iota(jnp.int32, (N_EXPERTS, tm), 0).astype(F32)
    lo = gidx * EXPERTS_PER_GROUP
    emask = (erow >= lo) & (erow < lo + EXPERTS_PER_GROUP)
    el = jnp.where(emask, lt[0:N_EXPERTS, :], neg)
    top1 = jnp.max(el, axis=0, keepdims=True)
    idx1 = jnp.min(jnp.where(el == top1, erow, big), axis=0, keepdims=True)
    el2 = jnp.where(erow == idx1, neg, el)
    top2 = jnp.max(el2, axis=0, keepdims=True)
    idx2 = jnp.min(jnp.where(el2 == top2, erow, big), axis=0, keepdims=True)
    e2 = jnp.exp(top2 - top1)
    w1 = g_w / (1.0 + e2)
    w2 = w1 * e2

    sel1 = erow == idx1
    sel2 = erow == idx2
    onehot = jnp.where(sel1, 1.0, 0.0) + jnp.where(sel2, 1.0, 0.0)
    r_i = lax.broadcasted_iota(jnp.int32, (tm, tm), 0)
    c_i = lax.broadcasted_iota(jnp.int32, (tm, tm), 1)
    earlier = jnp.where(r_i < c_i, 1.0, 0.0).astype(BF16)
    prefix = jnp.dot(onehot.astype(BF16), earlier, preferred_element_type=F32) + carry
    rank1 = jnp.sum(jnp.where(sel1, prefix, 0.0), axis=0, keepdims=True)
    rank2 = jnp.sum(jnp.where(sel2, prefix, 0.0), axis=0, keepdims=True)

    route_t = jnp.concatenate([w1, w2, idx1, idx2, rank1, rank2, jnp.zeros((2, tm), F32)], axis=0)
    route_t_ref[:, rs] = route_t
    route_ref[rs, :] = jnp.transpose(
        jnp.concatenate([route_t, jnp.zeros((LANES - SUBLANES, tm), F32)], axis=0))
    return carry + jnp.sum(onehot, axis=1, keepdims=True)


def _out_proj(ret3, conv3, x2, w_out_b, ln_g, ln_b, wr_b, br, tm=512):
    T = x2.shape[0]
    cur = lambda i: (0, i, 0)
    row = lambda i: (i, 0)
    const = lambda i: (0, 0)
    return pl.pallas_call(
        _outproj_kernel,
        grid=(T // tm,),
        in_specs=[pl.BlockSpec((HEADS, tm, LANES), cur),
                  pl.BlockSpec((CONV_GROUPS, tm, LANES), cur),
                  pl.BlockSpec((tm, D_MODEL), row),
                  pl.BlockSpec((D_MODEL, D_MODEL), const),
                  pl.BlockSpec((1, D_MODEL), const),
                  pl.BlockSpec((1, D_MODEL), const),
                  pl.BlockSpec((D_MODEL, LANES), const),
                  pl.BlockSpec((1, LANES), const)],
        out_specs=[pl.BlockSpec((tm, D_MODEL), row),
                   pl.BlockSpec((tm, LANES), row),
                   pl.BlockSpec((SUBLANES, tm), lambda i: (0, i)),
                   pl.BlockSpec((N_EXPERTS, LANES), const)],
        out_shape=[jax.ShapeDtypeStruct((T, D_MODEL), F32),
                   jax.ShapeDtypeStruct((T, LANES), F32),
                   jax.ShapeDtypeStruct((SUBLANES, T), F32),
                   jax.ShapeDtypeStruct((N_EXPERTS, LANES), F32)],
        scratch_shapes=[pltpu.VMEM((N_EXPERTS, LANES), F32)],
        compiler_params=pltpu.CompilerParams(
            dimension_semantics=("arbitrary",), vmem_limit_bytes=VMEM_LIMIT),
        name="out_proj_ln_route",
    )(ret3, conv3, x2, w_out_b, ln_g, ln_b, wr_b, br)


MOE_TM = 512
MOE_TB = 512


def _row_copy(src_ref, src_row, dst_ref, dst_row, sem):
    return pltpu.make_async_copy(src_ref.at[pl.ds(src_row, 1), :], dst_ref.at[pl.ds(dst_row, 1), :], sem)


def _experts_kernel(te_ref, nv_ref, src_ref, src_next_ref, h_ref, w1_ref, w3_ref, w2_ref, y_ref, x_ref, sems):
    i = pl.program_id(0)
    last = pl.num_programs(0) - 1
    slot = i % 2
    used = i < nv_ref[0]

    def gather_wait(s):
        pltpu.make_async_copy(h_ref.at[pl.ds(0, MOE_TM), :], x_ref.at[s, pl.ds(0, MOE_TM), :], sems.at[s]).wait()

    @pl.when(i == 0)
    def _():
        x_ref[:, MOE_TM:, :] = jnp.zeros((2, SUBLANES, D_MODEL), F32)

        def first(t, carry):
            _row_copy(h_ref, src_ref[0, 0, t], x_ref.at[0], t, sems.at[0]).start()
            return carry
        lax.fori_loop(0, MOE_TM, first, 0, unroll=8)

    @pl.when(used)
    def _():
        gather_wait(slot)

        def issue_half(k, width):
            for t in range(k * MOE_TM // 2, (k + 1) * MOE_TM // 2):
                _row_copy(h_ref, src_next_ref[0, 0, t], x_ref.at[1 - slot], t, sems.at[1 - slot]).start()
            spare = pltpu.bitcast(x_ref[1 - slot, MOE_TM:, k * LANES:(k + 1) * LANES], jnp.uint32)
            zero = pltpu.bitcast((spare >> 16) >> 16, F32)
            return jnp.tile(zero, (MOE_TM // SUBLANES, width // LANES))

        xb = x_ref[slot, 0:MOE_TM, :].astype(BF16)
        a1 = jnp.dot(xb, w1_ref[0], preferred_element_type=F32)
        a3 = jnp.dot(xb, w3_ref[0], preferred_element_type=F32) + issue_half(0, D_FF)
        act = (a1 * _sigmoid(a1) * a3).astype(BF16)
        y_ref[...] = jnp.dot(act, w2_ref[0], preferred_element_type=F32) + issue_half(1, D_MODEL)

        @pl.when(i == last)
        def _():
            gather_wait(1 - slot)

    @pl.when(jnp.logical_not(used))
    def _():
        @pl.when(i == nv_ref[0])
        def _():
            gather_wait(slot)
        y_ref[...] = jnp.zeros_like(y_ref)


def _experts(tile_expert, n_valid, src_tok, h, w1b, w3b, w2b):
    n_tiles = src_tok.shape[0]
    row = lambda i, te, nv: (i, 0)
    wsel = lambda i, te, nv: (te[i], 0, 0)
    smem_row = lambda f: pl.BlockSpec((1, 1, MOE_TM), f, memory_space=pltpu.SMEM)
    return pl.pallas_call(
        _experts_kernel,
        grid_spec=pltpu.PrefetchScalarGridSpec(
            num_scalar_prefetch=2,
            grid=(n_tiles,),
            in_specs=[smem_row(lambda i, te, nv: (i, 0, 0)),
                      smem_row(lambda i, te, nv: (jnp.minimum(i + 1, n_tiles - 1), 0, 0)),
                      pl.BlockSpec(memory_space=pl.ANY),
                      pl.BlockSpec((1, D_MODEL, D_FF), wsel),
                      pl.BlockSpec((1, D_MODEL, D_FF), wsel),
                      pl.BlockSpec((1, D_FF, D_MODEL), wsel)],
            out_specs=pl.BlockSpec((MOE_TM, D_MODEL), row),
            scratch_shapes=[pltpu.VMEM((2, MOE_TM + SUBLANES, D_MODEL), F32),
                            pltpu.SemaphoreType.DMA((2,))]),
        out_shape=jax.ShapeDtypeStruct((n_tiles * MOE_TM, D_MODEL), F32),
        compiler_params=pltpu.CompilerParams(
            dimension_semantics=("arbitrary",), vmem_limit_bytes=VMEM_LIMIT),
        name="moe_experts",
    )(tile_expert, n_valid, src_tok, src_tok, h, w1b, w3b, w2b)


def _combine_kernel(pos_ref, pos_next_ref, route_ref, h_ref, g_ref, b_ref, ys_ref, o_ref,
                    ya0_ref, yb0_ref, ya1_ref, yb1_ref, sems):
    tb = h_ref.shape[0]
    i = pl.program_id(0)
    bufs = ((ya0_ref, yb0_ref), (ya1_ref, yb1_ref))

    def gather_wait(s):
        for k in range(TOP_K):
            pltpu.make_async_copy(ys_ref.at[pl.ds(0, tb), :], bufs[s][k], sems.at[s]).wait()

    @pl.when(i == 0)
    def _():
        def first(t, carry):
            for k in range(TOP_K):
                _row_copy(ys_ref, pos_ref[0, 0, k * tb + t], bufs[0][k], t, sems.at[0]).start()
            return carry
        lax.fori_loop(0, tb, first, 0, unroll=8)

    for par in range(2):
        @pl.when(i % 2 == par)
        def _():
            gather_wait(par)
            for t in range(tb):
                for k in range(TOP_K):
                    _row_copy(ys_ref, pos_next_ref[0, 0, k * tb + t], bufs[1 - par][k], t, sems.at[1 - par]).start()
            route = route_ref[...]
            ffn = route[:, 0:1] * bufs[par][0][...] + route[:, 1:2] * bufs[par][1][...]
            o_ref[...] = _layer_norm(ALPHA * h_ref[...] + ffn, g_ref[...], b_ref[...])

            @pl.when(i == pl.num_programs(0) - 1)
            def _():
                gather_wait(1 - par)


def _combine(pos3, route, h, ln_g, ln_b, ys):
    T = h.shape[0]
    tb = MOE_TB
    row = lambda i: (i, 0)
    const = lambda i: (0, 0)
    return pl.pallas_call(
        _combine_kernel,
        grid=(T // tb,),
        in_specs=[pl.BlockSpec((1, 1, 2 * tb), lambda i: (i, 0, 0), memory_space=pltpu.SMEM),
                  pl.BlockSpec((1, 1, 2 * tb), lambda i: (jnp.minimum(i + 1, T // tb - 1), 0, 0),
                               memory_space=pltpu.SMEM),
                  pl.BlockSpec((tb, LANES), row),
                  pl.BlockSpec((tb, D_MODEL), row),
                  pl.BlockSpec((1, D_MODEL), const),
                  pl.BlockSpec((1, D_MODEL), const),
                  pl.BlockSpec(memory_space=pl.ANY)],
        out_specs=pl.BlockSpec((tb, D_MODEL), row),
        out_shape=jax.ShapeDtypeStruct((T, D_MODEL), F32),
        scratch_shapes=[pltpu.VMEM((tb, D_MODEL), F32)] * (2 * TOP_K) + [pltpu.SemaphoreType.DMA((2,))],
        compiler_params=pltpu.CompilerParams(
            dimension_semantics=("arbitrary",), vmem_limit_bytes=VMEM_LIMIT),
        name="moe_combine_ln",
    )(pos3, pos3, route, h, ln_g, ln_b, ys)


def _routed_moe(h, route, route_t, counts, w1b, w3b, w2b, ln_g, ln_b):
    T = h.shape[0]
    n_tiles = (TOP_K * T) // MOE_TM + N_EXPERTS
    i32 = jnp.int32
    cnt = counts[:, 0].astype(i32)
    ntile = (cnt + MOE_TM - 1) // MOE_TM
    tile_end = jnp.cumsum(ntile)
    n_valid = tile_end[-1:]
    seg_start = (tile_end - ntile) * MOE_TM
    info = route_t[2:6].astype(i32)
    eids = jnp.arange(N_EXPERTS, dtype=i32)
    start_of = lambda e: jnp.sum(jnp.where(e[None, :] == eids[:, None], seg_start[:, None], 0), axis=0)
    pos1 = start_of(info[0]) + info[2]
    pos2 = start_of(info[1]) + info[3]
    pos3 = jnp.concatenate([pos1.reshape(T // MOE_TB, 1, MOE_TB), pos2.reshape(T // MOE_TB, 1, MOE_TB)], axis=2)
    tile_ids = jnp.minimum(jnp.arange(n_tiles, dtype=i32), n_valid[0] - 1)
    tile_expert = jnp.sum((tile_ids[:, None] >= tile_end[None, :]).astype(i32), axis=1)

    order = jnp.argsort(jnp.concatenate([pos1, pos2])).astype(i32)
    n_rows = n_tiles * MOE_TM
    padded = jnp.concatenate([jnp.zeros((n_rows,), i32), jnp.where(order >= T, order - T, order),
                              jnp.zeros((2 * n_rows - TOP_K * T,), i32)])
    shift = seg_start - (jnp.cumsum(cnt) - cnt)
    row_expert = jnp.repeat(tile_expert, MOE_TM)
    src_tok = jnp.zeros((n_rows,), i32)
    for e in range(N_EXPERTS):
        view = lax.dynamic_slice(padded, (n_rows - shift[e],), (n_rows,))
        src_tok = jnp.where(row_expert == e, view, src_tok)

    ys = _experts(tile_expert, n_valid.astype(i32), src_tok.reshape(n_tiles, 1, MOE_TM), h, w1b, w3b, w2b)
    return _combine(pos3, route, h, ln_g, ln_b, ys)


def _retention_constants():
    hh = np.arange(HEADS, dtype=np.float64)
    log_gamma = np.log1p(-np.exp2(-5.0 - hh))
    pos = np.arange(CHUNK, dtype=np.float64)
    diff = pos[:, None] - pos[None, :]
    causal = diff >= 0
    decay = np.where(causal[None], np.exp(log_gamma[:, None, None] * np.where(causal, diff, 0.0)[None]), 0.0)
    zeta = np.exp(log_gamma[:, None] * (CHUNK - 1 - pos)[None])
    xi = np.exp(log_gamma[:, None] * (pos + 1.0)[None])
    cdec = np.exp(log_gamma * CHUNK)
    bc = lambda t: np.broadcast_to(t[:, :, None], (HEADS, CHUNK, LANES))
    return (jnp.asarray(cdec, F32), jnp.asarray(decay, F32),
            jnp.asarray(bc(zeta), F32), jnp.asarray(bc(xi), F32))


def _rope_tables(seq):
    inv = ROPE_BASE ** (-jnp.arange(0, HEAD_DIM, 2, dtype=F32) / HEAD_DIM)
    ang = jnp.arange(seq, dtype=F32)[:, None] * inv[None, :]
    cos = jnp.repeat(jnp.cos(ang), 2, axis=-1)
    sin = jnp.sin(ang)
    sin_signed = jnp.stack([-sin, sin], axis=-1).reshape(seq, HEAD_DIM)
    return cos, sin_signed


def kernel(x, w_in, ret_norm_g, ret_norm_b, conv_w, conv_b, conv_norm_g, conv_norm_b, w_out, ln1_g, ln1_b, w_router_group, b_router_group, w_router_expert, b_router_expert, w1, w3, w2, ln2_g, ln2_b):
    batch, seq, _ = x.shape
    T = batch * seq
    x2 = x.reshape(T, D_MODEL)

    kinds = HEAD_COLS // LANES
    w_heads = jnp.concatenate(
        [w_in[:, (c * HEADS + hd) * LANES:(c * HEADS + hd + 1) * LANES] for hd in range(HEADS) for c in range(kinds)],
        axis=1).astype(BF16)

    cos_t, sin_t = _rope_tables(seq)
    head_params = jnp.stack([ret_norm_g, ret_norm_b, conv_b, conv_norm_g, conv_norm_b]
                            + [jnp.zeros_like(conv_b)] * (SUBLANES - 5)).reshape(SUBLANES, HEADS, LANES)
    head_params = jnp.transpose(head_params, (1, 0, 2))
    cw = jnp.transpose(conv_w.reshape(CONV_TAPS, CONV_GROUPS, LANES), (1, 0, 2))
    ret3, conv3, w1b, w3b, w2b, w_out_b = _mixer(
        x2, w_heads, cos_t, sin_t, _retention_constants(), head_params, cw, w1, w3, w2, w_out, batch, seq)

    wr = jnp.concatenate([jnp.transpose(w_router_expert, (1, 0, 2)).reshape(D_MODEL, N_EXPERTS),
                          w_router_group], axis=1)
    wr = jnp.pad(wr, ((0, 0), (0, LANES - wr.shape[1])))
    br = jnp.pad(jnp.concatenate([b_router_expert.reshape(-1), b_router_group]),
                 (0, LANES - N_EXPERTS - N_GROUPS)).reshape(1, LANES)

    h, route, route_t, counts = _out_proj(ret3, conv3, x2, w_out_b, ln1_g.reshape(1, -1),
                                          ln1_b.reshape(1, -1), wr.astype(BF16), br)

    y = _routed_moe(h, route, route_t, counts, w1b.reshape(w1.shape), w3b.reshape(w3.shape),
                    w2b.reshape(w2.shape), ln2_g.reshape(1, -1), ln2_b.reshape(1, -1))
    return y.reshape(batch, seq, D_MODEL)
```

```python
import functools
import math

import numpy as np
import jax
import jax.numpy as jnp
from jax import lax
from jax.experimental import pallas as pl
from jax.experimental.pallas import tpu as pltpu

D_MODEL = 2048
RET_WIDTH = D_MODEL // 2
CONV_WIDTH = D_MODEL - RET_WIDTH
HEADS = 8
HEAD_DIM = RET_WIDTH // HEADS
CONV_GROUPS = 8
CONV_TAPS = 31
CHUNK = 128
ROPE_BASE = 10000.0
IN_COLS = 4 * RET_WIDTH + 2 * CONV_WIDTH
N_GROUPS = 4
EXPERTS_PER_GROUP = 4
N_EXPERTS = N_GROUPS * EXPERTS_PER_GROUP
TOP_K = 2
D_FF = D_MODEL // 2
LN_EPS = 1e-5
ALPHA = 2.0 ** 0.25

LANES = 128
SUBLANES = 8
VMEM_LIMIT = 56 * 1024 * 1024
CONV_HALO = 32

F32 = jnp.float32
BF16 = jnp.bfloat16


def _sigmoid(x):
    return 1.0 / (1.0 + jnp.exp(-x))


def _lane_norm(x, g, b):
    mu = jnp.mean(x, axis=-1, keepdims=True)
    xc = x - mu
    var = jnp.mean(xc * xc, axis=-1, keepdims=True)
    return xc * lax.rsqrt(var + LN_EPS) * g + b


HEADS_PER_STEP = 2
MIX_ROWS = 512
MIX_SUB = 512
HEAD_COLS = 6 * LANES
OUT_PROJ_SUB = 256


def _regroup_kernel(*refs):
    o_ref = refs[-1]
    for c, w_ref in enumerate(refs[:-1]):
        o_ref[:, c * LANES:(c + 1) * LANES] = w_ref[...].astype(BF16)


def _regroup_w_in(w_in):
    kinds = HEAD_COLS // LANES
    return pl.pallas_call(
        _regroup_kernel,
        grid=(HEADS,),
        in_specs=[pl.BlockSpec((D_MODEL, LANES), functools.partial(lambda c, hd: (0, c * HEADS + hd), c))
                  for c in range(kinds)],
        out_specs=pl.BlockSpec((D_MODEL, HEAD_COLS), lambda hd: (0, hd)),
        out_shape=jax.ShapeDtypeStruct((D_MODEL, HEADS * HEAD_COLS), BF16),
        compiler_params=pltpu.CompilerParams(dimension_semantics=("arbitrary",)),
        name="w_in_regroup",
    )(*([w_in] * kinds))


def _rotary(t, cos, sin_signed, even_lane):
    partner = jnp.where(even_lane, pltpu.roll(t, LANES - 1, 1), pltpu.roll(t, 1, 1))
    return t * cos + partner * sin_signed


def _mixer_kernel(cdec_ref, x_ref, w_ref, cos_ref, sin_ref, decay_ref, zeta_ref, xi_ref, prm_ref, cw_ref,
                  w1f_ref, w3f_ref, w2f_ref, wof_ref,
                  ret_ref, conv_ref, w1b_ref, w3b_ref, w2b_ref, wob_ref, state_ref, uext_ref):
    w1b_ref[...] = w1f_ref[...].astype(BF16)
    w3b_ref[...] = w3f_ref[...].astype(BF16)
    w2b_ref[...] = w2f_ref[...].astype(BF16)
    wob_ref[...] = wof_ref[...].astype(BF16)

    rows = x_ref.shape[0]
    hps = w_ref.shape[1] // HEAD_COLS
    n = pl.program_id(2)
    head0 = pl.program_id(1) * hps

    @pl.when(n == 0)
    def _():
        for j in range(hps):
            state_ref[head0 + j] = jnp.zeros((HEAD_DIM, HEAD_DIM), F32)
            uext_ref[head0 + j, 0:CONV_HALO, :] = jnp.zeros((CONV_HALO, LANES), F32)

    even_lane = (lax.broadcasted_iota(jnp.int32, (CHUNK, LANES), 1) % 2) == 0
    scale = HEAD_DIM ** -0.5
    blk = 64
    first = CONV_HALO - (CONV_TAPS - 1)
    xb = x_ref[...].astype(BF16)

    for j in range(hps):
        head = head0 + j
        cdec = cdec_ref[head]
        decay = decay_ref[j]
        zeta = zeta_ref[j]
        xi = xi_ref[j]
        rg, rb, cb, cg, cbeta = (prm_ref[j, r:r + 1, :] for r in range(5))
        state = state_ref[head]

        for s0 in range(0, rows, MIX_SUB):
            proj = jnp.dot(xb[s0:s0 + MIX_SUB], w_ref[:, j * HEAD_COLS:(j + 1) * HEAD_COLS],
                           preferred_element_type=F32)
            col = lambda c: proj[:, c * LANES:(c + 1) * LANES]

            for c in range(MIX_SUB // CHUNK):
                ls = slice(c * CHUNK, (c + 1) * CHUNK)
                rs = slice(s0 + c * CHUNK, s0 + (c + 1) * CHUNK)
                cos = cos_ref[rs, :]
                sin = sin_ref[rs, :]
                q = _rotary(col(0)[ls], cos, sin, even_lane)
                k = _rotary(col(1)[ls], cos, sin, even_lane) * scale
                v = col(2)[ls].astype(BF16)
                scores = lax.dot_general(q.astype(BF16), k.astype(BF16), (((1,), (1,)), ((), ())),
                                         preferred_element_type=F32) * decay
                inner = jnp.dot(scores.astype(BF16), v, preferred_element_type=F32)
                cross = jnp.dot((q * xi).astype(BF16), state.astype(BF16), preferred_element_type=F32)
                kz_t = jnp.transpose(k * zeta).astype(BF16)
                state = cdec * state + jnp.dot(kz_t, v, preferred_element_type=F32)
                y = _lane_norm(inner + cross, rg, rb)
                g = col(3)[ls]
                ret_ref[j, rs, :] = (g * _sigmoid(g) * y).astype(BF16)

            uext_ref[head, CONV_HALO + s0:CONV_HALO + s0 + MIX_SUB, :] = col(4) * _sigmoid(col(5))
            for r0 in range(s0, s0 + MIX_SUB, blk):
                acc = jnp.zeros((blk, LANES), F32)
                for t in range(CONV_TAPS):
                    acc = acc + cw_ref[j, t:t + 1, :] * uext_ref[head, r0 + first + t:r0 + first + t + blk, :]
                y = _lane_norm(acc + cb, cg, cbeta)
                conv_ref[j, r0:r0 + blk, :] = (y * _sigmoid(y)).astype(BF16)

        state_ref[head] = state
        uext_ref[head, 0:CONV_HALO, :] = uext_ref[head, rows:rows + CONV_HALO, :]


def _mixer(x2, w_heads, cos_t, sin_t, consts, head_params, cw, w1, w3, w2, w_out, batch, seq):
    T = batch * seq
    rows, hps = MIX_ROWS, HEADS_PER_STEP
    ns = seq // rows
    ng = HEADS // hps
    cdec, decay, zeta, xi = consts
    steps = batch * ns * ng
    w1s = w1.reshape(-1, w1.shape[-1])
    w3s = w3.reshape(-1, w3.shape[-1])
    w2s = w2.reshape(-1, w2.shape[-1])
    step_of = lambda b, g, n: ((b * ng + g) * ns + n, 0)
    wspec = lambda w: pl.BlockSpec((w.shape[0] // steps, w.shape[1]), step_of)
    for w in (w1s, w3s, w2s, w_out):
        assert w.shape[0] % (steps * 16) == 0

    def hspec(d1, d2=LANES):
        return pl.BlockSpec((hps, d1, d2), lambda b, g, n: (g, 0, 0))

    out_rows = pl.BlockSpec((hps, rows, LANES), lambda b, g, n: (g, b * ns + n, 0))
    return pl.pallas_call(
        _mixer_kernel,
        grid=(batch, ng, ns),
        in_specs=[pl.BlockSpec(memory_space=pltpu.SMEM),
                  pl.BlockSpec((rows, D_MODEL), lambda b, g, n: (b * ns + n, 0)),
                  pl.BlockSpec((D_MODEL, hps * HEAD_COLS), lambda b, g, n: (0, g)),
                  pl.BlockSpec((rows, LANES), lambda b, g, n: (n, 0)),
                  pl.BlockSpec((rows, LANES), lambda b, g, n: (n, 0)),
                  hspec(CHUNK), hspec(CHUNK), hspec(CHUNK),
                  hspec(SUBLANES), hspec(CONV_TAPS),
                  wspec(w1s), wspec(w3s), wspec(w2s), wspec(w_out)],
        out_specs=[out_rows, out_rows,
                   wspec(w1s), wspec(w3s), wspec(w2s), wspec(w_out)],
        out_shape=[jax.ShapeDtypeStruct((HEADS, T, LANES), BF16),
                   jax.ShapeDtypeStruct((CONV_GROUPS, T, LANES), BF16),
                   jax.ShapeDtypeStruct(w1s.shape, BF16),
                   jax.ShapeDtypeStruct(w3s.shape, BF16),
                   jax.ShapeDtypeStruct(w2s.shape, BF16),
                   jax.ShapeDtypeStruct(w_out.shape, BF16)],
        scratch_shapes=[pltpu.VMEM((HEADS, HEAD_DIM, HEAD_DIM), F32),
                        pltpu.VMEM((HEADS, rows + CONV_HALO, LANES), F32)],
        compiler_params=pltpu.CompilerParams(
            dimension_semantics=("arbitrary", "arbitrary", "arbitrary"), vmem_limit_bytes=VMEM_LIMIT),
        name="in_proj_mixer",
    )(cdec, x2, w_heads, cos_t, sin_t, decay, zeta, xi, head_params, cw, w1s, w3s, w2s, w_out)


def _layer_norm(z, g, b):
    mu = jnp.mean(z, axis=-1, keepdims=True)
    zc = z - mu
    var = jnp.mean(zc * zc, axis=-1, keepdims=True)
    return zc * lax.rsqrt(var + LN_EPS) * g + b


def _outproj_kernel(ret_ref, conv_ref, x_ref, w_ref, g_ref, b_ref, wr_ref, br_ref,
                    h_ref, route_ref, route_t_ref, counts_ref, carry_ref):
    @pl.when(pl.program_id(0) == 0)
    def _():
        carry_ref[...] = jnp.zeros_like(carry_ref)

    carry = carry_ref[:, 0:1]
    for r0 in range(0, x_ref.shape[0], OUT_PROJ_SUB):
        carry = _project_and_route(slice(r0, r0 + OUT_PROJ_SUB), carry, ret_ref, conv_ref, x_ref, w_ref,
                                   g_ref, b_ref, wr_ref, br_ref, h_ref, route_ref, route_t_ref)
    carry_ref[...] = jnp.broadcast_to(carry, carry_ref.shape)
    counts_ref[...] = jnp.broadcast_to(carry, counts_ref.shape)


def _project_and_route(rs, carry, ret_ref, conv_ref, x_ref, w_ref, g_ref, b_ref, wr_ref, br_ref,
                       h_ref, route_ref, route_t_ref):
    mix = jnp.concatenate([ret_ref[s, rs, :] for s in range(HEADS)] +
                          [conv_ref[s, rs, :] for s in range(CONV_GROUPS)], axis=-1)
    y = jnp.dot(mix, w_ref[...], preferred_element_type=F32)
    h = _layer_norm(ALPHA * x_ref[rs, :] + y, g_ref[...], b_ref[...])
    h_ref[rs, :] = h

    logits = jnp.dot(h.astype(BF16), wr_ref[...], preferred_element_type=F32) + br_ref[...]

    tm = logits.shape[0]
    lt = jnp.transpose(logits)
    neg = jnp.float32(-jnp.inf)
    big = jnp.float32(1e9)
    grow = lax.broadcasted_iota(jnp.int32, (SUBLANES, tm), 0).astype(F32)
    gmask = grow < N_GROUPS
    gl = jnp.where(gmask, lt[N_EXPERTS:N_EXPERTS + SUBLANES, :], neg)
    gmax = jnp.max(gl, axis=0, keepdims=True)
    gidx = jnp.min(jnp.where(gl == gmax, grow, big), axis=0, keepdims=True)
    g_w = 1.0 / jnp.sum(jnp.where(gmask, jnp.exp(gl - gmax), 0.0), axis=0, keepdims=True)
    erow = lax.broadcasted_iota(jnp.int32, (N_EXPERTS, tm), 0).astype(F32)
    lo = gidx * EXPERTS_PER_GROUP
    emask = (erow >= lo) & (erow < lo + EXPERTS_PER_GROUP)
    el = jnp.where(emask, lt[0:N_EXPERTS, :], neg)
    top1 = jnp.max(el, axis=0, keepdims=True)
    idx1 = jnp.min(jnp.where(el == top1, erow, big), axis=0, keepdims=True)
    el2 = jnp.where(erow == idx1, neg, el)
    top2 = jnp.max(el2, axis=0, keepdims=True)
    idx2 = jnp.min(jnp.where(el2 == top2, erow, big), axis=0, keepdims=True)
    e2 = jnp.exp(top2 - top1)
    w1 = g_w / (1.0 + e2)
    w2 = w1 * e2

    sel1 = erow == idx1
    sel2 = erow == idx2
    onehot = jnp.where(sel1, 1.0, 0.0) + jnp.where(sel2, 1.0, 0.0)
    r_i = lax.broadcasted_iota(jnp.int32, (tm, tm), 0)
    c_i = lax.broadcasted_iota(jnp.int32, (tm, tm), 1)
    earlier = jnp.where(r_i < c_i, 1.0, 0.0).astype(BF16)
    prefix = jnp.dot(onehot.astype(BF16), earlier, preferred_element_type=F32) + carry
    rank1 = jnp.sum(jnp.where(sel1, prefix, 0.0), axis=0, keepdims=True)
    rank2 = jnp.sum(jnp.where(sel2, prefix, 0.0), axis=0, keepdims=True)

    route_t = jnp.concatenate([w1, w2, idx1, idx2, rank1, rank2, jnp.zeros((2, tm), F32)], axis=0)
    route_t_ref[:, rs] = route_t
    route_ref[rs, :] = jnp.transpose(
        jnp.concatenate([route_t, jnp.zeros((LANES - SUBLANES, tm), F32)], axis=0))
    return carry + jnp.sum(onehot, axis=1, keepdims=True)


def _out_proj(ret3, conv3, x2, w_out_b, ln_g, ln_b, wr_b, br, tm=512):
    T = x2.shape[0]
    cur = lambda i: (0, i, 0)
    row = lambda i: (i, 0)
    const = lambda i: (0, 0)
    return pl.pallas_call(
        _outproj_kernel,
        grid=(T // tm,),
        in_specs=[pl.BlockSpec((HEADS, tm, LANES), cur),
                  pl.BlockSpec((CONV_GROUPS, tm, LANES), cur),
                  pl.BlockSpec((tm, D_MODEL), row),
                  pl.BlockSpec((D_MODEL, D_MODEL), const),
                  pl.BlockSpec((1, D_MODEL), const),
                  pl.BlockSpec((1, D_MODEL), const),
                  pl.BlockSpec((D_MODEL, LANES), const),
                  pl.BlockSpec((1, LANES), const)],
        out_specs=[pl.BlockSpec((tm, D_MODEL), row),
                   pl.BlockSpec((tm, LANES), row),
                   pl.BlockSpec((SUBLANES, tm), lambda i: (0, i)),
                   pl.BlockSpec((N_EXPERTS, LANES), const)],
        out_shape=[jax.ShapeDtypeStruct((T, D_MODEL), F32),
                   jax.ShapeDtypeStruct((T, LANES), F32),
                   jax.ShapeDtypeStruct((SUBLANES, T), F32),
                   jax.ShapeDtypeStruct((N_EXPERTS, LANES), F32)],
        scratch_shapes=[pltpu.VMEM((N_EXPERTS, LANES), F32)],
        compiler_params=pltpu.CompilerParams(
            dimension_semantics=("arbitrary",), vmem_limit_bytes=VMEM_LIMIT),
        name="out_proj_ln_route",
    )(ret3, conv3, x2, w_out_b, ln_g, ln_b, wr_b, br)


MOE_TM = 512
MOE_TB = 512


def _row_copy(src_ref, src_row, dst_ref, dst_row, sem):
    return pltpu.make_async_copy(src_ref.at[pl.ds(src_row, 1), :], dst_ref.at[pl.ds(dst_row, 1), :], sem)


def _experts_kernel(te_ref, nv_ref, src_ref, src_next_ref, h_ref, w1_ref, w3_ref, w2_ref, y_ref, x_ref, sems):
    i = pl.program_id(0)
    last = pl.num_programs(0) - 1
    slot = i % 2
    used = i < nv_ref[0]

    def gather_wait(s):
        pltpu.make_async_copy(h_ref.at[pl.ds(0, MOE_TM), :], x_ref.at[s, pl.ds(0, MOE_TM), :], sems.at[s]).wait()

    @pl.when(i == 0)
    def _():
        x_ref[:, MOE_TM:, :] = jnp.zeros((2, SUBLANES, D_MODEL), F32)

        def first(t, carry):
            _row_copy(h_ref, src_ref[0, 0, t], x_ref.at[0], t, sems.at[0]).start()
            return carry
        lax.fori_loop(0, MOE_TM, first, 0, unroll=8)

    @pl.when(used)
    def _():
        gather_wait(slot)

        def issue_half(k, width):
            for t in range(k * MOE_TM // 2, (k + 1) * MOE_TM // 2):
                _row_copy(h_ref, src_next_ref[0, 0, t], x_ref.at[1 - slot], t, sems.at[1 - slot]).start()
            spare = pltpu.bitcast(x_ref[1 - slot, MOE_TM:, k * LANES:(k + 1) * LANES], jnp.uint32)
            zero = pltpu.bitcast((spare >> 16) >> 16, F32)
            return jnp.tile(zero, (MOE_TM // SUBLANES, width // LANES))

        xb = x_ref[slot, 0:MOE_TM, :].astype(BF16)
        a1 = jnp.dot(xb, w1_ref[0], preferred_element_type=F32)
        a3 = jnp.dot(xb, w3_ref[0], preferred_element_type=F32) + issue_half(0, D_FF)
        act = (a1 * _sigmoid(a1) * a3).astype(BF16)
        y_ref[...] = jnp.dot(act, w2_ref[0], preferred_element_type=F32) + issue_half(1, D_MODEL)

        @pl.when(i == last)
        def _():
            gather_wait(1 - slot)

    @pl.when(jnp.logical_not(used))
    def _():
        @pl.when(i == nv_ref[0])
        def _():
            gather_wait(slot)
        y_ref[...] = jnp.zeros_like(y_ref)


def _experts(tile_expert, n_valid, src_tok, h, w1b, w3b, w2b):
    n_tiles = src_tok.shape[0]
    row = lambda i, te, nv: (i, 0)
    wsel = lambda i, te, nv: (te[i], 0, 0)
    smem_row = lambda f: pl.BlockSpec((1, 1, MOE_TM), f, memory_space=pltpu.SMEM)
    return pl.pallas_call(
        _experts_kernel,
        grid_spec=pltpu.PrefetchScalarGridSpec(
            num_scalar_prefetch=2,
            grid=(n_tiles,),
            in_specs=[smem_row(lambda i, te, nv: (i, 0, 0)),
                      smem_row(lambda i, te, nv: (jnp.minimum(i + 1, n_tiles - 1), 0, 0)),
                      pl.BlockSpec(memory_space=pl.ANY),
                      pl.BlockSpec((1, D_MODEL, D_FF), wsel),
                      pl.BlockSpec((1, D_MODEL, D_FF), wsel),
                      pl.BlockSpec((1, D_FF, D_MODEL), wsel)],
            out_specs=pl.BlockSpec((MOE_TM, D_MODEL), row),
            scratch_shapes=[pltpu.VMEM((2, MOE_TM + SUBLANES, D_MODEL), F32),
                            pltpu.SemaphoreType.DMA((2,))]),
        out_shape=jax.ShapeDtypeStruct((n_tiles * MOE_TM, D_MODEL), F32),
        compiler_params=pltpu.CompilerParams(
            dimension_semantics=("arbitrary",), vmem_limit_bytes=VMEM_LIMIT),
        name="moe_experts",
    )(tile_expert, n_valid, src_tok, src_tok, h, w1b, w3b, w2b)


def _combine_kernel(pos_ref, pos_next_ref, route_ref, h_ref, g_ref, b_ref, ys_ref, o_ref,
                    ya0_ref, yb0_ref, ya1_ref, yb1_ref, sems):
    tb = h_ref.shape[0]
    i = pl.program_id(0)
    bufs = ((ya0_ref, yb0_ref), (ya1_ref, yb1_ref))

    def gather_wait(s):
        for k in range(TOP_K):
            pltpu.make_async_copy(ys_ref.at[pl.ds(0, tb), :], bufs[s][k], sems.at[s]).wait()

    @pl.when(i == 0)
    def _():
        def first(t, carry):
            for k in range(TOP_K):
                _row_copy(ys_ref, pos_ref[0, 0, k * tb + t], bufs[0][k], t, sems.at[0]).start()
            return carry
        lax.fori_loop(0, tb, first, 0, unroll=8)

    for par in range(2):
        @pl.when(i % 2 == par)
        def _():
            gather_wait(par)
            for t in range(tb):
                for k in range(TOP_K):
                    _row_copy(ys_ref, pos_next_ref[0, 0, k * tb + t], bufs[1 - par][k], t, sems.at[1 - par]).start()
            route = route_ref[...]
            ffn = route[:, 0:1] * bufs[par][0][...] + route[:, 1:2] * bufs[par][1][...]
            o_ref[...] = _layer_norm(ALPHA * h_ref[...] + ffn, g_ref[...], b_ref[...])

            @pl.when(i == pl.num_programs(0) - 1)
            def _():
                gather_wait(1 - par)


def _combine(pos3, route, h, ln_g, ln_b, ys):
    T = h.shape[0]
    tb = MOE_TB
    row = lambda i: (i, 0)
    const = lambda i: (0, 0)
    return pl.pallas_call(
        _combine_kernel,
        grid=(T // tb,),
        in_specs=[pl.BlockSpec((1, 1, 2 * tb), lambda i: (i, 0, 0), memory_space=pltpu.SMEM),
                  pl.BlockSpec((1, 1, 2 * tb), lambda i: (jnp.minimum(i + 1, T // tb - 1), 0, 0),
                               memory_space=pltpu.SMEM),
                  pl.BlockSpec((tb, LANES), row),
                  pl.BlockSpec((tb, D_MODEL), row),
                  pl.BlockSpec((1, D_MODEL), const),
                  pl.BlockSpec((1, D_MODEL), const),
                  pl.BlockSpec(memory_space=pl.ANY)],
        out_specs=pl.BlockSpec((tb, D_MODEL), row),
        out_shape=jax.ShapeDtypeStruct((T, D_MODEL), F32),
        scratch_shapes=[pltpu.VMEM((tb, D_MODEL), F32)] * (2 * TOP_K) + [pltpu.SemaphoreType.DMA((2,))],
        compiler_params=pltpu.CompilerParams(
            dimension_semantics=("arbitrary",), vmem_limit_bytes=VMEM_LIMIT),
        name="moe_combine_ln",
    )(pos3, pos3, route, h, ln_g, ln_b, ys)


def _routed_moe(h, route, route_t, counts, w1b, w3b, w2b, ln_g, ln_b):
    T = h.shape[0]
    n_tiles = (TOP_K * T) // MOE_TM + N_EXPERTS
    i32 = jnp.int32
    cnt = counts[:, 0].astype(i32)
    ntile = (cnt + MOE_TM - 1) // MOE_TM
    tile_end = jnp.cumsum(ntile)
    n_valid = tile_end[-1:]
    seg_start = (tile_end - ntile) * MOE_TM
    info = route_t[2:6].astype(i32)
    eids = jnp.arange(N_EXPERTS, dtype=i32)
    start_of = lambda e: jnp.sum(jnp.where(e[None, :] == eids[:, None], seg_start[:, None], 0), axis=0)
    pos1 = start_of(info[0]) + info[2]
    pos2 = start_of(info[1]) + info[3]
    pos3 = jnp.concatenate([pos1.reshape(T // MOE_TB, 1, MOE_TB), pos2.reshape(T // MOE_TB, 1, MOE_TB)], axis=2)
    tile_ids = jnp.minimum(jnp.arange(n_tiles, dtype=i32), n_valid[0] - 1)
    tile_expert = jnp.sum((tile_ids[:, None] >= tile_end[None, :]).astype(i32), axis=1)

    order = jnp.argsort(jnp.concatenate([pos1, pos2])).astype(i32)
    n_rows = n_tiles * MOE_TM
    padded = jnp.concatenate([jnp.zeros((n_rows,), i32), jnp.where(order >= T, order - T, order),
                              jnp.zeros((2 * n_rows - TOP_K * T,), i32)])
    shift = seg_start - (jnp.cumsum(cnt) - cnt)
    row_expert = jnp.repeat(tile_expert, MOE_TM)
    src_tok = jnp.zeros((n_rows,), i32)
    for e in range(N_EXPERTS):
        view = lax.dynamic_slice(padded, (n_rows - shift[e],), (n_rows,))
        src_tok = jnp.where(row_expert == e, view, src_tok)

    ys = _experts(tile_expert, n_valid.astype(i32), src_tok.reshape(n_tiles, 1, MOE_TM), h, w1b, w3b, w2b)
    return _combine(pos3, route, h, ln_g, ln_b, ys)


def _retention_constants():
    hh = np.arange(HEADS, dtype=np.float64)
    log_gamma = np.log1p(-np.exp2(-5.0 - hh))
    pos = np.arange(CHUNK, dtype=np.float64)
    diff = pos[:, None] - pos[None, :]
    causal = diff >= 0
    decay = np.where(causal[None], np.exp(log_gamma[:, None, None] * np.where(causal, diff, 0.0)[None]), 0.0)
    zeta = np.exp(log_gamma[:, None] * (CHUNK - 1 - pos)[None])
    xi = np.exp(log_gamma[:, None] * (pos + 1.0)[None])
    cdec = np.exp(log_gamma * CHUNK)
    bc = lambda t: np.broadcast_to(t[:, :, None], (HEADS, CHUNK, LANES))
    return (jnp.asarray(cdec, F32), jnp.asarray(decay, F32),
            jnp.asarray(bc(zeta), F32), jnp.asarray(bc(xi), F32))


def _rope_tables(seq):
    inv = ROPE_BASE ** (-jnp.arange(0, HEAD_DIM, 2, dtype=F32) / HEAD_DIM)
    ang = jnp.arange(seq, dtype=F32)[:, None] * inv[None, :]
    cos = jnp.repeat(jnp.cos(ang), 2, axis=-1)
    sin = jnp.sin(ang)
    sin_signed = jnp.stack([-sin, sin], axis=-1).reshape(seq, HEAD_DIM)
    return cos, sin_signed


def kernel(x, w_in, ret_norm_g, ret_norm_b, conv_w, conv_b, conv_norm_g, conv_norm_b, w_out, ln1_g, ln1_b, w_router_group, b_router_group, w_router_expert, b_router_expert, w1, w3, w2, ln2_g, ln2_b):
    batch, seq, _ = x.shape
    T = batch * seq
    x2 = x.reshape(T, D_MODEL)

    w_heads = _regroup_w_in(w_in)

    cos_t, sin_t = _rope_tables(seq)
    head_params = jnp.stack([ret_norm_g, ret_norm_b, conv_b, conv_norm_g, conv_norm_b]
                            + [jnp.zeros_like(conv_b)] * (SUBLANES - 5)).reshape(SUBLANES, HEADS, LANES)
    head_params = jnp.transpose(head_params, (1, 0, 2))
    cw = jnp.transpose(conv_w.reshape(CONV_TAPS, CONV_GROUPS, LANES), (1, 0, 2))
    ret3, conv3, w1b, w3b, w2b, w_out_b = _mixer(
        x2, w_heads, cos_t, sin_t, _retention_constants(), head_params, cw, w1, w3, w2, w_out, batch, seq)

    wr = jnp.concatenate([jnp.transpose(w_router_expert, (1, 0, 2)).reshape(D_MODEL, N_EXPERTS),
                          w_router_group], axis=1)
    wr = jnp.pad(wr, ((0, 0), (0, LANES - wr.shape[1])))
    br = jnp.pad(jnp.concatenate([b_router_expert.reshape(-1), b_router_group]),
                 (0, LANES - N_EXPERTS - N_GROUPS)).reshape(1, LANES)

    h, route, route_t, counts = _out_proj(ret3, conv3, x2, w_out_b, ln1_g.reshape(1, -1),
                                          ln1_b.reshape(1, -1), wr.astype(BF16), br)

    y = _routed_moe(h, route, route_t, counts, w1b.reshape(w1.shape), w3b.reshape(w3.shape),
                    w2b.reshape(w2.shape), ln2_g.reshape(1, -1), ln2_b.reshape(1, -1))
    return y.reshape(batch, seq, D_MODEL)
```

```python
import functools

import numpy as np
import jax
import jax.numpy as jnp
from jax import lax
from jax.experimental import pallas as pl
from jax.experimental.pallas import tpu as pltpu

D_MODEL = 2048
RET_WIDTH = D_MODEL // 2
CONV_WIDTH = D_MODEL - RET_WIDTH
HEADS = 8
HEAD_DIM = RET_WIDTH // HEADS
CONV_GROUPS = 8
CONV_TAPS = 31
CHUNK = 128
ROPE_BASE = 10000.0
IN_COLS = 4 * RET_WIDTH + 2 * CONV_WIDTH
N_GROUPS = 4
EXPERTS_PER_GROUP = 4
N_EXPERTS = N_GROUPS * EXPERTS_PER_GROUP
TOP_K = 2
D_FF = D_MODEL // 2
LN_EPS = 1e-5
ALPHA = 2.0 ** 0.25

LANES = 128
SUBLANES = 8
VMEM_LIMIT = 56 * 1024 * 1024
CONV_HALO = 32

F32 = jnp.float32
BF16 = jnp.bfloat16


def _sigmoid(x):
    return 1.0 / (1.0 + jnp.exp(-x))


def _lane_norm(x, g, b):
    mu = jnp.mean(x, axis=-1, keepdims=True)
    xc = x - mu
    var = jnp.mean(xc * xc, axis=-1, keepdims=True)
    return xc * lax.rsqrt(var + LN_EPS) * g + b


HEADS_PER_STEP = 2
MIX_ROWS = 512
MIX_SUB = 512
HEAD_COLS = 6 * LANES
OUT_PROJ_SUB = 256


def _regroup_kernel(*refs):
    o_ref = refs[-1]
    for c, w_ref in enumerate(refs[:-1]):
        o_ref[:, c * LANES:(c + 1) * LANES] = w_ref[...].astype(BF16)


def _regroup_w_in(w_in):
    kinds = HEAD_COLS // LANES
    return pl.pallas_call(
        _regroup_kernel,
        grid=(HEADS,),
        in_specs=[pl.BlockSpec((D_MODEL, LANES), functools.partial(lambda c, hd: (0, c * HEADS + hd), c))
                  for c in range(kinds)],
        out_specs=pl.BlockSpec((D_MODEL, HEAD_COLS), lambda hd: (0, hd)),
        out_shape=jax.ShapeDtypeStruct((D_MODEL, HEADS * HEAD_COLS), BF16),
        compiler_params=pltpu.CompilerParams(dimension_semantics=("arbitrary",)),
        name="w_in_regroup",
    )(*([w_in] * kinds))


def _rotary(t, cos, sin_signed, even_lane):
    partner = jnp.where(even_lane, pltpu.roll(t, LANES - 1, 1), pltpu.roll(t, 1, 1))
    return t * cos + partner * sin_signed


def _mixer_kernel(cdec_ref, x_ref, w_ref, cos_ref, sin_ref, decay_ref, zeta_ref, xi_ref, prm_ref, cw_ref,
                  w1f_ref, w3f_ref, w2f_ref, wof_ref,
                  ret_ref, conv_ref, w1b_ref, w3b_ref, w2b_ref, wob_ref, state_ref, uext_ref):
    w1b_ref[...] = w1f_ref[...].astype(BF16)
    w3b_ref[...] = w3f_ref[...].astype(BF16)
    w2b_ref[...] = w2f_ref[...].astype(BF16)
    wob_ref[...] = wof_ref[...].astype(BF16)

    rows = x_ref.shape[0]
    hps = w_ref.shape[1] // HEAD_COLS
    n = pl.program_id(2)
    head0 = pl.program_id(1) * hps

    @pl.when(n == 0)
    def _():
        for j in range(hps):
            state_ref[head0 + j] = jnp.zeros((HEAD_DIM, HEAD_DIM), F32)
            uext_ref[head0 + j, 0:CONV_HALO, :] = jnp.zeros((CONV_HALO, LANES), F32)

    even_lane = (lax.broadcasted_iota(jnp.int32, (CHUNK, LANES), 1) % 2) == 0
    scale = HEAD_DIM ** -0.5
    blk = 64
    first = CONV_HALO - (CONV_TAPS - 1)
    xb = x_ref[...].astype(BF16)

    for j in range(hps):
        head = head0 + j
        cdec = cdec_ref[head]
        decay = decay_ref[j]
        zeta = zeta_ref[j]
        xi = xi_ref[j]
        rg, rb, cb, cg, cbeta = (prm_ref[j, r:r + 1, :] for r in range(5))
        state = state_ref[head]

        for s0 in range(0, rows, MIX_SUB):
            proj = jnp.dot(xb[s0:s0 + MIX_SUB], w_ref[:, j * HEAD_COLS:(j + 1) * HEAD_COLS],
                           preferred_element_type=F32)
            col = lambda c: proj[:, c * LANES:(c + 1) * LANES]

            for c in range(MIX_SUB // CHUNK):
                ls = slice(c * CHUNK, (c + 1) * CHUNK)
                rs = slice(s0 + c * CHUNK, s0 + (c + 1) * CHUNK)
                cos = cos_ref[rs, :]
                sin = sin_ref[rs, :]
                q = _rotary(col(0)[ls], cos, sin, even_lane)
                k = _rotary(col(1)[ls], cos, sin, even_lane) * scale
                v = col(2)[ls].astype(BF16)
                scores = lax.dot_general(q.astype(BF16), k.astype(BF16), (((1,), (1,)), ((), ())),
                                         preferred_element_type=F32) * decay
                inner = jnp.dot(scores.astype(BF16), v, preferred_element_type=F32)
                cross = jnp.dot((q * xi).astype(BF16), state.astype(BF16), preferred_element_type=F32)
                kz_t = jnp.transpose(k * zeta).astype(BF16)
                state = cdec * state + jnp.dot(kz_t, v, preferred_element_type=F32)
                y = _lane_norm(inner + cross, rg, rb)
                g = col(3)[ls]
                ret_ref[j, rs, :] = (g * _sigmoid(g) * y).astype(BF16)

            uext_ref[head, CONV_HALO + s0:CONV_HALO + s0 + MIX_SUB, :] = col(4) * _sigmoid(col(5))
            for r0 in range(s0, s0 + MIX_SUB, blk):
                acc = cb
                for t in range(CONV_TAPS):
                    acc = acc + cw_ref[j, t:t + 1, :] * uext_ref[head, r0 + first + t:r0 + first + t + blk, :]
                y = _lane_norm(acc, cg, cbeta)
                conv_ref[j, r0:r0 + blk, :] = (y * _sigmoid(y)).astype(BF16)

        state_ref[head] = state
        uext_ref[head, 0:CONV_HALO, :] = uext_ref[head, rows:rows + CONV_HALO, :]


def _mixer(x2, w_heads, cos_t, sin_t, consts, head_params, cw, w1, w3, w2, w_out, batch, seq):
    T = batch * seq
    rows, hps = MIX_ROWS, HEADS_PER_STEP
    ns = seq // rows
    ng = HEADS // hps
    cdec, decay, zeta, xi = consts
    steps = batch * ns * ng
    w1s = w1.reshape(-1, w1.shape[-1])
    w3s = w3.reshape(-1, w3.shape[-1])
    w2s = w2.reshape(-1, w2.shape[-1])
    step_of = lambda b, g, n: ((b * ng + g) * ns + n, 0)
    wspec = lambda w: pl.BlockSpec((w.shape[0] // steps, w.shape[1]), step_of)
    for w in (w1s, w3s, w2s, w_out):
        assert w.shape[0] % (steps * 16) == 0

    def hspec(d1, d2=LANES):
        return pl.BlockSpec((hps, d1, d2), lambda b, g, n: (g, 0, 0))

    out_rows = pl.BlockSpec((hps, rows, LANES), lambda b, g, n: (g, b * ns + n, 0))
    return pl.pallas_call(
        _mixer_kernel,
        grid=(batch, ng, ns),
        in_specs=[pl.BlockSpec(memory_space=pltpu.SMEM),
                  pl.BlockSpec((rows, D_MODEL), lambda b, g, n: (b * ns + n, 0)),
                  pl.BlockSpec((D_MODEL, hps * HEAD_COLS), lambda b, g, n: (0, g)),
                  pl.BlockSpec((rows, LANES), lambda b, g, n: (n, 0)),
                  pl.BlockSpec((rows, LANES), lambda b, g, n: (n, 0)),
                  hspec(CHUNK), hspec(CHUNK), hspec(CHUNK),
                  hspec(SUBLANES), hspec(CONV_TAPS),
                  wspec(w1s), wspec(w3s), wspec(w2s), wspec(w_out)],
        out_specs=[out_rows, out_rows,
                   wspec(w1s), wspec(w3s), wspec(w2s), wspec(w_out)],
        out_shape=[jax.ShapeDtypeStruct((HEADS, T, LANES), BF16),
                   jax.ShapeDtypeStruct((CONV_GROUPS, T, LANES), BF16),
                   jax.ShapeDtypeStruct(w1s.shape, BF16),
                   jax.ShapeDtypeStruct(w3s.shape, BF16),
                   jax.ShapeDtypeStruct(w2s.shape, BF16),
                   jax.ShapeDtypeStruct(w_out.shape, BF16)],
        scratch_shapes=[pltpu.VMEM((HEADS, HEAD_DIM, HEAD_DIM), F32),
                        pltpu.VMEM((HEADS, rows + CONV_HALO, LANES), F32)],
        compiler_params=pltpu.CompilerParams(
            dimension_semantics=("arbitrary", "arbitrary", "arbitrary"), vmem_limit_bytes=VMEM_LIMIT),
        name="in_proj_mixer",
    )(cdec, x2, w_heads, cos_t, sin_t, decay, zeta, xi, head_params, cw, w1s, w3s, w2s, w_out)


def _layer_norm(z, g, b):
    mu = jnp.mean(z, axis=-1, keepdims=True)
    zc = z - mu
    var = jnp.mean(zc * zc, axis=-1, keepdims=True)
    return zc * lax.rsqrt(var + LN_EPS) * g + b


def _outproj_kernel(ret_ref, conv_ref, x_ref, w_ref, g_ref, b_ref, wr_ref, br_ref,
                    h_ref, route_ref, route_t_ref, counts_ref, carry_ref):
    @pl.when(pl.program_id(0) == 0)
    def _():
        carry_ref[...] = jnp.zeros_like(carry_ref)

    carry = carry_ref[:, 0:1]
    for r0 in range(0, x_ref.shape[0], OUT_PROJ_SUB):
        carry = _project_and_route(slice(r0, r0 + OUT_PROJ_SUB), carry, ret_ref, conv_ref, x_ref, w_ref,
                                   g_ref, b_ref, wr_ref, br_ref, h_ref, route_ref, route_t_ref)
    carry_ref[...] = jnp.broadcast_to(carry, carry_ref.shape)
    counts_ref[...] = jnp.broadcast_to(carry, counts_ref.shape)


def _project_and_route(rs, carry, ret_ref, conv_ref, x_ref, w_ref, g_ref, b_ref, wr_ref, br_ref,
                       h_ref, route_ref, route_t_ref):
    mix = jnp.concatenate([ret_ref[s, rs, :] for s in range(HEADS)] +
                          [conv_ref[s, rs, :] for s in range(CONV_GROUPS)], axis=-1)
    y = jnp.dot(mix, w_ref[...], preferred_element_type=F32)
    h = _layer_norm(ALPHA * x_ref[rs, :] + y, g_ref[...], b_ref[...])
    h_ref[rs, :] = h

    logits = jnp.dot(h.astype(BF16), wr_ref[...], preferred_element_type=F32) + br_ref[...]

    tm = logits.shape[0]
    lt = jnp.transpose(logits)
    neg = jnp.float32(-jnp.inf)
    big = jnp.float32(1e9)
    grow = lax.broadcasted_iota(jnp.int32, (SUBLANES, tm), 0).astype(F32)
    gmask = grow < N_GROUPS
    gl = jnp.where(gmask, lt[N_EXPERTS:N_EXPERTS + SUBLANES, :], neg)
    gmax = jnp.max(gl, axis=0, keepdims=True)
    gidx = jnp.min(jnp.where(gl == gmax, grow, big), axis=0, keepdims=True)
    g_w = 1.0 / jnp.sum(jnp.where(gmask, jnp.exp(gl - gmax), 0.0), axis=0, keepdims=True)
    erow = lax.broadcasted_iota(jnp.int32, (N_EXPERTS, tm), 0).astype(F32)
    lo = gidx * EXPERTS_PER_GROUP
    emask = (erow >= lo) & (erow < lo + EXPERTS_PER_GROUP)
    el = jnp.where(emask, lt[0:N_EXPERTS, :], neg)
    top1 = jnp.max(el, axis=0, keepdims=True)
    idx1 = jnp.min(jnp.where(el == top1, erow, big), axis=0, keepdims=True)
    el2 = jnp.where(erow == idx1, neg, el)
    top2 = jnp.max(el2, axis=0, keepdims=True)
    idx2 = jnp.min(jnp.where(el2 == top2, erow, big), axis=0, keepdims=True)
    e2 = jnp.exp(top2 - top1)
    w1 = g_w / (1.0 + e2)
    w2 = w1 * e2

    sel1 = erow == idx1
    sel2 = erow == idx2
    onehot = jnp.where(sel1, 1.0, 0.0) + jnp.where(sel2, 1.0, 0.0)
    r_i = lax.broadcasted_iota(jnp.int32, (tm, tm), 0)
    c_i = lax.broadcasted_iota(jnp.int32, (tm, tm), 1)
    earlier = jnp.where(r_i < c_i, 1.0, 0.0).astype(BF16)
    prefix = jnp.dot(onehot.astype(BF16), earlier, preferred_element_type=F32) + carry
    rank1 = jnp.sum(jnp.where(sel1, prefix, 0.0), axis=0, keepdims=True)
    rank2 = jnp.sum(jnp.where(sel2, prefix, 0.0), axis=0, keepdims=True)

    route_t = jnp.concatenate([w1, w2, idx1, idx2, rank1, rank2, jnp.zeros((2, tm), F32)], axis=0)
    route_t_ref[:, rs] = route_t
    route_ref[rs, :] = jnp.transpose(
        jnp.concatenate([route_t, jnp.zeros((LANES - SUBLANES, tm), F32)], axis=0))
    return carry + jnp.sum(onehot, axis=1, keepdims=True)


def _out_proj(ret3, conv3, x2, w_out_b, ln_g, ln_b, wr_b, br, tm=512):
    T = x2.shape[0]
    cur = lambda i: (0, i, 0)
    row = lambda i: (i, 0)
    const = lambda i: (0, 0)
    return pl.pallas_call(
        _outproj_kernel,
        grid=(T // tm,),
        in_specs=[pl.BlockSpec((HEADS, tm, LANES), cur),
                  pl.BlockSpec((CONV_GROUPS, tm, LANES), cur),
                  pl.BlockSpec((tm, D_MODEL), row),
                  pl.BlockSpec((D_MODEL, D_MODEL), const),
                  pl.BlockSpec((1, D_MODEL), const),
                  pl.BlockSpec((1, D_MODEL), const),
                  pl.BlockSpec((D_MODEL, LANES), const),
                  pl.BlockSpec((1, LANES), const)],
        out_specs=[pl.BlockSpec((tm, D_MODEL), row),
                   pl.BlockSpec((tm, LANES), row),
                   pl.BlockSpec((SUBLANES, tm), lambda i: (0, i)),
                   pl.BlockSpec((N_EXPERTS, LANES), const)],
        out_shape=[jax.ShapeDtypeStruct((T, D_MODEL), F32),
                   jax.ShapeDtypeStruct((T, LANES), F32),
                   jax.ShapeDtypeStruct((SUBLANES, T), F32),
                   jax.ShapeDtypeStruct((N_EXPERTS, LANES), F32)],
        scratch_shapes=[pltpu.VMEM((N_EXPERTS, LANES), F32)],
        compiler_params=pltpu.CompilerParams(
            dimension_semantics=("arbitrary",), vmem_limit_bytes=VMEM_LIMIT),
        name="out_proj_ln_route",
    )(ret3, conv3, x2, w_out_b, ln_g, ln_b, wr_b, br)


MOE_TM = 512
MOE_TB = 512


def _row_copy(src_ref, src_row, dst_ref, dst_row, sem):
    return pltpu.make_async_copy(src_ref.at[pl.ds(src_row, 1), :], dst_ref.at[pl.ds(dst_row, 1), :], sem)


def _experts_kernel(te_ref, nv_ref, src_ref, src_next_ref, h_ref, w1_ref, w3_ref, w2_ref, y_ref, x_ref, sems):
    i = pl.program_id(0)
    last = pl.num_programs(0) - 1
    slot = i % 2
    used = i < nv_ref[0]

    def gather_wait(s):
        pltpu.make_async_copy(h_ref.at[pl.ds(0, MOE_TM), :], x_ref.at[s, pl.ds(0, MOE_TM), :], sems.at[s]).wait()

    @pl.when(i == 0)
    def _():
        x_ref[:, MOE_TM:, :] = jnp.zeros((2, SUBLANES, D_MODEL), F32)

        def first(t, carry):
            _row_copy(h_ref, src_ref[0, 0, t], x_ref.at[0], t, sems.at[0]).start()
            return carry
        lax.fori_loop(0, MOE_TM, first, 0, unroll=8)

    @pl.when(used)
    def _():
        gather_wait(slot)

        def issue_half(k, width):
            for t in range(k * MOE_TM // 2, (k + 1) * MOE_TM // 2):
                _row_copy(h_ref, src_next_ref[0, 0, t], x_ref.at[1 - slot], t, sems.at[1 - slot]).start()
            spare = pltpu.bitcast(x_ref[1 - slot, MOE_TM:, k * LANES:(k + 1) * LANES], jnp.uint32)
            zero = pltpu.bitcast((spare >> 16) >> 16, F32)
            return jnp.tile(zero, (MOE_TM // SUBLANES, width // LANES))

        xb = x_ref[slot, 0:MOE_TM, :].astype(BF16)
        a1 = jnp.dot(xb, w1_ref[0], preferred_element_type=F32)
        a3 = jnp.dot(xb, w3_ref[0], preferred_element_type=F32) + issue_half(0, D_FF)
        act = (a1 * _sigmoid(a1) * a3).astype(BF16)
        y_ref[...] = jnp.dot(act, w2_ref[0], preferred_element_type=F32) + issue_half(1, D_MODEL)

        @pl.when(i == last)
        def _():
            gather_wait(1 - slot)

    @pl.when(jnp.logical_not(used))
    def _():
        @pl.when(i == nv_ref[0])
        def _():
            gather_wait(slot)
        y_ref[...] = jnp.zeros_like(y_ref)


def _experts(tile_expert, n_valid, src_tok, h, w1b, w3b, w2b):
    n_tiles = src_tok.shape[0]
    row = lambda i, te, nv: (i, 0)
    wsel = lambda i, te, nv: (te[i], 0, 0)
    smem_row = lambda f: pl.BlockSpec((1, 1, MOE_TM), f, memory_space=pltpu.SMEM)
    return pl.pallas_call(
        _experts_kernel,
        grid_spec=pltpu.PrefetchScalarGridSpec(
            num_scalar_prefetch=2,
            grid=(n_tiles,),
            in_specs=[smem_row(lambda i, te, nv: (i, 0, 0)),
                      smem_row(lambda i, te, nv: (jnp.minimum(i + 1, n_tiles - 1), 0, 0)),
                      pl.BlockSpec(memory_space=pl.ANY),
                      pl.BlockSpec((1, D_MODEL, D_FF), wsel),
                      pl.BlockSpec((1, D_MODEL, D_FF), wsel),
                      pl.BlockSpec((1, D_FF, D_MODEL), wsel)],
            out_specs=pl.BlockSpec((MOE_TM, D_MODEL), row),
            scratch_shapes=[pltpu.VMEM((2, MOE_TM + SUBLANES, D_MODEL), F32),
                            pltpu.SemaphoreType.DMA((2,))]),
        out_shape=jax.ShapeDtypeStruct((n_tiles * MOE_TM, D_MODEL), F32),
        compiler_params=pltpu.CompilerParams(
            dimension_semantics=("arbitrary",), vmem_limit_bytes=VMEM_LIMIT),
        name="moe_experts",
    )(tile_expert, n_valid, src_tok, src_tok, h, w1b, w3b, w2b)


def _combine_kernel(pos_ref, pos_next_ref, route_ref, h_ref, g_ref, b_ref, ys_ref, o_ref,
                    ya0_ref, yb0_ref, ya1_ref, yb1_ref, sems):
    tb = h_ref.shape[0]
    i = pl.program_id(0)
    bufs = ((ya0_ref, yb0_ref), (ya1_ref, yb1_ref))

    def gather_wait(s):
        for k in range(TOP_K):
            pltpu.make_async_copy(ys_ref.at[pl.ds(0, tb), :], bufs[s][k], sems.at[s]).wait()

    @pl.when(i == 0)
    def _():
        def first(t, carry):
            for k in range(TOP_K):
                _row_copy(ys_ref, pos_ref[0, 0, k * tb + t], bufs[0][k], t, sems.at[0]).start()
            return carry
        lax.fori_loop(0, tb, first, 0, unroll=8)

    for par in range(2):
        @pl.when(i % 2 == par)
        def _():
            gather_wait(par)
            for t in range(tb):
                for k in range(TOP_K):
                    _row_copy(ys_ref, pos_next_ref[0, 0, k * tb + t], bufs[1 - par][k], t, sems.at[1 - par]).start()
            route = route_ref[...]
            ffn = route[:, 0:1] * bufs[par][0][...] + route[:, 1:2] * bufs[par][1][...]
            o_ref[...] = _layer_norm(ALPHA * h_ref[...] + ffn, g_ref[...], b_ref[...])

            @pl.when(i == pl.num_programs(0) - 1)
            def _():
                gather_wait(1 - par)


def _combine(pos3, route, h, ln_g, ln_b, ys):
    T = h.shape[0]
    tb = MOE_TB
    row = lambda i: (i, 0)
    const = lambda i: (0, 0)
    return pl.pallas_call(
        _combine_kernel,
        grid=(T // tb,),
        in_specs=[pl.BlockSpec((1, 1, 2 * tb), lambda i: (i, 0, 0), memory_space=pltpu.SMEM),
                  pl.BlockSpec((1, 1, 2 * tb), lambda i: (jnp.minimum(i + 1, T // tb - 1), 0, 0),
                               memory_space=pltpu.SMEM),
                  pl.BlockSpec((tb, LANES), row),
                  pl.BlockSpec((tb, D_MODEL), row),
                  pl.BlockSpec((1, D_MODEL), const),
                  pl.BlockSpec((1, D_MODEL), const),
                  pl.BlockSpec(memory_space=pl.ANY)],
        out_specs=pl.BlockSpec((tb, D_MODEL), row),
        out_shape=jax.ShapeDtypeStruct((T, D_MODEL), F32),
        scratch_shapes=[pltpu.VMEM((tb, D_MODEL), F32)] * (2 * TOP_K) + [pltpu.SemaphoreType.DMA((2,))],
        compiler_params=pltpu.CompilerParams(
            dimension_semantics=("arbitrary",), vmem_limit_bytes=VMEM_LIMIT),
        name="moe_combine_ln",
    )(pos3, pos3, route, h, ln_g, ln_b, ys)


def _routed_moe(h, route, route_t, counts, w1b, w3b, w2b, ln_g, ln_b):
    T = h.shape[0]
    n_tiles = (TOP_K * T) // MOE_TM + N_EXPERTS
    i32 = jnp.int32
    cnt = counts[:, 0].astype(i32)
    ntile = (cnt + MOE_TM - 1) // MOE_TM
    tile_end = jnp.cumsum(ntile)
    n_valid = tile_end[-1:]
    seg_start = (tile_end - ntile) * MOE_TM
    info = route_t[2:6].astype(i32)
    eids = jnp.arange(N_EXPERTS, dtype=i32)
    start_of = lambda e: jnp.sum(jnp.where(e[None, :] == eids[:, None], seg_start[:, None], 0), axis=0)
    pos1 = start_of(info[0]) + info[2]
    pos2 = start_of(info[1]) + info[3]
    pos3 = jnp.concatenate([pos1.reshape(T // MOE_TB, 1, MOE_TB), pos2.reshape(T // MOE_TB, 1, MOE_TB)], axis=2)
    tile_ids = jnp.minimum(jnp.arange(n_tiles, dtype=i32), n_valid[0] - 1)
    tile_expert = jnp.sum((tile_ids[:, None] >= tile_end[None, :]).astype(i32), axis=1)

    order = jnp.argsort(jnp.concatenate([pos1, pos2])).astype(i32)
    n_rows = n_tiles * MOE_TM
    padded = jnp.concatenate([jnp.zeros((n_rows,), i32), jnp.where(order >= T, order - T, order),
                              jnp.zeros((2 * n_rows - TOP_K * T,), i32)])
    shift = seg_start - (jnp.cumsum(cnt) - cnt)
    row_expert = jnp.repeat(tile_expert, MOE_TM)
    src_tok = jnp.zeros((n_rows,), i32)
    for e in range(N_EXPERTS):
        view = lax.dynamic_slice(padded, (n_rows - shift[e],), (n_rows,))
        src_tok = jnp.where(row_expert == e, view, src_tok)

    ys = _experts(tile_expert, n_valid.astype(i32), src_tok.reshape(n_tiles, 1, MOE_TM), h, w1b, w3b, w2b)
    return _combine(pos3, route, h, ln_g, ln_b, ys)


def _retention_constants():
    hh = np.arange(HEADS, dtype=np.float64)
    log_gamma = np.log1p(-np.exp2(-5.0 - hh))
    pos = np.arange(CHUNK, dtype=np.float64)
    diff = pos[:, None] - pos[None, :]
    causal = diff >= 0
    decay = np.where(causal[None], np.exp(log_gamma[:, None, None] * np.where(causal, diff, 0.0)[None]), 0.0)
    zeta = np.exp(log_gamma[:, None] * (CHUNK - 1 - pos)[None])
    xi = np.exp(log_gamma[:, None] * (pos + 1.0)[None])
    cdec = np.exp(log_gamma * CHUNK)
    bc = lambda t: np.broadcast_to(t[:, :, None], (HEADS, CHUNK, LANES))
    return (jnp.asarray(cdec, F32), jnp.asarray(decay, F32),
            jnp.asarray(bc(zeta), F32), jnp.asarray(bc(xi), F32))


def _rope_tables(seq):
    f32 = np.float32
    inv = np.power(f32(ROPE_BASE), -np.arange(0, HEAD_DIM, 2, dtype=f32) / f32(HEAD_DIM)).astype(f32)
    ang = (np.arange(seq, dtype=f32)[:, None] * inv[None, :]).astype(f32).astype(np.float64)
    cos = np.repeat(np.cos(ang), 2, axis=-1)
    sin = np.sin(ang)
    sin_signed = np.stack([-sin, sin], axis=-1).reshape(seq, HEAD_DIM)
    return jnp.asarray(cos, F32), jnp.asarray(sin_signed, F32)


def kernel(x, w_in, ret_norm_g, ret_norm_b, conv_w, conv_b, conv_norm_g, conv_norm_b, w_out, ln1_g, ln1_b, w_router_group, b_router_group, w_router_expert, b_router_expert, w1, w3, w2, ln2_g, ln2_b):
    batch, seq, _ = x.shape
    T = batch * seq
    x2 = x.reshape(T, D_MODEL)

    w_heads = _regroup_w_in(w_in)

    cos_t, sin_t = _rope_tables(seq)
    head_params = jnp.stack([ret_norm_g, ret_norm_b, conv_b, conv_norm_g, conv_norm_b]
                            + [jnp.zeros_like(conv_b)] * (SUBLANES - 5)).reshape(SUBLANES, HEADS, LANES)
    head_params = jnp.transpose(head_params, (1, 0, 2))
    cw = jnp.transpose(conv_w.reshape(CONV_TAPS, CONV_GROUPS, LANES), (1, 0, 2))
    ret3, conv3, w1b, w3b, w2b, w_out_b = _mixer(
        x2, w_heads, cos_t, sin_t, _retention_constants(), head_params, cw, w1, w3, w2, w_out, batch, seq)

    wr = jnp.concatenate([jnp.transpose(w_router_expert, (1, 0, 2)).reshape(D_MODEL, N_EXPERTS),
                          w_router_group], axis=1)
    wr = jnp.pad(wr, ((0, 0), (0, LANES - wr.shape[1])))
    br = jnp.pad(jnp.concatenate([b_router_expert.reshape(-1), b_router_group]),
                 (0, LANES - N_EXPERTS - N_GROUPS)).reshape(1, LANES)

    h, route, route_t, counts = _out_proj(ret3, conv3, x2, w_out_b, ln1_g.reshape(1, -1),
                                          ln1_b.reshape(1, -1), wr.astype(BF16), br)

    y = _routed_moe(h, route, route_t, counts, w1b.reshape(w1.shape), w3b.reshape(w3.shape),
                    w2b.reshape(w2.shape), ln2_g.reshape(1, -1), ln2_b.reshape(1, -1))
    return y.reshape(batch, seq, D_MODEL)
```

```python
import functools

import numpy as np
import jax
import jax.numpy as jnp
from jax import lax
from jax.experimental import pallas as pl
from jax.experimental.pallas import tpu as pltpu

D_MODEL = 2048
RET_WIDTH = D_MODEL // 2
CONV_WIDTH = D_MODEL - RET_WIDTH
HEADS = 8
HEAD_DIM = RET_WIDTH // HEADS
CONV_GROUPS = 8
CONV_TAPS = 31
CHUNK = 128
ROPE_BASE = 10000.0
IN_COLS = 4 * RET_WIDTH + 2 * CONV_WIDTH
N_GROUPS = 4
EXPERTS_PER_GROUP = 4
N_EXPERTS = N_GROUPS * EXPERTS_PER_GROUP
TOP_K = 2
D_FF = D_MODEL // 2
LN_EPS = 1e-5
ALPHA = 2.0 ** 0.25

LANES = 128
SUBLANES = 8
VMEM_LIMIT = 56 * 1024 * 1024
CONV_HALO = 32

F32 = jnp.float32
BF16 = jnp.bfloat16


def _sigmoid(x):
    return 1.0 / (1.0 + jnp.exp(-x))


def _lane_norm(x, g, b):
    mu = jnp.mean(x, axis=-1, keepdims=True)
    xc = x - mu
    var = jnp.mean(xc * xc, axis=-1, keepdims=True)
    return xc * lax.rsqrt(var + LN_EPS) * g + b


HEADS_PER_STEP = 2
MIX_ROWS = 512
HEAD_COLS = 6 * LANES
OUT_PROJ_SUB = 256


def _regroup_kernel(*refs):
    o_ref = refs[-1]
    for c, w_ref in enumerate(refs[:-1]):
        o_ref[:, c * LANES:(c + 1) * LANES] = w_ref[...].astype(BF16)


def _regroup_w_in(w_in):
    kinds = HEAD_COLS // LANES
    return pl.pallas_call(
        _regroup_kernel,
        grid=(HEADS,),
        in_specs=[pl.BlockSpec((D_MODEL, LANES), functools.partial(lambda c, hd: (0, c * HEADS + hd), c))
                  for c in range(kinds)],
        out_specs=pl.BlockSpec((D_MODEL, HEAD_COLS), lambda hd: (0, hd)),
        out_shape=jax.ShapeDtypeStruct((D_MODEL, HEADS * HEAD_COLS), BF16),
        compiler_params=pltpu.CompilerParams(dimension_semantics=("arbitrary",)),
        name="w_in_regroup",
    )(*([w_in] * kinds))


def _rotary(t, cos, sin_signed, even_lane):
    partner = jnp.where(even_lane, pltpu.roll(t, LANES - 1, 1), pltpu.roll(t, 1, 1))
    return t * cos + partner * sin_signed


def _mixer_kernel(cdec_ref, x_ref, w_ref, cos_ref, sin_ref, decay_ref, zeta_ref, xi_ref, prm_ref, cw_ref,
                  w1f_ref, w3f_ref, w2f_ref, wof_ref,
                  ret_ref, conv_ref, w1b_ref, w3b_ref, w2b_ref, wob_ref, state_ref, uext_ref):
    w1b_ref[...] = w1f_ref[...].astype(BF16)
    w3b_ref[...] = w3f_ref[...].astype(BF16)
    w2b_ref[...] = w2f_ref[...].astype(BF16)
    wob_ref[...] = wof_ref[...].astype(BF16)

    rows = x_ref.shape[0]
    hps = w_ref.shape[1] // HEAD_COLS
    n = pl.program_id(2)
    head0 = pl.program_id(1) * hps

    @pl.when(n == 0)
    def _():
        for j in range(hps):
            state_ref[head0 + j] = jnp.zeros((HEAD_DIM, HEAD_DIM), F32)
            uext_ref[head0 + j, 0:CONV_HALO, :] = jnp.zeros((CONV_HALO, LANES), F32)

    even_lane = (lax.broadcasted_iota(jnp.int32, (CHUNK, LANES), 1) % 2) == 0
    scale = HEAD_DIM ** -0.5
    blk = 64
    first = CONV_HALO - (CONV_TAPS - 1)
    xb = x_ref[...].astype(BF16)

    for j in range(hps):
        head = head0 + j
        cdec = cdec_ref[head]
        decay = decay_ref[j]
        zeta = zeta_ref[j]
        xi = xi_ref[j]
        rg, rb, cb, cg, cbeta = (prm_ref[j, r:r + 1, :] for r in range(5))
        proj = jnp.dot(xb, w_ref[:, j * HEAD_COLS:(j + 1) * HEAD_COLS], preferred_element_type=F32)
        col = lambda c: proj[:, c * LANES:(c + 1) * LANES]

        state = state_ref[head]
        for c in range(rows // CHUNK):
            rs = slice(c * CHUNK, (c + 1) * CHUNK)
            cos = cos_ref[rs, :]
            sin = sin_ref[rs, :]
            q = _rotary(col(0)[rs], cos, sin, even_lane)
            k = _rotary(col(1)[rs], cos, sin, even_lane) * scale
            v = col(2)[rs].astype(BF16)
            scores = lax.dot_general(q.astype(BF16), k.astype(BF16), (((1,), (1,)), ((), ())),
                                     preferred_element_type=F32) * decay
            inner = jnp.dot(scores.astype(BF16), v, preferred_element_type=F32)
            cross = jnp.dot((q * xi).astype(BF16), state.astype(BF16), preferred_element_type=F32)
            kz_t = jnp.transpose(k * zeta).astype(BF16)
            state = cdec * state + jnp.dot(kz_t, v, preferred_element_type=F32)
            y = _lane_norm(inner + cross, rg, rb)
            g = col(3)[rs]
            ret_ref[j, rs, :] = (g * _sigmoid(g) * y).astype(BF16)
        state_ref[head] = state

        uext_ref[head, CONV_HALO:CONV_HALO + rows, :] = col(4) * _sigmoid(col(5))
        for r0 in range(0, rows, blk):
            acc = cb
            for t in range(CONV_TAPS):
                acc = acc + cw_ref[j, t:t + 1, :] * uext_ref[head, r0 + first + t:r0 + first + t + blk, :]
            y = _lane_norm(acc, cg, cbeta)
            conv_ref[j, r0:r0 + blk, :] = (y * _sigmoid(y)).astype(BF16)
        uext_ref[head, 0:CONV_HALO, :] = uext_ref[head, rows:rows + CONV_HALO, :]


def _mixer(x2, w_heads, cos_t, sin_t, consts, head_params, cw, w1, w3, w2, w_out, batch, seq):
    T = batch * seq
    rows, hps = MIX_ROWS, HEADS_PER_STEP
    ns = seq // rows
    ng = HEADS // hps
    cdec, decay, zeta, xi = consts
    steps = batch * ns * ng
    w1s = w1.reshape(-1, w1.shape[-1])
    w3s = w3.reshape(-1, w3.shape[-1])
    w2s = w2.reshape(-1, w2.shape[-1])
    step_of = lambda b, g, n: ((b * ng + g) * ns + n, 0)
    wspec = lambda w: pl.BlockSpec((w.shape[0] // steps, w.shape[1]), step_of)
    for w in (w1s, w3s, w2s, w_out):
        assert w.shape[0] % (steps * 16) == 0

    def hspec(d1, d2=LANES):
        return pl.BlockSpec((hps, d1, d2), lambda b, g, n: (g, 0, 0))

    out_rows = pl.BlockSpec((hps, rows, LANES), lambda b, g, n: (g, b * ns + n, 0))
    return pl.pallas_call(
        _mixer_kernel,
        grid=(batch, ng, ns),
        in_specs=[pl.BlockSpec(memory_space=pltpu.SMEM),
                  pl.BlockSpec((rows, D_MODEL), lambda b, g, n: (b * ns + n, 0)),
                  pl.BlockSpec((D_MODEL, hps * HEAD_COLS), lambda b, g, n: (0, g)),
                  pl.BlockSpec((rows, LANES), lambda b, g, n: (n, 0)),
                  pl.BlockSpec((rows, LANES), lambda b, g, n: (n, 0)),
                  hspec(CHUNK), hspec(CHUNK), hspec(CHUNK),
                  hspec(SUBLANES), hspec(CONV_TAPS),
                  wspec(w1s), wspec(w3s), wspec(w2s), wspec(w_out)],
        out_specs=[out_rows, out_rows,
                   wspec(w1s), wspec(w3s), wspec(w2s), wspec(w_out)],
        out_shape=[jax.ShapeDtypeStruct((HEADS, T, LANES), BF16),
                   jax.ShapeDtypeStruct((CONV_GROUPS, T, LANES), BF16),
                   jax.ShapeDtypeStruct(w1s.shape, BF16),
                   jax.ShapeDtypeStruct(w3s.shape, BF16),
                   jax.ShapeDtypeStruct(w2s.shape, BF16),
                   jax.ShapeDtypeStruct(w_out.shape, BF16)],
        scratch_shapes=[pltpu.VMEM((HEADS, HEAD_DIM, HEAD_DIM), F32),
                        pltpu.VMEM((HEADS, rows + CONV_HALO, LANES), F32)],
        compiler_params=pltpu.CompilerParams(
            dimension_semantics=("arbitrary", "arbitrary", "arbitrary"), vmem_limit_bytes=VMEM_LIMIT),
        name="in_proj_mixer",
    )(cdec, x2, w_heads, cos_t, sin_t, decay, zeta, xi, head_params, cw, w1s, w3s, w2s, w_out)


def _layer_norm(z, g, b):
    mu = jnp.mean(z, axis=-1, keepdims=True)
    zc = z - mu
    var = jnp.mean(zc * zc, axis=-1, keepdims=True)
    return zc * lax.rsqrt(var + LN_EPS) * g + b


def _outproj_kernel(ret_ref, conv_ref, x_ref, w_ref, g_ref, b_ref, wr_ref, br_ref,
                    h_ref, route_ref, route_t_ref, counts_ref, carry_ref):
    @pl.when(pl.program_id(0) == 0)
    def _():
        carry_ref[...] = jnp.zeros_like(carry_ref)

    carry = carry_ref[:, 0:1]
    for r0 in range(0, x_ref.shape[0], OUT_PROJ_SUB):
        carry = _project_and_route(slice(r0, r0 + OUT_PROJ_SUB), carry, ret_ref, conv_ref, x_ref, w_ref,
                                   g_ref, b_ref, wr_ref, br_ref, h_ref, route_ref, route_t_ref)
    carry_ref[...] = jnp.broadcast_to(carry, carry_ref.shape)
    counts_ref[...] = jnp.broadcast_to(carry, counts_ref.shape)


def _project_and_route(rs, carry, ret_ref, conv_ref, x_ref, w_ref, g_ref, b_ref, wr_ref, br_ref,
                       h_ref, route_ref, route_t_ref):
    mix = jnp.concatenate([ret_ref[s, rs, :] for s in range(HEADS)] +
                          [conv_ref[s, rs, :] for s in range(CONV_GROUPS)], axis=-1)
    y = jnp.dot(mix, w_ref[...], preferred_element_type=F32)
    h = _layer_norm(y + ALPHA * x_ref[rs, :], g_ref[...], b_ref[...])
    h_ref[rs, :] = h

    logits = jnp.dot(h.astype(BF16), wr_ref[...], preferred_element_type=F32) + br_ref[...]

    tm = logits.shape[0]
    lt = jnp.transpose(logits)
    neg = jnp.float32(-jnp.inf)
    big = jnp.float32(1e9)
    grow = lax.broadcasted_iota(jnp.int32, (SUBLANES, tm), 0).astype(F32)
    gmask = grow < N_GROUPS
    gl = jnp.where(gmask, lt[N_EXPERTS:N_EXPERTS + SUBLANES, :], neg)
    gmax = jnp.max(gl, axis=0, keepdims=True)
    gidx = jnp.min(jnp.where(gl == gmax, grow, big), axis=0, keepdims=True)
    g_w = 1.0 / jnp.sum(jnp.where(gmask, jnp.exp(gl - gmax), 0.0), axis=0, keepdims=True)
    erow = lax.broadcasted_iota(jnp.int32, (N_EXPERTS, tm), 0).astype(F32)
    lo = gidx * EXPERTS_PER_GROUP
    emask = (erow >= lo) & (erow < lo + EXPERTS_PER_GROUP)
    el = jnp.where(emask, lt[0:N_EXPERTS, :], neg)
    top1 = jnp.max(el, axis=0, keepdims=True)
    idx1 = jnp.min(jnp.where(el == top1, erow, big), axis=0, keepdims=True)
    el2 = jnp.where(erow == idx1, neg, el)
    top2 = jnp.max(el2, axis=0, keepdims=True)
    idx2 = jnp.min(jnp.where(el2 == top2, erow, big), axis=0, keepdims=True)
    e2 = jnp.exp(top2 - top1)
    w1 = g_w / (1.0 + e2)
    w2 = w1 * e2

    sel1 = erow == idx1
    sel2 = erow == idx2
    onehot = jnp.where(sel1, 1.0, 0.0) + jnp.where(sel2, 1.0, 0.0)
    r_i = lax.broadcasted_iota(jnp.int32, (tm, tm), 0)
    c_i = lax.broadcasted_iota(jnp.int32, (tm, tm), 1)
    earlier = jnp.where(r_i < c_i, 1.0, 0.0).astype(BF16)
    prefix = jnp.dot(onehot.astype(BF16), earlier, preferred_element_type=F32) + carry
    rank1 = jnp.sum(jnp.where(sel1, prefix, 0.0), axis=0, keepdims=True)
    rank2 = jnp.sum(jnp.where(sel2, prefix, 0.0), axis=0, keepdims=True)

    route_t = jnp.concatenate([w1, w2, idx1, idx2, rank1, rank2, jnp.zeros((2, tm), F32)], axis=0)
    route_t_ref[:, rs] = route_t
    route_ref[rs, :] = jnp.transpose(
        jnp.concatenate([route_t, jnp.zeros((LANES - SUBLANES, tm), F32)], axis=0))
    return carry + jnp.sum(onehot, axis=1, keepdims=True)


def _out_proj(ret3, conv3, x2, w_out_b, ln_g, ln_b, wr_b, br, tm=512):
    T = x2.shape[0]
    cur = lambda i: (0, i, 0)
    row = lambda i: (i, 0)
    const = lambda i: (0, 0)
    return pl.pallas_call(
        _outproj_kernel,
        grid=(T // tm,),
        in_specs=[pl.BlockSpec((HEADS, tm, LANES), cur),
                  pl.BlockSpec((CONV_GROUPS, tm, LANES), cur),
                  pl.BlockSpec((tm, D_MODEL), row),
                  pl.BlockSpec((D_MODEL, D_MODEL), const),
                  pl.BlockSpec((1, D_MODEL), const),
                  pl.BlockSpec((1, D_MODEL), const),
                  pl.BlockSpec((D_MODEL, LANES), const),
                  pl.BlockSpec((1, LANES), const)],
        out_specs=[pl.BlockSpec((tm, D_MODEL), row),
                   pl.BlockSpec((tm, LANES), row),
                   pl.BlockSpec((SUBLANES, tm), lambda i: (0, i)),
                   pl.BlockSpec((N_EXPERTS, LANES), const)],
        out_shape=[jax.ShapeDtypeStruct((T, D_MODEL), F32),
                   jax.ShapeDtypeStruct((T, LANES), F32),
                   jax.ShapeDtypeStruct((SUBLANES, T), F32),
                   jax.ShapeDtypeStruct((N_EXPERTS, LANES), F32)],
        scratch_shapes=[pltpu.VMEM((N_EXPERTS, LANES), F32)],
        compiler_params=pltpu.CompilerParams(
            dimension_semantics=("arbitrary",), vmem_limit_bytes=VMEM_LIMIT),
        name="out_proj_ln_route",
    )(ret3, conv3, x2, w_out_b, ln_g, ln_b, wr_b, br)


MOE_TM = 512
MOE_TB = 512


def _row_copy(src_ref, src_row, dst_ref, dst_row, sem):
    return pltpu.make_async_copy(src_ref.at[pl.ds(src_row, 1), :], dst_ref.at[pl.ds(dst_row, 1), :], sem)


def _experts_kernel(te_ref, nv_ref, src_ref, src_next_ref, h_ref, w1_ref, w3_ref, w2_ref, y_ref, x_ref, sems):
    i = pl.program_id(0)
    last = pl.num_programs(0) - 1
    slot = i % 2
    used = i < nv_ref[0]

    def gather_wait(s):
        pltpu.make_async_copy(h_ref.at[pl.ds(0, MOE_TM), :], x_ref.at[s, pl.ds(0, MOE_TM), :], sems.at[s]).wait()

    @pl.when(i == 0)
    def _():
        x_ref[:, MOE_TM:, :] = jnp.zeros((2, SUBLANES, D_MODEL), F32)

        def first(t, carry):
            _row_copy(h_ref, src_ref[0, 0, t], x_ref.at[0], t, sems.at[0]).start()
            return carry
        lax.fori_loop(0, MOE_TM, first, 0, unroll=8)

    @pl.when(used)
    def _():
        gather_wait(slot)

        def issue_half(k, width):
            for t in range(k * MOE_TM // 2, (k + 1) * MOE_TM // 2):
                _row_copy(h_ref, src_next_ref[0, 0, t], x_ref.at[1 - slot], t, sems.at[1 - slot]).start()
            spare = pltpu.bitcast(x_ref[1 - slot, MOE_TM:, k * LANES:(k + 1) * LANES], jnp.uint32)
            zero = pltpu.bitcast((spare >> 16) >> 16, F32)
            return jnp.tile(zero, (MOE_TM // SUBLANES, width // LANES))

        xb = x_ref[slot, 0:MOE_TM, :].astype(BF16)
        a1 = jnp.dot(xb, w1_ref[0], preferred_element_type=F32)
        a3 = jnp.dot(xb, w3_ref[0], preferred_element_type=F32) + issue_half(0, D_FF)
        act = (a1 * _sigmoid(a1) * a3).astype(BF16)
        y_ref[...] = jnp.dot(act, w2_ref[0], preferred_element_type=F32) + issue_half(1, D_MODEL)

        @pl.when(i == last)
        def _():
            gather_wait(1 - slot)

    @pl.when(jnp.logical_not(used))
    def _():
        @pl.when(i == nv_ref[0])
        def _():
            gather_wait(slot)
        y_ref[...] = jnp.zeros_like(y_ref)


def _experts(tile_expert, n_valid, src_tok, h, w1b, w3b, w2b):
    n_tiles = src_tok.shape[0]
    row = lambda i, te, nv: (i, 0)
    wsel = lambda i, te, nv: (te[i], 0, 0)
    smem_row = lambda f: pl.BlockSpec((1, 1, MOE_TM), f, memory_space=pltpu.SMEM)
    return pl.pallas_call(
        _experts_kernel,
        grid_spec=pltpu.PrefetchScalarGridSpec(
            num_scalar_prefetch=2,
            grid=(n_tiles,),
            in_specs=[smem_row(lambda i, te, nv: (i, 0, 0)),
                      smem_row(lambda i, te, nv: (jnp.minimum(i + 1, n_tiles - 1), 0, 0)),
                      pl.BlockSpec(memory_space=pl.ANY),
                      pl.BlockSpec((1, D_MODEL, D_FF), wsel),
                      pl.BlockSpec((1, D_MODEL, D_FF), wsel),
                      pl.BlockSpec((1, D_FF, D_MODEL), wsel)],
            out_specs=pl.BlockSpec((MOE_TM, D_MODEL), row),
            scratch_shapes=[pltpu.VMEM((2, MOE_TM + SUBLANES, D_MODEL), F32),
                            pltpu.SemaphoreType.DMA((2,))]),
        out_shape=jax.ShapeDtypeStruct((n_tiles * MOE_TM, D_MODEL), F32),
        compiler_params=pltpu.CompilerParams(
            dimension_semantics=("arbitrary",), vmem_limit_bytes=VMEM_LIMIT),
        name="moe_experts",
    )(tile_expert, n_valid, src_tok, src_tok, h, w1b, w3b, w2b)


def _combine_kernel(pos_ref, pos_next_ref, route_ref, h_ref, g_ref, b_ref, ys_ref, o_ref,
                    ya0_ref, yb0_ref, ya1_ref, yb1_ref, sems):
    tb = h_ref.shape[0]
    i = pl.program_id(0)
    bufs = ((ya0_ref, yb0_ref), (ya1_ref, yb1_ref))

    def gather_wait(s):
        for k in range(TOP_K):
            pltpu.make_async_copy(ys_ref.at[pl.ds(0, tb), :], bufs[s][k], sems.at[s]).wait()

    @pl.when(i == 0)
    def _():
        def first(t, carry):
            for k in range(TOP_K):
                _row_copy(ys_ref, pos_ref[0, 0, k * tb + t], bufs[0][k], t, sems.at[0]).start()
            return carry
        lax.fori_loop(0, tb, first, 0, unroll=8)

    for par in range(2):
        @pl.when(i % 2 == par)
        def _():
            gather_wait(par)
            for t in range(tb):
                for k in range(TOP_K):
                    _row_copy(ys_ref, pos_next_ref[0, 0, k * tb + t], bufs[1 - par][k], t, sems.at[1 - par]).start()
            route = route_ref[...]
            ffn = route[:, 0:1] * bufs[par][0][...] + route[:, 1:2] * bufs[par][1][...]
            o_ref[...] = _layer_norm(ALPHA * h_ref[...] + ffn, g_ref[...], b_ref[...])

            @pl.when(i == pl.num_programs(0) - 1)
            def _():
                gather_wait(1 - par)


def _combine(pos3, route, h, ln_g, ln_b, ys):
    T = h.shape[0]
    tb = MOE_TB
    row = lambda i: (i, 0)
    const = lambda i: (0, 0)
    return pl.pallas_call(
        _combine_kernel,
        grid=(T // tb,),
        in_specs=[pl.BlockSpec((1, 1, 2 * tb), lambda i: (i, 0, 0), memory_space=pltpu.SMEM),
                  pl.BlockSpec((1, 1, 2 * tb), lambda i: (jnp.minimum(i + 1, T // tb - 1), 0, 0),
                               memory_space=pltpu.SMEM),
                  pl.BlockSpec((tb, LANES), row),
                  pl.BlockSpec((tb, D_MODEL), row),
                  pl.BlockSpec((1, D_MODEL), const),
                  pl.BlockSpec((1, D_MODEL), const),
                  pl.BlockSpec(memory_space=pl.ANY)],
        out_specs=pl.BlockSpec((tb, D_MODEL), row),
        out_shape=jax.ShapeDtypeStruct((T, D_MODEL), F32),
        scratch_shapes=[pltpu.VMEM((tb, D_MODEL), F32)] * (2 * TOP_K) + [pltpu.SemaphoreType.DMA((2,))],
        compiler_params=pltpu.CompilerParams(
            dimension_semantics=("arbitrary",), vmem_limit_bytes=VMEM_LIMIT),
        name="moe_combine_ln",
    )(pos3, pos3, route, h, ln_g, ln_b, ys)


def _routed_moe(h, route, route_t, counts, w1b, w3b, w2b, ln_g, ln_b):
    T = h.shape[0]
    n_tiles = (TOP_K * T) // MOE_TM + N_EXPERTS
    i32 = jnp.int32
    cnt = counts[:, 0].astype(i32)
    ntile = (cnt + MOE_TM - 1) // MOE_TM
    tile_end = jnp.cumsum(ntile)
    n_valid = tile_end[-1:]
    seg_start = (tile_end - ntile) * MOE_TM
    info = route_t[2:6].astype(i32)
    eids = jnp.arange(N_EXPERTS, dtype=i32)
    start_of = lambda e: jnp.sum(jnp.where(e[None, :] == eids[:, None], seg_start[:, None], 0), axis=0)
    pos1 = start_of(info[0]) + info[2]
    pos2 = start_of(info[1]) + info[3]
    pos3 = jnp.concatenate([pos1.reshape(T // MOE_TB, 1, MOE_TB), pos2.reshape(T // MOE_TB, 1, MOE_TB)], axis=2)
    tile_ids = jnp.minimum(jnp.arange(n_tiles, dtype=i32), n_valid[0] - 1)
    tile_expert = jnp.sum((tile_ids[:, None] >= tile_end[None, :]).astype(i32), axis=1)

    order = jnp.argsort(jnp.concatenate([pos1, pos2])).astype(i32)
    n_rows = n_tiles * MOE_TM
    padded = jnp.concatenate([jnp.zeros((n_rows,), i32), jnp.where(order >= T, order - T, order),
                              jnp.zeros((2 * n_rows - TOP_K * T,), i32)])
    shift = seg_start - (jnp.cumsum(cnt) - cnt)
    row_expert = jnp.repeat(tile_expert, MOE_TM)
    src_tok = jnp.zeros((n_rows,), i32)
    for e in range(N_EXPERTS):
        view = lax.dynamic_slice(padded, (n_rows - shift[e],), (n_rows,))
        src_tok = jnp.where(row_expert == e, view, src_tok)

    ys = _experts(tile_expert, n_valid.astype(i32), src_tok.reshape(n_tiles, 1, MOE_TM), h, w1b, w3b, w2b)
    return _combine(pos3, route, h, ln_g, ln_b, ys)


def _retention_constants():
    hh = np.arange(HEADS, dtype=np.float64)
    log_gamma = np.log1p(-np.exp2(-5.0 - hh))
    pos = np.arange(CHUNK, dtype=np.float64)
    diff = pos[:, None] - pos[None, :]
    causal = diff >= 0
    decay = np.where(causal[None], np.exp(log_gamma[:, None, None] * np.where(causal, diff, 0.0)[None]), 0.0)
    zeta = np.exp(log_gamma[:, None] * (CHUNK - 1 - pos)[None])
    xi = np.exp(log_gamma[:, None] * (pos + 1.0)[None])
    cdec = np.exp(log_gamma * CHUNK)
    bc = lambda t: np.broadcast_to(t[:, :, None], (HEADS, CHUNK, LANES))
    return (jnp.asarray(cdec, F32), jnp.asarray(decay, F32),
            jnp.asarray(bc(zeta), F32), jnp.asarray(bc(xi), F32))


def _rope_tables(seq):
    f32 = np.float32
    inv = np.power(f32(ROPE_BASE), -np.arange(0, HEAD_DIM, 2, dtype=f32) / f32(HEAD_DIM)).astype(f32)
    ang = (np.arange(seq, dtype=f32)[:, None] * inv[None, :]).astype(f32).astype(np.float64)
    cos = np.repeat(np.cos(ang), 2, axis=-1)
    sin = np.sin(ang)
    sin_signed = np.stack([-sin, sin], axis=-1).reshape(seq, HEAD_DIM)
    return jnp.asarray(cos, F32), jnp.asarray(sin_signed, F32)


def kernel(x, w_in, ret_norm_g, ret_norm_b, conv_w, conv_b, conv_norm_g, conv_norm_b, w_out, ln1_g, ln1_b, w_router_group, b_router_group, w_router_expert, b_router_expert, w1, w3, w2, ln2_g, ln2_b):
    batch, seq, _ = x.shape
    T = batch * seq
    x2 = x.reshape(T, D_MODEL)

    w_heads = _regroup_w_in(w_in)

    cos_t, sin_t = _rope_tables(seq)
    head_params = jnp.stack([ret_norm_g, ret_norm_b, conv_b, conv_norm_g, conv_norm_b]
                            + [jnp.zeros_like(conv_b)] * (SUBLANES - 5)).reshape(SUBLANES, HEADS, LANES)
    head_params = jnp.transpose(head_params, (1, 0, 2))
    cw = jnp.transpose(conv_w.reshape(CONV_TAPS, CONV_GROUPS, LANES), (1, 0, 2))
    ret3, conv3, w1b, w3b, w2b, w_out_b = _mixer(
        x2, w_heads, cos_t, sin_t, _retention_constants(), head_params, cw, w1, w3, w2, w_out, batch, seq)

    wr = jnp.concatenate([jnp.transpose(w_router_expert, (1, 0, 2)).reshape(D_MODEL, N_EXPERTS),
                          w_router_group], axis=1)
    wr = jnp.pad(wr, ((0, 0), (0, LANES - wr.shape[1])))
    br = jnp.pad(jnp.concatenate([b_router_expert.reshape(-1), b_router_group]),
                 (0, LANES - N_EXPERTS - N_GROUPS)).reshape(1, LANES)

    h, route, route_t, counts = _out_proj(ret3, conv3, x2, w_out_b, ln1_g.reshape(1, -1),
                                          ln1_b.reshape(1, -1), wr.astype(BF16), br)

    y = _routed_moe(h, route, route_t, counts, w1b.reshape(w1.shape), w3b.reshape(w3.shape),
                    w2b.reshape(w2.shape), ln2_g.reshape(1, -1), ln2_b.reshape(1, -1))
    return y.reshape(batch, seq, D_MODEL)
```

```python
import functools

import numpy as np
import jax
import jax.numpy as jnp
from jax import lax
from jax.experimental import pallas as pl
from jax.experimental.pallas import tpu as pltpu

D_MODEL = 2048
RET_WIDTH = D_MODEL // 2
CONV_WIDTH = D_MODEL - RET_WIDTH
HEADS = 8
HEAD_DIM = RET_WIDTH // HEADS
CONV_GROUPS = 8
CONV_TAPS = 31
CHUNK = 128
ROPE_BASE = 10000.0
IN_COLS = 4 * RET_WIDTH + 2 * CONV_WIDTH
N_GROUPS = 4
EXPERTS_PER_GROUP = 4
N_EXPERTS = N_GROUPS * EXPERTS_PER_GROUP
TOP_K = 2
D_FF = D_MODEL // 2
LN_EPS = 1e-5
ALPHA = 2.0 ** 0.25

LANES = 128
SUBLANES = 8
VMEM_LIMIT = 56 * 1024 * 1024
CONV_HALO = 32

F32 = jnp.float32
BF16 = jnp.bfloat16


def _sigmoid(x):
    return 1.0 / (1.0 + jnp.exp(-x))


def _lane_norm(x, g, b):
    mu = jnp.mean(x, axis=-1, keepdims=True)
    xc = x - mu
    var = jnp.mean(xc * xc, axis=-1, keepdims=True)
    return xc * lax.rsqrt(var + LN_EPS) * g + b


HEADS_PER_STEP = 2
MIX_ROWS = 512
HEAD_COLS = 6 * LANES
OUT_PROJ_SUB = 256


def _regroup_kernel(*refs):
    o_ref = refs[-1]
    for c, w_ref in enumerate(refs[:-1]):
        o_ref[:, c * LANES:(c + 1) * LANES] = w_ref[...].astype(BF16)


def _regroup_w_in(w_in):
    kinds = HEAD_COLS // LANES
    return pl.pallas_call(
        _regroup_kernel,
        grid=(HEADS,),
        in_specs=[pl.BlockSpec((D_MODEL, LANES), functools.partial(lambda c, hd: (0, c * HEADS + hd), c))
                  for c in range(kinds)],
        out_specs=pl.BlockSpec((D_MODEL, HEAD_COLS), lambda hd: (0, hd)),
        out_shape=jax.ShapeDtypeStruct((D_MODEL, HEADS * HEAD_COLS), BF16),
        compiler_params=pltpu.CompilerParams(dimension_semantics=("arbitrary",)),
        name="w_in_regroup",
    )(*([w_in] * kinds))


def _rotary(t, cos, sin_signed, even_lane):
    partner = jnp.where(even_lane, pltpu.roll(t, LANES - 1, 1), pltpu.roll(t, 1, 1))
    return t * cos + partner * sin_signed


def _mixer_kernel(cdec_ref, x_ref, w_ref, cos_ref, sin_ref, decay_ref, zeta_ref, xi_ref, prm_ref, cw_ref,
                  w1f_ref, w3f_ref, w2f_ref, wof_ref,
                  ret_ref, conv_ref, w1b_ref, w3b_ref, w2b_ref, wob_ref, state_ref, uext_ref):
    w1b_ref[...] = w1f_ref[...].astype(BF16)
    w3b_ref[...] = w3f_ref[...].astype(BF16)
    w2b_ref[...] = w2f_ref[...].astype(BF16)
    wob_ref[...] = wof_ref[...].astype(BF16)

    rows = x_ref.shape[0]
    hps = w_ref.shape[1] // HEAD_COLS
    n = pl.program_id(2)
    head0 = pl.program_id(1) * hps

    @pl.when(n == 0)
    def _():
        for j in range(hps):
            state_ref[head0 + j] = jnp.zeros((HEAD_DIM, HEAD_DIM), F32)
            uext_ref[head0 + j, 0:CONV_HALO, :] = jnp.zeros((CONV_HALO, LANES), F32)

    even_lane = (lax.broadcasted_iota(jnp.int32, (CHUNK, LANES), 1) % 2) == 0
    scale = HEAD_DIM ** -0.5
    blk = 64
    first = CONV_HALO - (CONV_TAPS - 1)
    xb = x_ref[...].astype(BF16)

    for j in range(hps):
        head = head0 + j
        cdec = cdec_ref[head]
        decay = decay_ref[j]
        zeta = zeta_ref[j]
        xi = xi_ref[j]
        rg, rb, cb, cg, cbeta = (prm_ref[j, r:r + 1, :] for r in range(5))
        proj = jnp.dot(xb, w_ref[:, j * HEAD_COLS:(j + 1) * HEAD_COLS], preferred_element_type=F32)
        col = lambda c: proj[:, c * LANES:(c + 1) * LANES]

        state = state_ref[head]
        for c in range(rows // CHUNK):
            rs = slice(c * CHUNK, (c + 1) * CHUNK)
            cos = cos_ref[rs, :]
            sin = sin_ref[rs, :]
            q = _rotary(col(0)[rs], cos, sin, even_lane)
            k = _rotary(col(1)[rs], cos, sin, even_lane) * scale
            v = col(2)[rs].astype(BF16)
            scores = lax.dot_general(q.astype(BF16), k.astype(BF16), (((1,), (1,)), ((), ())),
                                     preferred_element_type=F32) * decay
            inner = jnp.dot(scores.astype(BF16), v, preferred_element_type=F32)
            cross = jnp.dot((q * xi).astype(BF16), state.astype(BF16), preferred_element_type=F32)
            kz_t = jnp.transpose(k * zeta).astype(BF16)
            state = cdec * state + jnp.dot(kz_t, v, preferred_element_type=F32)
            y = _lane_norm(inner + cross, rg, rb)
            g = col(3)[rs]
            ret_ref[j, rs, :] = (g * _sigmoid(g) * y).astype(BF16)
        state_ref[head] = state

        uext_ref[head, CONV_HALO:CONV_HALO + rows, :] = col(4) * _sigmoid(col(5))
        for r0 in range(0, rows, blk):
            acc = cb
            for t in range(CONV_TAPS):
                acc = acc + cw_ref[j, t:t + 1, :] * uext_ref[head, r0 + first + t:r0 + first + t + blk, :]
            y = _lane_norm(acc, cg, cbeta)
            conv_ref[j, r0:r0 + blk, :] = (y * _sigmoid(y)).astype(BF16)
        uext_ref[head, 0:CONV_HALO, :] = uext_ref[head, rows:rows + CONV_HALO, :]


def _mixer(x2, w_heads, cos_t, sin_t, consts, head_params, cw, w1, w3, w2, w_out, batch, seq):
    T = batch * seq
    rows, hps = MIX_ROWS, HEADS_PER_STEP
    ns = seq // rows
    ng = HEADS // hps
    cdec, decay, zeta, xi = consts
    steps = batch * ns * ng
    w1s = w1.reshape(-1, w1.shape[-1])
    w3s = w3.reshape(-1, w3.shape[-1])
    w2s = w2.reshape(-1, w2.shape[-1])
    step_of = lambda b, g, n: ((b * ng + g) * ns + n, 0)
    wspec = lambda w: pl.BlockSpec((w.shape[0] // steps, w.shape[1]), step_of)
    for w in (w1s, w3s, w2s, w_out):
        assert w.shape[0] % (steps * 16) == 0

    def hspec(d1, d2=LANES):
        return pl.BlockSpec((hps, d1, d2), lambda b, g, n: (g, 0, 0))

    out_rows = pl.BlockSpec((hps, rows, LANES), lambda b, g, n: (g, b * ns + n, 0))
    return pl.pallas_call(
        _mixer_kernel,
        grid=(batch, ng, ns),
        in_specs=[pl.BlockSpec(memory_space=pltpu.SMEM),
                  pl.BlockSpec((rows, D_MODEL), lambda b, g, n: (b * ns + n, 0)),
                  pl.BlockSpec((D_MODEL, hps * HEAD_COLS), lambda b, g, n: (0, g)),
                  pl.BlockSpec((rows, LANES), lambda b, g, n: (n, 0)),
                  pl.BlockSpec((rows, LANES), lambda b, g, n: (n, 0)),
                  hspec(CHUNK), hspec(CHUNK), hspec(CHUNK),
                  hspec(SUBLANES), hspec(CONV_TAPS),
                  wspec(w1s), wspec(w3s), wspec(w2s), wspec(w_out)],
        out_specs=[out_rows, out_rows,
                   wspec(w1s), wspec(w3s), wspec(w2s), wspec(w_out)],
        out_shape=[jax.ShapeDtypeStruct((HEADS, T, LANES), BF16),
                   jax.ShapeDtypeStruct((CONV_GROUPS, T, LANES), BF16),
                   jax.ShapeDtypeStruct(w1s.shape, BF16),
                   jax.ShapeDtypeStruct(w3s.shape, BF16),
                   jax.ShapeDtypeStruct(w2s.shape, BF16),
                   jax.ShapeDtypeStruct(w_out.shape, BF16)],
        scratch_shapes=[pltpu.VMEM((HEADS, HEAD_DIM, HEAD_DIM), F32),
                        pltpu.VMEM((HEADS, rows + CONV_HALO, LANES), F32)],
        compiler_params=pltpu.CompilerParams(
            dimension_semantics=("arbitrary", "arbitrary", "arbitrary"), vmem_limit_bytes=VMEM_LIMIT),
        name="in_proj_mixer",
    )(cdec, x2, w_heads, cos_t, sin_t, decay, zeta, xi, head_params, cw, w1s, w3s, w2s, w_out)


def _layer_norm(z, g, b):
    mu = jnp.mean(z, axis=-1, keepdims=True)
    zc = z - mu
    var = jnp.mean(zc * zc, axis=-1, keepdims=True)
    return zc * lax.rsqrt(var + LN_EPS) * g + b


def _outproj_kernel(ret_ref, conv_ref, x_ref, w_ref, g_ref, b_ref, wr_ref, br_ref,
                    h_ref, route_ref, route_t_ref, counts_ref, carry_ref):
    @pl.when(pl.program_id(0) == 0)
    def _():
        carry_ref[...] = jnp.zeros_like(carry_ref)

    def exact_zero(v):
        return pltpu.bitcast((pltpu.bitcast(v, jnp.uint32) >> 16) >> 16, F32)

    def mix_of(rs):
        return jnp.concatenate([ret_ref[s, rs, :] for s in range(HEADS)] +
                               [conv_ref[s, rs, :] for s in range(CONV_GROUPS)], axis=-1)

    half = D_MODEL // 2
    rs_a, rs_b = (slice(r0, r0 + OUT_PROJ_SUB) for r0 in range(0, x_ref.shape[0], OUT_PROJ_SUB))
    y_a = jnp.dot(mix_of(rs_a), w_ref[...], preferred_element_type=F32)
    mix_b = mix_of(rs_b)
    y_b1 = jnp.dot(mix_b, w_ref[:, 0:half], preferred_element_type=F32)
    h_a = _layer_norm(y_a + ALPHA * x_ref[rs_a, :], g_ref[...], b_ref[...])
    h_ref[rs_a, :] = h_a
    y_b2 = (jnp.dot(mix_b, w_ref[:, half:], preferred_element_type=F32)
            + jnp.tile(exact_zero(h_a[0:SUBLANES, 0:LANES]), (OUT_PROJ_SUB // SUBLANES, half // LANES)))
    carry = carry_ref[:, 0:1]
    carry = _route(rs_a, h_a, exact_zero(y_b1[0:SUBLANES, half - LANES:half]), carry, wr_ref, br_ref,
                   route_ref, route_t_ref)
    h_b = _layer_norm(jnp.concatenate([y_b1, y_b2], axis=-1) + ALPHA * x_ref[rs_b, :], g_ref[...], b_ref[...])
    h_ref[rs_b, :] = h_b
    carry = _route(rs_b, h_b, None, carry, wr_ref, br_ref, route_ref, route_t_ref)
    carry_ref[...] = jnp.broadcast_to(carry, carry_ref.shape)
    counts_ref[...] = jnp.broadcast_to(carry, counts_ref.shape)


def _route(rs, h, zero, carry, wr_ref, br_ref, route_ref, route_t_ref):
    logits = jnp.dot(h.astype(BF16), wr_ref[...], preferred_element_type=F32) + br_ref[...]

    tm = logits.shape[0]
    lt = jnp.transpose(logits)
    neg = jnp.float32(-jnp.inf)
    big = jnp.float32(1e9)
    grow = lax.broadcasted_iota(jnp.int32, (SUBLANES, tm), 0).astype(F32)
    gmask = grow < N_GROUPS
    gl = jnp.where(gmask, lt[N_EXPERTS:N_EXPERTS + SUBLANES, :], neg)
    gmax = jnp.max(gl, axis=0, keepdims=True)
    gidx = jnp.min(jnp.where(gl == gmax, grow, big), axis=0, keepdims=True)
    g_w = 1.0 / jnp.sum(jnp.where(gmask, jnp.exp(gl - gmax), 0.0), axis=0, keepdims=True)
    erow = lax.broadcasted_iota(jnp.int32, (N_EXPERTS, tm), 0).astype(F32)
    lo = gidx * EXPERTS_PER_GROUP
    emask = (erow >= lo) & (erow < lo + EXPERTS_PER_GROUP)
    el = jnp.where(emask, lt[0:N_EXPERTS, :], neg)
    top1 = jnp.max(el, axis=0, keepdims=True)
    idx1 = jnp.min(jnp.where(el == top1, erow, big), axis=0, keepdims=True)
    el2 = jnp.where(erow == idx1, neg, el)
    top2 = jnp.max(el2, axis=0, keepdims=True)
    idx2 = jnp.min(jnp.where(el2 == top2, erow, big), axis=0, keepdims=True)
    e2 = jnp.exp(top2 - top1)
    w1 = g_w / (1.0 + e2)
    w2 = w1 * e2

    sel1 = erow == idx1
    sel2 = erow == idx2
    onehot = jnp.where(sel1, 1.0, 0.0) + jnp.where(sel2, 1.0, 0.0)
    r_i = lax.broadcasted_iota(jnp.int32, (tm, tm), 0)
    c_i = lax.broadcasted_iota(jnp.int32, (tm, tm), 1)
    earlier = jnp.where(r_i < c_i, 1.0, 0.0).astype(BF16)
    prefix = jnp.dot(onehot.astype(BF16), earlier, preferred_element_type=F32) + carry
    rank1 = jnp.sum(jnp.where(sel1, prefix, 0.0), axis=0, keepdims=True)
    rank2 = jnp.sum(jnp.where(sel2, prefix, 0.0), axis=0, keepdims=True)

    route_t = jnp.concatenate([w1, w2, idx1, idx2, rank1, rank2, jnp.zeros((2, tm), F32)], axis=0)
    if zero is not None:
        route_t = route_t + jnp.tile(zero, (1, tm // LANES))
    route_t_ref[:, rs] = route_t
    route_ref[rs, :] = jnp.transpose(
        jnp.concatenate([route_t, jnp.zeros((LANES - SUBLANES, tm), F32)], axis=0))
    return carry + jnp.sum(onehot, axis=1, keepdims=True)


def _out_proj(ret3, conv3, x2, w_out_b, ln_g, ln_b, wr_b, br, tm=512):
    T = x2.shape[0]
    cur = lambda i: (0, i, 0)
    row = lambda i: (i, 0)
    const = lambda i: (0, 0)
    return pl.pallas_call(
        _outproj_kernel,
        grid=(T // tm,),
        in_specs=[pl.BlockSpec((HEADS, tm, LANES), cur),
                  pl.BlockSpec((CONV_GROUPS, tm, LANES), cur),
                  pl.BlockSpec((tm, D_MODEL), row),
                  pl.BlockSpec((D_MODEL, D_MODEL), const),
                  pl.BlockSpec((1, D_MODEL), const),
                  pl.BlockSpec((1, D_MODEL), const),
                  pl.BlockSpec((D_MODEL, LANES), const),
                  pl.BlockSpec((1, LANES), const)],
        out_specs=[pl.BlockSpec((tm, D_MODEL), row),
                   pl.BlockSpec((tm, LANES), row),
                   pl.BlockSpec((SUBLANES, tm), lambda i: (0, i)),
                   pl.BlockSpec((N_EXPERTS, LANES), const)],
        out_shape=[jax.ShapeDtypeStruct((T, D_MODEL), F32),
                   jax.ShapeDtypeStruct((T, LANES), F32),
                   jax.ShapeDtypeStruct((SUBLANES, T), F32),
                   jax.ShapeDtypeStruct((N_EXPERTS, LANES), F32)],
        scratch_shapes=[pltpu.VMEM((N_EXPERTS, LANES), F32)],
        compiler_params=pltpu.CompilerParams(
            dimension_semantics=("arbitrary",), vmem_limit_bytes=VMEM_LIMIT),
        name="out_proj_ln_route",
    )(ret3, conv3, x2, w_out_b, ln_g, ln_b, wr_b, br)


MOE_TM = 512
MOE_TB = 512


def _row_copy(src_ref, src_row, dst_ref, dst_row, sem):
    return pltpu.make_async_copy(src_ref.at[pl.ds(src_row, 1), :], dst_ref.at[pl.ds(dst_row, 1), :], sem)


def _experts_kernel(te_ref, nv_ref, src_ref, src_next_ref, h_ref, w1_ref, w3_ref, w2_ref, y_ref, x_ref, sems):
    i = pl.program_id(0)
    last = pl.num_programs(0) - 1
    slot = i % 2
    used = i < nv_ref[0]

    def gather_wait(s):
        pltpu.make_async_copy(h_ref.at[pl.ds(0, MOE_TM), :], x_ref.at[s, pl.ds(0, MOE_TM), :], sems.at[s]).wait()

    @pl.when(i == 0)
    def _():
        x_ref[:, MOE_TM:, :] = jnp.zeros((2, SUBLANES, D_MODEL), F32)

        def first(t, carry):
            _row_copy(h_ref, src_ref[0, 0, t], x_ref.at[0], t, sems.at[0]).start()
            return carry
        lax.fori_loop(0, MOE_TM, first, 0, unroll=8)

    @pl.when(used)
    def _():
        gather_wait(slot)

        def issue_half(k, width):
            for t in range(k * MOE_TM // 2, (k + 1) * MOE_TM // 2):
                _row_copy(h_ref, src_next_ref[0, 0, t], x_ref.at[1 - slot], t, sems.at[1 - slot]).start()
            spare = pltpu.bitcast(x_ref[1 - slot, MOE_TM:, k * LANES:(k + 1) * LANES], jnp.uint32)
            zero = pltpu.bitcast((spare >> 16) >> 16, F32)
            return jnp.tile(zero, (MOE_TM // SUBLANES, width // LANES))

        xb = x_ref[slot, 0:MOE_TM, :].astype(BF16)
        a1 = jnp.dot(xb, w1_ref[0], preferred_element_type=F32)
        a3 = jnp.dot(xb, w3_ref[0], preferred_element_type=F32) + issue_half(0, D_FF)
        act = (a1 * _sigmoid(a1) * a3).astype(BF16)
        y_ref[...] = jnp.dot(act, w2_ref[0], preferred_element_type=F32) + issue_half(1, D_MODEL)

        @pl.when(i == last)
        def _():
            gather_wait(1 - slot)

    @pl.when(jnp.logical_not(used))
    def _():
        @pl.when(i == nv_ref[0])
        def _():
            gather_wait(slot)
        y_ref[...] = jnp.zeros_like(y_ref)


def _experts(tile_expert, n_valid, src_tok, h, w1b, w3b, w2b):
    n_tiles = src_tok.shape[0]
    row = lambda i, te, nv: (i, 0)
    wsel = lambda i, te, nv: (te[i], 0, 0)
    smem_row = lambda f: pl.BlockSpec((1, 1, MOE_TM), f, memory_space=pltpu.SMEM)
    return pl.pallas_call(
        _experts_kernel,
        grid_spec=pltpu.PrefetchScalarGridSpec(
            num_scalar_prefetch=2,
            grid=(n_tiles,),
            in_specs=[smem_row(lambda i, te, nv: (i, 0, 0)),
                      smem_row(lambda i, te, nv: (jnp.minimum(i + 1, n_tiles - 1), 0, 0)),
                      pl.BlockSpec(memory_space=pl.ANY),
                      pl.BlockSpec((1, D_MODEL, D_FF), wsel),
                      pl.BlockSpec((1, D_MODEL, D_FF), wsel),
                      pl.BlockSpec((1, D_FF, D_MODEL), wsel)],
            out_specs=pl.BlockSpec((MOE_TM, D_MODEL), row),
            scratch_shapes=[pltpu.VMEM((2, MOE_TM + SUBLANES, D_MODEL), F32),
                            pltpu.SemaphoreType.DMA((2,))]),
        out_shape=jax.ShapeDtypeStruct((n_tiles * MOE_TM, D_MODEL), F32),
        compiler_params=pltpu.CompilerParams(
            dimension_semantics=("arbitrary",), vmem_limit_bytes=VMEM_LIMIT),
        name="moe_experts",
    )(tile_expert, n_valid, src_tok, src_tok, h, w1b, w3b, w2b)


def _combine_kernel(pos_ref, pos_next_ref, route_ref, h_ref, g_ref, b_ref, ys_ref, o_ref,
                    ya0_ref, yb0_ref, ya1_ref, yb1_ref, sems):
    tb = h_ref.shape[0]
    i = pl.program_id(0)
    bufs = ((ya0_ref, yb0_ref), (ya1_ref, yb1_ref))

    def gather_wait(s):
        for k in range(TOP_K):
            pltpu.make_async_copy(ys_ref.at[pl.ds(0, tb), :], bufs[s][k], sems.at[s]).wait()

    @pl.when(i == 0)
    def _():
        def first(t, carry):
            for k in range(TOP_K):
                _row_copy(ys_ref, pos_ref[0, 0, k * tb + t], bufs[0][k], t, sems.at[0]).start()
            return carry
        lax.fori_loop(0, tb, first, 0, unroll=8)

    for par in range(2):
        @pl.when(i % 2 == par)
        def _():
            gather_wait(par)
            for t in range(tb):
                for k in range(TOP_K):
                    _row_copy(ys_ref, pos_next_ref[0, 0, k * tb + t], bufs[1 - par][k], t, sems.at[1 - par]).start()
            route = route_ref[...]
            ffn = route[:, 0:1] * bufs[par][0][...] + route[:, 1:2] * bufs[par][1][...]
            o_ref[...] = _layer_norm(ALPHA * h_ref[...] + ffn, g_ref[...], b_ref[...])

            @pl.when(i == pl.num_programs(0) - 1)
            def _():
                gather_wait(1 - par)


def _combine(pos3, route, h, ln_g, ln_b, ys):
    T = h.shape[0]
    tb = MOE_TB
    row = lambda i: (i, 0)
    const = lambda i: (0, 0)
    return pl.pallas_call(
        _combine_kernel,
        grid=(T // tb,),
        in_specs=[pl.BlockSpec((1, 1, 2 * tb), lambda i: (i, 0, 0), memory_space=pltpu.SMEM),
                  pl.BlockSpec((1, 1, 2 * tb), lambda i: (jnp.minimum(i + 1, T // tb - 1), 0, 0),
                               memory_space=pltpu.SMEM),
                  pl.BlockSpec((tb, LANES), row),
                  pl.BlockSpec((tb, D_MODEL), row),
                  pl.BlockSpec((1, D_MODEL), const),
                  pl.BlockSpec((1, D_MODEL), const),
                  pl.BlockSpec(memory_space=pl.ANY)],
        out_specs=pl.BlockSpec((tb, D_MODEL), row),
        out_shape=jax.ShapeDtypeStruct((T, D_MODEL), F32),
        scratch_shapes=[pltpu.VMEM((tb, D_MODEL), F32)] * (2 * TOP_K) + [pltpu.SemaphoreType.DMA((2,))],
        compiler_params=pltpu.CompilerParams(
            dimension_semantics=("arbitrary",), vmem_limit_bytes=VMEM_LIMIT),
        name="moe_combine_ln",
    )(pos3, pos3, route, h, ln_g, ln_b, ys)


def _routed_moe(h, route, route_t, counts, w1b, w3b, w2b, ln_g, ln_b):
    T = h.shape[0]
    n_tiles = (TOP_K * T) // MOE_TM + N_EXPERTS
    i32 = jnp.int32
    cnt = counts[:, 0].astype(i32)
    ntile = (cnt + MOE_TM - 1) // MOE_TM
    tile_end = jnp.cumsum(ntile)
    n_valid = tile_end[-1:]
    seg_start = (tile_end - ntile) * MOE_TM
    info = route_t[2:6].astype(i32)
    eids = jnp.arange(N_EXPERTS, dtype=i32)
    start_of = lambda e: jnp.sum(jnp.where(e[None, :] == eids[:, None], seg_start[:, None], 0), axis=0)
    pos1 = start_of(info[0]) + info[2]
    pos2 = start_of(info[1]) + info[3]
    pos3 = jnp.concatenate([pos1.reshape(T // MOE_TB, 1, MOE_TB), pos2.reshape(T // MOE_TB, 1, MOE_TB)], axis=2)
    tile_ids = jnp.minimum(jnp.arange(n_tiles, dtype=i32), n_valid[0] - 1)
    tile_expert = jnp.sum((tile_ids[:, None] >= tile_end[None, :]).astype(i32), axis=1)

    order = jnp.argsort(jnp.concatenate([pos1, pos2])).astype(i32)
    n_rows = n_tiles * MOE_TM
    padded = jnp.concatenate([jnp.zeros((n_rows,), i32), jnp.where(order >= T, order - T, order),
                              jnp.zeros((2 * n_rows - TOP_K * T,), i32)])
    shift = seg_start - (jnp.cumsum(cnt) - cnt)
    row_expert = jnp.repeat(tile_expert, MOE_TM)
    src_tok = jnp.zeros((n_rows,), i32)
    for e in range(N_EXPERTS):
        view = lax.dynamic_slice(padded, (n_rows - shift[e],), (n_rows,))
        src_tok = jnp.where(row_expert == e, view, src_tok)

    ys = _experts(tile_expert, n_valid.astype(i32), src_tok.reshape(n_tiles, 1, MOE_TM), h, w1b, w3b, w2b)
    return _combine(pos3, route, h, ln_g, ln_b, ys)


def _retention_constants():
    hh = np.arange(HEADS, dtype=np.float64)
    log_gamma = np.log1p(-np.exp2(-5.0 - hh))
    pos = np.arange(CHUNK, dtype=np.float64)
    diff = pos[:, None] - pos[None, :]
    causal = diff >= 0
    decay = np.where(causal[None], np.exp(log_gamma[:, None, None] * np.where(causal, diff, 0.0)[None]), 0.0)
    zeta = np.exp(log_gamma[:, None] * (CHUNK - 1 - pos)[None])
    xi = np.exp(log_gamma[:, None] * (pos + 1.0)[None])
    cdec = np.exp(log_gamma * CHUNK)
    bc = lambda t: np.broadcast_to(t[:, :, None], (HEADS, CHUNK, LANES))
    return (jnp.asarray(cdec, F32), jnp.asarray(decay, F32),
            jnp.asarray(bc(zeta), F32), jnp.asarray(bc(xi), F32))


def _rope_tables(seq):
    f32 = np.float32
    inv = np.power(f32(ROPE_BASE), -np.arange(0, HEAD_DIM, 2, dtype=f32) / f32(HEAD_DIM)).astype(f32)
    ang = (np.arange(seq, dtype=f32)[:, None] * inv[None, :]).astype(f32).astype(np.float64)
    cos = np.repeat(np.cos(ang), 2, axis=-1)
    sin = np.sin(ang)
    sin_signed = np.stack([-sin, sin], axis=-1).reshape(seq, HEAD_DIM)
    return jnp.asarray(cos, F32), jnp.asarray(sin_signed, F32)


def kernel(x, w_in, ret_norm_g, ret_norm_b, conv_w, conv_b, conv_norm_g, conv_norm_b, w_out, ln1_g, ln1_b, w_router_group, b_router_group, w_router_expert, b_router_expert, w1, w3, w2, ln2_g, ln2_b):
    batch, seq, _ = x.shape
    T = batch * seq
    x2 = x.reshape(T, D_MODEL)

    w_heads = _regroup_w_in(w_in)

    cos_t, sin_t = _rope_tables(seq)
    head_params = jnp.stack([ret_norm_g, ret_norm_b, conv_b, conv_norm_g, conv_norm_b]
                            + [jnp.zeros_like(conv_b)] * (SUBLANES - 5)).reshape(SUBLANES, HEADS, LANES)
    head_params = jnp.transpose(head_params, (1, 0, 2))
    cw = jnp.transpose(conv_w.reshape(CONV_TAPS, CONV_GROUPS, LANES), (1, 0, 2))
    ret3, conv3, w1b, w3b, w2b, w_out_b = _mixer(
        x2, w_heads, cos_t, sin_t, _retention_constants(), head_params, cw, w1, w3, w2, w_out, batch, seq)

    wr = jnp.concatenate([jnp.transpose(w_router_expert, (1, 0, 2)).reshape(D_MODEL, N_EXPERTS),
                          w_router_group], axis=1)
    wr = jnp.pad(wr, ((0, 0), (0, LANES - wr.shape[1])))
    br = jnp.pad(jnp.concatenate([b_router_expert.reshape(-1), b_router_group]),
                 (0, LANES - N_EXPERTS - N_GROUPS)).reshape(1, LANES)

    h, route, route_t, counts = _out_proj(ret3, conv3, x2, w_out_b, ln1_g.reshape(1, -1),
                                          ln1_b.reshape(1, -1), wr.astype(BF16), br)

    y = _routed_moe(h, route, route_t, counts, w1b.reshape(w1.shape), w3b.reshape(w3.shape),
                    w2b.reshape(w2.shape), ln2_g.reshape(1, -1), ln2_b.reshape(1, -1))
    return y.reshape(batch, seq, D_MODEL)
```

```python
import functools

import numpy as np
import jax
import jax.numpy as jnp
from jax import lax
from jax.experimental import pallas as pl
from jax.experimental.pallas import tpu as pltpu

D_MODEL = 2048
RET_WIDTH = D_MODEL // 2
CONV_WIDTH = D_MODEL - RET_WIDTH
HEADS = 8
HEAD_DIM = RET_WIDTH // HEADS
CONV_GROUPS = 8
CONV_TAPS = 31
CHUNK = 128
ROPE_BASE = 10000.0
IN_COLS = 4 * RET_WIDTH + 2 * CONV_WIDTH
N_GROUPS = 4
EXPERTS_PER_GROUP = 4
N_EXPERTS = N_GROUPS * EXPERTS_PER_GROUP
TOP_K = 2
D_FF = D_MODEL // 2
LN_EPS = 1e-5
ALPHA = 2.0 ** 0.25

LANES = 128
SUBLANES = 8
VMEM_LIMIT = 56 * 1024 * 1024
CONV_HALO = 32

F32 = jnp.float32
BF16 = jnp.bfloat16


def _sigmoid(x):
    return 1.0 / (1.0 + jnp.exp(-x))


def _lane_norm(x, g, b):
    mu = jnp.mean(x, axis=-1, keepdims=True)
    xc = x - mu
    var = jnp.mean(xc * xc, axis=-1, keepdims=True)
    return xc * lax.rsqrt(var + LN_EPS) * g + b


HEADS_PER_STEP = 2
MIX_ROWS = 512
HEAD_COLS = 6 * LANES
OUT_PROJ_SUB = 256


def _regroup_kernel(*refs):
    o_ref = refs[-1]
    for c, w_ref in enumerate(refs[:-1]):
        o_ref[:, c * LANES:(c + 1) * LANES] = w_ref[...].astype(BF16)


def _regroup_w_in(w_in):
    kinds = HEAD_COLS // LANES
    return pl.pallas_call(
        _regroup_kernel,
        grid=(HEADS,),
        in_specs=[pl.BlockSpec((D_MODEL, LANES), functools.partial(lambda c, hd: (0, c * HEADS + hd), c))
                  for c in range(kinds)],
        out_specs=pl.BlockSpec((D_MODEL, HEAD_COLS), lambda hd: (0, hd)),
        out_shape=jax.ShapeDtypeStruct((D_MODEL, HEADS * HEAD_COLS), BF16),
        compiler_params=pltpu.CompilerParams(dimension_semantics=("arbitrary",)),
        name="w_in_regroup",
    )(*([w_in] * kinds))


def _rotary(t, cos, sin_signed, even_lane):
    partner = jnp.where(even_lane, pltpu.roll(t, LANES - 1, 1), pltpu.roll(t, 1, 1))
    return t * cos + partner * sin_signed


def _mixer_kernel(cdec_ref, x_ref, w_ref, cos_ref, sin_ref, decay_ref, zeta_ref, xi_ref, prm_ref, cw_ref,
                  w1f_ref, w3f_ref, w2f_ref, wof_ref,
                  ret_ref, conv_ref, w13b_ref, w2b_ref, wob_ref, state_ref, uext_ref):
    w13b_ref[:, 0:D_FF] = w1f_ref[...].astype(BF16)
    w13b_ref[:, D_FF:] = w3f_ref[...].astype(BF16)
    w2b_ref[...] = w2f_ref[...].astype(BF16)
    wob_ref[...] = wof_ref[...].astype(BF16)

    rows = x_ref.shape[0]
    hps = w_ref.shape[1] // HEAD_COLS
    n = pl.program_id(2)
    head0 = pl.program_id(1) * hps

    @pl.when(n == 0)
    def _():
        for j in range(hps):
            state_ref[head0 + j] = jnp.zeros((HEAD_DIM, HEAD_DIM), F32)
            uext_ref[head0 + j, 0:CONV_HALO, :] = jnp.zeros((CONV_HALO, LANES), F32)

    even_lane = (lax.broadcasted_iota(jnp.int32, (CHUNK, LANES), 1) % 2) == 0
    scale = HEAD_DIM ** -0.5
    blk = 64
    first = CONV_HALO - (CONV_TAPS - 1)
    xb = x_ref[...].astype(BF16)

    for j in range(hps):
        head = head0 + j
        cdec = cdec_ref[head]
        decay = decay_ref[j]
        zeta = zeta_ref[j]
        xi = xi_ref[j]
        rg, rb, cb, cg, cbeta = (prm_ref[j, r:r + 1, :] for r in range(5))
        proj = jnp.dot(xb, w_ref[:, j * HEAD_COLS:(j + 1) * HEAD_COLS], preferred_element_type=F32)
        col = lambda c: proj[:, c * LANES:(c + 1) * LANES]

        state = state_ref[head]
        for c in range(rows // CHUNK):
            rs = slice(c * CHUNK, (c + 1) * CHUNK)
            cos = cos_ref[rs, :]
            sin = sin_ref[rs, :]
            q = _rotary(col(0)[rs], cos, sin, even_lane)
            k = _rotary(col(1)[rs], cos, sin, even_lane) * scale
            v = col(2)[rs].astype(BF16)
            scores = lax.dot_general(q.astype(BF16), k.astype(BF16), (((1,), (1,)), ((), ())),
                                     preferred_element_type=F32) * decay
            inner = jnp.dot(scores.astype(BF16), v, preferred_element_type=F32)
            cross = jnp.dot((q * xi).astype(BF16), state.astype(BF16), preferred_element_type=F32)
            kz_t = jnp.transpose(k * zeta).astype(BF16)
            state = cdec * state + jnp.dot(kz_t, v, preferred_element_type=F32)
            y = _lane_norm(inner + cross, rg, rb)
            g = col(3)[rs]
            ret_ref[j, rs, :] = (g * _sigmoid(g) * y).astype(BF16)
        state_ref[head] = state

        uext_ref[head, CONV_HALO:CONV_HALO + rows, :] = col(4) * _sigmoid(col(5))
        for r0 in range(0, rows, blk):
            acc = cb
            for t in range(CONV_TAPS):
                acc = acc + cw_ref[j, t:t + 1, :] * uext_ref[head, r0 + first + t:r0 + first + t + blk, :]
            y = _lane_norm(acc, cg, cbeta)
            conv_ref[j, r0:r0 + blk, :] = (y * _sigmoid(y)).astype(BF16)
        uext_ref[head, 0:CONV_HALO, :] = uext_ref[head, rows:rows + CONV_HALO, :]


def _mixer(x2, w_heads, cos_t, sin_t, consts, head_params, cw, w1, w3, w2, w_out, batch, seq):
    T = batch * seq
    rows, hps = MIX_ROWS, HEADS_PER_STEP
    ns = seq // rows
    ng = HEADS // hps
    cdec, decay, zeta, xi = consts
    steps = batch * ns * ng
    w1s = w1.reshape(-1, w1.shape[-1])
    w3s = w3.reshape(-1, w3.shape[-1])
    w2s = w2.reshape(-1, w2.shape[-1])
    step_of = lambda b, g, n: ((b * ng + g) * ns + n, 0)
    wspec = lambda w: pl.BlockSpec((w.shape[0] // steps, w.shape[1]), step_of)
    for w in (w1s, w3s, w2s, w_out):
        assert w.shape[0] % (steps * 16) == 0

    def hspec(d1, d2=LANES):
        return pl.BlockSpec((hps, d1, d2), lambda b, g, n: (g, 0, 0))

    out_rows = pl.BlockSpec((hps, rows, LANES), lambda b, g, n: (g, b * ns + n, 0))
    return pl.pallas_call(
        _mixer_kernel,
        grid=(batch, ng, ns),
        in_specs=[pl.BlockSpec(memory_space=pltpu.SMEM),
                  pl.BlockSpec((rows, D_MODEL), lambda b, g, n: (b * ns + n, 0)),
                  pl.BlockSpec((D_MODEL, hps * HEAD_COLS), lambda b, g, n: (0, g)),
                  pl.BlockSpec((rows, LANES), lambda b, g, n: (n, 0)),
                  pl.BlockSpec((rows, LANES), lambda b, g, n: (n, 0)),
                  hspec(CHUNK), hspec(CHUNK), hspec(CHUNK),
                  hspec(SUBLANES), hspec(CONV_TAPS),
                  wspec(w1s), wspec(w3s), wspec(w2s), wspec(w_out)],
        out_specs=[out_rows, out_rows,
                   pl.BlockSpec((w1s.shape[0] // steps, 2 * D_FF), step_of), wspec(w2s), wspec(w_out)],
        out_shape=[jax.ShapeDtypeStruct((HEADS, T, LANES), BF16),
                   jax.ShapeDtypeStruct((CONV_GROUPS, T, LANES), BF16),
                   jax.ShapeDtypeStruct((w1s.shape[0], 2 * D_FF), BF16),
                   jax.ShapeDtypeStruct(w2s.shape, BF16),
                   jax.ShapeDtypeStruct(w_out.shape, BF16)],
        scratch_shapes=[pltpu.VMEM((HEADS, HEAD_DIM, HEAD_DIM), F32),
                        pltpu.VMEM((HEADS, rows + CONV_HALO, LANES), F32)],
        compiler_params=pltpu.CompilerParams(
            dimension_semantics=("arbitrary", "arbitrary", "arbitrary"), vmem_limit_bytes=VMEM_LIMIT),
        name="in_proj_mixer",
    )(cdec, x2, w_heads, cos_t, sin_t, decay, zeta, xi, head_params, cw, w1s, w3s, w2s, w_out)


def _layer_norm(z, g, b):
    mu = jnp.mean(z, axis=-1, keepdims=True)
    zc = z - mu
    var = jnp.mean(zc * zc, axis=-1, keepdims=True)
    return zc * lax.rsqrt(var + LN_EPS) * g + b


def _outproj_kernel(ret_ref, conv_ref, x_ref, w_ref, g_ref, b_ref, wr_ref, br_ref,
                    h_ref, route_ref, route_t_ref, counts_ref, carry_ref):
    @pl.when(pl.program_id(0) == 0)
    def _():
        carry_ref[...] = jnp.zeros_like(carry_ref)

    def exact_zero(v):
        return pltpu.bitcast((pltpu.bitcast(v, jnp.uint32) >> 16) >> 16, F32)

    def mix_of(rs):
        return jnp.concatenate([ret_ref[s, rs, :] for s in range(HEADS)] +
                               [conv_ref[s, rs, :] for s in range(CONV_GROUPS)], axis=-1)

    half = D_MODEL // 2
    rs_a, rs_b = (slice(r0, r0 + OUT_PROJ_SUB) for r0 in range(0, x_ref.shape[0], OUT_PROJ_SUB))
    y_a = jnp.dot(mix_of(rs_a), w_ref[...], preferred_element_type=F32)
    mix_b = mix_of(rs_b)
    y_b1 = jnp.dot(mix_b, w_ref[:, 0:half], preferred_element_type=F32)
    h_a = _layer_norm(y_a + ALPHA * x_ref[rs_a, :], g_ref[...], b_ref[...])
    h_ref[rs_a, :] = h_a
    y_b2 = (jnp.dot(mix_b, w_ref[:, half:], preferred_element_type=F32)
            + jnp.tile(exact_zero(h_a[0:SUBLANES, 0:LANES]), (OUT_PROJ_SUB // SUBLANES, half // LANES)))
    carry = carry_ref[:, 0:1]
    carry = _route(rs_a, h_a, exact_zero(y_b1[0:SUBLANES, half - LANES:half]), carry, wr_ref, br_ref,
                   route_ref, route_t_ref)
    h_b = _layer_norm(jnp.concatenate([y_b1, y_b2], axis=-1) + ALPHA * x_ref[rs_b, :], g_ref[...], b_ref[...])
    h_ref[rs_b, :] = h_b
    carry = _route(rs_b, h_b, None, carry, wr_ref, br_ref, route_ref, route_t_ref)
    carry_ref[...] = jnp.broadcast_to(carry, carry_ref.shape)
    counts_ref[...] = jnp.broadcast_to(carry, counts_ref.shape)


def _route(rs, h, zero, carry, wr_ref, br_ref, route_ref, route_t_ref):
    logits = jnp.dot(h.astype(BF16), wr_ref[...], preferred_element_type=F32) + br_ref[...]

    tm = logits.shape[0]
    lt = jnp.transpose(logits)
    neg = jnp.float32(-jnp.inf)
    big = jnp.float32(1e9)
    grow = lax.broadcasted_iota(jnp.int32, (SUBLANES, tm), 0).astype(F32)
    gmask = grow < N_GROUPS
    gl = jnp.where(gmask, lt[N_EXPERTS:N_EXPERTS + SUBLANES, :], neg)
    gmax = jnp.max(gl, axis=0, keepdims=True)
    gidx = jnp.min(jnp.where(gl == gmax, grow, big), axis=0, keepdims=True)
    g_w = 1.0 / jnp.sum(jnp.where(gmask, jnp.exp(gl - gmax), 0.0), axis=0, keepdims=True)
    erow = lax.broadcasted_iota(jnp.int32, (N_EXPERTS, tm), 0).astype(F32)
    lo = gidx * EXPERTS_PER_GROUP
    emask = (erow >= lo) & (erow < lo + EXPERTS_PER_GROUP)
    el = jnp.where(emask, lt[0:N_EXPERTS, :], neg)
    top1 = jnp.max(el, axis=0, keepdims=True)
    idx1 = jnp.min(jnp.where(el == top1, erow, big), axis=0, keepdims=True)
    el2 = jnp.where(erow == idx1, neg, el)
    top2 = jnp.max(el2, axis=0, keepdims=True)
    idx2 = jnp.min(jnp.where(el2 == top2, erow, big), axis=0, keepdims=True)
    e2 = jnp.exp(top2 - top1)
    w1 = g_w / (1.0 + e2)
    w2 = w1 * e2

    sel1 = erow == idx1
    sel2 = erow == idx2
    onehot = jnp.where(sel1, 1.0, 0.0) + jnp.where(sel2, 1.0, 0.0)
    r_i = lax.broadcasted_iota(jnp.int32, (tm, tm), 0)
    c_i = lax.broadcasted_iota(jnp.int32, (tm, tm), 1)
    earlier = jnp.where(r_i < c_i, 1.0, 0.0).astype(BF16)
    prefix = jnp.dot(onehot.astype(BF16), earlier, preferred_element_type=F32) + carry
    rank1 = jnp.sum(jnp.where(sel1, prefix, 0.0), axis=0, keepdims=True)
    rank2 = jnp.sum(jnp.where(sel2, prefix, 0.0), axis=0, keepdims=True)

    route_t = jnp.concatenate([w1, w2, idx1, idx2, rank1, rank2, jnp.zeros((2, tm), F32)], axis=0)
    if zero is not None:
        route_t = route_t + jnp.tile(zero, (1, tm // LANES))
    route_t_ref[:, rs] = route_t
    route_ref[rs, :] = jnp.transpose(
        jnp.concatenate([route_t, jnp.zeros((LANES - SUBLANES, tm), F32)], axis=0))
    return carry + jnp.sum(onehot, axis=1, keepdims=True)


def _out_proj(ret3, conv3, x2, w_out_b, ln_g, ln_b, wr_b, br, tm=512):
    T = x2.shape[0]
    cur = lambda i: (0, i, 0)
    row = lambda i: (i, 0)
    const = lambda i: (0, 0)
    return pl.pallas_call(
        _outproj_kernel,
        grid=(T // tm,),
        in_specs=[pl.BlockSpec((HEADS, tm, LANES), cur),
                  pl.BlockSpec((CONV_GROUPS, tm, LANES), cur),
                  pl.BlockSpec((tm, D_MODEL), row),
                  pl.BlockSpec((D_MODEL, D_MODEL), const),
                  pl.BlockSpec((1, D_MODEL), const),
                  pl.BlockSpec((1, D_MODEL), const),
                  pl.BlockSpec((D_MODEL, LANES), const),
                  pl.BlockSpec((1, LANES), const)],
        out_specs=[pl.BlockSpec((tm, D_MODEL), row),
                   pl.BlockSpec((tm, LANES), row),
                   pl.BlockSpec((SUBLANES, tm), lambda i: (0, i)),
                   pl.BlockSpec((N_EXPERTS, LANES), const)],
        out_shape=[jax.ShapeDtypeStruct((T, D_MODEL), F32),
                   jax.ShapeDtypeStruct((T, LANES), F32),
                   jax.ShapeDtypeStruct((SUBLANES, T), F32),
                   jax.ShapeDtypeStruct((N_EXPERTS, LANES), F32)],
        scratch_shapes=[pltpu.VMEM((N_EXPERTS, LANES), F32)],
        compiler_params=pltpu.CompilerParams(
            dimension_semantics=("arbitrary",), vmem_limit_bytes=VMEM_LIMIT),
        name="out_proj_ln_route",
    )(ret3, conv3, x2, w_out_b, ln_g, ln_b, wr_b, br)


MOE_TM = 512
MOE_TB = 512


def _row_copy(src_ref, src_row, dst_ref, dst_row, sem):
    return pltpu.make_async_copy(src_ref.at[pl.ds(src_row, 1), :], dst_ref.at[pl.ds(dst_row, 1), :], sem)


def _experts_kernel(te_ref, nv_ref, src_ref, src_next_ref, h_ref, w13_ref, w2_ref, y_ref, x_ref, sems):
    i = pl.program_id(0)
    last = pl.num_programs(0) - 1
    slot = i % 2
    used = i < nv_ref[0]

    def gather_wait(s):
        pltpu.make_async_copy(h_ref.at[pl.ds(0, MOE_TM), :], x_ref.at[s, pl.ds(0, MOE_TM), :], sems.at[s]).wait()

    @pl.when(i == 0)
    def _():
        x_ref[:, MOE_TM:, :] = jnp.zeros((2, SUBLANES, D_MODEL), F32)

        def first(t, carry):
            _row_copy(h_ref, src_ref[0, 0, t], x_ref.at[0], t, sems.at[0]).start()
            return carry
        lax.fori_loop(0, MOE_TM, first, 0, unroll=8)

    @pl.when(used)
    def _():
        gather_wait(slot)

        def issue_half(k, width):
            for t in range(k * MOE_TM // 2, (k + 1) * MOE_TM // 2):
                _row_copy(h_ref, src_next_ref[0, 0, t], x_ref.at[1 - slot], t, sems.at[1 - slot]).start()
            spare = pltpu.bitcast(x_ref[1 - slot, MOE_TM:, k * LANES:(k + 1) * LANES], jnp.uint32)
            zero = pltpu.bitcast((spare >> 16) >> 16, F32)
            return jnp.tile(zero, (MOE_TM // SUBLANES, width // LANES))

        xb = x_ref[slot, 0:MOE_TM, :].astype(BF16)
        a13 = jnp.dot(xb, w13_ref[0], preferred_element_type=F32) + issue_half(0, 2 * D_FF)
        a1 = a13[:, 0:D_FF]
        a3 = a13[:, D_FF:]
        act = (a1 * _sigmoid(a1) * a3).astype(BF16)
        y_ref[...] = jnp.dot(act, w2_ref[0], preferred_element_type=F32) + issue_half(1, D_MODEL)

        @pl.when(i == last)
        def _():
            gather_wait(1 - slot)

    @pl.when(jnp.logical_not(used))
    def _():
        @pl.when(i == nv_ref[0])
        def _():
            gather_wait(slot)
        y_ref[...] = jnp.zeros_like(y_ref)


def _experts(tile_expert, n_valid, src_tok, h, w13b, w2b):
    n_tiles = src_tok.shape[0]
    row = lambda i, te, nv: (i, 0)
    wsel = lambda i, te, nv: (te[i], 0, 0)
    smem_row = lambda f: pl.BlockSpec((1, 1, MOE_TM), f, memory_space=pltpu.SMEM)
    return pl.pallas_call(
        _experts_kernel,
        grid_spec=pltpu.PrefetchScalarGridSpec(
            num_scalar_prefetch=2,
            grid=(n_tiles,),
            in_specs=[smem_row(lambda i, te, nv: (i, 0, 0)),
                      smem_row(lambda i, te, nv: (jnp.minimum(i + 1, n_tiles - 1), 0, 0)),
                      pl.BlockSpec(memory_space=pl.ANY),
                      pl.BlockSpec((1, D_MODEL, 2 * D_FF), wsel),
                      pl.BlockSpec((1, D_FF, D_MODEL), wsel)],
            out_specs=pl.BlockSpec((MOE_TM, D_MODEL), row),
            scratch_shapes=[pltpu.VMEM((2, MOE_TM + SUBLANES, D_MODEL), F32),
                            pltpu.SemaphoreType.DMA((2,))]),
        out_shape=jax.ShapeDtypeStruct((n_tiles * MOE_TM, D_MODEL), F32),
        compiler_params=pltpu.CompilerParams(
            dimension_semantics=("arbitrary",), vmem_limit_bytes=VMEM_LIMIT),
        name="moe_experts",
    )(tile_expert, n_valid, src_tok, src_tok, h, w13b, w2b)


def _combine_kernel(pos_ref, pos_next_ref, route_ref, h_ref, g_ref, b_ref, ys_ref, o_ref,
                    ya0_ref, yb0_ref, ya1_ref, yb1_ref, sems):
    tb = h_ref.shape[0]
    i = pl.program_id(0)
    bufs = ((ya0_ref, yb0_ref), (ya1_ref, yb1_ref))

    def gather_wait(s):
        for k in range(TOP_K):
            pltpu.make_async_copy(ys_ref.at[pl.ds(0, tb), :], bufs[s][k], sems.at[s]).wait()

    @pl.when(i == 0)
    def _():
        def first(t, carry):
            for k in range(TOP_K):
                _row_copy(ys_ref, pos_ref[0, 0, k * tb + t], bufs[0][k], t, sems.at[0]).start()
            return carry
        lax.fori_loop(0, tb, first, 0, unroll=8)

    for par in range(2):
        @pl.when(i % 2 == par)
        def _():
            gather_wait(par)
            for t in range(tb):
                for k in range(TOP_K):
                    _row_copy(ys_ref, pos_next_ref[0, 0, k * tb + t], bufs[1 - par][k], t, sems.at[1 - par]).start()
            route = route_ref[...]
            ffn = route[:, 0:1] * bufs[par][0][...] + route[:, 1:2] * bufs[par][1][...]
            o_ref[...] = _layer_norm(ALPHA * h_ref[...] + ffn, g_ref[...], b_ref[...])

            @pl.when(i == pl.num_programs(0) - 1)
            def _():
                gather_wait(1 - par)


def _combine(pos3, route, h, ln_g, ln_b, ys):
    T = h.shape[0]
    tb = MOE_TB
    row = lambda i: (i, 0)
    const = lambda i: (0, 0)
    return pl.pallas_call(
        _combine_kernel,
        grid=(T // tb,),
        in_specs=[pl.BlockSpec((1, 1, 2 * tb), lambda i: (i, 0, 0), memory_space=pltpu.SMEM),
                  pl.BlockSpec((1, 1, 2 * tb), lambda i: (jnp.minimum(i + 1, T // tb - 1), 0, 0),
                               memory_space=pltpu.SMEM),
                  pl.BlockSpec((tb, LANES), row),
                  pl.BlockSpec((tb, D_MODEL), row),
                  pl.BlockSpec((1, D_MODEL), const),
                  pl.BlockSpec((1, D_MODEL), const),
                  pl.BlockSpec(memory_space=pl.ANY)],
        out_specs=pl.BlockSpec((tb, D_MODEL), row),
        out_shape=jax.ShapeDtypeStruct((T, D_MODEL), F32),
        scratch_shapes=[pltpu.VMEM((tb, D_MODEL), F32)] * (2 * TOP_K) + [pltpu.SemaphoreType.DMA((2,))],
        compiler_params=pltpu.CompilerParams(
            dimension_semantics=("arbitrary",), vmem_limit_bytes=VMEM_LIMIT),
        name="moe_combine_ln",
    )(pos3, pos3, route, h, ln_g, ln_b, ys)


def _routed_moe(h, route, route_t, counts, w13b, w2b, ln_g, ln_b):
    T = h.shape[0]
    n_tiles = (TOP_K * T) // MOE_TM + N_EXPERTS
    i32 = jnp.int32
    cnt = counts[:, 0].astype(i32)
    ntile = (cnt + MOE_TM - 1) // MOE_TM
    tile_end = jnp.cumsum(ntile)
    n_valid = tile_end[-1:]
    seg_start = (tile_end - ntile) * MOE_TM
    info = route_t[2:6].astype(i32)
    eids = jnp.arange(N_EXPERTS, dtype=i32)
    start_of = lambda e: jnp.sum(jnp.where(e[None, :] == eids[:, None], seg_start[:, None], 0), axis=0)
    pos1 = start_of(info[0]) + info[2]
    pos2 = start_of(info[1]) + info[3]
    pos3 = jnp.concatenate([pos1.reshape(T // MOE_TB, 1, MOE_TB), pos2.reshape(T // MOE_TB, 1, MOE_TB)], axis=2)
    tile_ids = jnp.minimum(jnp.arange(n_tiles, dtype=i32), n_valid[0] - 1)
    tile_expert = jnp.sum((tile_ids[:, None] >= tile_end[None, :]).astype(i32), axis=1)

    order = jnp.argsort(jnp.concatenate([pos1, pos2])).astype(i32)
    n_rows = n_tiles * MOE_TM
    padded = jnp.concatenate([jnp.zeros((n_rows,), i32), jnp.where(order >= T, order - T, order),
                              jnp.zeros((2 * n_rows - TOP_K * T,), i32)])
    shift = seg_start - (jnp.cumsum(cnt) - cnt)
    row_expert = jnp.repeat(tile_expert, MOE_TM)
    src_tok = jnp.zeros((n_rows,), i32)
    for e in range(N_EXPERTS):
        view = lax.dynamic_slice(padded, (n_rows - shift[e],), (n_rows,))
        src_tok = jnp.where(row_expert == e, view, src_tok)

    ys = _experts(tile_expert, n_valid.astype(i32), src_tok.reshape(n_tiles, 1, MOE_TM), h, w13b, w2b)
    return _combine(pos3, route, h, ln_g, ln_b, ys)


def _retention_constants():
    hh = np.arange(HEADS, dtype=np.float64)
    log_gamma = np.log1p(-np.exp2(-5.0 - hh))
    pos = np.arange(CHUNK, dtype=np.float64)
    diff = pos[:, None] - pos[None, :]
    causal = diff >= 0
    decay = np.where(causal[None], np.exp(log_gamma[:, None, None] * np.where(causal, diff, 0.0)[None]), 0.0)
    zeta = np.exp(log_gamma[:, None] * (CHUNK - 1 - pos)[None])
    xi = np.exp(log_gamma[:, None] * (pos + 1.0)[None])
    cdec = np.exp(log_gamma * CHUNK)
    bc = lambda t: np.broadcast_to(t[:, :, None], (HEADS, CHUNK, LANES))
    return (jnp.asarray(cdec, F32), jnp.asarray(decay, F32),
            jnp.asarray(bc(zeta), F32), jnp.asarray(bc(xi), F32))


def _rope_tables(seq):
    f32 = np.float32
    inv = np.power(f32(ROPE_BASE), -np.arange(0, HEAD_DIM, 2, dtype=f32) / f32(HEAD_DIM)).astype(f32)
    ang = (np.arange(seq, dtype=f32)[:, None] * inv[None, :]).astype(f32).astype(np.float64)
    cos = np.repeat(np.cos(ang), 2, axis=-1)
    sin = np.sin(ang)
    sin_signed = np.stack([-sin, sin], axis=-1).reshape(seq, HEAD_DIM)
    return jnp.asarray(cos, F32), jnp.asarray(sin_signed, F32)


def kernel(x, w_in, ret_norm_g, ret_norm_b, conv_w, conv_b, conv_norm_g, conv_norm_b, w_out, ln1_g, ln1_b, w_router_group, b_router_group, w_router_expert, b_router_expert, w1, w3, w2, ln2_g, ln2_b):
    batch, seq, _ = x.shape
    T = batch * seq
    x2 = x.reshape(T, D_MODEL)

    w_heads = _regroup_w_in(w_in)

    cos_t, sin_t = _rope_tables(seq)
    head_params = jnp.stack([ret_norm_g, ret_norm_b, conv_b, conv_norm_g, conv_norm_b]
                            + [jnp.zeros_like(conv_b)] * (SUBLANES - 5)).reshape(SUBLANES, HEADS, LANES)
    head_params = jnp.transpose(head_params, (1, 0, 2))
    cw = jnp.transpose(conv_w.reshape(CONV_TAPS, CONV_GROUPS, LANES), (1, 0, 2))
    ret3, conv3, w13b, w2b, w_out_b = _mixer(
        x2, w_heads, cos_t, sin_t, _retention_constants(), head_params, cw, w1, w3, w2, w_out, batch, seq)

    wr = jnp.concatenate([jnp.transpose(w_router_expert, (1, 0, 2)).reshape(D_MODEL, N_EXPERTS),
                          w_router_group], axis=1)
    wr = jnp.pad(wr, ((0, 0), (0, LANES - wr.shape[1])))
    br = jnp.pad(jnp.concatenate([b_router_expert.reshape(-1), b_router_group]),
                 (0, LANES - N_EXPERTS - N_GROUPS)).reshape(1, LANES)

    h, route, route_t, counts = _out_proj(ret3, conv3, x2, w_out_b, ln1_g.reshape(1, -1),
                                          ln1_b.reshape(1, -1), wr.astype(BF16), br)

    y = _routed_moe(h, route, route_t, counts, w13b.reshape(N_EXPERTS, D_MODEL, 2 * D_FF),
                    w2b.reshape(w2.shape), ln2_g.reshape(1, -1), ln2_b.reshape(1, -1))
    return y.reshape(batch, seq, D_MODEL)
```

```python
import functools

import numpy as np
import jax
import jax.numpy as jnp
from jax import lax
from jax.experimental import pallas as pl
from jax.experimental.pallas import tpu as pltpu

D_MODEL = 2048
RET_WIDTH = D_MODEL // 2
CONV_WIDTH = D_MODEL - RET_WIDTH
HEADS = 8
HEAD_DIM = RET_WIDTH // HEADS
CONV_GROUPS = 8
CONV_TAPS = 31
CHUNK = 128
ROPE_BASE = 10000.0
IN_COLS = 4 * RET_WIDTH + 2 * CONV_WIDTH
N_GROUPS = 4
EXPERTS_PER_GROUP = 4
N_EXPERTS = N_GROUPS * EXPERTS_PER_GROUP
TOP_K = 2
D_FF = D_MODEL // 2
LN_EPS = 1e-5
ALPHA = 2.0 ** 0.25

LANES = 128
SUBLANES = 8
VMEM_LIMIT = 56 * 1024 * 1024
CONV_HALO = 32

F32 = jnp.float32
BF16 = jnp.bfloat16


def _sigmoid(x):
    return 1.0 / (1.0 + jnp.exp(-x))


def _lane_norm(x, g, b):
    mu = jnp.mean(x, axis=-1, keepdims=True)
    xc = x - mu
    var = jnp.mean(xc * xc, axis=-1, keepdims=True)
    return xc * lax.rsqrt(var + LN_EPS) * g + b


HEADS_PER_STEP = 2
MIX_ROWS = 512
HEAD_COLS = 6 * LANES
OUT_PROJ_SUB = 256


def _regroup_kernel(*refs):
    o_ref = refs[-1]
    for c, w_ref in enumerate(refs[:-1]):
        o_ref[:, c * LANES:(c + 1) * LANES] = w_ref[...].astype(BF16)


def _regroup_w_in(w_in):
    kinds = HEAD_COLS // LANES
    return pl.pallas_call(
        _regroup_kernel,
        grid=(HEADS,),
        in_specs=[pl.BlockSpec((D_MODEL, LANES), functools.partial(lambda c, hd: (0, c * HEADS + hd), c))
                  for c in range(kinds)],
        out_specs=pl.BlockSpec((D_MODEL, HEAD_COLS), lambda hd: (0, hd)),
        out_shape=jax.ShapeDtypeStruct((D_MODEL, HEADS * HEAD_COLS), BF16),
        compiler_params=pltpu.CompilerParams(dimension_semantics=("arbitrary",)),
        name="w_in_regroup",
    )(*([w_in] * kinds))


def _rotary(t, cos, sin_signed, even_lane):
    partner = jnp.where(even_lane, pltpu.roll(t, LANES - 1, 1), pltpu.roll(t, 1, 1))
    return t * cos + partner * sin_signed


def _mixer_kernel(cdec_ref, x_ref, w_ref, cos_ref, sin_ref, decay_ref, zeta_ref, xi_ref, prm_ref, cw_ref,
                  w1f_ref, w3f_ref, w2f_ref, wof_ref,
                  ret_ref, conv_ref, w13b_ref, w2b_ref, wob_ref, state_ref, uext_ref):
    w13b_ref[:, 0:D_FF] = w1f_ref[...].astype(BF16)
    w13b_ref[:, D_FF:] = w3f_ref[...].astype(BF16)
    w2b_ref[...] = w2f_ref[...].astype(BF16)
    wob_ref[...] = wof_ref[...].astype(BF16)

    rows = x_ref.shape[0]
    hps = w_ref.shape[1] // HEAD_COLS
    n = pl.program_id(2)
    head0 = pl.program_id(1) * hps

    @pl.when(n == 0)
    def _():
        for j in range(hps):
            state_ref[head0 + j] = jnp.zeros((HEAD_DIM, HEAD_DIM), F32)
            uext_ref[head0 + j, 0:CONV_HALO, :] = jnp.zeros((CONV_HALO, LANES), F32)

    even_lane = (lax.broadcasted_iota(jnp.int32, (CHUNK, LANES), 1) % 2) == 0
    scale = HEAD_DIM ** -0.5
    blk = 64
    first = CONV_HALO - (CONV_TAPS - 1)
    xb = x_ref[...].astype(BF16)

    for j in range(hps):
        head = head0 + j
        cdec = cdec_ref[head]
        decay = decay_ref[j]
        zeta = zeta_ref[j]
        xi = xi_ref[j]
        rg, rb, cb, cg, cbeta = (prm_ref[j, r:r + 1, :] for r in range(5))
        proj = jnp.dot(xb, w_ref[:, j * HEAD_COLS:(j + 1) * HEAD_COLS], preferred_element_type=F32)
        col = lambda c: proj[:, c * LANES:(c + 1) * LANES]

        state = state_ref[head]
        for c in range(rows // CHUNK):
            rs = slice(c * CHUNK, (c + 1) * CHUNK)
            cos = cos_ref[rs, :]
            sin = sin_ref[rs, :]
            q = _rotary(col(0)[rs], cos, sin, even_lane)
            k = _rotary(col(1)[rs], cos, sin, even_lane) * scale
            v = col(2)[rs].astype(BF16)
            scores = lax.dot_general(q.astype(BF16), k.astype(BF16), (((1,), (1,)), ((), ())),
                                     preferred_element_type=F32) * decay
            inner = jnp.dot(scores.astype(BF16), v, preferred_element_type=F32)
            cross = jnp.dot((q * xi).astype(BF16), state.astype(BF16), preferred_element_type=F32)
            kz_t = jnp.transpose(k * zeta).astype(BF16)
            state = cdec * state + jnp.dot(kz_t, v, preferred_element_type=F32)
            y = _lane_norm(inner + cross, rg, rb)
            g = col(3)[rs]
            ret_ref[j, rs, :] = (g * _sigmoid(g) * y).astype(BF16)
        state_ref[head] = state

        uext_ref[head, CONV_HALO:CONV_HALO + rows, :] = col(4) * _sigmoid(col(5))
        for r0 in range(0, rows, blk):
            acc = cb
            for t in range(CONV_TAPS):
                acc = acc + cw_ref[j, t:t + 1, :] * uext_ref[head, r0 + first + t:r0 + first + t + blk, :]
            y = _lane_norm(acc, cg, cbeta)
            conv_ref[j, r0:r0 + blk, :] = (y * _sigmoid(y)).astype(BF16)
        uext_ref[head, 0:CONV_HALO, :] = uext_ref[head, rows:rows + CONV_HALO, :]


def _mixer(x2, w_heads, cos_t, sin_t, consts, head_params, cw, w1, w3, w2, w_out, batch, seq):
    T = batch * seq
    rows, hps = MIX_ROWS, HEADS_PER_STEP
    ns = seq // rows
    ng = HEADS // hps
    cdec, decay, zeta, xi = consts
    steps = batch * ns * ng
    w1s = w1.reshape(-1, w1.shape[-1])
    w3s = w3.reshape(-1, w3.shape[-1])
    w2s = w2.reshape(-1, w2.shape[-1])
    step_of = lambda b, g, n: ((b * ng + g) * ns + n, 0)
    wspec = lambda w: pl.BlockSpec((w.shape[0] // steps, w.shape[1]), step_of)
    for w in (w1s, w3s, w2s, w_out):
        assert w.shape[0] % (steps * 16) == 0

    def hspec(d1, d2=LANES):
        return pl.BlockSpec((hps, d1, d2), lambda b, g, n: (g, 0, 0))

    out_rows = pl.BlockSpec((hps, rows, LANES), lambda b, g, n: (g, b * ns + n, 0))
    return pl.pallas_call(
        _mixer_kernel,
        grid=(batch, ng, ns),
        in_specs=[pl.BlockSpec(memory_space=pltpu.SMEM),
                  pl.BlockSpec((rows, D_MODEL), lambda b, g, n: (b * ns + n, 0)),
                  pl.BlockSpec((D_MODEL, hps * HEAD_COLS), lambda b, g, n: (0, g)),
                  pl.BlockSpec((rows, LANES), lambda b, g, n: (n, 0)),
                  pl.BlockSpec((rows, LANES), lambda b, g, n: (n, 0)),
                  hspec(CHUNK), hspec(CHUNK), hspec(CHUNK),
                  hspec(SUBLANES), hspec(CONV_TAPS),
                  wspec(w1s), wspec(w3s), wspec(w2s), wspec(w_out)],
        out_specs=[out_rows, out_rows,
                   pl.BlockSpec((w1s.shape[0] // steps, 2 * D_FF), step_of), wspec(w2s), wspec(w_out)],
        out_shape=[jax.ShapeDtypeStruct((HEADS, T, LANES), BF16),
                   jax.ShapeDtypeStruct((CONV_GROUPS, T, LANES), BF16),
                   jax.ShapeDtypeStruct((w1s.shape[0], 2 * D_FF), BF16),
                   jax.ShapeDtypeStruct(w2s.shape, BF16),
                   jax.ShapeDtypeStruct(w_out.shape, BF16)],
        scratch_shapes=[pltpu.VMEM((HEADS, HEAD_DIM, HEAD_DIM), F32),
                        pltpu.VMEM((HEADS, rows + CONV_HALO, LANES), F32)],
        compiler_params=pltpu.CompilerParams(
            dimension_semantics=("arbitrary", "arbitrary", "arbitrary"), vmem_limit_bytes=VMEM_LIMIT),
        name="in_proj_mixer",
    )(cdec, x2, w_heads, cos_t, sin_t, decay, zeta, xi, head_params, cw, w1s, w3s, w2s, w_out)


def _layer_norm(z, g, b):
    mu = jnp.mean(z, axis=-1, keepdims=True)
    zc = z - mu
    var = jnp.mean(zc * zc, axis=-1, keepdims=True)
    return zc * lax.rsqrt(var + LN_EPS) * g + b


def _outproj_kernel(ret_ref, conv_ref, x_ref, w_ref, g_ref, b_ref, wr_ref, br_ref,
                    h_ref, route_ref, route_t_ref, counts_ref, carry_ref):
    @pl.when(pl.program_id(0) == 0)
    def _():
        carry_ref[...] = jnp.zeros_like(carry_ref)

    def exact_zero(v):
        return pltpu.bitcast((pltpu.bitcast(v, jnp.uint32) >> 16) >> 16, F32)

    def mix_of(rs):
        return jnp.concatenate([ret_ref[s, rs, :] for s in range(HEADS)] +
                               [conv_ref[s, rs, :] for s in range(CONV_GROUPS)], axis=-1)

    half = D_MODEL // 2
    rs_a, rs_b = (slice(r0, r0 + OUT_PROJ_SUB) for r0 in range(0, x_ref.shape[0], OUT_PROJ_SUB))
    y_a = jnp.dot(mix_of(rs_a), w_ref[...], preferred_element_type=F32)
    mix_b = mix_of(rs_b)
    y_b1 = jnp.dot(mix_b, w_ref[:, 0:half], preferred_element_type=F32)
    h_a = _layer_norm(y_a + ALPHA * x_ref[rs_a, :], g_ref[...], b_ref[...])
    h_ref[rs_a, :] = h_a
    y_b2 = (jnp.dot(mix_b, w_ref[:, half:], preferred_element_type=F32)
            + jnp.tile(exact_zero(h_a[0:SUBLANES, 0:LANES]), (OUT_PROJ_SUB // SUBLANES, half // LANES)))
    carry = carry_ref[:, 0:1]
    carry = _route(rs_a, h_a, exact_zero(y_b1[0:SUBLANES, half - LANES:half]), carry, wr_ref, br_ref,
                   route_ref, route_t_ref)
    h_b = _layer_norm(jnp.concatenate([y_b1, y_b2], axis=-1) + ALPHA * x_ref[rs_b, :], g_ref[...], b_ref[...])
    h_ref[rs_b, :] = h_b
    carry = _route(rs_b, h_b, None, carry, wr_ref, br_ref, route_ref, route_t_ref)
    carry_ref[...] = jnp.broadcast_to(carry, carry_ref.shape)
    counts_ref[...] = jnp.broadcast_to(carry, counts_ref.shape)


def _route(rs, h, zero, carry, wr_ref, br_ref, route_ref, route_t_ref):
    logits = jnp.dot(h.astype(BF16), wr_ref[...], preferred_element_type=F32) + br_ref[...]

    tm = logits.shape[0]
    lt = jnp.transpose(logits)
    neg = jnp.float32(-jnp.inf)
    big = jnp.float32(1e9)
    grow = lax.broadcasted_iota(jnp.int32, (SUBLANES, tm), 0).astype(F32)
    gmask = grow < N_GROUPS
    gl = jnp.where(gmask, lt[N_EXPERTS:N_EXPERTS + SUBLANES, :], neg)
    gmax = jnp.max(gl, axis=0, keepdims=True)
    gidx = jnp.min(jnp.where(gl == gmax, grow, big), axis=0, keepdims=True)
    g_w = 1.0 / jnp.sum(jnp.where(gmask, jnp.exp(gl - gmax), 0.0), axis=0, keepdims=True)
    erow = lax.broadcasted_iota(jnp.int32, (N_EXPERTS, tm), 0).astype(F32)
    lo = gidx * EXPERTS_PER_GROUP
    emask = (erow >= lo) & (erow < lo + EXPERTS_PER_GROUP)
    el = jnp.where(emask, lt[0:N_EXPERTS, :], neg)
    top1 = jnp.max(el, axis=0, keepdims=True)
    idx1 = jnp.min(jnp.where(el == top1, erow, big), axis=0, keepdims=True)
    el2 = jnp.where(erow == idx1, neg, el)
    top2 = jnp.max(el2, axis=0, keepdims=True)
    idx2 = jnp.min(jnp.where(el2 == top2, erow, big), axis=0, keepdims=True)
    e2 = jnp.exp(top2 - top1)
    w1 = g_w / (1.0 + e2)
    w2 = w1 * e2

    sel1 = erow == idx1
    sel2 = erow == idx2
    onehot = jnp.where(sel1, 1.0, 0.0) + jnp.where(sel2, 1.0, 0.0)
    r_i = lax.broadcasted_iota(jnp.int32, (tm, tm), 0)
    c_i = lax.broadcasted_iota(jnp.int32, (tm, tm), 1)
    earlier = jnp.where(r_i < c_i, 1.0, 0.0).astype(BF16)
    prefix = jnp.dot(onehot.astype(BF16), earlier, preferred_element_type=F32) + carry
    rank1 = jnp.sum(jnp.where(sel1, prefix, 0.0), axis=0, keepdims=True)
    rank2 = jnp.sum(jnp.where(sel2, prefix, 0.0), axis=0, keepdims=True)

    route_t = jnp.concatenate([w1, w2, idx1, idx2, rank1, rank2, jnp.zeros((2, tm), F32)], axis=0)
    if zero is not None:
        route_t = route_t + jnp.tile(zero, (1, tm // LANES))
    route_t_ref[:, rs] = route_t
    route_ref[rs, :] = jnp.transpose(
        jnp.concatenate([route_t, jnp.zeros((LANES - SUBLANES, tm), F32)], axis=0))
    return carry + jnp.sum(onehot, axis=1, keepdims=True)


def _out_proj(ret3, conv3, x2, w_out_b, ln_g, ln_b, wr_b, br, tm=512):
    T = x2.shape[0]
    cur = lambda i: (0, i, 0)
    row = lambda i: (i, 0)
    const = lambda i: (0, 0)
    return pl.pallas_call(
        _outproj_kernel,
        grid=(T // tm,),
        in_specs=[pl.BlockSpec((HEADS, tm, LANES), cur),
                  pl.BlockSpec((CONV_GROUPS, tm, LANES), cur),
                  pl.BlockSpec((tm, D_MODEL), row),
                  pl.BlockSpec((D_MODEL, D_MODEL), const),
                  pl.BlockSpec((1, D_MODEL), const),
                  pl.BlockSpec((1, D_MODEL), const),
                  pl.BlockSpec((D_MODEL, LANES), const),
                  pl.BlockSpec((1, LANES), const)],
        out_specs=[pl.BlockSpec((tm, D_MODEL), row),
                   pl.BlockSpec((tm, LANES), row),
                   pl.BlockSpec((SUBLANES, tm), lambda i: (0, i)),
                   pl.BlockSpec((N_EXPERTS, LANES), const)],
        out_shape=[jax.ShapeDtypeStruct((T, D_MODEL), F32),
                   jax.ShapeDtypeStruct((T, LANES), F32),
                   jax.ShapeDtypeStruct((SUBLANES, T), F32),
                   jax.ShapeDtypeStruct((N_EXPERTS, LANES), F32)],
        scratch_shapes=[pltpu.VMEM((N_EXPERTS, LANES), F32)],
        compiler_params=pltpu.CompilerParams(
            dimension_semantics=("arbitrary",), vmem_limit_bytes=VMEM_LIMIT),
        name="out_proj_ln_route",
    )(ret3, conv3, x2, w_out_b, ln_g, ln_b, wr_b, br)


MOE_TM = 512
MOE_TB = 512


def _row_copy(src_ref, src_row, dst_ref, dst_row, sem):
    return pltpu.make_async_copy(src_ref.at[pl.ds(src_row, 1), :], dst_ref.at[pl.ds(dst_row, 1), :], sem)


def _experts_kernel(te_ref, nv_ref, src_ref, src_next_ref, h_ref, w13_ref, w2_ref, y_ref, x_ref, sems):
    i = pl.program_id(0)
    last = pl.num_programs(0) - 1
    slot = i % 2
    used = i < nv_ref[0]

    def gather_wait(s):
        pltpu.make_async_copy(h_ref.at[pl.ds(0, MOE_TM), :], x_ref.at[s, pl.ds(0, MOE_TM), :], sems.at[s]).wait()

    @pl.when(i == 0)
    def _():
        x_ref[:, MOE_TM:, :] = jnp.zeros((2, SUBLANES, D_MODEL), F32)

        def first(t, carry):
            _row_copy(h_ref, src_ref[0, 0, t], x_ref.at[0], t, sems.at[0]).start()
            return carry
        lax.fori_loop(0, MOE_TM, first, 0, unroll=8)

    @pl.when(used)
    def _():
        gather_wait(slot)

        def issue_half(k, width):
            for t in range(k * MOE_TM // 2, (k + 1) * MOE_TM // 2):
                _row_copy(h_ref, src_next_ref[0, 0, t], x_ref.at[1 - slot], t, sems.at[1 - slot]).start(priority=1)
            spare = pltpu.bitcast(x_ref[1 - slot, MOE_TM:, k * LANES:(k + 1) * LANES], jnp.uint32)
            zero = pltpu.bitcast((spare >> 16) >> 16, F32)
            return jnp.tile(zero, (MOE_TM // SUBLANES, width // LANES))

        xb = x_ref[slot, 0:MOE_TM, :].astype(BF16)
        a13 = jnp.dot(xb, w13_ref[0], preferred_element_type=F32) + issue_half(0, 2 * D_FF)
        a1 = a13[:, 0:D_FF]
        a3 = a13[:, D_FF:]
        act = (a1 * _sigmoid(a1) * a3).astype(BF16)
        y_ref[...] = jnp.dot(act, w2_ref[0], preferred_element_type=F32) + issue_half(1, D_MODEL)

        @pl.when(i == last)
        def _():
            gather_wait(1 - slot)

    @pl.when(jnp.logical_not(used))
    def _():
        @pl.when(i == nv_ref[0])
        def _():
            gather_wait(slot)
        y_ref[...] = jnp.zeros_like(y_ref)


def _experts(tile_expert, n_valid, src_tok, h, w13b, w2b):
    n_tiles = src_tok.shape[0]
    row = lambda i, te, nv: (i, 0)
    wsel = lambda i, te, nv: (te[i], 0, 0)
    smem_row = lambda f: pl.BlockSpec((1, 1, MOE_TM), f, memory_space=pltpu.SMEM)
    return pl.pallas_call(
        _experts_kernel,
        grid_spec=pltpu.PrefetchScalarGridSpec(
            num_scalar_prefetch=2,
            grid=(n_tiles,),
            in_specs=[smem_row(lambda i, te, nv: (i, 0, 0)),
                      smem_row(lambda i, te, nv: (jnp.minimum(i + 1, n_tiles - 1), 0, 0)),
                      pl.BlockSpec(memory_space=pl.ANY),
                      pl.BlockSpec((1, D_MODEL, 2 * D_FF), wsel),
                      pl.BlockSpec((1, D_FF, D_MODEL), wsel)],
            out_specs=pl.BlockSpec((MOE_TM, D_MODEL), row),
            scratch_shapes=[pltpu.VMEM((2, MOE_TM + SUBLANES, D_MODEL), F32),
                            pltpu.SemaphoreType.DMA((2,))]),
        out_shape=jax.ShapeDtypeStruct((n_tiles * MOE_TM, D_MODEL), F32),
        compiler_params=pltpu.CompilerParams(
            dimension_semantics=("arbitrary",), vmem_limit_bytes=VMEM_LIMIT),
        name="moe_experts",
    )(tile_expert, n_valid, src_tok, src_tok, h, w13b, w2b)


def _combine_kernel(pos_ref, pos_next_ref, route_ref, h_ref, g_ref, b_ref, ys_ref, o_ref,
                    ya0_ref, yb0_ref, ya1_ref, yb1_ref, sems):
    tb = h_ref.shape[0]
    i = pl.program_id(0)
    bufs = ((ya0_ref, yb0_ref), (ya1_ref, yb1_ref))

    def gather_wait(s):
        for k in range(TOP_K):
            pltpu.make_async_copy(ys_ref.at[pl.ds(0, tb), :], bufs[s][k], sems.at[s]).wait()

    @pl.when(i == 0)
    def _():
        def first(t, carry):
            for k in range(TOP_K):
                _row_copy(ys_ref, pos_ref[0, 0, k * tb + t], bufs[0][k], t, sems.at[0]).start()
            return carry
        lax.fori_loop(0, tb, first, 0, unroll=8)

    for par in range(2):
        @pl.when(i % 2 == par)
        def _():
            gather_wait(par)
            for t in range(tb):
                for k in range(TOP_K):
                    _row_copy(ys_ref, pos_next_ref[0, 0, k * tb + t], bufs[1 - par][k], t,
                              sems.at[1 - par]).start(priority=k)
            route = route_ref[...]
            ffn = route[:, 0:1] * bufs[par][0][...] + route[:, 1:2] * bufs[par][1][...]
            o_ref[...] = _layer_norm(ALPHA * h_ref[...] + ffn, g_ref[...], b_ref[...])

            @pl.when(i == pl.num_programs(0) - 1)
            def _():
                gather_wait(1 - par)


def _combine(pos3, route, h, ln_g, ln_b, ys):
    T = h.shape[0]
    tb = MOE_TB
    row = lambda i: (i, 0)
    const = lambda i: (0, 0)
    return pl.pallas_call(
        _combine_kernel,
        grid=(T // tb,),
        in_specs=[pl.BlockSpec((1, 1, 2 * tb), lambda i: (i, 0, 0), memory_space=pltpu.SMEM),
                  pl.BlockSpec((1, 1, 2 * tb), lambda i: (jnp.minimum(i + 1, T // tb - 1), 0, 0),
                               memory_space=pltpu.SMEM),
                  pl.BlockSpec((tb, LANES), row),
                  pl.BlockSpec((tb, D_MODEL), row),
                  pl.BlockSpec((1, D_MODEL), const),
                  pl.BlockSpec((1, D_MODEL), const),
                  pl.BlockSpec(memory_space=pl.ANY)],
        out_specs=pl.BlockSpec((tb, D_MODEL), row),
        out_shape=jax.ShapeDtypeStruct((T, D_MODEL), F32),
        scratch_shapes=[pltpu.VMEM((tb, D_MODEL), F32)] * (2 * TOP_K) + [pltpu.SemaphoreType.DMA((2,))],
        compiler_params=pltpu.CompilerParams(
            dimension_semantics=("arbitrary",), vmem_limit_bytes=VMEM_LIMIT),
        name="moe_combine_ln",
    )(pos3, pos3, route, h, ln_g, ln_b, ys)


def _routed_moe(h, route, route_t, counts, w13b, w2b, ln_g, ln_b):
    T = h.shape[0]
    n_tiles = (TOP_K * T) // MOE_TM + N_EXPERTS
    i32 = jnp.int32
    cnt = counts[:, 0].astype(i32)
    ntile = (cnt + MOE_TM - 1) // MOE_TM
    tile_end = jnp.cumsum(ntile)
    n_valid = tile_end[-1:]
    seg_start = (tile_end - ntile) * MOE_TM
    info = route_t[2:6].astype(i32)
    eids = jnp.arange(N_EXPERTS, dtype=i32)
    start_of = lambda e: jnp.sum(jnp.where(e[None, :] == eids[:, None], seg_start[:, None], 0), axis=0)
    pos1 = start_of(info[0]) + info[2]
    pos2 = start_of(info[1]) + info[3]
    pos3 = jnp.concatenate([pos1.reshape(T // MOE_TB, 1, MOE_TB), pos2.reshape(T // MOE_TB, 1, MOE_TB)], axis=2)
    tile_ids = jnp.minimum(jnp.arange(n_tiles, dtype=i32), n_valid[0] - 1)
    tile_expert = jnp.sum((tile_ids[:, None] >= tile_end[None, :]).astype(i32), axis=1)

    order = jnp.argsort(jnp.concatenate([pos1, pos2])).astype(i32)
    n_rows = n_tiles * MOE_TM
    padded = jnp.concatenate([jnp.zeros((n_rows,), i32), jnp.where(order >= T, order - T, order),
                              jnp.zeros((2 * n_rows - TOP_K * T,), i32)])
    shift = seg_start - (jnp.cumsum(cnt) - cnt)
    row_expert = jnp.repeat(tile_expert, MOE_TM)
    src_tok = jnp.zeros((n_rows,), i32)
    for e in range(N_EXPERTS):
        view = lax.dynamic_slice(padded, (n_rows - shift[e],), (n_rows,))
        src_tok = jnp.where(row_expert == e, view, src_tok)

    ys = _experts(tile_expert, n_valid.astype(i32), src_tok.reshape(n_tiles, 1, MOE_TM), h, w13b, w2b)
    return _combine(pos3, route, h, ln_g, ln_b, ys)


def _retention_constants():
    hh = np.arange(HEADS, dtype=np.float64)
    log_gamma = np.log1p(-np.exp2(-5.0 - hh))
    pos = np.arange(CHUNK, dtype=np.float64)
    diff = pos[:, None] - pos[None, :]
    causal = diff >= 0
    decay = np.where(causal[None], np.exp(log_gamma[:, None, None] * np.where(causal, diff, 0.0)[None]), 0.0)
    zeta = np.exp(log_gamma[:, None] * (CHUNK - 1 - pos)[None])
    xi = np.exp(log_gamma[:, None] * (pos + 1.0)[None])
    cdec = np.exp(log_gamma * CHUNK)
    bc = lambda t: np.broadcast_to(t[:, :, None], (HEADS, CHUNK, LANES))
    return (jnp.asarray(cdec, F32), jnp.asarray(decay, F32),
            jnp.asarray(bc(zeta), F32), jnp.asarray(bc(xi), F32))


def _rope_tables(seq):
    f32 = np.float32
    inv = np.power(f32(ROPE_BASE), -np.arange(0, HEAD_DIM, 2, dtype=f32) / f32(HEAD_DIM)).astype(f32)
    ang = (np.arange(seq, dtype=f32)[:, None] * inv[None, :]).astype(f32).astype(np.float64)
    cos = np.repeat(np.cos(ang), 2, axis=-1)
    sin = np.sin(ang)
    sin_signed = np.stack([-sin, sin], axis=-1).reshape(seq, HEAD_DIM)
    return jnp.asarray(cos, F32), jnp.asarray(sin_signed, F32)


def kernel(x, w_in, ret_norm_g, ret_norm_b, conv_w, conv_b, conv_norm_g, conv_norm_b, w_out, ln1_g, ln1_b, w_router_group, b_router_group, w_router_expert, b_router_expert, w1, w3, w2, ln2_g, ln2_b):
    batch, seq, _ = x.shape
    T = batch * seq
    x2 = x.reshape(T, D_MODEL)

    w_heads = _regroup_w_in(w_in)

    cos_t, sin_t = _rope_tables(seq)
    head_params = jnp.stack([ret_norm_g, ret_norm_b, conv_b, conv_norm_g, conv_norm_b]
                            + [jnp.zeros_like(conv_b)] * (SUBLANES - 5)).reshape(SUBLANES, HEADS, LANES)
    head_params = jnp.transpose(head_params, (1, 0, 2))
    cw = jnp.transpose(conv_w.reshape(CONV_TAPS, CONV_GROUPS, LANES), (1, 0, 2))
    ret3, conv3, w13b, w2b, w_out_b = _mixer(
        x2, w_heads, cos_t, sin_t, _retention_constants(), head_params, cw, w1, w3, w2, w_out, batch, seq)

    wr = jnp.concatenate([jnp.transpose(w_router_expert, (1, 0, 2)).reshape(D_MODEL, N_EXPERTS),
                          w_router_group], axis=1)
    wr = jnp.pad(wr, ((0, 0), (0, LANES - wr.shape[1])))
    br = jnp.pad(jnp.concatenate([b_router_expert.reshape(-1), b_router_group]),
                 (0, LANES - N_EXPERTS - N_GROUPS)).reshape(1, LANES)

    h, route, route_t, counts = _out_proj(ret3, conv3, x2, w_out_b, ln1_g.reshape(1, -1),
                                          ln1_b.reshape(1, -1), wr.astype(BF16), br)

    y = _routed_moe(h, route, route_t, counts, w13b.reshape(N_EXPERTS, D_MODEL, 2 * D_FF),
                    w2b.reshape(w2.shape), ln2_g.reshape(1, -1), ln2_b.reshape(1, -1))
    return y.reshape(batch, seq, D_MODEL)
```

```python
import functools

import numpy as np
import jax
import jax.numpy as jnp
from jax import lax
from jax.experimental import pallas as pl
from jax.experimental.pallas import tpu as pltpu

D_MODEL = 2048
RET_WIDTH = D_MODEL // 2
CONV_WIDTH = D_MODEL - RET_WIDTH
HEADS = 8
HEAD_DIM = RET_WIDTH // HEADS
CONV_GROUPS = 8
CONV_TAPS = 31
CHUNK = 128
ROPE_BASE = 10000.0
IN_COLS = 4 * RET_WIDTH + 2 * CONV_WIDTH
N_GROUPS = 4
EXPERTS_PER_GROUP = 4
N_EXPERTS = N_GROUPS * EXPERTS_PER_GROUP
TOP_K = 2
D_FF = D_MODEL // 2
LN_EPS = 1e-5
ALPHA = 2.0 ** 0.25

LANES = 128
SUBLANES = 8
VMEM_LIMIT = 56 * 1024 * 1024
CONV_HALO = 32

F32 = jnp.float32
BF16 = jnp.bfloat16


def _sigmoid(x):
    return 1.0 / (1.0 + jnp.exp(-x))


def _lane_norm(x, g, b):
    mu = jnp.mean(x, axis=-1, keepdims=True)
    xc = x - mu
    var = jnp.mean(xc * xc, axis=-1, keepdims=True)
    return xc * lax.rsqrt(var + LN_EPS) * g + b


HEADS_PER_STEP = 2
MIX_ROWS = 512
HEAD_COLS = 6 * LANES
OUT_PROJ_SUB = 256


def _regroup_kernel(*refs):
    o_ref = refs[-1]
    for c, w_ref in enumerate(refs[:-1]):
        o_ref[:, c * LANES:(c + 1) * LANES] = w_ref[...].astype(BF16)


def _regroup_w_in(w_in):
    kinds = HEAD_COLS // LANES
    return pl.pallas_call(
        _regroup_kernel,
        grid=(HEADS,),
        in_specs=[pl.BlockSpec((D_MODEL, LANES), functools.partial(lambda c, hd: (0, c * HEADS + hd), c))
                  for c in range(kinds)],
        out_specs=pl.BlockSpec((D_MODEL, HEAD_COLS), lambda hd: (0, hd)),
        out_shape=jax.ShapeDtypeStruct((D_MODEL, HEADS * HEAD_COLS), BF16),
        compiler_params=pltpu.CompilerParams(dimension_semantics=("arbitrary",)),
        name="w_in_regroup",
    )(*([w_in] * kinds))


def _rotary(t, cos, sin_signed, even_lane):
    partner = jnp.where(even_lane, pltpu.roll(t, LANES - 1, 1), pltpu.roll(t, 1, 1))
    return t * cos + partner * sin_signed


def _mixer_kernel(cdec_ref, x_ref, w_ref, cos_ref, sin_ref, decay_ref, zeta_ref, xi_ref, prm_ref, cw_ref,
                  w1f_ref, w3f_ref, w2f_ref, wof_ref,
                  ret_ref, conv_ref, w13b_ref, w2b_ref, wob_ref, state_ref, uext_ref):
    w13b_ref[:, 0:D_FF] = w1f_ref[...].astype(BF16)
    w13b_ref[:, D_FF:] = w3f_ref[...].astype(BF16)
    w2b_ref[...] = w2f_ref[...].astype(BF16)
    wob_ref[...] = wof_ref[...].astype(BF16)

    rows = x_ref.shape[0]
    hps = w_ref.shape[1] // HEAD_COLS
    n = pl.program_id(2)
    head0 = pl.program_id(1) * hps

    @pl.when(n == 0)
    def _():
        for j in range(hps):
            state_ref[head0 + j] = jnp.zeros((HEAD_DIM, HEAD_DIM), F32)
            uext_ref[head0 + j, 0:CONV_HALO, :] = jnp.zeros((CONV_HALO, LANES), F32)

    even_lane = (lax.broadcasted_iota(jnp.int32, (CHUNK, LANES), 1) % 2) == 0
    scale = HEAD_DIM ** -0.5
    blk = 64
    first = CONV_HALO - (CONV_TAPS - 1)
    xb = x_ref[...].astype(BF16)

    for j in range(hps):
        head = head0 + j
        cdec = cdec_ref[head]
        decay = decay_ref[j]
        zeta = zeta_ref[j]
        xi = xi_ref[j]
        rg, rb, cb, cg, cbeta = (prm_ref[j, r:r + 1, :] for r in range(5))
        proj = jnp.dot(xb, w_ref[:, j * HEAD_COLS:(j + 1) * HEAD_COLS], preferred_element_type=F32)
        col = lambda c: proj[:, c * LANES:(c + 1) * LANES]

        state = state_ref[head]
        for c in range(rows // CHUNK):
            rs = slice(c * CHUNK, (c + 1) * CHUNK)
            cos = cos_ref[rs, :]
            sin = sin_ref[rs, :]
            q = _rotary(col(0)[rs], cos, sin, even_lane)
            k = _rotary(col(1)[rs], cos, sin, even_lane) * scale
            v = col(2)[rs].astype(BF16)
            scores = lax.dot_general(q.astype(BF16), k.astype(BF16), (((1,), (1,)), ((), ())),
                                     preferred_element_type=F32) * decay
            inner = jnp.dot(scores.astype(BF16), v, preferred_element_type=F32)
            cross = jnp.dot((q * xi).astype(BF16), state.astype(BF16), preferred_element_type=F32)
            kz_t = jnp.transpose(k * zeta).astype(BF16)
            state = cdec * state + jnp.dot(kz_t, v, preferred_element_type=F32)
            y = _lane_norm(inner + cross, rg, rb)
            g = col(3)[rs]
            ret_ref[j, rs, :] = (g * _sigmoid(g) * y).astype(BF16)
        state_ref[head] = state

        uext_ref[head, CONV_HALO:CONV_HALO + rows, :] = col(4) * _sigmoid(col(5))
        for r0 in range(0, rows, blk):
            acc = cb
            for t in range(CONV_TAPS):
                acc = acc + cw_ref[j, t:t + 1, :] * uext_ref[head, r0 + first + t:r0 + first + t + blk, :]
            y = _lane_norm(acc, cg, cbeta)
            conv_ref[j, r0:r0 + blk, :] = (y * _sigmoid(y)).astype(BF16)
        uext_ref[head, 0:CONV_HALO, :] = uext_ref[head, rows:rows + CONV_HALO, :]


def _mixer(x2, w_heads, cos_t, sin_t, consts, head_params, cw, w1, w3, w2, w_out, batch, seq):
    T = batch * seq
    rows, hps = MIX_ROWS, HEADS_PER_STEP
    ns = seq // rows
    ng = HEADS // hps
    cdec, decay, zeta, xi = consts
    steps = batch * ns * ng
    w1s = w1.reshape(-1, w1.shape[-1])
    w3s = w3.reshape(-1, w3.shape[-1])
    w2s = w2.reshape(-1, w2.shape[-1])
    step_of = lambda b, g, n: ((b * ng + g) * ns + n, 0)
    wspec = lambda w: pl.BlockSpec((w.shape[0] // steps, w.shape[1]), step_of)
    for w in (w1s, w3s, w2s, w_out):
        assert w.shape[0] % (steps * 16) == 0

    def hspec(d1, d2=LANES):
        return pl.BlockSpec((hps, d1, d2), lambda b, g, n: (g, 0, 0))

    out_rows = pl.BlockSpec((hps, rows, LANES), lambda b, g, n: (g, b * ns + n, 0))
    return pl.pallas_call(
        _mixer_kernel,
        grid=(batch, ng, ns),
        in_specs=[pl.BlockSpec(memory_space=pltpu.SMEM),
                  pl.BlockSpec((rows, D_MODEL), lambda b, g, n: (b * ns + n, 0)),
                  pl.BlockSpec((D_MODEL, hps * HEAD_COLS), lambda b, g, n: (0, g)),
                  pl.BlockSpec((rows, LANES), lambda b, g, n: (n, 0)),
                  pl.BlockSpec((rows, LANES), lambda b, g, n: (n, 0)),
                  hspec(CHUNK), hspec(CHUNK), hspec(CHUNK),
                  hspec(SUBLANES), hspec(CONV_TAPS),
                  wspec(w1s), wspec(w3s), wspec(w2s), wspec(w_out)],
        out_specs=[out_rows, out_rows,
                   pl.BlockSpec((w1s.shape[0] // steps, 2 * D_FF), step_of), wspec(w2s), wspec(w_out)],
        out_shape=[jax.ShapeDtypeStruct((HEADS, T, LANES), BF16),
                   jax.ShapeDtypeStruct((CONV_GROUPS, T, LANES), BF16),
                   jax.ShapeDtypeStruct((w1s.shape[0], 2 * D_FF), BF16),
                   jax.ShapeDtypeStruct(w2s.shape, BF16),
                   jax.ShapeDtypeStruct(w_out.shape, BF16)],
        scratch_shapes=[pltpu.VMEM((HEADS, HEAD_DIM, HEAD_DIM), F32),
                        pltpu.VMEM((HEADS, rows + CONV_HALO, LANES), F32)],
        compiler_params=pltpu.CompilerParams(
            dimension_semantics=("arbitrary", "arbitrary", "arbitrary"), vmem_limit_bytes=VMEM_LIMIT),
        name="in_proj_mixer",
    )(cdec, x2, w_heads, cos_t, sin_t, decay, zeta, xi, head_params, cw, w1s, w3s, w2s, w_out)


def _layer_norm(z, g, b):
    mu = jnp.mean(z, axis=-1, keepdims=True)
    zc = z - mu
    var = jnp.mean(zc * zc, axis=-1, keepdims=True)
    return zc * lax.rsqrt(var + LN_EPS) * g + b


def _outproj_kernel(ret_ref, conv_ref, x_ref, w_ref, g_ref, b_ref, wr_ref, br_ref,
                    h_ref, route_ref, route_t_ref, counts_ref, carry_ref):
    @pl.when(pl.program_id(0) == 0)
    def _():
        carry_ref[...] = jnp.zeros_like(carry_ref)

    def exact_zero(v):
        return pltpu.bitcast((pltpu.bitcast(v, jnp.uint32) >> 16) >> 16, F32)

    def mix_of(rs):
        return jnp.concatenate([ret_ref[s, rs, :] for s in range(HEADS)] +
                               [conv_ref[s, rs, :] for s in range(CONV_GROUPS)], axis=-1)

    half = D_MODEL // 2
    rs_a, rs_b = (slice(r0, r0 + OUT_PROJ_SUB) for r0 in range(0, x_ref.shape[0], OUT_PROJ_SUB))
    y_a = jnp.dot(mix_of(rs_a), w_ref[...], preferred_element_type=F32)
    mix_b = mix_of(rs_b)
    y_b1 = jnp.dot(mix_b, w_ref[:, 0:half], preferred_element_type=F32)
    h_a = _layer_norm(y_a + ALPHA * x_ref[rs_a, :], g_ref[...], b_ref[...])
    h_ref[rs_a, :] = h_a
    y_b2 = (jnp.dot(mix_b, w_ref[:, half:], preferred_element_type=F32)
            + jnp.tile(exact_zero(h_a[0:SUBLANES, 0:LANES]), (OUT_PROJ_SUB // SUBLANES, half // LANES)))
    carry = carry_ref[:, 0:1]
    carry = _route(rs_a, h_a, exact_zero(y_b1[0:SUBLANES, half - LANES:half]), carry, wr_ref, br_ref,
                   route_ref, route_t_ref)
    h_b = _layer_norm(jnp.concatenate([y_b1, y_b2], axis=-1) + ALPHA * x_ref[rs_b, :], g_ref[...], b_ref[...])
    h_ref[rs_b, :] = h_b
    carry = _route(rs_b, h_b, None, carry, wr_ref, br_ref, route_ref, route_t_ref)
    carry_ref[...] = jnp.broadcast_to(carry, carry_ref.shape)
    counts_ref[...] = jnp.broadcast_to(carry, counts_ref.shape)


def _route(rs, h, zero, carry, wr_ref, br_ref, route_ref, route_t_ref):
    logits = jnp.dot(h.astype(BF16), wr_ref[...], preferred_element_type=F32) + br_ref[...]

    tm = logits.shape[0]
    lt = jnp.transpose(logits)
    neg = jnp.float32(-jnp.inf)
    big = jnp.float32(1e9)
    grow = lax.broadcasted_iota(jnp.int32, (SUBLANES, tm), 0).astype(F32)
    gmask = grow < N_GROUPS
    gl = jnp.where(gmask, lt[N_EXPERTS:N_EXPERTS + SUBLANES, :], neg)
    gmax = jnp.max(gl, axis=0, keepdims=True)
    gidx = jnp.min(jnp.where(gl == gmax, grow, big), axis=0, keepdims=True)
    g_w = 1.0 / jnp.sum(jnp.where(gmask, jnp.exp(gl - gmax), 0.0), axis=0, keepdims=True)
    erow = lax.broadcasted_iota(jnp.int32, (N_EXPERTS, tm), 0).astype(F32)
    lo = gidx * EXPERTS_PER_GROUP
    emask = (erow >= lo) & (erow < lo + EXPERTS_PER_GROUP)
    el = jnp.where(emask, lt[0:N_EXPERTS, :], neg)
    top1 = jnp.max(el, axis=0, keepdims=True)
    idx1 = jnp.min(jnp.where(el == top1, erow, big), axis=0, keepdims=True)
    el2 = jnp.where(erow == idx1, neg, el)
    top2 = jnp.max(el2, axis=0, keepdims=True)
    idx2 = jnp.min(jnp.where(el2 == top2, erow, big), axis=0, keepdims=True)
    e2 = jnp.exp(top2 - top1)
    w1 = g_w / (1.0 + e2)
    w2 = w1 * e2

    sel1 = erow == idx1
    sel2 = erow == idx2
    onehot = jnp.where(sel1, 1.0, 0.0) + jnp.where(sel2, 1.0, 0.0)
    r_i = lax.broadcasted_iota(jnp.int32, (tm, tm), 0)
    c_i = lax.broadcasted_iota(jnp.int32, (tm, tm), 1)
    earlier = jnp.where(r_i < c_i, 1.0, 0.0).astype(BF16)
    prefix = jnp.dot(onehot.astype(BF16), earlier, preferred_element_type=F32) + carry
    rank1 = jnp.sum(jnp.where(sel1, prefix, 0.0), axis=0, keepdims=True)
    rank2 = jnp.sum(jnp.where(sel2, prefix, 0.0), axis=0, keepdims=True)

    route_t = jnp.concatenate([w1, w2, idx1, idx2, rank1, rank2, jnp.zeros((2, tm), F32)], axis=0)
    if zero is not None:
        route_t = route_t + jnp.tile(zero, (1, tm // LANES))
    route_t_ref[:, rs] = route_t
    route_ref[rs, :] = jnp.transpose(
        jnp.concatenate([route_t, jnp.zeros((LANES - SUBLANES, tm), F32)], axis=0))
    return carry + jnp.sum(onehot, axis=1, keepdims=True)


def _out_proj(ret3, conv3, x2, w_out_b, ln_g, ln_b, wr_b, br, tm=512):
    T = x2.shape[0]
    cur = lambda i: (0, i, 0)
    row = lambda i: (i, 0)
    const = lambda i: (0, 0)
    return pl.pallas_call(
        _outproj_kernel,
        grid=(T // tm,),
        in_specs=[pl.BlockSpec((HEADS, tm, LANES), cur),
                  pl.BlockSpec((CONV_GROUPS, tm, LANES), cur),
                  pl.BlockSpec((tm, D_MODEL), row),
                  pl.BlockSpec((D_MODEL, D_MODEL), const),
                  pl.BlockSpec((1, D_MODEL), const),
                  pl.BlockSpec((1, D_MODEL), const),
                  pl.BlockSpec((D_MODEL, LANES), const),
                  pl.BlockSpec((1, LANES), const)],
        out_specs=[pl.BlockSpec((tm, D_MODEL), row),
                   pl.BlockSpec((tm, LANES), row),
                   pl.BlockSpec((SUBLANES, tm), lambda i: (0, i)),
                   pl.BlockSpec((N_EXPERTS, LANES), const)],
        out_shape=[jax.ShapeDtypeStruct((T, D_MODEL), F32),
                   jax.ShapeDtypeStruct((T, LANES), F32),
                   jax.ShapeDtypeStruct((SUBLANES, T), F32),
                   jax.ShapeDtypeStruct((N_EXPERTS, LANES), F32)],
        scratch_shapes=[pltpu.VMEM((N_EXPERTS, LANES), F32)],
        compiler_params=pltpu.CompilerParams(
            dimension_semantics=("arbitrary",), vmem_limit_bytes=VMEM_LIMIT),
        name="out_proj_ln_route",
    )(ret3, conv3, x2, w_out_b, ln_g, ln_b, wr_b, br)


MOE_TM = 512
MOE_TB = 512


def _row_copy(src_ref, src_row, dst_ref, dst_row, sem):
    return pltpu.make_async_copy(src_ref.at[pl.ds(src_row, 1), :], dst_ref.at[pl.ds(dst_row, 1), :], sem)


def _experts_kernel(te_ref, nv_ref, src_ref, src_next_ref, h_ref, w13_ref, w2_ref, y_ref, x_ref, sems):
    i = pl.program_id(0)
    last = pl.num_programs(0) - 1
    slot = i % 2
    used = i < nv_ref[0]

    def gather_wait(s):
        pltpu.make_async_copy(h_ref.at[pl.ds(0, MOE_TM), :], x_ref.at[s, pl.ds(0, MOE_TM), :], sems.at[s]).wait()

    @pl.when(i == 0)
    def _():
        x_ref[:, MOE_TM:, :] = jnp.zeros((2, SUBLANES, D_MODEL), F32)

        def first(t, carry):
            _row_copy(h_ref, src_ref[0, 0, t], x_ref.at[0], t, sems.at[0]).start()
            return carry
        lax.fori_loop(0, MOE_TM, first, 0, unroll=8)

    @pl.when(used)
    def _():
        gather_wait(slot)

        def issue_half(k, width):
            for t in range(k * MOE_TM // 2, (k + 1) * MOE_TM // 2):
                _row_copy(h_ref, src_next_ref[0, 0, t], x_ref.at[1 - slot], t, sems.at[1 - slot]).start(priority=1)
            spare = pltpu.bitcast(x_ref[1 - slot, MOE_TM:, k * LANES:(k + 1) * LANES], jnp.uint32)
            zero = pltpu.bitcast((spare >> 16) >> 16, F32)
            return jnp.tile(zero, (MOE_TM // SUBLANES, width // LANES))

        xb = x_ref[slot, 0:MOE_TM, :].astype(BF16)
        a13 = jnp.dot(xb, w13_ref[0], preferred_element_type=F32) + issue_half(0, 2 * D_FF)
        a1 = a13[:, 0:D_FF]
        a3 = a13[:, D_FF:]
        act = (a1 * _sigmoid(a1) * a3).astype(BF16)
        y = jnp.dot(act, w2_ref[0], preferred_element_type=F32) + issue_half(1, D_MODEL)
        y_ref[...] = pltpu.pack_elementwise([y[:, 0:D_MODEL // 2], y[:, D_MODEL // 2:]], packed_dtype=BF16)

        @pl.when(i == last)
        def _():
            gather_wait(1 - slot)

    @pl.when(jnp.logical_not(used))
    def _():
        @pl.when(i == nv_ref[0])
        def _():
            gather_wait(slot)
        y_ref[...] = jnp.zeros_like(y_ref)


def _experts(tile_expert, n_valid, src_tok, h, w13b, w2b):
    n_tiles = src_tok.shape[0]
    row = lambda i, te, nv: (i, 0)
    wsel = lambda i, te, nv: (te[i], 0, 0)
    smem_row = lambda f: pl.BlockSpec((1, 1, MOE_TM), f, memory_space=pltpu.SMEM)
    return pl.pallas_call(
        _experts_kernel,
        grid_spec=pltpu.PrefetchScalarGridSpec(
            num_scalar_prefetch=2,
            grid=(n_tiles,),
            in_specs=[smem_row(lambda i, te, nv: (i, 0, 0)),
                      smem_row(lambda i, te, nv: (jnp.minimum(i + 1, n_tiles - 1), 0, 0)),
                      pl.BlockSpec(memory_space=pl.ANY),
                      pl.BlockSpec((1, D_MODEL, 2 * D_FF), wsel),
                      pl.BlockSpec((1, D_FF, D_MODEL), wsel)],
            out_specs=pl.BlockSpec((MOE_TM, D_MODEL // 2), row),
            scratch_shapes=[pltpu.VMEM((2, MOE_TM + SUBLANES, D_MODEL), F32),
                            pltpu.SemaphoreType.DMA((2,))]),
        out_shape=jax.ShapeDtypeStruct((n_tiles * MOE_TM, D_MODEL // 2), jnp.uint32),
        compiler_params=pltpu.CompilerParams(
            dimension_semantics=("arbitrary",), vmem_limit_bytes=VMEM_LIMIT),
        name="moe_experts",
    )(tile_expert, n_valid, src_tok, src_tok, h, w13b, w2b)


def _combine_kernel(pos_ref, pos_next_ref, route_ref, h_ref, g_ref, b_ref, ys_ref, o_ref,
                    ya0_ref, yb0_ref, ya1_ref, yb1_ref, sems):
    tb = h_ref.shape[0]
    i = pl.program_id(0)
    bufs = ((ya0_ref, yb0_ref), (ya1_ref, yb1_ref))

    def gather_wait(s):
        for k in range(TOP_K):
            pltpu.make_async_copy(ys_ref.at[pl.ds(0, tb), :], bufs[s][k], sems.at[s]).wait()

    @pl.when(i == 0)
    def _():
        def first(t, carry):
            for k in range(TOP_K):
                _row_copy(ys_ref, pos_ref[0, 0, k * tb + t], bufs[0][k], t, sems.at[0]).start()
            return carry
        lax.fori_loop(0, tb, first, 0, unroll=8)

    for par in range(2):
        @pl.when(i % 2 == par)
        def _():
            gather_wait(par)
            for t in range(tb):
                for k in range(TOP_K):
                    _row_copy(ys_ref, pos_next_ref[0, 0, k * tb + t], bufs[1 - par][k], t,
                              sems.at[1 - par]).start(priority=k)
            route = route_ref[...]
            def expert_rows(buf):
                words = buf[...]
                return jnp.concatenate(
                    [pltpu.unpack_elementwise(words, index=half, packed_dtype=BF16, unpacked_dtype=F32)
                     for half in range(2)], axis=1)

            ffn = route[:, 0:1] * expert_rows(bufs[par][0]) + route[:, 1:2] * expert_rows(bufs[par][1])
            o_ref[...] = _layer_norm(ALPHA * h_ref[...] + ffn, g_ref[...], b_ref[...])

            @pl.when(i == pl.num_programs(0) - 1)
            def _():
                gather_wait(1 - par)


def _combine(pos3, route, h, ln_g, ln_b, ys):
    T = h.shape[0]
    tb = MOE_TB
    row = lambda i: (i, 0)
    const = lambda i: (0, 0)
    return pl.pallas_call(
        _combine_kernel,
        grid=(T // tb,),
        in_specs=[pl.BlockSpec((1, 1, 2 * tb), lambda i: (i, 0, 0), memory_space=pltpu.SMEM),
                  pl.BlockSpec((1, 1, 2 * tb), lambda i: (jnp.minimum(i + 1, T // tb - 1), 0, 0),
                               memory_space=pltpu.SMEM),
                  pl.BlockSpec((tb, LANES), row),
                  pl.BlockSpec((tb, D_MODEL), row),
                  pl.BlockSpec((1, D_MODEL), const),
                  pl.BlockSpec((1, D_MODEL), const),
                  pl.BlockSpec(memory_space=pl.ANY)],
        out_specs=pl.BlockSpec((tb, D_MODEL), row),
        out_shape=jax.ShapeDtypeStruct((T, D_MODEL), F32),
        scratch_shapes=([pltpu.VMEM((tb, D_MODEL // 2), jnp.uint32)] * (2 * TOP_K)
                        + [pltpu.SemaphoreType.DMA((2,))]),
        compiler_params=pltpu.CompilerParams(
            dimension_semantics=("arbitrary",), vmem_limit_bytes=VMEM_LIMIT),
        name="moe_combine_ln",
    )(pos3, pos3, route, h, ln_g, ln_b, ys)


def _routed_moe(h, route, route_t, counts, w13b, w2b, ln_g, ln_b):
    T = h.shape[0]
    n_tiles = (TOP_K * T) // MOE_TM + N_EXPERTS
    i32 = jnp.int32
    cnt = counts[:, 0].astype(i32)
    ntile = (cnt + MOE_TM - 1) // MOE_TM
    tile_end = jnp.cumsum(ntile)
    n_valid = tile_end[-1:]
    seg_start = (tile_end - ntile) * MOE_TM
    info = route_t[2:6].astype(i32)
    eids = jnp.arange(N_EXPERTS, dtype=i32)
    start_of = lambda e: jnp.sum(jnp.where(e[None, :] == eids[:, None], seg_start[:, None], 0), axis=0)
    pos1 = start_of(info[0]) + info[2]
    pos2 = start_of(info[1]) + info[3]
    pos3 = jnp.concatenate([pos1.reshape(T // MOE_TB, 1, MOE_TB), pos2.reshape(T // MOE_TB, 1, MOE_TB)], axis=2)
    tile_ids = jnp.minimum(jnp.arange(n_tiles, dtype=i32), n_valid[0] - 1)
    tile_expert = jnp.sum((tile_ids[:, None] >= tile_end[None, :]).astype(i32), axis=1)

    order = jnp.argsort(jnp.concatenate([pos1, pos2])).astype(i32)
    n_rows = n_tiles * MOE_TM
    padded = jnp.concatenate([jnp.zeros((n_rows,), i32), jnp.where(order >= T, order - T, order),
                              jnp.zeros((2 * n_rows - TOP_K * T,), i32)])
    shift = seg_start - (jnp.cumsum(cnt) - cnt)
    row_expert = jnp.repeat(tile_expert, MOE_TM)
    src_tok = jnp.zeros((n_rows,), i32)
    for e in range(N_EXPERTS):
        view = lax.dynamic_slice(padded, (n_rows - shift[e],), (n_rows,))
        src_tok = jnp.where(row_expert == e, view, src_tok)

    ys = _experts(tile_expert, n_valid.astype(i32), src_tok.reshape(n_tiles, 1, MOE_TM), h, w13b, w2b)
    return _combine(pos3, route, h, ln_g, ln_b, ys)


def _retention_constants():
    hh = np.arange(HEADS, dtype=np.float64)
    log_gamma = np.log1p(-np.exp2(-5.0 - hh))
    pos = np.arange(CHUNK, dtype=np.float64)
    diff = pos[:, None] - pos[None, :]
    causal = diff >= 0
    decay = np.where(causal[None], np.exp(log_gamma[:, None, None] * np.where(causal, diff, 0.0)[None]), 0.0)
    zeta = np.exp(log_gamma[:, None] * (CHUNK - 1 - pos)[None])
    xi = np.exp(log_gamma[:, None] * (pos + 1.0)[None])
    cdec = np.exp(log_gamma * CHUNK)
    bc = lambda t: np.broadcast_to(t[:, :, None], (HEADS, CHUNK, LANES))
    return (jnp.asarray(cdec, F32), jnp.asarray(decay, F32),
            jnp.asarray(bc(zeta), F32), jnp.asarray(bc(xi), F32))


def _rope_tables(seq):
    f32 = np.float32
    inv = np.power(f32(ROPE_BASE), -np.arange(0, HEAD_DIM, 2, dtype=f32) / f32(HEAD_DIM)).astype(f32)
    ang = (np.arange(seq, dtype=f32)[:, None] * inv[None, :]).astype(f32).astype(np.float64)
    cos = np.repeat(np.cos(ang), 2, axis=-1)
    sin = np.sin(ang)
    sin_signed = np.stack([-sin, sin], axis=-1).reshape(seq, HEAD_DIM)
    return jnp.asarray(cos, F32), jnp.asarray(sin_signed, F32)


def kernel(x, w_in, ret_norm_g, ret_norm_b, conv_w, conv_b, conv_norm_g, conv_norm_b, w_out, ln1_g, ln1_b, w_router_group, b_router_group, w_router_expert, b_router_expert, w1, w3, w2, ln2_g, ln2_b):
    batch, seq, _ = x.shape
    T = batch * seq
    x2 = x.reshape(T, D_MODEL)

    w_heads = _regroup_w_in(w_in)

    cos_t, sin_t = _rope_tables(seq)
    head_params = jnp.stack([ret_norm_g, ret_norm_b, conv_b, conv_norm_g, conv_norm_b]
                            + [jnp.zeros_like(conv_b)] * (SUBLANES - 5)).reshape(SUBLANES, HEADS, LANES)
    head_params = jnp.transpose(head_params, (1, 0, 2))
    cw = jnp.transpose(conv_w.reshape(CONV_TAPS, CONV_GROUPS, LANES), (1, 0, 2))
    ret3, conv3, w13b, w2b, w_out_b = _mixer(
        x2, w_heads, cos_t, sin_t, _retention_constants(), head_params, cw, w1, w3, w2, w_out, batch, seq)

    wr = jnp.concatenate([jnp.transpose(w_router_expert, (1, 0, 2)).reshape(D_MODEL, N_EXPERTS),
                          w_router_group], axis=1)
    wr = jnp.pad(wr, ((0, 0), (0, LANES - wr.shape[1])))
    br = jnp.pad(jnp.concatenate([b_router_expert.reshape(-1), b_router_group]),
                 (0, LANES - N_EXPERTS - N_GROUPS)).reshape(1, LANES)

    h, route, route_t, counts = _out_proj(ret3, conv3, x2, w_out_b, ln1_g.reshape(1, -1),
                                          ln1_b.reshape(1, -1), wr.astype(BF16), br)

    y = _routed_moe(h, route, route_t, counts, w13b.reshape(N_EXPERTS, D_MODEL, 2 * D_FF),
                    w2b.reshape(w2.shape), ln2_g.reshape(1, -1), ln2_b.reshape(1, -1))
    return y.reshape(batch, seq, D_MODEL)
```

```python
import functools

import numpy as np
import jax
import jax.numpy as jnp
from jax import lax
from jax.experimental import pallas as pl
from jax.experimental.pallas import tpu as pltpu

D_MODEL = 2048
RET_WIDTH = D_MODEL // 2
CONV_WIDTH = D_MODEL - RET_WIDTH
HEADS = 8
HEAD_DIM = RET_WIDTH // HEADS
CONV_GROUPS = 8
CONV_TAPS = 31
CHUNK = 128
ROPE_BASE = 10000.0
IN_COLS = 4 * RET_WIDTH + 2 * CONV_WIDTH
N_GROUPS = 4
EXPERTS_PER_GROUP = 4
N_EXPERTS = N_GROUPS * EXPERTS_PER_GROUP
TOP_K = 2
D_FF = D_MODEL // 2
LN_EPS = 1e-5
ALPHA = 2.0 ** 0.25

LANES = 128
SUBLANES = 8
VMEM_LIMIT = 56 * 1024 * 1024
CONV_HALO = 32

F32 = jnp.float32
BF16 = jnp.bfloat16


def _sigmoid(x):
    return 1.0 / (1.0 + jnp.exp(-x))


def _lane_norm(x, g, b):
    mu = jnp.mean(x, axis=-1, keepdims=True)
    xc = x - mu
    var = jnp.mean(xc * xc, axis=-1, keepdims=True)
    return xc * lax.rsqrt(var + LN_EPS) * g + b


HEADS_PER_STEP = 2
MIX_ROWS = 512
HEAD_COLS = 6 * LANES
OUT_PROJ_SUB = 256


def _regroup_kernel(*refs):
    o_ref = refs[-1]
    for c, w_ref in enumerate(refs[:-1]):
        o_ref[:, c * LANES:(c + 1) * LANES] = w_ref[...].astype(BF16)


def _regroup_w_in(w_in):
    kinds = HEAD_COLS // LANES
    return pl.pallas_call(
        _regroup_kernel,
        grid=(HEADS,),
        in_specs=[pl.BlockSpec((D_MODEL, LANES), functools.partial(lambda c, hd: (0, c * HEADS + hd), c))
                  for c in range(kinds)],
        out_specs=pl.BlockSpec((D_MODEL, HEAD_COLS), lambda hd: (0, hd)),
        out_shape=jax.ShapeDtypeStruct((D_MODEL, HEADS * HEAD_COLS), BF16),
        compiler_params=pltpu.CompilerParams(dimension_semantics=("arbitrary",)),
        name="w_in_regroup",
    )(*([w_in] * kinds))


def _rotary(t, cos, sin_signed, even_lane):
    partner = jnp.where(even_lane, pltpu.roll(t, LANES - 1, 1), pltpu.roll(t, 1, 1))
    return t * cos + partner * sin_signed


def _mixer_kernel(cdec_ref, x_ref, w_ref, cos_ref, sin_ref, decay_ref, zeta_ref, xi_ref, prm_ref, cw_ref,
                  w1f_ref, w3f_ref, w2f_ref, wof_ref,
                  ret_ref, conv_ref, w13b_ref, w2b_ref, wob_ref, state_ref, uext_ref):
    w13b_ref[:, 0:D_FF] = w1f_ref[...].astype(BF16)
    w13b_ref[:, D_FF:] = w3f_ref[...].astype(BF16)
    w2b_ref[...] = w2f_ref[...].astype(BF16)
    wob_ref[...] = wof_ref[...].astype(BF16)

    rows = x_ref.shape[0]
    hps = w_ref.shape[1] // HEAD_COLS
    n = pl.program_id(2)
    head0 = pl.program_id(1) * hps

    @pl.when(n == 0)
    def _():
        for j in range(hps):
            state_ref[head0 + j] = jnp.zeros((HEAD_DIM, HEAD_DIM), F32)
            uext_ref[head0 + j, 0:CONV_HALO, :] = jnp.zeros((CONV_HALO, LANES), F32)

    even_lane = (lax.broadcasted_iota(jnp.int32, (CHUNK, LANES), 1) % 2) == 0
    scale = HEAD_DIM ** -0.5
    blk = 64
    first = CONV_HALO - (CONV_TAPS - 1)
    xb = x_ref[...].astype(BF16)

    for j in range(hps):
        head = head0 + j
        cdec = cdec_ref[head]
        decay = decay_ref[j]
        zeta = zeta_ref[j]
        xi = xi_ref[j]
        rg, rb, cb, cg, cbeta = (prm_ref[j, r:r + 1, :] for r in range(5))
        proj = jnp.dot(xb, w_ref[:, j * HEAD_COLS:(j + 1) * HEAD_COLS], preferred_element_type=F32)
        col = lambda c: proj[:, c * LANES:(c + 1) * LANES]

        state = state_ref[head]
        for c in range(rows // CHUNK):
            rs = slice(c * CHUNK, (c + 1) * CHUNK)
            cos = cos_ref[rs, :]
            sin = sin_ref[rs, :]
            q = _rotary(col(0)[rs], cos, sin, even_lane)
            k = _rotary(col(1)[rs], cos, sin, even_lane) * scale
            v = col(2)[rs].astype(BF16)
            scores = lax.dot_general(q.astype(BF16), k.astype(BF16), (((1,), (1,)), ((), ())),
                                     preferred_element_type=F32) * decay
            inner = jnp.dot(scores.astype(BF16), v, preferred_element_type=F32)
            cross = jnp.dot((q * xi).astype(BF16), state.astype(BF16), preferred_element_type=F32)
            kz_t = jnp.transpose(k * zeta).astype(BF16)
            state = cdec * state + jnp.dot(kz_t, v, preferred_element_type=F32)
            y = _lane_norm(inner + cross, rg, rb)
            g = col(3)[rs]
            ret_ref[j, rs, :] = (g * _sigmoid(g) * y).astype(BF16)
        state_ref[head] = state

        uext_ref[head, CONV_HALO:CONV_HALO + rows, :] = col(4) * _sigmoid(col(5))
        for r0 in range(0, rows, blk):
            acc = cb
            for t in range(CONV_TAPS):
                acc = acc + cw_ref[j, t:t + 1, :] * uext_ref[head, r0 + first + t:r0 + first + t + blk, :]
            y = _lane_norm(acc, cg, cbeta)
            conv_ref[j, r0:r0 + blk, :] = (y * _sigmoid(y)).astype(BF16)
        uext_ref[head, 0:CONV_HALO, :] = uext_ref[head, rows:rows + CONV_HALO, :]


def _mixer(x2, w_heads, cos_t, sin_t, consts, head_params, cw, w1, w3, w2, w_out, batch, seq):
    T = batch * seq
    rows, hps = MIX_ROWS, HEADS_PER_STEP
    ns = seq // rows
    ng = HEADS // hps
    cdec, decay, zeta, xi = consts
    steps = batch * ns * ng
    w1s = w1.reshape(-1, w1.shape[-1])
    w3s = w3.reshape(-1, w3.shape[-1])
    w2s = w2.reshape(-1, w2.shape[-1])
    step_of = lambda b, g, n: ((b * ng + g) * ns + n, 0)
    wspec = lambda w: pl.BlockSpec((w.shape[0] // steps, w.shape[1]), step_of)
    for w in (w1s, w3s, w2s, w_out):
        assert w.shape[0] % (steps * 16) == 0

    def hspec(d1, d2=LANES):
        return pl.BlockSpec((hps, d1, d2), lambda b, g, n: (g, 0, 0))

    out_rows = pl.BlockSpec((hps, rows, LANES), lambda b, g, n: (g, b * ns + n, 0))
    return pl.pallas_call(
        _mixer_kernel,
        grid=(batch, ng, ns),
        in_specs=[pl.BlockSpec(memory_space=pltpu.SMEM),
                  pl.BlockSpec((rows, D_MODEL), lambda b, g, n: (b * ns + n, 0)),
                  pl.BlockSpec((D_MODEL, hps * HEAD_COLS), lambda b, g, n: (0, g)),
                  pl.BlockSpec((rows, LANES), lambda b, g, n: (n, 0)),
                  pl.BlockSpec((rows, LANES), lambda b, g, n: (n, 0)),
                  hspec(CHUNK), hspec(CHUNK), hspec(CHUNK),
                  hspec(SUBLANES), hspec(CONV_TAPS),
                  wspec(w1s), wspec(w3s), wspec(w2s), wspec(w_out)],
        out_specs=[out_rows, out_rows,
                   pl.BlockSpec((w1s.shape[0] // steps, 2 * D_FF), step_of), wspec(w2s), wspec(w_out)],
        out_shape=[jax.ShapeDtypeStruct((HEADS, T, LANES), BF16),
                   jax.ShapeDtypeStruct((CONV_GROUPS, T, LANES), BF16),
                   jax.ShapeDtypeStruct((w1s.shape[0], 2 * D_FF), BF16),
                   jax.ShapeDtypeStruct(w2s.shape, BF16),
                   jax.ShapeDtypeStruct(w_out.shape, BF16)],
        scratch_shapes=[pltpu.VMEM((HEADS, HEAD_DIM, HEAD_DIM), F32),
                        pltpu.VMEM((HEADS, rows + CONV_HALO, LANES), F32)],
        compiler_params=pltpu.CompilerParams(
            dimension_semantics=("arbitrary", "arbitrary", "arbitrary"), vmem_limit_bytes=VMEM_LIMIT),
        name="in_proj_mixer",
    )(cdec, x2, w_heads, cos_t, sin_t, decay, zeta, xi, head_params, cw, w1s, w3s, w2s, w_out)


def _layer_norm(z, g, b):
    mu = jnp.mean(z, axis=-1, keepdims=True)
    zc = z - mu
    var = jnp.mean(zc * zc, axis=-1, keepdims=True)
    return zc * lax.rsqrt(var + LN_EPS) * g + b


def _outproj_kernel(ret_ref, conv_ref, x_ref, w_ref, g_ref, b_ref, wr_ref, br_ref,
                    h_ref, route_ref, route_t_ref, counts_ref, carry_ref):
    @pl.when(pl.program_id(0) == 0)
    def _():
        carry_ref[...] = jnp.zeros_like(carry_ref)

    def exact_zero(v):
        return pltpu.bitcast((pltpu.bitcast(v, jnp.uint32) >> 16) >> 16, F32)

    def mix_of(rs):
        return jnp.concatenate([ret_ref[s, rs, :] for s in range(HEADS)] +
                               [conv_ref[s, rs, :] for s in range(CONV_GROUPS)], axis=-1)

    half = D_MODEL // 2
    rs_a, rs_b = (slice(r0, r0 + OUT_PROJ_SUB) for r0 in range(0, x_ref.shape[0], OUT_PROJ_SUB))
    y_a = jnp.dot(mix_of(rs_a), w_ref[...], preferred_element_type=F32)
    mix_b = mix_of(rs_b)
    y_b1 = jnp.dot(mix_b, w_ref[:, 0:half], preferred_element_type=F32)
    h_a = _layer_norm(y_a + ALPHA * x_ref[rs_a, :], g_ref[...], b_ref[...])
    h_ref[rs_a, :] = h_a
    y_b2 = (jnp.dot(mix_b, w_ref[:, half:], preferred_element_type=F32)
            + jnp.tile(exact_zero(h_a[0:SUBLANES, 0:LANES]), (OUT_PROJ_SUB // SUBLANES, half // LANES)))
    carry = carry_ref[:, 0:1]
    carry = _route(rs_a, h_a, exact_zero(y_b1[0:SUBLANES, half - LANES:half]), carry, wr_ref, br_ref,
                   route_ref, route_t_ref)
    h_b = _layer_norm(jnp.concatenate([y_b1, y_b2], axis=-1) + ALPHA * x_ref[rs_b, :], g_ref[...], b_ref[...])
    h_ref[rs_b, :] = h_b
    carry = _route(rs_b, h_b, None, carry, wr_ref, br_ref, route_ref, route_t_ref)
    carry_ref[...] = jnp.broadcast_to(carry, carry_ref.shape)
    counts_ref[...] = jnp.broadcast_to(carry, counts_ref.shape)


def _route(rs, h, zero, carry, wr_ref, br_ref, route_ref, route_t_ref):
    logits = jnp.dot(h.astype(BF16), wr_ref[...], preferred_element_type=F32) + br_ref[...]

    tm = logits.shape[0]
    lt = jnp.transpose(logits)
    neg = jnp.float32(-jnp.inf)
    big = jnp.float32(1e9)
    grow = lax.broadcasted_iota(jnp.int32, (SUBLANES, tm), 0).astype(F32)
    gmask = grow < N_GROUPS
    gl = jnp.where(gmask, lt[N_EXPERTS:N_EXPERTS + SUBLANES, :], neg)
    gmax = jnp.max(gl, axis=0, keepdims=True)
    gidx = jnp.min(jnp.where(gl == gmax, grow, big), axis=0, keepdims=True)
    g_w = 1.0 / jnp.sum(jnp.where(gmask, jnp.exp(gl - gmax), 0.0), axis=0, keepdims=True)
    erow = lax.broadcasted_iota(jnp.int32, (N_EXPERTS, tm), 0).astype(F32)
    lo = gidx * EXPERTS_PER_GROUP
    emask = (erow >= lo) & (erow < lo + EXPERTS_PER_GROUP)
    el = jnp.where(emask, lt[0:N_EXPERTS, :], neg)
    top1 = jnp.max(el, axis=0, keepdims=True)
    idx1 = jnp.min(jnp.where(el == top1, erow, big), axis=0, keepdims=True)
    el2 = jnp.where(erow == idx1, neg, el)
    top2 = jnp.max(el2, axis=0, keepdims=True)
    idx2 = jnp.min(jnp.where(el2 == top2, erow, big), axis=0, keepdims=True)
    e2 = jnp.exp(top2 - top1)
    w1 = g_w / (1.0 + e2)
    w2 = w1 * e2

    sel1 = erow == idx1
    sel2 = erow == idx2
    onehot = jnp.where(sel1, 1.0, 0.0) + jnp.where(sel2, 1.0, 0.0)
    r_i = lax.broadcasted_iota(jnp.int32, (tm, tm), 0)
    c_i = lax.broadcasted_iota(jnp.int32, (tm, tm), 1)
    earlier = jnp.where(r_i < c_i, 1.0, 0.0).astype(BF16)
    prefix = jnp.dot(onehot.astype(BF16), earlier, preferred_element_type=F32) + carry
    rank1 = jnp.sum(jnp.where(sel1, prefix, 0.0), axis=0, keepdims=True)
    rank2 = jnp.sum(jnp.where(sel2, prefix, 0.0), axis=0, keepdims=True)

    route_t = jnp.concatenate([w1, w2, idx1, idx2, rank1, rank2, jnp.zeros((2, tm), F32)], axis=0)
    if zero is not None:
        route_t = route_t + jnp.tile(zero, (1, tm // LANES))
    route_t_ref[:, rs] = route_t
    route_ref[rs, :] = jnp.transpose(
        jnp.concatenate([route_t, jnp.zeros((LANES - SUBLANES, tm), F32)], axis=0))
    return carry + jnp.sum(onehot, axis=1, keepdims=True)


def _out_proj(ret3, conv3, x2, w_out_b, ln_g, ln_b, wr_b, br, tm=512):
    T = x2.shape[0]
    cur = lambda i: (0, i, 0)
    row = lambda i: (i, 0)
    const = lambda i: (0, 0)
    return pl.pallas_call(
        _outproj_kernel,
        grid=(T // tm,),
        in_specs=[pl.BlockSpec((HEADS, tm, LANES), cur),
                  pl.BlockSpec((CONV_GROUPS, tm, LANES), cur),
                  pl.BlockSpec((tm, D_MODEL), row),
                  pl.BlockSpec((D_MODEL, D_MODEL), const),
                  pl.BlockSpec((1, D_MODEL), const),
                  pl.BlockSpec((1, D_MODEL), const),
                  pl.BlockSpec((D_MODEL, LANES), const),
                  pl.BlockSpec((1, LANES), const)],
        out_specs=[pl.BlockSpec((tm, D_MODEL), row),
                   pl.BlockSpec((tm, LANES), row),
                   pl.BlockSpec((SUBLANES, tm), lambda i: (0, i)),
                   pl.BlockSpec((N_EXPERTS, LANES), const)],
        out_shape=[jax.ShapeDtypeStruct((T, D_MODEL), F32),
                   jax.ShapeDtypeStruct((T, LANES), F32),
                   jax.ShapeDtypeStruct((SUBLANES, T), F32),
                   jax.ShapeDtypeStruct((N_EXPERTS, LANES), F32)],
        scratch_shapes=[pltpu.VMEM((N_EXPERTS, LANES), F32)],
        compiler_params=pltpu.CompilerParams(
            dimension_semantics=("arbitrary",), vmem_limit_bytes=VMEM_LIMIT),
        name="out_proj_ln_route",
    )(ret3, conv3, x2, w_out_b, ln_g, ln_b, wr_b, br)


MOE_TM = 512
MOE_TB = 512


def _row_copy(src_ref, src_row, dst_ref, dst_row, sem):
    return pltpu.make_async_copy(src_ref.at[pl.ds(src_row, 1), :], dst_ref.at[pl.ds(dst_row, 1), :], sem)


def _experts_kernel(te_ref, nv_ref, src_ref, src_next_ref, h_ref, w13_ref, w2_ref, y_ref, x_ref, sems):
    i = pl.program_id(0)
    last = pl.num_programs(0) - 1
    slot = i % 2
    used = i < nv_ref[0]

    def gather_wait(s):
        pltpu.make_async_copy(h_ref.at[pl.ds(0, MOE_TM), :], x_ref.at[s, pl.ds(0, MOE_TM), :], sems.at[s]).wait()

    @pl.when(i == 0)
    def _():
        x_ref[:, MOE_TM:, :] = jnp.zeros((2, SUBLANES, D_MODEL), F32)

        def first(t, carry):
            _row_copy(h_ref, src_ref[0, 0, t], x_ref.at[0], t, sems.at[0]).start()
            return carry
        lax.fori_loop(0, MOE_TM, first, 0, unroll=8)

    @pl.when(used)
    def _():
        gather_wait(slot)

        def issue_half(k, width):
            for t in range(k * MOE_TM // 2, (k + 1) * MOE_TM // 2):
                _row_copy(h_ref, src_next_ref[0, 0, t], x_ref.at[1 - slot], t, sems.at[1 - slot]).start(priority=1)
            spare = pltpu.bitcast(x_ref[1 - slot, MOE_TM:, k * LANES:(k + 1) * LANES], jnp.uint32)
            zero = pltpu.bitcast((spare >> 16) >> 16, F32)
            return jnp.tile(zero, (MOE_TM // SUBLANES, width // LANES))

        xb = x_ref[slot, 0:MOE_TM, :].astype(BF16)
        a13 = jnp.dot(xb, w13_ref[0], preferred_element_type=F32) + issue_half(0, 2 * D_FF)
        a1 = a13[:, 0:D_FF]
        a3 = a13[:, D_FF:]
        act = (a1 * _sigmoid(a1) * a3).astype(BF16)
        y = jnp.dot(act, w2_ref[0], preferred_element_type=F32) + issue_half(1, D_MODEL)
        y_ref[...] = pltpu.pack_elementwise([y[:, 0:D_MODEL // 2], y[:, D_MODEL // 2:]], packed_dtype=BF16)

        @pl.when(i == last)
        def _():
            gather_wait(1 - slot)

    @pl.when(jnp.logical_not(used))
    def _():
        @pl.when(i == nv_ref[0])
        def _():
            gather_wait(slot)
        y_ref[...] = jnp.zeros_like(y_ref)


def _experts(tile_expert, n_valid, src_tok, h, w13b, w2b):
    n_tiles = src_tok.shape[0]
    row = lambda i, te, nv: (i, 0)
    wsel = lambda i, te, nv: (te[i], 0, 0)
    smem_row = lambda f: pl.BlockSpec((1, 1, MOE_TM), f, memory_space=pltpu.SMEM)
    return pl.pallas_call(
        _experts_kernel,
        grid_spec=pltpu.PrefetchScalarGridSpec(
            num_scalar_prefetch=2,
            grid=(n_tiles,),
            in_specs=[smem_row(lambda i, te, nv: (i, 0, 0)),
                      smem_row(lambda i, te, nv: (jnp.minimum(i + 1, n_tiles - 1), 0, 0)),
                      pl.BlockSpec(memory_space=pl.ANY),
                      pl.BlockSpec((1, D_MODEL, 2 * D_FF), wsel),
                      pl.BlockSpec((1, D_FF, D_MODEL), wsel)],
            out_specs=pl.BlockSpec((MOE_TM, D_MODEL // 2), row),
            scratch_shapes=[pltpu.VMEM((2, MOE_TM + SUBLANES, D_MODEL), F32),
                            pltpu.SemaphoreType.DMA((2,))]),
        out_shape=jax.ShapeDtypeStruct((n_tiles * MOE_TM, D_MODEL // 2), jnp.uint32),
        compiler_params=pltpu.CompilerParams(
            dimension_semantics=("arbitrary",), vmem_limit_bytes=VMEM_LIMIT),
        name="moe_experts",
    )(tile_expert, n_valid, src_tok, src_tok, h, w13b, w2b)


def _combine_kernel(pos_ref, pos_next_ref, route_ref, h_ref, g_ref, b_ref, ys_ref, o_ref,
                    ya0_ref, yb0_ref, ya1_ref, yb1_ref, sems):
    tb = h_ref.shape[0]
    i = pl.program_id(0)
    bufs = ((ya0_ref, yb0_ref), (ya1_ref, yb1_ref))

    hb = tb // 2

    def gather_wait(s, hf):
        for k in range(TOP_K):
            pltpu.make_async_copy(ys_ref.at[pl.ds(0, hb), :], bufs[s][k].at[pl.ds(hf * hb, hb), :],
                                  sems.at[2 * s + hf]).wait()

    @pl.when(i == 0)
    def _():
        for hf in range(2):
            def first(t, carry):
                for k in range(TOP_K):
                    _row_copy(ys_ref, pos_ref[0, 0, k * tb + t], bufs[0][k], t, sems.at[hf]).start()
                return carry
            lax.fori_loop(hf * hb, (hf + 1) * hb, first, 0, unroll=8)

    def expert_rows(buf, rs):
        words = buf[rs, :]
        return jnp.concatenate(
            [pltpu.unpack_elementwise(words, index=half, packed_dtype=BF16, unpacked_dtype=F32)
             for half in range(2)], axis=1)

    for par in range(2):
        @pl.when(i % 2 == par)
        def _():
            for hf in range(2):
                rs = slice(hf * hb, (hf + 1) * hb)
                gather_wait(par, hf)
                for t in range(hf * hb, (hf + 1) * hb):
                    for k in range(TOP_K):
                        _row_copy(ys_ref, pos_next_ref[0, 0, k * tb + t], bufs[1 - par][k], t,
                                  sems.at[2 * (1 - par) + hf]).start(priority=k)
                route = route_ref[rs, :]
                ffn = (route[:, 0:1] * expert_rows(bufs[par][0], rs)
                       + route[:, 1:2] * expert_rows(bufs[par][1], rs))
                o_ref[rs, :] = _layer_norm(ALPHA * h_ref[rs, :] + ffn, g_ref[...], b_ref[...])

            @pl.when(i == pl.num_programs(0) - 1)
            def _():
                for hf in range(2):
                    gather_wait(1 - par, hf)


def _combine(pos3, route, h, ln_g, ln_b, ys):
    T = h.shape[0]
    tb = MOE_TB
    row = lambda i: (i, 0)
    const = lambda i: (0, 0)
    return pl.pallas_call(
        _combine_kernel,
        grid=(T // tb,),
        in_specs=[pl.BlockSpec((1, 1, 2 * tb), lambda i: (i, 0, 0), memory_space=pltpu.SMEM),
                  pl.BlockSpec((1, 1, 2 * tb), lambda i: (jnp.minimum(i + 1, T // tb - 1), 0, 0),
                               memory_space=pltpu.SMEM),
                  pl.BlockSpec((tb, LANES), row),
                  pl.BlockSpec((tb, D_MODEL), row),
                  pl.BlockSpec((1, D_MODEL), const),
                  pl.BlockSpec((1, D_MODEL), const),
                  pl.BlockSpec(memory_space=pl.ANY)],
        out_specs=pl.BlockSpec((tb, D_MODEL), row),
        out_shape=jax.ShapeDtypeStruct((T, D_MODEL), F32),
        scratch_shapes=([pltpu.VMEM((tb, D_MODEL // 2), jnp.uint32)] * (2 * TOP_K)
                        + [pltpu.SemaphoreType.DMA((4,))]),
        compiler_params=pltpu.CompilerParams(
            dimension_semantics=("arbitrary",), vmem_limit_bytes=VMEM_LIMIT),
        name="moe_combine_ln",
    )(pos3, pos3, route, h, ln_g, ln_b, ys)


def _routed_moe(h, route, route_t, counts, w13b, w2b, ln_g, ln_b):
    T = h.shape[0]
    n_tiles = (TOP_K * T) // MOE_TM + N_EXPERTS
    i32 = jnp.int32
    cnt = counts[:, 0].astype(i32)
    ntile = (cnt + MOE_TM - 1) // MOE_TM
    tile_end = jnp.cumsum(ntile)
    n_valid = tile_end[-1:]
    seg_start = (tile_end - ntile) * MOE_TM
    info = route_t[2:6].astype(i32)
    eids = jnp.arange(N_EXPERTS, dtype=i32)
    start_of = lambda e: jnp.sum(jnp.where(e[None, :] == eids[:, None], seg_start[:, None], 0), axis=0)
    pos1 = start_of(info[0]) + info[2]
    pos2 = start_of(info[1]) + info[3]
    pos3 = jnp.concatenate([pos1.reshape(T // MOE_TB, 1, MOE_TB), pos2.reshape(T // MOE_TB, 1, MOE_TB)], axis=2)
    tile_ids = jnp.minimum(jnp.arange(n_tiles, dtype=i32), n_valid[0] - 1)
    tile_expert = jnp.sum((tile_ids[:, None] >= tile_end[None, :]).astype(i32), axis=1)

    order = jnp.argsort(jnp.concatenate([pos1, pos2])).astype(i32)
    n_rows = n_tiles * MOE_TM
    padded = jnp.concatenate([jnp.zeros((n_rows,), i32), jnp.where(order >= T, order - T, order),
                              jnp.zeros((2 * n_rows - TOP_K * T,), i32)])
    shift = seg_start - (jnp.cumsum(cnt) - cnt)
    row_expert = jnp.repeat(tile_expert, MOE_TM)
    src_tok = jnp.zeros((n_rows,), i32)
    for e in range(N_EXPERTS):
        view = lax.dynamic_slice(padded, (n_rows - shift[e],), (n_rows,))
        src_tok = jnp.where(row_expert == e, view, src_tok)

    ys = _experts(tile_expert, n_valid.astype(i32), src_tok.reshape(n_tiles, 1, MOE_TM), h, w13b, w2b)
    return _combine(pos3, route, h, ln_g, ln_b, ys)


def _retention_constants():
    hh = np.arange(HEADS, dtype=np.float64)
    log_gamma = np.log1p(-np.exp2(-5.0 - hh))
    pos = np.arange(CHUNK, dtype=np.float64)
    diff = pos[:, None] - pos[None, :]
    causal = diff >= 0
    decay = np.where(causal[None], np.exp(log_gamma[:, None, None] * np.where(causal, diff, 0.0)[None]), 0.0)
    zeta = np.exp(log_gamma[:, None] * (CHUNK - 1 - pos)[None])
    xi = np.exp(log_gamma[:, None] * (pos + 1.0)[None])
    cdec = np.exp(log_gamma * CHUNK)
    bc = lambda t: np.broadcast_to(t[:, :, None], (HEADS, CHUNK, LANES))
    return (jnp.asarray(cdec, F32), jnp.asarray(decay, F32),
            jnp.asarray(bc(zeta), F32), jnp.asarray(bc(xi), F32))


def _rope_tables(seq):
    f32 = np.float32
    inv = np.power(f32(ROPE_BASE), -np.arange(0, HEAD_DIM, 2, dtype=f32) / f32(HEAD_DIM)).astype(f32)
    ang = (np.arange(seq, dtype=f32)[:, None] * inv[None, :]).astype(f32).astype(np.float64)
    cos = np.repeat(np.cos(ang), 2, axis=-1)
    sin = np.sin(ang)
    sin_signed = np.stack([-sin, sin], axis=-1).reshape(seq, HEAD_DIM)
    return jnp.asarray(cos, F32), jnp.asarray(sin_signed, F32)


def kernel(x, w_in, ret_norm_g, ret_norm_b, conv_w, conv_b, conv_norm_g, conv_norm_b, w_out, ln1_g, ln1_b, w_router_group, b_router_group, w_router_expert, b_router_expert, w1, w3, w2, ln2_g, ln2_b):
    batch, seq, _ = x.shape
    T = batch * seq
    x2 = x.reshape(T, D_MODEL)

    w_heads = _regroup_w_in(w_in)

    cos_t, sin_t = _rope_tables(seq)
    head_params = jnp.stack([ret_norm_g, ret_norm_b, conv_b, conv_norm_g, conv_norm_b]
                            + [jnp.zeros_like(conv_b)] * (SUBLANES - 5)).reshape(SUBLANES, HEADS, LANES)
    head_params = jnp.transpose(head_params, (1, 0, 2))
    cw = jnp.transpose(conv_w.reshape(CONV_TAPS, CONV_GROUPS, LANES), (1, 0, 2))
    ret3, conv3, w13b, w2b, w_out_b = _mixer(
        x2, w_heads, cos_t, sin_t, _retention_constants(), head_params, cw, w1, w3, w2, w_out, batch, seq)

    wr = jnp.concatenate([jnp.transpose(w_router_expert, (1, 0, 2)).reshape(D_MODEL, N_EXPERTS),
                          w_router_group], axis=1)
    wr = jnp.pad(wr, ((0, 0), (0, LANES - wr.shape[1])))
    br = jnp.pad(jnp.concatenate([b_router_expert.reshape(-1), b_router_group]),
                 (0, LANES - N_EXPERTS - N_GROUPS)).reshape(1, LANES)

    h, route, route_t, counts = _out_proj(ret3, conv3, x2, w_out_b, ln1_g.reshape(1, -1),
                                          ln1_b.reshape(1, -1), wr.astype(BF16), br)

    y = _routed_moe(h, route, route_t, counts, w13b.reshape(N_EXPERTS, D_MODEL, 2 * D_FF),
                    w2b.reshape(w2.shape), ln2_g.reshape(1, -1), ln2_b.reshape(1, -1))
    return y.reshape(batch, seq, D_MODEL)
```

```python
import functools

import numpy as np
import jax
import jax.numpy as jnp
from jax import lax
from jax.experimental import pallas as pl
from jax.experimental.pallas import tpu as pltpu

D_MODEL = 2048
RET_WIDTH = D_MODEL // 2
CONV_WIDTH = D_MODEL - RET_WIDTH
HEADS = 8
HEAD_DIM = RET_WIDTH // HEADS
CONV_GROUPS = 8
CONV_TAPS = 31
CHUNK = 128
ROPE_BASE = 10000.0
IN_COLS = 4 * RET_WIDTH + 2 * CONV_WIDTH
N_GROUPS = 4
EXPERTS_PER_GROUP = 4
N_EXPERTS = N_GROUPS * EXPERTS_PER_GROUP
TOP_K = 2
D_FF = D_MODEL // 2
LN_EPS = 1e-5
ALPHA = 2.0 ** 0.25

LANES = 128
SUBLANES = 8
VMEM_LIMIT = 56 * 1024 * 1024
CONV_HALO = 32

F32 = jnp.float32
BF16 = jnp.bfloat16


def _sigmoid(x):
    return 1.0 / (1.0 + jnp.exp(-x))


def _lane_norm(x, g, b):
    mu = jnp.mean(x, axis=-1, keepdims=True)
    xc = x - mu
    var = jnp.mean(xc * xc, axis=-1, keepdims=True)
    return xc * lax.rsqrt(var + LN_EPS) * g + b


HEADS_PER_STEP = 2
MIX_ROWS = 512
HEAD_COLS = 6 * LANES
OUT_PROJ_SUB = 256


def _regroup_kernel(*refs):
    o_ref = refs[-1]
    for c, w_ref in enumerate(refs[:-1]):
        o_ref[:, c * LANES:(c + 1) * LANES] = w_ref[...].astype(BF16)


def _regroup_w_in(w_in):
    kinds = HEAD_COLS // LANES
    return pl.pallas_call(
        _regroup_kernel,
        grid=(HEADS,),
        in_specs=[pl.BlockSpec((D_MODEL, LANES), functools.partial(lambda c, hd: (0, c * HEADS + hd), c))
                  for c in range(kinds)],
        out_specs=pl.BlockSpec((D_MODEL, HEAD_COLS), lambda hd: (0, hd)),
        out_shape=jax.ShapeDtypeStruct((D_MODEL, HEADS * HEAD_COLS), BF16),
        compiler_params=pltpu.CompilerParams(dimension_semantics=("arbitrary",)),
        name="w_in_regroup",
    )(*([w_in] * kinds))


def _rotary(t, cos, sin_signed, even_lane):
    partner = jnp.where(even_lane, pltpu.roll(t, LANES - 1, 1), pltpu.roll(t, 1, 1))
    return t * cos + partner * sin_signed


def _mixer_kernel(cdec_ref, x_ref, w_ref, cos_ref, sin_ref, decay_ref, zeta_ref, xi_ref, prm_ref, cw_ref,
                  w1f_ref, w3f_ref, w2f_ref, wof_ref,
                  ret_ref, conv_ref, w13b_ref, w2b_ref, wob_ref, state_ref, uext_ref):
    w13b_ref[:, 0:D_FF] = w1f_ref[...].astype(BF16)
    w13b_ref[:, D_FF:] = w3f_ref[...].astype(BF16)
    w2b_ref[...] = w2f_ref[...].astype(BF16)
    wob_ref[...] = wof_ref[...].astype(BF16)

    rows = x_ref.shape[0]
    hps = w_ref.shape[1] // HEAD_COLS
    n = pl.program_id(2)
    head0 = pl.program_id(1) * hps

    @pl.when(n == 0)
    def _():
        for j in range(hps):
            state_ref[head0 + j] = jnp.zeros((HEAD_DIM, HEAD_DIM), F32)
            uext_ref[head0 + j, 0:CONV_HALO, :] = jnp.zeros((CONV_HALO, LANES), F32)

    even_lane = (lax.broadcasted_iota(jnp.int32, (CHUNK, LANES), 1) % 2) == 0
    scale = HEAD_DIM ** -0.5
    blk = 64
    first = CONV_HALO - (CONV_TAPS - 1)
    xb = x_ref[...].astype(BF16)

    for j in range(hps):
        head = head0 + j
        cdec = cdec_ref[head]
        decay = decay_ref[j]
        zeta = zeta_ref[j]
        xi = xi_ref[j]
        rg, rb, cb, cg, cbeta = (prm_ref[j, r:r + 1, :] for r in range(5))
        proj = jnp.dot(xb, w_ref[:, j * HEAD_COLS:(j + 1) * HEAD_COLS], preferred_element_type=F32)
        col = lambda c: proj[:, c * LANES:(c + 1) * LANES]

        state = state_ref[head]
        for c in range(rows // CHUNK):
            rs = slice(c * CHUNK, (c + 1) * CHUNK)
            cos = cos_ref[rs, :]
            sin = sin_ref[rs, :]
            q = _rotary(col(0)[rs], cos, sin, even_lane)
            k = _rotary(col(1)[rs], cos, sin, even_lane) * scale
            v = col(2)[rs].astype(BF16)
            scores = lax.dot_general(q.astype(BF16), k.astype(BF16), (((1,), (1,)), ((), ())),
                                     preferred_element_type=F32) * decay
            inner = jnp.dot(scores.astype(BF16), v, preferred_element_type=F32)
            cross = jnp.dot((q * xi).astype(BF16), state.astype(BF16), preferred_element_type=F32)
            kz_t = jnp.transpose(k * zeta).astype(BF16)
            state = cdec * state + jnp.dot(kz_t, v, preferred_element_type=F32)
            y = _lane_norm(inner + cross, rg, rb)
            g = col(3)[rs]
            ret_ref[j, rs, :] = (g * _sigmoid(g) * y).astype(BF16)
        state_ref[head] = state

        uext_ref[head, CONV_HALO:CONV_HALO + rows, :] = col(4) * _sigmoid(col(5))
        for r0 in range(0, rows, blk):
            acc = cb
            for t in range(CONV_TAPS):
                acc = acc + cw_ref[j, t:t + 1, :] * uext_ref[head, r0 + first + t:r0 + first + t + blk, :]
            y = _lane_norm(acc, cg, cbeta)
            conv_ref[j, r0:r0 + blk, :] = (y * _sigmoid(y)).astype(BF16)
        uext_ref[head, 0:CONV_HALO, :] = uext_ref[head, rows:rows + CONV_HALO, :]


def _mixer(x2, w_heads, cos_t, sin_t, consts, head_params, cw, w1, w3, w2, w_out, batch, seq):
    T = batch * seq
    rows, hps = MIX_ROWS, HEADS_PER_STEP
    ns = seq // rows
    ng = HEADS // hps
    cdec, decay, zeta, xi = consts
    steps = batch * ns * ng
    w1s = w1.reshape(-1, w1.shape[-1])
    w3s = w3.reshape(-1, w3.shape[-1])
    w2s = w2.reshape(-1, w2.shape[-1])
    step_of = lambda b, g, n: ((b * ng + g) * ns + n, 0)
    wspec = lambda w: pl.BlockSpec((w.shape[0] // steps, w.shape[1]), step_of)
    for w in (w1s, w3s, w2s, w_out):
        assert w.shape[0] % (steps * 16) == 0

    def hspec(d1, d2=LANES):
        return pl.BlockSpec((hps, d1, d2), lambda b, g, n: (g, 0, 0))

    out_rows = pl.BlockSpec((hps, rows, LANES), lambda b, g, n: (g, b * ns + n, 0))
    return pl.pallas_call(
        _mixer_kernel,
        grid=(batch, ng, ns),
        in_specs=[pl.BlockSpec(memory_space=pltpu.SMEM),
                  pl.BlockSpec((rows, D_MODEL), lambda b, g, n: (b * ns + n, 0)),
                  pl.BlockSpec((D_MODEL, hps * HEAD_COLS), lambda b, g, n: (0, g)),
                  pl.BlockSpec((rows, LANES), lambda b, g, n: (n, 0)),
                  pl.BlockSpec((rows, LANES), lambda b, g, n: (n, 0)),
                  hspec(CHUNK), hspec(CHUNK), hspec(CHUNK),
                  hspec(SUBLANES), hspec(CONV_TAPS),
                  wspec(w1s), wspec(w3s), wspec(w2s), wspec(w_out)],
        out_specs=[out_rows, out_rows,
                   pl.BlockSpec((w1s.shape[0] // steps, 2 * D_FF), step_of), wspec(w2s), wspec(w_out)],
        out_shape=[jax.ShapeDtypeStruct((HEADS, T, LANES), BF16),
                   jax.ShapeDtypeStruct((CONV_GROUPS, T, LANES), BF16),
                   jax.ShapeDtypeStruct((w1s.shape[0], 2 * D_FF), BF16),
                   jax.ShapeDtypeStruct(w2s.shape, BF16),
                   jax.ShapeDtypeStruct(w_out.shape, BF16)],
        scratch_shapes=[pltpu.VMEM((HEADS, HEAD_DIM, HEAD_DIM), F32),
                        pltpu.VMEM((HEADS, rows + CONV_HALO, LANES), F32)],
        compiler_params=pltpu.CompilerParams(
            dimension_semantics=("arbitrary", "arbitrary", "arbitrary"), vmem_limit_bytes=VMEM_LIMIT),
        name="in_proj_mixer",
    )(cdec, x2, w_heads, cos_t, sin_t, decay, zeta, xi, head_params, cw, w1s, w3s, w2s, w_out)


def _layer_norm(z, g, b):
    mu = jnp.mean(z, axis=-1, keepdims=True)
    zc = z - mu
    var = jnp.mean(zc * zc, axis=-1, keepdims=True)
    return zc * lax.rsqrt(var + LN_EPS) * g + b


def _outproj_kernel(ret_ref, conv_ref, x_ref, w_ref, g_ref, b_ref, wr_ref, br_ref,
                    h_ref, hp_ref, route_ref, route_t_ref, counts_ref, carry_ref):
    @pl.when(pl.program_id(0) == 0)
    def _():
        carry_ref[...] = jnp.zeros_like(carry_ref)

    def exact_zero(v):
        return pltpu.bitcast((pltpu.bitcast(v, jnp.uint32) >> 16) >> 16, F32)

    def mix_of(rs):
        return jnp.concatenate([ret_ref[s, rs, :] for s in range(HEADS)] +
                               [conv_ref[s, rs, :] for s in range(CONV_GROUPS)], axis=-1)

    half = D_MODEL // 2
    rs_a, rs_b = (slice(r0, r0 + OUT_PROJ_SUB) for r0 in range(0, x_ref.shape[0], OUT_PROJ_SUB))
    y_a = jnp.dot(mix_of(rs_a), w_ref[...], preferred_element_type=F32)
    mix_b = mix_of(rs_b)
    y_b1 = jnp.dot(mix_b, w_ref[:, 0:half], preferred_element_type=F32)
    h_a = _layer_norm(y_a + ALPHA * x_ref[rs_a, :], g_ref[...], b_ref[...])
    h_ref[rs_a, :] = h_a
    hp_ref[rs_a, :] = pltpu.pack_elementwise([h_a[:, 0:half], h_a[:, half:]], packed_dtype=BF16)
    y_b2 = (jnp.dot(mix_b, w_ref[:, half:], preferred_element_type=F32)
            + jnp.tile(exact_zero(h_a[0:SUBLANES, 0:LANES]), (OUT_PROJ_SUB // SUBLANES, half // LANES)))
    carry = carry_ref[:, 0:1]
    carry = _route(rs_a, h_a, exact_zero(y_b1[0:SUBLANES, half - LANES:half]), carry, wr_ref, br_ref,
                   route_ref, route_t_ref)
    h_b = _layer_norm(jnp.concatenate([y_b1, y_b2], axis=-1) + ALPHA * x_ref[rs_b, :], g_ref[...], b_ref[...])
    h_ref[rs_b, :] = h_b
    hp_ref[rs_b, :] = pltpu.pack_elementwise([h_b[:, 0:half], h_b[:, half:]], packed_dtype=BF16)
    carry = _route(rs_b, h_b, None, carry, wr_ref, br_ref, route_ref, route_t_ref)
    carry_ref[...] = jnp.broadcast_to(carry, carry_ref.shape)
    counts_ref[...] = jnp.broadcast_to(carry, counts_ref.shape)


def _route(rs, h, zero, carry, wr_ref, br_ref, route_ref, route_t_ref):
    logits = jnp.dot(h.astype(BF16), wr_ref[...], preferred_element_type=F32) + br_ref[...]

    tm = logits.shape[0]
    lt = jnp.transpose(logits)
    neg = jnp.float32(-jnp.inf)
    big = jnp.float32(1e9)
    grow = lax.broadcasted_iota(jnp.int32, (SUBLANES, tm), 0).astype(F32)
    gmask = grow < N_GROUPS
    gl = jnp.where(gmask, lt[N_EXPERTS:N_EXPERTS + SUBLANES, :], neg)
    gmax = jnp.max(gl, axis=0, keepdims=True)
    gidx = jnp.min(jnp.where(gl == gmax, grow, big), axis=0, keepdims=True)
    g_w = 1.0 / jnp.sum(jnp.where(gmask, jnp.exp(gl - gmax), 0.0), axis=0, keepdims=True)
    erow = lax.broadcasted_iota(jnp.int32, (N_EXPERTS, tm), 0).astype(F32)
    lo = gidx * EXPERTS_PER_GROUP
    emask = (erow >= lo) & (erow < lo + EXPERTS_PER_GROUP)
    el = jnp.where(emask, lt[0:N_EXPERTS, :], neg)
    top1 = jnp.max(el, axis=0, keepdims=True)
    idx1 = jnp.min(jnp.where(el == top1, erow, big), axis=0, keepdims=True)
    el2 = jnp.where(erow == idx1, neg, el)
    top2 = jnp.max(el2, axis=0, keepdims=True)
    idx2 = jnp.min(jnp.where(el2 == top2, erow, big), axis=0, keepdims=True)
    e2 = jnp.exp(top2 - top1)
    w1 = g_w / (1.0 + e2)
    w2 = w1 * e2

    sel1 = erow == idx1
    sel2 = erow == idx2
    onehot = jnp.where(sel1, 1.0, 0.0) + jnp.where(sel2, 1.0, 0.0)
    r_i = lax.broadcasted_iota(jnp.int32, (tm, tm), 0)
    c_i = lax.broadcasted_iota(jnp.int32, (tm, tm), 1)
    earlier = jnp.where(r_i < c_i, 1.0, 0.0).astype(BF16)
    prefix = jnp.dot(onehot.astype(BF16), earlier, preferred_element_type=F32) + carry
    rank1 = jnp.sum(jnp.where(sel1, prefix, 0.0), axis=0, keepdims=True)
    rank2 = jnp.sum(jnp.where(sel2, prefix, 0.0), axis=0, keepdims=True)

    route_t = jnp.concatenate([w1, w2, idx1, idx2, rank1, rank2, jnp.zeros((2, tm), F32)], axis=0)
    if zero is not None:
        route_t = route_t + jnp.tile(zero, (1, tm // LANES))
    route_t_ref[:, rs] = route_t
    route_ref[rs, :] = jnp.transpose(
        jnp.concatenate([route_t, jnp.zeros((LANES - SUBLANES, tm), F32)], axis=0))
    return carry + jnp.sum(onehot, axis=1, keepdims=True)


def _out_proj(ret3, conv3, x2, w_out_b, ln_g, ln_b, wr_b, br, tm=512):
    T = x2.shape[0]
    cur = lambda i: (0, i, 0)
    row = lambda i: (i, 0)
    const = lambda i: (0, 0)
    return pl.pallas_call(
        _outproj_kernel,
        grid=(T // tm,),
        in_specs=[pl.BlockSpec((HEADS, tm, LANES), cur),
                  pl.BlockSpec((CONV_GROUPS, tm, LANES), cur),
                  pl.BlockSpec((tm, D_MODEL), row),
                  pl.BlockSpec((D_MODEL, D_MODEL), const),
                  pl.BlockSpec((1, D_MODEL), const),
                  pl.BlockSpec((1, D_MODEL), const),
                  pl.BlockSpec((D_MODEL, LANES), const),
                  pl.BlockSpec((1, LANES), const)],
        out_specs=[pl.BlockSpec((tm, D_MODEL), row),
                   pl.BlockSpec((tm, D_MODEL // 2), row),
                   pl.BlockSpec((tm, LANES), row),
                   pl.BlockSpec((SUBLANES, tm), lambda i: (0, i)),
                   pl.BlockSpec((N_EXPERTS, LANES), const)],
        out_shape=[jax.ShapeDtypeStruct((T, D_MODEL), F32),
                   jax.ShapeDtypeStruct((T, D_MODEL // 2), jnp.uint32),
                   jax.ShapeDtypeStruct((T, LANES), F32),
                   jax.ShapeDtypeStruct((SUBLANES, T), F32),
                   jax.ShapeDtypeStruct((N_EXPERTS, LANES), F32)],
        scratch_shapes=[pltpu.VMEM((N_EXPERTS, LANES), F32)],
        compiler_params=pltpu.CompilerParams(
            dimension_semantics=("arbitrary",), vmem_limit_bytes=VMEM_LIMIT),
        name="out_proj_ln_route",
    )(ret3, conv3, x2, w_out_b, ln_g, ln_b, wr_b, br)


MOE_TM = 512
MOE_TB = 512


def _row_copy(src_ref, src_row, dst_ref, dst_row, sem):
    return pltpu.make_async_copy(src_ref.at[pl.ds(src_row, 1), :], dst_ref.at[pl.ds(dst_row, 1), :], sem)


def _experts_kernel(te_ref, nv_ref, src_ref, src_next_ref, h_ref, w13_ref, w2_ref, y_ref, x_ref, sems):
    i = pl.program_id(0)
    last = pl.num_programs(0) - 1
    slot = i % 2
    used = i < nv_ref[0]

    def gather_wait(s):
        pltpu.make_async_copy(h_ref.at[pl.ds(0, MOE_TM), :], x_ref.at[s, pl.ds(0, MOE_TM), :], sems.at[s]).wait()

    @pl.when(i == 0)
    def _():
        x_ref[:, MOE_TM:, :] = jnp.zeros((2, SUBLANES, D_MODEL // 2), jnp.uint32)

        def first(t, carry):
            _row_copy(h_ref, src_ref[0, 0, t], x_ref.at[0], t, sems.at[0]).start()
            return carry
        lax.fori_loop(0, MOE_TM, first, 0, unroll=8)

    @pl.when(used)
    def _():
        gather_wait(slot)

        def issue_half(k, width):
            for t in range(k * MOE_TM // 2, (k + 1) * MOE_TM // 2):
                _row_copy(h_ref, src_next_ref[0, 0, t], x_ref.at[1 - slot], t, sems.at[1 - slot]).start(priority=1)
            spare = x_ref[1 - slot, MOE_TM:, k * LANES:(k + 1) * LANES]
            zero = pltpu.bitcast((spare >> 16) >> 16, F32)
            return jnp.tile(zero, (MOE_TM // SUBLANES, width // LANES))

        words = x_ref[slot, 0:MOE_TM, :]
        xb = jnp.concatenate(
            [pltpu.unpack_elementwise(words, index=half, packed_dtype=BF16, unpacked_dtype=F32).astype(BF16)
             for half in range(2)], axis=1)
        a13 = jnp.dot(xb, w13_ref[0], preferred_element_type=F32) + issue_half(0, 2 * D_FF)
        a1 = a13[:, 0:D_FF]
        a3 = a13[:, D_FF:]
        act = (a1 * _sigmoid(a1) * a3).astype(BF16)
        y = jnp.dot(act, w2_ref[0], preferred_element_type=F32) + issue_half(1, D_MODEL)
        y_ref[...] = pltpu.pack_elementwise([y[:, 0:D_MODEL // 2], y[:, D_MODEL // 2:]], packed_dtype=BF16)

        @pl.when(i == last)
        def _():
            gather_wait(1 - slot)

    @pl.when(jnp.logical_not(used))
    def _():
        @pl.when(i == nv_ref[0])
        def _():
            gather_wait(slot)
        y_ref[...] = jnp.zeros_like(y_ref)


def _experts(tile_expert, n_valid, src_tok, h, w13b, w2b):
    n_tiles = src_tok.shape[0]
    row = lambda i, te, nv: (i, 0)
    wsel = lambda i, te, nv: (te[i], 0, 0)
    smem_row = lambda f: pl.BlockSpec((1, 1, MOE_TM), f, memory_space=pltpu.SMEM)
    return pl.pallas_call(
        _experts_kernel,
        grid_spec=pltpu.PrefetchScalarGridSpec(
            num_scalar_prefetch=2,
            grid=(n_tiles,),
            in_specs=[smem_row(lambda i, te, nv: (i, 0, 0)),
                      smem_row(lambda i, te, nv: (jnp.minimum(i + 1, n_tiles - 1), 0, 0)),
                      pl.BlockSpec(memory_space=pl.ANY),
                      pl.BlockSpec((1, D_MODEL, 2 * D_FF), wsel),
                      pl.BlockSpec((1, D_FF, D_MODEL), wsel)],
            out_specs=pl.BlockSpec((MOE_TM, D_MODEL // 2), row),
            scratch_shapes=[pltpu.VMEM((2, MOE_TM + SUBLANES, D_MODEL // 2), jnp.uint32),
                            pltpu.SemaphoreType.DMA((2,))]),
        out_shape=jax.ShapeDtypeStruct((n_tiles * MOE_TM, D_MODEL // 2), jnp.uint32),
        compiler_params=pltpu.CompilerParams(
            dimension_semantics=("arbitrary",), vmem_limit_bytes=VMEM_LIMIT),
        name="moe_experts",
    )(tile_expert, n_valid, src_tok, src_tok, h, w13b, w2b)


def _combine_kernel(pos_ref, pos_next_ref, route_ref, h_ref, g_ref, b_ref, ys_ref, o_ref,
                    ya0_ref, yb0_ref, ya1_ref, yb1_ref, sems):
    tb = h_ref.shape[0]
    i = pl.program_id(0)
    bufs = ((ya0_ref, yb0_ref), (ya1_ref, yb1_ref))

    hb = tb // 2

    def gather_wait(s, hf):
        for k in range(TOP_K):
            pltpu.make_async_copy(ys_ref.at[pl.ds(0, hb), :], bufs[s][k].at[pl.ds(hf * hb, hb), :],
                                  sems.at[2 * s + hf]).wait()

    @pl.when(i == 0)
    def _():
        for hf in range(2):
            def first(t, carry):
                for k in range(TOP_K):
                    _row_copy(ys_ref, pos_ref[0, 0, k * tb + t], bufs[0][k], t, sems.at[hf]).start()
                return carry
            lax.fori_loop(hf * hb, (hf + 1) * hb, first, 0, unroll=8)

    def expert_rows(buf, rs):
        words = buf[rs, :]
        return jnp.concatenate(
            [pltpu.unpack_elementwise(words, index=half, packed_dtype=BF16, unpacked_dtype=F32)
             for half in range(2)], axis=1)

    for par in range(2):
        @pl.when(i % 2 == par)
        def _():
            for hf in range(2):
                rs = slice(hf * hb, (hf + 1) * hb)
                gather_wait(par, hf)
                for t in range(hf * hb, (hf + 1) * hb):
                    for k in range(TOP_K):
                        _row_copy(ys_ref, pos_next_ref[0, 0, k * tb + t], bufs[1 - par][k], t,
                                  sems.at[2 * (1 - par) + hf]).start(priority=k)
                route = route_ref[rs, :]
                ffn = (route[:, 0:1] * expert_rows(bufs[par][0], rs)
                       + route[:, 1:2] * expert_rows(bufs[par][1], rs))
                o_ref[rs, :] = _layer_norm(ALPHA * h_ref[rs, :] + ffn, g_ref[...], b_ref[...])

            @pl.when(i == pl.num_programs(0) - 1)
            def _():
                for hf in range(2):
                    gather_wait(1 - par, hf)


def _combine(pos3, route, h, ln_g, ln_b, ys):
    T = h.shape[0]
    tb = MOE_TB
    row = lambda i: (i, 0)
    const = lambda i: (0, 0)
    return pl.pallas_call(
        _combine_kernel,
        grid=(T // tb,),
        in_specs=[pl.BlockSpec((1, 1, 2 * tb), lambda i: (i, 0, 0), memory_space=pltpu.SMEM),
                  pl.BlockSpec((1, 1, 2 * tb), lambda i: (jnp.minimum(i + 1, T // tb - 1), 0, 0),
                               memory_space=pltpu.SMEM),
                  pl.BlockSpec((tb, LANES), row),
                  pl.BlockSpec((tb, D_MODEL), row),
                  pl.BlockSpec((1, D_MODEL), const),
                  pl.BlockSpec((1, D_MODEL), const),
                  pl.BlockSpec(memory_space=pl.ANY)],
        out_specs=pl.BlockSpec((tb, D_MODEL), row),
        out_shape=jax.ShapeDtypeStruct((T, D_MODEL), F32),
        scratch_shapes=([pltpu.VMEM((tb, D_MODEL // 2), jnp.uint32)] * (2 * TOP_K)
                        + [pltpu.SemaphoreType.DMA((4,))]),
        compiler_params=pltpu.CompilerParams(
            dimension_semantics=("arbitrary",), vmem_limit_bytes=VMEM_LIMIT),
        name="moe_combine_ln",
    )(pos3, pos3, route, h, ln_g, ln_b, ys)


def _routed_moe(h, hp, route, route_t, counts, w13b, w2b, ln_g, ln_b):
    T = h.shape[0]
    n_tiles = (TOP_K * T) // MOE_TM + N_EXPERTS
    i32 = jnp.int32
    cnt = counts[:, 0].astype(i32)
    ntile = (cnt + MOE_TM - 1) // MOE_TM
    tile_end = jnp.cumsum(ntile)
    n_valid = tile_end[-1:]
    seg_start = (tile_end - ntile) * MOE_TM
    info = route_t[2:6].astype(i32)
    eids = jnp.arange(N_EXPERTS, dtype=i32)
    start_of = lambda e: jnp.sum(jnp.where(e[None, :] == eids[:, None], seg_start[:, None], 0), axis=0)
    pos1 = start_of(info[0]) + info[2]
    pos2 = start_of(info[1]) + info[3]
    pos3 = jnp.concatenate([pos1.reshape(T // MOE_TB, 1, MOE_TB), pos2.reshape(T // MOE_TB, 1, MOE_TB)], axis=2)
    tile_ids = jnp.minimum(jnp.arange(n_tiles, dtype=i32), n_valid[0] - 1)
    tile_expert = jnp.sum((tile_ids[:, None] >= tile_end[None, :]).astype(i32), axis=1)

    order = jnp.argsort(jnp.concatenate([pos1, pos2])).astype(i32)
    n_rows = n_tiles * MOE_TM
    padded = jnp.concatenate([jnp.zeros((n_rows,), i32), jnp.where(order >= T, order - T, order),
                              jnp.zeros((2 * n_rows - TOP_K * T,), i32)])
    shift = seg_start - (jnp.cumsum(cnt) - cnt)
    row_expert = jnp.repeat(tile_expert, MOE_TM)
    src_tok = jnp.zeros((n_rows,), i32)
    for e in range(N_EXPERTS):
        view = lax.dynamic_slice(padded, (n_rows - shift[e],), (n_rows,))
        src_tok = jnp.where(row_expert == e, view, src_tok)

    ys = _experts(tile_expert, n_valid.astype(i32), src_tok.reshape(n_tiles, 1, MOE_TM), hp, w13b, w2b)
    return _combine(pos3, route, h, ln_g, ln_b, ys)


def _retention_constants():
    hh = np.arange(HEADS, dtype=np.float64)
    log_gamma = np.log1p(-np.exp2(-5.0 - hh))
    pos = np.arange(CHUNK, dtype=np.float64)
    diff = pos[:, None] - pos[None, :]
    causal = diff >= 0
    decay = np.where(causal[None], np.exp(log_gamma[:, None, None] * np.where(causal, diff, 0.0)[None]), 0.0)
    zeta = np.exp(log_gamma[:, None] * (CHUNK - 1 - pos)[None])
    xi = np.exp(log_gamma[:, None] * (pos + 1.0)[None])
    cdec = np.exp(log_gamma * CHUNK)
    bc = lambda t: np.broadcast_to(t[:, :, None], (HEADS, CHUNK, LANES))
    return (jnp.asarray(cdec, F32), jnp.asarray(decay, F32),
            jnp.asarray(bc(zeta), F32), jnp.asarray(bc(xi), F32))


def _rope_tables(seq):
    f32 = np.float32
    inv = np.power(f32(ROPE_BASE), -np.arange(0, HEAD_DIM, 2, dtype=f32) / f32(HEAD_DIM)).astype(f32)
    ang = (np.arange(seq, dtype=f32)[:, None] * inv[None, :]).astype(f32).astype(np.float64)
    cos = np.repeat(np.cos(ang), 2, axis=-1)
    sin = np.sin(ang)
    sin_signed = np.stack([-sin, sin], axis=-1).reshape(seq, HEAD_DIM)
    return jnp.asarray(cos, F32), jnp.asarray(sin_signed, F32)


def kernel(x, w_in, ret_norm_g, ret_norm_b, conv_w, conv_b, conv_norm_g, conv_norm_b, w_out, ln1_g, ln1_b, w_router_group, b_router_group, w_router_expert, b_router_expert, w1, w3, w2, ln2_g, ln2_b):
    batch, seq, _ = x.shape
    T = batch * seq
    x2 = x.reshape(T, D_MODEL)

    w_heads = _regroup_w_in(w_in)

    cos_t, sin_t = _rope_tables(seq)
    head_params = jnp.stack([ret_norm_g, ret_norm_b, conv_b, conv_norm_g, conv_norm_b]
                            + [jnp.zeros_like(conv_b)] * (SUBLANES - 5)).reshape(SUBLANES, HEADS, LANES)
    head_params = jnp.transpose(head_params, (1, 0, 2))
    cw = jnp.transpose(conv_w.reshape(CONV_TAPS, CONV_GROUPS, LANES), (1, 0, 2))
    ret3, conv3, w13b, w2b, w_out_b = _mixer(
        x2, w_heads, cos_t, sin_t, _retention_constants(), head_params, cw, w1, w3, w2, w_out, batch, seq)

    wr = jnp.concatenate([jnp.transpose(w_router_expert, (1, 0, 2)).reshape(D_MODEL, N_EXPERTS),
                          w_router_group], axis=1)
    wr = jnp.pad(wr, ((0, 0), (0, LANES - wr.shape[1])))
    br = jnp.pad(jnp.concatenate([b_router_expert.reshape(-1), b_router_group]),
                 (0, LANES - N_EXPERTS - N_GROUPS)).reshape(1, LANES)

    h, hp, route, route_t, counts = _out_proj(ret3, conv3, x2, w_out_b, ln1_g.reshape(1, -1),
                                          ln1_b.reshape(1, -1), wr.astype(BF16), br)

    y = _routed_moe(h, hp, route, route_t, counts, w13b.reshape(N_EXPERTS, D_MODEL, 2 * D_FF),
                    w2b.reshape(w2.shape), ln2_g.reshape(1, -1), ln2_b.reshape(1, -1))
    return y.reshape(batch, seq, D_MODEL)
```
